```python
import math
import jax, jax.numpy as jnp
from jax import lax
import numpy as np

D_MODEL = 1024
BATCH = 4
SEQ = 4096
DEPTH = 4

GRID_W = 64
CTX_LEN = 256
D_MIX = D_MODEL
HG_WIDTH = D_MIX // 2
HG_HEADS = 4
HG_DK = HG_WIDTH // HG_HEADS
RET_WIDTH = D_MIX // 4
RET_HEADS = 4
RET_DK = RET_WIDTH // RET_HEADS
S5_WIDTH = D_MIX - HG_WIDTH - RET_WIDTH
S5_GROUP_CH = 16
S5_GROUPS = S5_WIDTH // S5_GROUP_CH
S5_STATE = 64
IN_COLS = 5 * HG_WIDTH + 4 * RET_WIDTH + S5_WIDTH
CHUNK = 64
N_GROUPS = 4
EXPERTS_PER_GROUP = 4
N_EXPERTS = N_GROUPS * EXPERTS_PER_GROUP
TOP_K_INNER = 2
EXPERT_HIDDEN = D_MODEL // 2
ALPHA = (2.0 * DEPTH) ** 0.25
BETA = (8.0 * DEPTH) ** -0.25
LN_EPS = 1e-5
ROPE_BASE = 10000.0

kernel_name = 'hybrid_hgrn2_retention_s5_hmoe_dit_block'


def _layer_norm(x, g, b):
    xf = x.astype(jnp.float32)
    mu = jnp.mean(xf, -1, keepdims=True)
    var = jnp.mean(jnp.square(xf - mu), -1, keepdims=True)
    return ((xf - mu) * lax.rsqrt(var + LN_EPS) * g + b).astype(x.dtype)


def _rms_norm(o, g):
    return o * lax.rsqrt(jnp.mean(jnp.square(o), -1, keepdims=True) + LN_EPS) * g.astype(jnp.float32)


def _head_group_norm(o, g, b):
    mu = jnp.mean(o, -1, keepdims=True)
    var = jnp.mean(jnp.square(o - mu), -1, keepdims=True)
    return _merge((o - mu) * lax.rsqrt(var + LN_EPS)) * g.astype(jnp.float32) + b.astype(jnp.float32)


def _heads(a, n_heads):
    bsz, t, w = a.shape
    return a.reshape(bsz, t, n_heads, w // n_heads).transpose(0, 2, 1, 3)


def _merge(a):
    bsz, h, t, d = a.shape
    return a.transpose(0, 2, 1, 3).reshape(bsz, t, h * d)


def _split_columns(p):
    sizes = [HG_WIDTH] * 5 + [RET_WIDTH] * 4 + [S5_WIDTH]
    cuts = [int(v) for v in np.cumsum(sizes)[:-1]]
    return jnp.split(p, cuts, axis=-1)


def _rotate_half(z, pos):
    m = z.shape[-1] // 2
    inv = ROPE_BASE ** (-jnp.arange(m, dtype=jnp.float32) / m)
    ang = pos.astype(jnp.float32)[:, None] * inv
    cos, sin = jnp.cos(ang), jnp.sin(ang)
    z1, z2 = z[..., :m], z[..., m:]
    return jnp.concatenate([z1 * cos - z2 * sin, z1 * sin + z2 * cos], -1)


def _axial_rotary(a, rows, cols):
    n = a.shape[-1] // 2
    return jnp.concatenate([_rotate_half(a[..., :n], rows), _rotate_half(a[..., n:], cols)], -1)


def _chunk_gated_scan(q, k, v, log_f, s0):
    bsz, h, t, dk = q.shape
    dv = v.shape[-1]
    n = t // CHUNK

    def chunks(a):
        return jnp.moveaxis(a.reshape(bsz, h, n, CHUNK, a.shape[-1]), 2, 0)

    lower = jnp.tril(jnp.ones((CHUNK, CHUNK), dtype=bool))[:, :, None]

    def step(s, blk):
        qc, kc, vc, fc = blk
        cum = jnp.cumsum(fc, axis=2)
        inter = jnp.einsum('bhik,bhkv->bhiv', qc * jnp.exp(cum), s)
        rel = jnp.where(lower, cum[:, :, :, None, :] - cum[:, :, None, :, :], -jnp.inf)
        scores = jnp.einsum('bhik,bhjk,bhijk->bhij', qc, kc, jnp.exp(rel))
        o = inter + jnp.einsum('bhij,bhjv->bhiv', scores, vc)
        tail = cum[:, :, -1:, :]
        s = jnp.exp(tail[:, :, 0, :, None]) * s + jnp.einsum('bhjk,bhjv->bhkv', kc * jnp.exp(tail - cum), vc)
        return s, o

    s_fin, o = lax.scan(step, s0, (chunks(q), chunks(k), chunks(v), chunks(log_f)))
    return jnp.moveaxis(o, 0, 2).reshape(bsz, h, t, dv), s_fin


def _bidir_gated_recurrence(ctx_in, lat_in):
    q_c, v_c, gates_c = ctx_in
    q_l, v_l, gates_l = lat_in
    bsz, h, _, dk = q_c.shape
    s0 = jnp.zeros((bsz, h, dk, v_c.shape[-1]), jnp.float32)
    o_c, o_l = 0.0, 0.0
    for d in range(2):
        flip = (lambda a: jnp.flip(a, axis=2)) if d else (lambda a: a)
        k_c, lf_c = gates_c[d]
        k_l, lf_l = gates_l[d]
        oc, s_ctx = _chunk_gated_scan(flip(q_c), flip(k_c), flip(v_c), flip(lf_c), s0)
        ol, _ = _chunk_gated_scan(flip(q_l), flip(k_l), flip(v_l), flip(lf_l), s_ctx)
        o_c = o_c + flip(oc)
        o_l = o_l + flip(ol)
    return o_c, o_l


def _s5_discretize(lam_re, lam_im, log_dt, b_re, b_im):
    f32 = jnp.float32
    lam_re = jnp.minimum(lam_re.astype(f32), -1e-4)
    lam_im = lam_im.astype(f32)
    dt = jnp.exp(log_dt.astype(f32))[:, None]
    mag = jnp.exp(dt * lam_re)
    abar_re, abar_im = mag * jnp.cos(dt * lam_im), mag * jnp.sin(dt * lam_im)
    den = jnp.square(lam_re) + jnp.square(lam_im)
    nr, ni = abar_re - 1.0, abar_im
    coef_re = ((nr * lam_re + ni * lam_im) / den)[..., None]
    coef_im = ((ni * lam_re - nr * lam_im) / den)[..., None]
    b_re, b_im = b_re.astype(f32), b_im.astype(f32)
    bbar_re = coef_re * b_re - coef_im * b_im
    bbar_im = coef_re * b_im + coef_im * b_re
    return abar_re, abar_im, bbar_re, bbar_im


def _s5_scan(u, abar_re, abar_im, bbar_re, bbar_im, h0_re, h0_im):
    bu_re = jnp.einsum('btgc,gpc->btgp', u, bbar_re)
    bu_im = jnp.einsum('btgc,gpc->btgp', u, bbar_im)
    bu_re = bu_re.at[:, 0].add(abar_re * h0_re - abar_im * h0_im)
    bu_im = bu_im.at[:, 0].add(abar_re * h0_im + abar_im * h0_re)
    a_re = jnp.broadcast_to(abar_re, bu_re.shape)
    a_im = jnp.broadcast_to(abar_im, bu_re.shape)

    def combine(e1, e2):
        a1r, a1i, b1r, b1i = e1
        a2r, a2i, b2r, b2i = e2
        return (a2r * a1r - a2i * a1i, a2r * a1i + a2i * a1r,
                a2r * b1r - a2i * b1i + b2r, a2r * b1i + a2i * b1r + b2i)

    _, _, h_re, h_im = lax.associative_scan(combine, (a_re, a_im, bu_re, bu_im), axis=1)
    return h_re, h_im


def _s5_bidirectional(u_c, u_l, lam_re, lam_im, log_dt, b_re, b_im, c_re, c_im):
    bsz = u_c.shape[0]
    gc = lambda u: u.reshape(u.shape[0], u.shape[1], S5_GROUPS, S5_GROUP_CH)
    ug_c, ug_l = gc(u_c), gc(u_l)
    zeros = jnp.zeros((bsz, S5_GROUPS, S5_STATE), jnp.float32)
    y_c, y_l = 0.0, 0.0
    for d in range(2):
        flip = (lambda a: jnp.flip(a, axis=1)) if d else (lambda a: a)
        disc = _s5_discretize(lam_re[d], lam_im[d], log_dt[d], b_re[d], b_im[d])
        hc_re, hc_im = _s5_scan(flip(ug_c), *disc, zeros, zeros)
        hl_re, hl_im = _s5_scan(flip(ug_l), *disc, hc_re[:, -1], hc_im[:, -1])
        cr, ci = c_re[d].astype(jnp.float32), c_im[d].astype(jnp.float32)
        read = lambda hr, hi: jnp.einsum('btgp,gcp->btgc', hr, cr) - jnp.einsum('btgp,gcp->btgc', hi, ci)
        y_c = y_c + flip(read(hc_re, hc_im))
        y_l = y_l + flip(read(hl_re, hl_im))
    return y_c.reshape(u_c.shape), y_l.reshape(u_l.shape)


def _hybrid_mixer(h_c, h_l, rows, cols, w_in, hg_lb, hg_norm_g, ret_decay_raw, ret_gn_g, ret_gn_b,
                  s5_lam_re, s5_lam_im, s5_log_dt, s5_b_re, s5_b_im, s5_c_re, s5_c_im, s5_d, s5_glu_w, s5_glu_b,
                  w_out):
    f32 = jnp.float32
    parts_c = _split_columns((h_c @ w_in).astype(f32))
    parts_l = _split_columns((h_l @ w_in).astype(f32))
    lb = hg_lb.astype(f32).reshape(2, HG_HEADS, 1, HG_DK)
    log_gamma = jax.nn.log_sigmoid(ret_decay_raw.astype(f32))

    def hgrn2_inputs(p):
        q = jax.nn.silu(_heads(p[0], HG_HEADS))
        v = _heads(p[1], HG_HEADS)
        gates = []
        for d in range(2):
            f = lb[d] + (1.0 - lb[d]) * jax.nn.sigmoid(_heads(p[3 + d], HG_HEADS))
            gates.append((1.0 - f, jnp.log(f)))
        return q, v, gates

    def retention_inputs(p, rotate):
        q = _heads(p[5], RET_HEADS)
        k = _heads(p[6], RET_HEADS) * RET_DK ** -0.5
        v = _heads(p[7], RET_HEADS)
        if rotate:
            q, k = _axial_rotary(q, rows, cols), _axial_rotary(k, rows, cols)
        gates = [(k, jnp.broadcast_to(log_gamma[d][:, None, None], k.shape)) for d in range(2)]
        return q, v, gates

    o_hg_c, o_hg_l = _bidir_gated_recurrence(hgrn2_inputs(parts_c), hgrn2_inputs(parts_l))
    o_rt_c, o_rt_l = _bidir_gated_recurrence(retention_inputs(parts_c, False), retention_inputs(parts_l, True))
    y5_c, y5_l = _s5_bidirectional(parts_c[9], parts_l[9], s5_lam_re, s5_lam_im, s5_log_dt,
                                   s5_b_re, s5_b_im, s5_c_re, s5_c_im)

    def merge_groups(p, o_hg, o_rt, y5):
        hg = _merge(_rms_norm(o_hg, hg_norm_g)) * jax.nn.silu(p[2])
        rt = _head_group_norm(o_rt, ret_gn_g, ret_gn_b) * jax.nn.silu(p[8])
        s5 = jax.nn.gelu(y5 + s5_d.astype(f32) * p[9])
        s5 = s5 * jax.nn.sigmoid(s5 @ s5_glu_w.astype(f32) + s5_glu_b.astype(f32))
        return jnp.concatenate([hg, rt, s5], -1).astype(w_out.dtype) @ w_out

    return merge_groups(parts_c, o_hg_c, o_rt_c, y5_c), merge_groups(parts_l, o_hg_l, o_rt_l, y5_l)


def _hier_moe(h, rg_w, rg_b, re_w, re_b, w_gate, w_up, w_down):
    f32 = jnp.float32
    g_prob = jax.nn.softmax((h @ rg_w + rg_b).astype(f32), -1)
    g_p, g_idx = lax.top_k(g_prob, 1)
    e_all = (jnp.einsum('nd,dge->nge', h, re_w) + re_b).astype(f32)
    e_logits = jnp.take_along_axis(e_all, g_idx[:, :, None], axis=1)[:, 0]
    e_top, e_idx = lax.top_k(e_logits, TOP_K_INNER)
    e_w = jax.nn.softmax(e_top, -1) * g_p
    flat_idx = g_idx * EXPERTS_PER_GROUP + e_idx
    combine = jnp.einsum('nk,nke->ne', e_w, jax.nn.one_hot(flat_idx, N_EXPERTS, dtype=f32)).astype(h.dtype)
    y = jnp.zeros_like(h)
    for e in range(N_EXPERTS):
        a = h @ w_gate[e]
        b = h @ w_up[e]
        y = y + combine[:, e:e + 1] * ((jax.nn.silu(a) * b) @ w_down[e])
    return y


def setup_inputs(seed: int = 0) -> dict:
    key = jax.random.key(seed)
    ks = jax.random.split(key, 40)
    f32 = jnp.float32

    def nrm(i, shape, scale):
        return scale * jax.random.normal(ks[i], shape, f32)

    s5_shape = (DEPTH, 2, S5_GROUPS, S5_STATE)
    n_state = jnp.arange(S5_STATE, dtype=f32)
    ret_init = jnp.log(2.0 ** (5.0 + jnp.arange(RET_HEADS, dtype=f32)) - 1.0)
    return {
        'x': nrm(0, (BATCH, SEQ, D_MODEL), 1.0),
        'c': nrm(1, (BATCH, D_MODEL), 1.0),
        'ctx': nrm(2, (BATCH, CTX_LEN, D_MODEL), 1.0),
        'c_ctx': nrm(3, (D_MODEL,), 1.0),
        'w_mod': nrm(4, (DEPTH, D_MODEL, 6 * D_MODEL), D_MODEL ** -0.5),
        'b_mod': nrm(5, (DEPTH, 6 * D_MODEL), 0.02),
        'w_in': nrm(6, (DEPTH, D_MODEL, IN_COLS), D_MODEL ** -0.5),
        'hg_lb_raw': nrm(7, (DEPTH, 2, HG_WIDTH), 0.5),
        'hg_norm_g': 1.0 + nrm(8, (DEPTH, HG_DK), 0.02),
        'ret_decay_raw': ret_init + nrm(9, (DEPTH, 2, RET_HEADS), 0.05),
        'ret_gn_g': 1.0 + nrm(10, (DEPTH, RET_WIDTH), 0.02),
        'ret_gn_b': nrm(11, (DEPTH, RET_WIDTH), 0.02),
        's5_lam_re': -0.5 + nrm(12, s5_shape, 0.01),
        's5_lam_im': math.pi * n_state + nrm(13, s5_shape, 0.01),
        's5_log_dt': jax.random.uniform(ks[14], (DEPTH, 2, S5_GROUPS), f32, math.log(1e-3), math.log(1e-1)),
        's5_b_re': nrm(15, (DEPTH, 2, S5_GROUPS, S5_STATE, S5_GROUP_CH), (2.0 * S5_GROUP_CH) ** -0.5),
        's5_b_im': nrm(16, (DEPTH, 2, S5_GROUPS, S5_STATE, S5_GROUP_CH), (2.0 * S5_GROUP_CH) ** -0.5),
        's5_c_re': nrm(17, (DEPTH, 2, S5_GROUPS, S5_GROUP_CH, S5_STATE), S5_STATE ** -0.5),
        's5_c_im': nrm(18, (DEPTH, 2, S5_GROUPS, S5_GROUP_CH, S5_STATE), S5_STATE ** -0.5),
        's5_d': nrm(19, (DEPTH, S5_WIDTH), 1.0),
        's5_glu_w': nrm(20, (DEPTH, S5_WIDTH, S5_WIDTH), S5_WIDTH ** -0.5),
        's5_glu_b': nrm(21, (DEPTH, S5_WIDTH), 0.01),
        'w_out': nrm(22, (DEPTH, D_MIX, D_MODEL), BETA * D_MIX ** -0.5),
        'ln1_g': 1.0 + nrm(23, (DEPTH, D_MODEL), 0.02),
        'ln1_b': nrm(24, (DEPTH, D_MODEL), 0.02),
        'ln2_g': 1.0 + nrm(25, (DEPTH, D_MODEL), 0.02),
        'ln2_b': nrm(26, (DEPTH, D_MODEL), 0.02),
        'rg_w': nrm(27, (DEPTH, D_MODEL, N_GROUPS), D_MODEL ** -0.5),
        'rg_b': nrm(28, (DEPTH, N_GROUPS), 0.01),
        're_w': nrm(29, (DEPTH, D_MODEL, N_GROUPS, EXPERTS_PER_GROUP), D_MODEL ** -0.5),
        're_b': nrm(30, (DEPTH, N_GROUPS, EXPERTS_PER_GROUP), 0.01),
        'exp_w_gate': nrm(31, (DEPTH, N_EXPERTS, D_MODEL, EXPERT_HIDDEN), D_MODEL ** -0.5),
        'exp_w_up': nrm(32, (DEPTH, N_EXPERTS, D_MODEL, EXPERT_HIDDEN), D_MODEL ** -0.5),
        'exp_w_down': nrm(33, (DEPTH, N_EXPERTS, EXPERT_HIDDEN, D_MODEL), BETA * EXPERT_HIDDEN ** -0.5),
    }


def reference(x, c, ctx, c_ctx, w_mod, b_mod, w_in, hg_lb_raw, hg_norm_g, ret_decay_raw, ret_gn_g, ret_gn_b,
              s5_lam_re, s5_lam_im, s5_log_dt, s5_b_re, s5_b_im, s5_c_re, s5_c_im, s5_d, s5_glu_w, s5_glu_b,
              w_out, ln1_g, ln1_b, ln2_g, ln2_b, rg_w, rg_b, re_w, re_b, exp_w_gate, exp_w_up, exp_w_down):
    bsz, t, d_model = x.shape
    ctx_len = ctx.shape[1]
    ROWS = t // GRID_W
    rows = jnp.repeat(jnp.arange(ROWS, dtype=jnp.int32), GRID_W)
    cols = jnp.tile(jnp.arange(GRID_W, dtype=jnp.int32), ROWS)
    hg_lb = jnp.cumsum(jax.nn.softmax(hg_lb_raw.astype(jnp.float32), axis=0), axis=0)
    hg_lb = hg_lb - hg_lb[:1]
    sc_lat = jax.nn.silu(c)
    sc_ctx = jax.nn.silu(c_ctx)
    for l in range(DEPTH):
        last = l == DEPTH - 1
        sh1, s1, g1, sh2, s2, g2 = jnp.split((sc_lat @ w_mod[l] + b_mod[l])[:, None, :], 6, axis=-1)
        sh1c, s1c, g1c, sh2c, s2c, g2c = jnp.split(sc_ctx @ w_mod[l] + b_mod[l], 6, axis=-1)
        y_c, y_l = _hybrid_mixer(ctx * (1.0 + s1c) + sh1c, x * (1.0 + s1) + sh1, rows, cols, w_in[l], hg_lb[l],
                                 hg_norm_g[l], ret_decay_raw[l], ret_gn_g[l], ret_gn_b[l], s5_lam_re[l],
                                 s5_lam_im[l], s5_log_dt[l], s5_b_re[l], s5_b_im[l], s5_c_re[l], s5_c_im[l],
                                 s5_d[l], s5_glu_w[l], s5_glu_b[l], w_out[l])
        x = _layer_norm(ALPHA * x + g1 * y_l, ln1_g[l], ln1_b[l])
        h_l = x * (1.0 + s2) + sh2
        if last:
            tokens = h_l
        else:
            ctx = _layer_norm(ALPHA * ctx + g1c * y_c, ln1_g[l], ln1_b[l])
            tokens = jnp.concatenate([ctx * (1.0 + s2c) + sh2c, h_l], axis=1)
        y = _hier_moe(tokens.reshape(-1, d_model), rg_w[l], rg_b[l], re_w[l], re_b[l],
                      exp_w_gate[l], exp_w_up[l], exp_w_down[l]).reshape(bsz, -1, d_model)
        x = _layer_norm(ALPHA * x + g2 * y[:, y.shape[1] - t:], ln2_g[l], ln2_b[l])
        if not last:
            ctx = _layer_norm(ALPHA * ctx + g2c * y[:, :ctx_len], ln2_g[l], ln2_b[l])
    return x
```

```python
import functools
import math

import numpy as np
import jax
import jax.numpy as jnp
from jax import lax
from jax.experimental import pallas as pl
from jax.experimental.pallas import tpu as pltpu

F32 = jnp.float32
BF16 = jnp.bfloat16

D_MODEL = 1024
HG_WIDTH = 512
HG_HEADS = 4
HG_DK = HG_WIDTH // HG_HEADS
RET_WIDTH = 256
RET_HEADS = 4
RET_DK = RET_WIDTH // RET_HEADS
S5_WIDTH = 256
S5_GROUP_CH = 16
S5_GROUPS = S5_WIDTH // S5_GROUP_CH
S5_STATE = 64
IN_COLS = 5 * HG_WIDTH + 4 * RET_WIDTH + S5_WIDTH
CHUNK = 64
N_GROUPS = 4
EXPERTS_PER_GROUP = 4
N_EXPERTS = N_GROUPS * EXPERTS_PER_GROUP
EXPERT_HIDDEN = D_MODEL // 2
LN_EPS = 1e-5
ROPE_BASE = 10000.0
GRID_W = 64

LANES = 128
SUBLANES = 8
S5_CHUNK = 32
VMEM_LIMIT = 56 * 1024 * 1024


def _cparams(sem):
    return pltpu.CompilerParams(dimension_semantics=sem, vmem_limit_bytes=VMEM_LIMIT)


def _split_bf16(x):
    hi = x.astype(BF16)
    lo = (x - hi.astype(F32)).astype(BF16)
    return hi, lo


def _dot(a, b):
    return jnp.dot(a, b, preferred_element_type=F32)


def _dot3(a, b):
    ah, al = _split_bf16(a)
    bh, bl = _split_bf16(b)
    return _dot(ah, bh) + _dot(ah, bl) + _dot(al, bh)


def _dot2(a, b_bf16):
    ah, al = _split_bf16(a)
    return _dot(ah, b_bf16) + _dot(al, b_bf16)


def _dot_nt(a, b):
    return lax.dot_general(a, b, (((1,), (1,)), ((), ())), preferred_element_type=F32)


def _dot_tn(a, b):
    return lax.dot_general(a, b, (((0,), (0,)), ((), ())), preferred_element_type=F32)


def _silu(x):
    return x * jax.nn.sigmoid(x)


def _mod_kernel(c_ref, w_ref, b_ref, o_ref):
    sc = _silu(c_ref[...])
    o_ref[0] = _dot3(sc, w_ref[0]) + b_ref[0]


def _modulation(cvec, w_mod, b_mod):
    depth, d, n = w_mod.shape
    rows = cvec.shape[0]
    tn = 1536
    return pl.pallas_call(
        _mod_kernel,
        grid=(depth, n // tn),
        in_specs=[
            pl.BlockSpec((rows, d), lambda l, j: (0, 0)),
            pl.BlockSpec((1, d, tn), lambda l, j: (l, 0, j)),
            pl.BlockSpec((1, 1, tn), lambda l, j: (l, 0, j)),
        ],
        out_specs=pl.BlockSpec((1, rows, tn), lambda l, j: (l, 0, j)),
        out_shape=jax.ShapeDtypeStruct((depth, rows, n), F32),
        compiler_params=_cparams(("arbitrary", "arbitrary")),
        name="modulation",
    )(cvec, w_mod, b_mod.reshape(depth, 1, n))


def _inproj_kernel(x_ref, m_ref, w_ref, o_ref):
    m = m_ref[0, 0]
    h = x_ref[0] * (1.0 + m[1:2, :]) + m[0:1, :]
    o_ref[0] = _dot(h.astype(BF16), w_ref[...])


def _inproj(x, modtab, w_in_bf16, tm, nct):
    b, s, d = x.shape
    n = w_in_bf16.shape[1]
    return pl.pallas_call(
        _inproj_kernel,
        grid=(b, s // tm),
        in_specs=[
            pl.BlockSpec((1, tm, d), lambda i, j: (i, j, 0)),
            pl.BlockSpec((1, 1, 6, d), lambda i, j: (i, jnp.minimum(j // nct, 1), 0, 0)),
            pl.BlockSpec((d, n), lambda i, j: (0, 0)),
        ],
        out_specs=pl.BlockSpec((1, tm, n), lambda i, j: (i, j, 0)),
        out_shape=jax.ShapeDtypeStruct((b, s, n), F32),
        compiler_params=_cparams(("arbitrary", "arbitrary")),
        name="inproj",
    )(x, modtab, w_in_bf16)


def _block_sums(lf, reverse):
    c = lf.shape[0]
    row = lax.broadcasted_iota(jnp.int32, (c, 1), 0)
    a, z = lf, lf
    out = []
    s = 1
    while s < c:
        out.append((a, z - a))
        up = pltpu.roll(z, s, 0)
        dn = pltpu.roll(z, c - s, 0)
        odd = (row & s) != 0
        if reverse:
            a = a + jnp.where(odd, 0.0, dn)
        else:
            a = a + jnp.where(odd, up, 0.0)
        z = z + jnp.where(odd, up, dn)
        s *= 2
    out.append((a, z - a))
    return out, z


def _hgrn_kernel(qf_ref, vf_ref, zf_ref, qb_ref, vb_ref, zb_ref, lb_ref, of_ref, ob_ref, st_ref):
    c = CHUNK

    @pl.when(pl.program_id(1) == 0)
    def _():
        st_ref[...] = jnp.zeros_like(st_ref)

    ri = lax.broadcasted_iota(jnp.int32, (c, c), 0)
    ci = lax.broadcasted_iota(jnp.int32, (c, c), 1)
    for d, (q_ref, v_ref, z_ref, o_ref) in enumerate(
        ((qf_ref, vf_ref, zf_ref, of_ref), (qb_ref, vb_ref, zb_ref, ob_ref))
    ):
        reverse = d == 1
        q = _silu(q_ref[0])
        v = v_ref[0].astype(BF16)
        lb = lb_ref[d : d + 1, :]
        f = lb + (1.0 - lb) * jax.nn.sigmoid(z_ref[0])
        k = 1.0 - f
        lf = jnp.log(f)
        levels, tot = _block_sums(lf, reverse)
        a_full, b_full = levels[-1]
        qs = [(q * jnp.exp(a)).astype(BF16) for a, _ in levels]
        ks = [(k * jnp.exp(bb)).astype(BF16) for _, bb in levels]
        q0 = q.astype(BF16)
        k0 = k.astype(BF16)
        tail = jnp.exp(tot[0:1, :])
        causal = (ri < ci) if reverse else (ri > ci)
        for h in range(HG_HEADS):
            hs = slice(h * HG_DK, (h + 1) * HG_DK)
            scores = jnp.where(ri == ci, _dot_nt(q0[:, hs], k0[:, hs]), 0.0)
            for lvl in range(len(levels) - 1):
                partner = ((ri >> lvl) ^ (ci >> lvl)) == 1
                scores = scores + jnp.where(partner & causal, _dot_nt(qs[lvl][:, hs], ks[lvl][:, hs]), 0.0)
            st = st_ref[d, h]
            o = _dot(scores.astype(BF16), v[:, hs]) + _dot_nt(qs[-1][:, hs], st.astype(BF16))
            o_ref[0, :, hs] = o
            st_ref[d, h] = st * tail[:, hs] + _dot_tn(v[:, hs], ks[-1][:, hs])


def _bwd_chunk(n, nc_ctx, nc_all):
    return jnp.where(n < nc_ctx, nc_ctx - 1 - n, nc_all + nc_ctx - 1 - n)


def _hgrn(p, lb, nc_ctx):
    b, s, _ = p.shape
    nc = s // CHUNK
    w = HG_WIDTH

    def fwd(col):
        return pl.BlockSpec((1, CHUNK, w), lambda i, n: (i, n, col))

    def bwd(col):
        return pl.BlockSpec((1, CHUNK, w), lambda i, n: (i, _bwd_chunk(n, nc_ctx, nc), col))

    out = jax.ShapeDtypeStruct((b, s, w), F32)
    return pl.pallas_call(
        _hgrn_kernel,
        grid=(b, nc),
        in_specs=[fwd(0), fwd(1), fwd(3), bwd(0), bwd(1), bwd(4), pl.BlockSpec((2, w), lambda i, n: (0, 0))],
        out_specs=[fwd(0), bwd(0)],
        out_shape=[out, out],
        scratch_shapes=[pltpu.VMEM((2, HG_HEADS, HG_DK, HG_DK), F32)],
        compiler_params=_cparams(("arbitrary", "arbitrary")),
        name="hgrn2_scan",
    )(p, p, p, p, p, p, lb)


def _swap_halves(x, half):
    n = x.shape[-1]
    lane = lax.broadcasted_iota(jnp.int32, (1, n), 1)
    lower = (lane & half) == 0
    return jnp.where(lower, pltpu.roll(x, n - half, 1), pltpu.roll(x, half, 1))


def _ret_kernel(qf_ref, kf_ref, vf_ref, cf_ref, sf_ref, qb_ref, kb_ref, vb_ref, cb_ref, sb_ref,
                dmat_ref, rq_ref, rk_ref, cd_ref, of_ref, ob_ref, st_ref):
    @pl.when(pl.program_id(1) == 0)
    def _():
        st_ref[...] = jnp.zeros_like(st_ref)

    half = RET_DK // 4
    for d, (q_ref, k_ref, v_ref, c_ref, s_ref, o_ref) in enumerate(
        ((qf_ref, kf_ref, vf_ref, cf_ref, sf_ref, of_ref), (qb_ref, kb_ref, vb_ref, cb_ref, sb_ref, ob_ref))
    ):
        cos = c_ref[...]
        sin = s_ref[...]
        q = q_ref[0]
        k = k_ref[0] * (RET_DK ** -0.5)
        q = q * cos + _swap_halves(q, half) * sin
        k = k * cos + _swap_halves(k, half) * sin
        v = v_ref[0].astype(BF16)
        q0 = q.astype(BF16)
        k0 = k.astype(BF16)
        qd = (q * rq_ref[d]).astype(BF16)
        kd = (k * rk_ref[d]).astype(BF16)
        cd = cd_ref[d : d + 1, :]
        for h in range(RET_HEADS):
            hs = slice(h * RET_DK, (h + 1) * RET_DK)
            scores = _dot_nt(q0[:, hs], k0[:, hs]) * dmat_ref[d, h]
            st = st_ref[d, h]
            o_ref[0, :, hs] = _dot(scores.astype(BF16), v[:, hs]) + _dot_nt(qd[:, hs], st.astype(BF16))
            st_ref[d, h] = st * cd[:, hs] + _dot_tn(v[:, hs], kd[:, hs])


def _retention(p, cos_tab, sin_tab, dmat, rq, rk, cdec, nc_ctx):
    b, s, _ = p.shape
    nc = s // CHUNK
    w = RET_WIDTH
    base = 5 * HG_WIDTH // w

    def fwd(col):
        return pl.BlockSpec((1, CHUNK, w), lambda i, n: (i, n, col))

    def bwd(col):
        return pl.BlockSpec((1, CHUNK, w), lambda i, n: (i, _bwd_chunk(n, nc_ctx, nc), col))

    tab_f = pl.BlockSpec((CHUNK, w), lambda i, n: (n, 0))
    tab_b = pl.BlockSpec((CHUNK, w), lambda i, n: (_bwd_chunk(n, nc_ctx, nc), 0))

    def whole(a):
        return pl.BlockSpec(a.shape, lambda i, n: (0,) * a.ndim)

    out = jax.ShapeDtypeStruct((b, s, w), F32)
    return pl.pallas_call(
        _ret_kernel,
        grid=(b, nc),
        in_specs=[fwd(base), fwd(base + 1), fwd(base + 2), tab_f, tab_f,
                  bwd(base), bwd(base + 1), bwd(base + 2), tab_b, tab_b,
                  whole(dmat), whole(rq), whole(rk), whole(cdec)],
        out_specs=[fwd(0), bwd(0)],
        out_shape=[out, out],
        scratch_shapes=[pltpu.VMEM((2, RET_HEADS, RET_DK, RET_DK), F32)],
        compiler_params=_cparams(("arbitrary", "arbitrary")),
        name="retention_scan",
    )(p, p, p, cos_tab, sin_tab, p, p, p, cos_tab, sin_tab, dmat, rq, rk, cdec)


def _rope_tables(t_lat, t_ctx):
    m = RET_DK // 4
    inv = ROPE_BASE ** (-jnp.arange(m, dtype=F32) / m)
    rows = jnp.repeat(jnp.arange(t_lat // GRID_W, dtype=jnp.int32), GRID_W).astype(F32)
    cols = jnp.tile(jnp.arange(GRID_W, dtype=jnp.int32), t_lat // GRID_W).astype(F32)

    def half_tables(pos):
        ang = pos[:, None] * inv
        c, s = jnp.cos(ang), jnp.sin(ang)
        return jnp.concatenate([c, c], -1), jnp.concatenate([-s, s], -1)

    cr, sr = half_tables(rows)
    cc, sc = half_tables(cols)
    cos_h = jnp.concatenate([cr, cc], -1)
    sin_h = jnp.concatenate([sr, sc], -1)
    cos = jnp.tile(cos_h, (1, RET_HEADS))
    sin = jnp.tile(sin_h, (1, RET_HEADS))
    cos = jnp.concatenate([jnp.ones((t_ctx, RET_WIDTH), F32), cos], 0)
    sin = jnp.concatenate([jnp.zeros((t_ctx, RET_WIDTH), F32), sin], 0)
    return cos, sin


def _retention_decay_tables(log_gamma):
    c = CHUNK
    i = jnp.arange(c, dtype=F32)
    diff = i[:, None] - i[None, :]
    lg = log_gamma[:, :, None, None]
    d_f = jnp.where(diff >= 0, jnp.exp(lg[0] * diff), 0.0)
    d_b = jnp.where(diff <= 0, jnp.exp(lg[1] * (-diff)), 0.0)
    dmat = jnp.stack([d_f, d_b], 0)
    lane_lg = jnp.repeat(log_gamma, RET_DK, axis=1)
    rq = jnp.stack([jnp.exp(lane_lg[0][None, :] * (i[:, None] + 1.0)), jnp.exp(lane_lg[1][None, :] * (c - i[:, None]))], 0)
    rk = jnp.stack([jnp.exp(lane_lg[0][None, :] * (c - 1.0 - i[:, None])), jnp.exp(lane_lg[1][None, :] * i[:, None])], 0)
    cdec = jnp.exp(lane_lg * c)
    return dmat, rq, rk, cdec


def _s5_kernel(u_ref, w1_ref, w2_ref, ac_ref, y_ref, a_scr, hf_scr, hb_scr, *, n_ctx, n_all):
    width = S5_CHUNK * S5_GROUP_CH
    st2 = 2 * S5_STATE
    a_scr[...] = _dot(u_ref[0], w1_ref[0])
    ac = ac_ref[0]

    def step(n, carry):
        hf, hb = carry
        rf = pl.multiple_of(n * SUBLANES, SUBLANES)
        nb = jnp.where(n < n_ctx, n_ctx - 1 - n, n_all + n_ctx - 1 - n)
        rb = pl.multiple_of(nb * SUBLANES, SUBLANES)
        hf_scr[pl.ds(rf, SUBLANES), :] = hf
        hb_scr[pl.ds(rb, SUBLANES), :] = hb
        inj_f = a_scr[pl.ds(rf, SUBLANES), width : width + st2]
        inj_b = a_scr[pl.ds(rb, SUBLANES), width + st2 : width + 2 * st2]
        hf = hf * ac[0:1, :] + pltpu.roll(hf, S5_STATE, 1) * ac[1:2, :] + inj_f
        hb = hb * ac[2:3, :] + pltpu.roll(hb, S5_STATE, 1) * ac[3:4, :] + inj_b
        return hf, hb

    zero = jnp.zeros((SUBLANES, st2), F32)
    lax.fori_loop(0, n_all, step, (zero, zero))
    y_ref[0] = (a_scr[:, :width]
                + _dot(hf_scr[...].astype(BF16), w2_ref[0, 0])
                + _dot(hb_scr[...].astype(BF16), w2_ref[0, 1]))


def _s5_conv(u_g, w1, w2, ac, n_ctx):
    g, m, width = u_g.shape
    n_all = m // SUBLANES
    st2 = 2 * S5_STATE
    return pl.pallas_call(
        functools.partial(_s5_kernel, n_ctx=n_ctx, n_all=n_all),
        grid=(g,),
        in_specs=[
            pl.BlockSpec((1, m, width), lambda i: (i, 0, 0)),
            pl.BlockSpec((1, width, width + 2 * st2), lambda i: (i, 0, 0)),
            pl.BlockSpec((1, 2, st2, width), lambda i: (i, 0, 0, 0)),
            pl.BlockSpec((1, SUBLANES, st2), lambda i: (i, 0, 0)),
        ],
        out_specs=pl.BlockSpec((1, m, width), lambda i: (i, 0, 0)),
        out_shape=jax.ShapeDtypeStruct((g, m, width), F32),
        scratch_shapes=[pltpu.VMEM((m, width + 2 * st2), F32), pltpu.VMEM((m, st2), F32), pltpu.VMEM((m, st2), F32)],
        compiler_params=_cparams(("arbitrary",)),
        name="s5_conv",
    )(u_g, w1, w2, ac)


def _s5_weights(lam_re, lam_im, log_dt, b_re, b_im, c_re, c_im):
    cs, ch, p = S5_CHUNK, S5_GROUP_CH, S5_STATE
    lam_re = jnp.minimum(lam_re.astype(F32), -1e-4)
    lam_im = lam_im.astype(F32)
    dt = jnp.exp(log_dt.astype(F32))[..., None]
    mag = jnp.exp(dt * lam_re)
    abar_re, abar_im = mag * jnp.cos(dt * lam_im), mag * jnp.sin(dt * lam_im)
    den = jnp.square(lam_re) + jnp.square(lam_im)
    nr, ni = abar_re - 1.0, abar_im
    coef_re = ((nr * lam_re + ni * lam_im) / den)[..., None]
    coef_im = ((ni * lam_re - nr * lam_im) / den)[..., None]
    b_re, b_im = b_re.astype(F32), b_im.astype(F32)
    bb_re = coef_re * b_re - coef_im * b_im
    bb_im = coef_re * b_im + coef_im * b_re
    c_re, c_im = c_re.astype(F32), c_im.astype(F32)
    tau = jnp.arange(cs + 1, dtype=F32)[:, None, None, None]
    pw_mag = jnp.exp(tau * (dt * lam_re)[None])
    pw_re = pw_mag * jnp.cos(tau * (dt * lam_im)[None])
    pw_im = pw_mag * jnp.sin(tau * (dt * lam_im)[None])
    hi = lax.Precision.HIGHEST
    ab_re = pw_re[..., None] * bb_re[None] - pw_im[..., None] * bb_im[None]
    ab_im = pw_re[..., None] * bb_im[None] + pw_im[..., None] * bb_re[None]
    kern = (jnp.einsum('dgcp,tdgpe->tdgce', c_re, ab_re, precision=hi)
            - jnp.einsum('dgcp,tdgpe->tdgce', c_im, ab_im, precision=hi))
    s_i = jnp.arange(cs)[:, None]
    t_i = jnp.arange(cs)[None, :]
    lag_f = t_i - s_i
    lag_b = s_i - t_i
    kf = jnp.where((lag_f >= 0)[:, :, None, None, None], kern[jnp.clip(lag_f, 0, cs), 0], 0.0)
    kb = jnp.where((lag_b >= 0)[:, :, None, None, None], kern[jnp.clip(lag_b, 0, cs), 1], 0.0)
    toep = jnp.transpose(kf + kb, (2, 0, 4, 1, 3)).reshape(S5_GROUPS, cs * ch, cs * ch)
    inj_f = jnp.concatenate([ab_re[cs - 1 - jnp.arange(cs), 0], ab_im[cs - 1 - jnp.arange(cs), 0]], axis=2)
    inj_b = jnp.concatenate([ab_re[jnp.arange(cs), 1], ab_im[jnp.arange(cs), 1]], axis=2)
    inj_f = jnp.transpose(inj_f, (1, 0, 3, 2)).reshape(S5_GROUPS, cs * ch, 2 * p)
    inj_b = jnp.transpose(inj_b, (1, 0, 3, 2)).reshape(S5_GROUPS, cs * ch, 2 * p)
    w1 = jnp.concatenate([toep, inj_f, inj_b], axis=-1)
    def readout(d, expo):
        cr = c_re[d][None] * pw_re[expo, d][:, :, None, :] - c_im[d][None] * pw_im[expo, d][:, :, None, :]
        ci = c_re[d][None] * pw_im[expo, d][:, :, None, :] + c_im[d][None] * pw_re[expo, d][:, :, None, :]
        w = jnp.concatenate([cr, -ci], axis=-1)
        return jnp.transpose(w, (1, 3, 0, 2)).reshape(S5_GROUPS, 2 * p, cs * ch)
    w2 = jnp.stack([readout(0, jnp.arange(cs) + 1), readout(1, cs - jnp.arange(cs))], axis=1)
    ac_rows = []
    for d in range(2):
        ar, ai = pw_re[cs, d], pw_im[cs, d]
        ac_rows += [jnp.concatenate([ar, ar], -1), jnp.concatenate([-ai, ai], -1)]
    ac = jnp.stack(ac_rows + [jnp.zeros_like(ac_rows[0])] * (SUBLANES - 4), axis=1)
    return w1.astype(BF16), w2.astype(BF16), ac


def _s5_group_layout(u):
    b, s, _ = u.shape
    cs, ch = S5_CHUNK, S5_GROUP_CH
    x = u.reshape(b, s // cs, cs, S5_GROUPS, ch)
    x = jnp.transpose(x, (3, 1, 0, 2, 4))
    x = jnp.pad(x, ((0, 0), (0, 0), (0, SUBLANES - b), (0, 0), (0, 0)))
    return x.reshape(S5_GROUPS, (s // cs) * SUBLANES, cs * ch)


def _s5_token_layout(y, b, s):
    cs, ch = S5_CHUNK, S5_GROUP_CH
    x = y.reshape(S5_GROUPS, s // cs, SUBLANES, cs, ch)[:, :, :b]
    return jnp.transpose(x, (2, 1, 3, 0, 4)).reshape(b, s, S5_WIDTH)


def _layer_norm(x, g, b):
    mu = jnp.mean(x, -1, keepdims=True)
    xc = x - mu
    var = jnp.mean(xc * xc, -1, keepdims=True)
    return xc * lax.rsqrt(var + LN_EPS) * g + b


def _route(logits):
    col = lambda i: logits[:, i : i + 1]
    gl = [col(i) for i in range(N_GROUPS)]
    gmax = functools.reduce(jnp.maximum, gl)
    g_idx = jnp.full_like(gmax, N_GROUPS - 1).astype(jnp.int32)
    for i in reversed(range(N_GROUPS - 1)):
        g_idx = jnp.where(gl[i] == gmax, i, g_idx)
    g_p = 1.0 / functools.reduce(lambda a, b: a + b, [jnp.exp(x - gmax) for x in gl])
    el = []
    for e in range(EXPERTS_PER_GROUP):
        v = col(N_GROUPS + (N_GROUPS - 1) * EXPERTS_PER_GROUP + e)
        for g in reversed(range(N_GROUPS - 1)):
            v = jnp.where(g_idx == g, col(N_GROUPS + g * EXPERTS_PER_GROUP + e), v)
        el.append(v)
    m1 = functools.reduce(jnp.maximum, el)
    i1 = jnp.full_like(g_idx, EXPERTS_PER_GROUP - 1)
    for e in reversed(range(EXPERTS_PER_GROUP - 1)):
        i1 = jnp.where(el[e] == m1, e, i1)
    rest = [jnp.where(i1 == e, -jnp.inf, el[e]) for e in range(EXPERTS_PER_GROUP)]
    m2 = functools.reduce(jnp.maximum, rest)
    i2 = jnp.full_like(g_idx, EXPERTS_PER_GROUP - 1)
    for e in reversed(range(EXPERTS_PER_GROUP - 1)):
        i2 = jnp.where((rest[e] == m2) & (i1 != e), e, i2)
    t = jnp.exp(m2 - m1)
    w1 = g_p / (1.0 + t)
    w2 = g_p * t / (1.0 + t)
    lane = lax.broadcasted_iota(jnp.int32, logits.shape, 1)
    base = g_idx * EXPERTS_PER_GROUP
    return jnp.where(lane == base + i1, w1, 0.0) + jnp.where(lane == base + i2, w2, 0.0)


def _merge_kernel(gate_ref, rg_ref, u_ref, hf_ref, hb_ref, rf_ref, rb_ref, y5_ref, x_ref, m_ref,
                  hgn_ref, gng_ref, gnb_ref, d_ref, gw_ref, gb_ref, wo_ref, lng_ref, lnb_ref,
                  wr_ref, br_ref, a128_ref, a64_ref, x1_ref, h2_ref, cmb_ref, *, alpha):
    o_hg = hf_ref[0] + hb_ref[0]
    ms = _dot2(o_hg * o_hg, a128_ref[...])
    hg = o_hg * lax.rsqrt(ms + LN_EPS) * hgn_ref[...] * _silu(gate_ref[0])
    o_rt = rf_ref[0] + rb_ref[0]
    mu = _dot2(o_rt, a64_ref[...])
    xc = o_rt - mu
    var = _dot2(xc * xc, a64_ref[...])
    rt = (xc * lax.rsqrt(var + LN_EPS) * gng_ref[...] + gnb_ref[...]) * _silu(rg_ref[0])
    s5 = jax.nn.gelu(y5_ref[0] + d_ref[...] * u_ref[0])
    s5 = s5 * jax.nn.sigmoid(_dot(s5.astype(BF16), gw_ref[...]) + gb_ref[...])
    cat = jnp.concatenate([hg, rt, s5], axis=-1).astype(BF16)
    y = _dot(cat, wo_ref[...])
    m = m_ref[0, 0]
    x1 = _layer_norm(alpha * x_ref[0] + m[2:3, :] * y, lng_ref[...], lnb_ref[...])
    x1_ref[0] = x1
    h2 = x1 * (1.0 + m[4:5, :]) + m[3:4, :]
    h2_ref[0] = h2.astype(BF16)
    logits = _dot3(h2, wr_ref[...]) + br_ref[...]
    cmb_ref[0] = _route(logits)


def _merge(p, o_hf, o_hb, o_rf, o_rb, y5, x, modtab, prm, tm, nct, alpha):
    b, s, d = x.shape

    def tok(width, col):
        return pl.BlockSpec((1, tm, width), lambda i, j: (i, j, col))

    def whole(a):
        return pl.BlockSpec(a.shape, lambda i, j: (0,) * a.ndim)

    rbase = 5 * HG_WIDTH // RET_WIDTH
    in_specs = [tok(HG_WIDTH, 2), tok(RET_WIDTH, rbase + 3), tok(S5_WIDTH, rbase + 4),
                tok(HG_WIDTH, 0), tok(HG_WIDTH, 0), tok(RET_WIDTH, 0), tok(RET_WIDTH, 0), tok(S5_WIDTH, 0),
                tok(d, 0),
                pl.BlockSpec((1, 1, 6, d), lambda i, j: (i, jnp.minimum(j // nct, 1), 0, 0))]
    in_specs += [whole(a) for a in prm]
    return pl.pallas_call(
        functools.partial(_merge_kernel, alpha=alpha),
        grid=(b, s // tm),
        in_specs=in_specs,
        out_specs=[tok(d, 0), tok(d, 0), tok(LANES, 0)],
        out_shape=[jax.ShapeDtypeStruct((b, s, d), F32), jax.ShapeDtypeStruct((b, s, d), BF16),
                   jax.ShapeDtypeStruct((b, s, LANES), F32)],
        compiler_params=_cparams(("arbitrary", "arbitrary")),
        name="merge_ln1_router",
    )(p, p, p, o_hf, o_hb, o_rf, o_rb, y5, x, modtab, *prm)


def _moe_kernel(h_ref, cmb_ref, x1_ref, m_ref, wg_ref, wu_ref, wd_ref, lng_ref, lnb_ref, o_ref, acc_ref, *, alpha):
    e = pl.program_id(2)

    @pl.when(e == 0)
    def _():
        acc_ref[...] = jnp.zeros_like(acc_ref)

    h = h_ref[0]
    a = _dot(h, wg_ref[0])
    u = _dot(h, wu_ref[0])
    act = (_silu(a) * u).astype(BF16)
    lane = lax.broadcasted_iota(jnp.int32, cmb_ref.shape[1:], 1)
    w = jnp.sum(jnp.where(lane == e, cmb_ref[0], 0.0), axis=-1, keepdims=True)
    acc_ref[...] += w * _dot(act, wd_ref[0])

    @pl.when(e == N_EXPERTS - 1)
    def _():
        m = m_ref[0, 0]
        o_ref[0] = _layer_norm(alpha * x1_ref[0] + m[5:6, :] * acc_ref[...], lng_ref[...], lnb_ref[...])


def _moe(h2, cmb, x1, modtab, wg, wu, wd, ln_g, ln_b, tm, nct, alpha):
    b, s, d = x1.shape
    eh = wg.shape[-1]
    tok = lambda width: pl.BlockSpec((1, tm, width), lambda i, j, e: (i, j, 0))
    return pl.pallas_call(
        functools.partial(_moe_kernel, alpha=alpha),
        grid=(b, s // tm, N_EXPERTS),
        in_specs=[tok(d), tok(LANES), tok(d),
                  pl.BlockSpec((1, 1, 6, d), lambda i, j, e: (i, jnp.minimum(j // nct, 1), 0, 0)),
                  pl.BlockSpec((1, d, eh), lambda i, j, e: (e, 0, 0)),
                  pl.BlockSpec((1, d, eh), lambda i, j, e: (e, 0, 0)),
                  pl.BlockSpec((1, eh, d), lambda i, j, e: (e, 0, 0)),
                  pl.BlockSpec((1, d), lambda i, j, e: (0, 0)),
                  pl.BlockSpec((1, d), lambda i, j, e: (0, 0))],
        out_specs=tok(d),
        out_shape=jax.ShapeDtypeStruct((b, s, d), F32),
        scratch_shapes=[pltpu.VMEM((tm, d), F32)],
        compiler_params=_cparams(("arbitrary", "arbitrary", "arbitrary")),
        name="experts_ln2",
    )(h2, cmb, x1, modtab, wg, wu, wd, ln_g, ln_b)


def _block_avg(width, group):
    idx = np.arange(width) // group
    return jnp.asarray((idx[:, None] == idx[None, :]).astype(np.float32) / group, dtype=BF16)


def _token_tile(t_ctx, t_lat):
    for tm in (256, 128, 64):
        if t_ctx % tm == 0 and t_lat % tm == 0:
            return tm
    raise ValueError("context and latent lengths must be multiples of 64")


def kernel(x, c, ctx, c_ctx, w_mod, b_mod, w_in, hg_lb_raw, hg_norm_g, ret_decay_raw, ret_gn_g, ret_gn_b, s5_lam_re, s5_lam_im, s5_log_dt, s5_b_re, s5_b_im, s5_c_re, s5_c_im, s5_d, s5_glu_w, s5_glu_b, w_out, ln1_g, ln1_b, ln2_g, ln2_b, rg_w, rg_b, re_w, re_b, exp_w_gate, exp_w_up, exp_w_down):
    bsz, t_lat, d = x.shape
    t_ctx = ctx.shape[1]
    depth = w_mod.shape[0]
    alpha = (2.0 * depth) ** 0.25
    assert d == D_MODEL and t_lat % GRID_W == 0 and t_ctx % CHUNK == 0 and bsz < SUBLANES
    assert t_ctx % S5_CHUNK == 0 and t_lat % S5_CHUNK == 0
    tm = _token_tile(t_ctx, t_lat)
    nct = t_ctx // tm
    s = t_ctx + t_lat

    rows = SUBLANES
    cvec = jnp.concatenate([c, c_ctx[None, :], jnp.zeros((rows - bsz - 1, d), F32)], 0)
    mod_all = _modulation(cvec, w_mod, b_mod)

    hg_lb = jnp.cumsum(jax.nn.softmax(hg_lb_raw.astype(F32), axis=0), axis=0)
    hg_lb = hg_lb - hg_lb[:1]
    log_gamma = jax.nn.log_sigmoid(ret_decay_raw.astype(F32))
    cos_tab, sin_tab = _rope_tables(t_lat, t_ctx)
    a128 = _block_avg(HG_WIDTH, HG_DK)
    a64 = _block_avg(RET_WIDTH, RET_DK)

    xs = jnp.concatenate([ctx, x], axis=1)
    for l in range(depth):
        m = mod_all[l]
        lat = m[:bsz].reshape(bsz, 6, d)
        cm = jnp.broadcast_to(m[bsz].reshape(1, 6, d), (bsz, 6, d))
        modtab = jnp.stack([cm, lat], axis=1)

        p = _inproj(xs, modtab, w_in[l].astype(BF16), tm, nct)
        o_hf, o_hb = _hgrn(p, hg_lb[l], t_ctx // CHUNK)
        dmat, rq, rk, cdec = _retention_decay_tables(log_gamma[l])
        o_rf, o_rb = _retention(p, cos_tab, sin_tab, dmat, rq, rk, cdec, t_ctx // CHUNK)
        w1, w2, ac = _s5_weights(s5_lam_re[l], s5_lam_im[l], s5_log_dt[l], s5_b_re[l], s5_b_im[l], s5_c_re[l], s5_c_im[l])
        u_g = _s5_group_layout(p[:, :, IN_COLS - S5_WIDTH:]).astype(BF16)
        y5 = _s5_token_layout(_s5_conv(u_g, w1, w2, ac, t_ctx // S5_CHUNK), bsz, s)

        wr = jnp.concatenate([rg_w[l], re_w[l].reshape(d, N_EXPERTS),
                              jnp.zeros((d, LANES - N_GROUPS - N_EXPERTS), F32)], axis=1)
        br = jnp.concatenate([rg_b[l], re_b[l].reshape(N_EXPERTS),
                              jnp.zeros((LANES - N_GROUPS - N_EXPERTS,), F32)])[None, :]
        prm = [jnp.tile(hg_norm_g[l], HG_HEADS)[None, :], ret_gn_g[l][None, :], ret_gn_b[l][None, :],
               s5_d[l][None, :], s5_glu_w[l].astype(BF16), s5_glu_b[l][None, :], w_out[l].astype(BF16),
               ln1_g[l][None, :], ln1_b[l][None, :], wr, br, a128, a64]
        x1, h2, cmb = _merge(p, o_hf, o_hb, o_rf, o_rb, y5, xs, modtab, prm, tm, nct, alpha)
        xs = _moe(h2, cmb, x1, modtab, exp_w_gate[l].astype(BF16), exp_w_up[l].astype(BF16),
                  exp_w_down[l].astype(BF16), ln2_g[l][None, :], ln2_b[l][None, :], tm, nct, alpha)
    return xs[:, t_ctx:, :]
```

```python
import functools
import math

import numpy as np
import jax
import jax.numpy as jnp
from jax import lax
from jax.experimental import pallas as pl
from jax.experimental.pallas import tpu as pltpu

F32 = jnp.float32
BF16 = jnp.bfloat16

D_MODEL = 1024
HG_WIDTH = 512
HG_HEADS = 4
HG_DK = HG_WIDTH // HG_HEADS
RET_WIDTH = 256
RET_HEADS = 4
RET_DK = RET_WIDTH // RET_HEADS
S5_WIDTH = 256
S5_GROUP_CH = 16
S5_GROUPS = S5_WIDTH // S5_GROUP_CH
S5_STATE = 64
IN_COLS = 5 * HG_WIDTH + 4 * RET_WIDTH + S5_WIDTH
CHUNK = 64
N_GROUPS = 4
EXPERTS_PER_GROUP = 4
N_EXPERTS = N_GROUPS * EXPERTS_PER_GROUP
N_PAIRS = EXPERTS_PER_GROUP * (EXPERTS_PER_GROUP - 1) // 2
N_BUCKETS = N_GROUPS * N_PAIRS
PAIR_LO = (0, 0, 0, 1, 1, 2)
PAIR_HI = (1, 2, 3, 2, 3, 3)
EXPERT_HIDDEN = D_MODEL // 2
LN_EPS = 1e-5
ROPE_BASE = 10000.0
GRID_W = 64

LANES = 128
SUBLANES = 8
S5_CHUNK = 32
EXPERT_TILE = 256
VMEM_LIMIT = 56 * 1024 * 1024


def _cparams(sem):
    return pltpu.CompilerParams(dimension_semantics=sem, vmem_limit_bytes=VMEM_LIMIT)


def _split_bf16(x):
    hi = x.astype(BF16)
    lo = (x - hi.astype(F32)).astype(BF16)
    return hi, lo


def _dot(a, b):
    return jnp.dot(a, b, preferred_element_type=F32)


def _dot3(a, b):
    ah, al = _split_bf16(a)
    bh, bl = _split_bf16(b)
    return _dot(ah, bh) + _dot(ah, bl) + _dot(al, bh)


def _dot2(a, b_bf16):
    ah, al = _split_bf16(a)
    return _dot(ah, b_bf16) + _dot(al, b_bf16)


def _dot_nt(a, b):
    return lax.dot_general(a, b, (((1,), (1,)), ((), ())), preferred_element_type=F32)


def _dot_tn(a, b):
    return lax.dot_general(a, b, (((0,), (0,)), ((), ())), preferred_element_type=F32)


def _silu(x):
    return x * jax.nn.sigmoid(x)


def _mod_kernel(c_ref, w_ref, b_ref, o_ref):
    sc = _silu(c_ref[...])
    o_ref[0] = _dot3(sc, w_ref[0]) + b_ref[0]


def _modulation(cvec, w_mod, b_mod):
    depth, d, n = w_mod.shape
    rows = cvec.shape[0]
    tn = 1536
    return pl.pallas_call(
        _mod_kernel,
        grid=(depth, n // tn),
        in_specs=[
            pl.BlockSpec((rows, d), lambda l, j: (0, 0)),
            pl.BlockSpec((1, d, tn), lambda l, j: (l, 0, j)),
            pl.BlockSpec((1, 1, tn), lambda l, j: (l, 0, j)),
        ],
        out_specs=pl.BlockSpec((1, rows, tn), lambda l, j: (l, 0, j)),
        out_shape=jax.ShapeDtypeStruct((depth, rows, n), F32),
        compiler_params=_cparams(("arbitrary", "arbitrary")),
        name="modulation",
    )(cvec, w_mod, b_mod.reshape(depth, 1, n))


def _inproj_kernel(x_ref, m_ref, w_ref, o_ref):
    m = m_ref[0, 0]
    h = x_ref[0] * (1.0 + m[1:2, :]) + m[0:1, :]
    o_ref[0] = _dot(h.astype(BF16), w_ref[...])


def _inproj(x, modtab, w_in_bf16, tm, nct):
    b, s, d = x.shape
    n = w_in_bf16.shape[1]
    return pl.pallas_call(
        _inproj_kernel,
        grid=(b, s // tm),
        in_specs=[
            pl.BlockSpec((1, tm, d), lambda i, j: (i, j, 0)),
            pl.BlockSpec((1, 1, 6, d), lambda i, j: (i, jnp.minimum(j // nct, 1), 0, 0)),
            pl.BlockSpec((d, n), lambda i, j: (0, 0)),
        ],
        out_specs=pl.BlockSpec((1, tm, n), lambda i, j: (i, j, 0)),
        out_shape=jax.ShapeDtypeStruct((b, s, n), F32),
        compiler_params=_cparams(("arbitrary", "arbitrary")),
        name="inproj",
    )(x, modtab, w_in_bf16)


def _block_sums(lf, reverse):
    c = lf.shape[0]
    row = lax.broadcasted_iota(jnp.int32, (c, 1), 0)
    a, z = lf, lf
    out = []
    s = 1
    while s < c:
        out.append((a, z - a))
        up = pltpu.roll(z, s, 0)
        dn = pltpu.roll(z, c - s, 0)
        odd = (row & s) != 0
        if reverse:
            a = a + jnp.where(odd, 0.0, dn)
        else:
            a = a + jnp.where(odd, up, 0.0)
        z = z + jnp.where(odd, up, dn)
        s *= 2
    out.append((a, z - a))
    return out, z


def _hgrn_kernel(qf_ref, vf_ref, zf_ref, qb_ref, vb_ref, zb_ref, lb_ref, of_ref, ob_ref, st_ref):
    c = CHUNK

    @pl.when(pl.program_id(1) == 0)
    def _():
        st_ref[...] = jnp.zeros_like(st_ref)

    ri = lax.broadcasted_iota(jnp.int32, (c, c), 0)
    ci = lax.broadcasted_iota(jnp.int32, (c, c), 1)
    for d, (q_ref, v_ref, z_ref, o_ref) in enumerate(
        ((qf_ref, vf_ref, zf_ref, of_ref), (qb_ref, vb_ref, zb_ref, ob_ref))
    ):
        reverse = d == 1
        q = _silu(q_ref[0])
        v = v_ref[0].astype(BF16)
        lb = lb_ref[d : d + 1, :]
        f = lb + (1.0 - lb) * jax.nn.sigmoid(z_ref[0])
        k = 1.0 - f
        lf = jnp.log(f)
        levels, tot = _block_sums(lf, reverse)
        a_full, b_full = levels[-1]
        qs = [(q * jnp.exp(a)).astype(BF16) for a, _ in levels]
        ks = [(k * jnp.exp(bb)).astype(BF16) for _, bb in levels]
        q0 = q.astype(BF16)
        k0 = k.astype(BF16)
        tail = jnp.exp(tot[0:1, :])
        causal = (ri < ci) if reverse else (ri > ci)
        for h in range(HG_HEADS):
            hs = slice(h * HG_DK, (h + 1) * HG_DK)
            scores = jnp.where(ri == ci, _dot_nt(q0[:, hs], k0[:, hs]), 0.0)
            for lvl in range(len(levels) - 1):
                partner = ((ri >> lvl) ^ (ci >> lvl)) == 1
                scores = scores + jnp.where(partner & causal, _dot_nt(qs[lvl][:, hs], ks[lvl][:, hs]), 0.0)
            st = st_ref[d, h]
            o = _dot(scores.astype(BF16), v[:, hs]) + _dot_nt(qs[-1][:, hs], st.astype(BF16))
            o_ref[0, :, hs] = o
            st_ref[d, h] = st * tail[:, hs] + _dot_tn(v[:, hs], ks[-1][:, hs])


def _bwd_chunk(n, nc_ctx, nc_all):
    return jnp.where(n < nc_ctx, nc_ctx - 1 - n, nc_all + nc_ctx - 1 - n)


def _hgrn(p, lb, nc_ctx):
    b, s, _ = p.shape
    nc = s // CHUNK
    w = HG_WIDTH

    def fwd(col):
        return pl.BlockSpec((1, CHUNK, w), lambda i, n: (i, n, col))

    def bwd(col):
        return pl.BlockSpec((1, CHUNK, w), lambda i, n: (i, _bwd_chunk(n, nc_ctx, nc), col))

    out = jax.ShapeDtypeStruct((b, s, w), F32)
    return pl.pallas_call(
        _hgrn_kernel,
        grid=(b, nc),
        in_specs=[fwd(0), fwd(1), fwd(3), bwd(0), bwd(1), bwd(4), pl.BlockSpec((2, w), lambda i, n: (0, 0))],
        out_specs=[fwd(0), bwd(0)],
        out_shape=[out, out],
        scratch_shapes=[pltpu.VMEM((2, HG_HEADS, HG_DK, HG_DK), F32)],
        compiler_params=_cparams(("arbitrary", "arbitrary")),
        name="hgrn2_scan",
    )(p, p, p, p, p, p, lb)


def _swap_halves(x, half):
    n = x.shape[-1]
    lane = lax.broadcasted_iota(jnp.int32, (1, n), 1)
    lower = (lane & half) == 0
    return jnp.where(lower, pltpu.roll(x, n - half, 1), pltpu.roll(x, half, 1))


def _ret_kernel(qf_ref, kf_ref, vf_ref, cf_ref, sf_ref, qb_ref, kb_ref, vb_ref, cb_ref, sb_ref,
                dmat_ref, rq_ref, rk_ref, cd_ref, of_ref, ob_ref, st_ref):
    @pl.when(pl.program_id(1) == 0)
    def _():
        st_ref[...] = jnp.zeros_like(st_ref)

    half = RET_DK // 4
    for d, (q_ref, k_ref, v_ref, c_ref, s_ref, o_ref) in enumerate(
        ((qf_ref, kf_ref, vf_ref, cf_ref, sf_ref, of_ref), (qb_ref, kb_ref, vb_ref, cb_ref, sb_ref, ob_ref))
    ):
        cos = c_ref[...]
        sin = s_ref[...]
        q = q_ref[0]
        k = k_ref[0] * (RET_DK ** -0.5)
        q = q * cos + _swap_halves(q, half) * sin
        k = k * cos + _swap_halves(k, half) * sin
        v = v_ref[0].astype(BF16)
        q0 = q.astype(BF16)
        k0 = k.astype(BF16)
        qd = (q * rq_ref[d]).astype(BF16)
        kd = (k * rk_ref[d]).astype(BF16)
        cd = cd_ref[d : d + 1, :]
        for h in range(RET_HEADS):
            hs = slice(h * RET_DK, (h + 1) * RET_DK)
            scores = _dot_nt(q0[:, hs], k0[:, hs]) * dmat_ref[d, h]
            st = st_ref[d, h]
            o_ref[0, :, hs] = _dot(scores.astype(BF16), v[:, hs]) + _dot_nt(qd[:, hs], st.astype(BF16))
            st_ref[d, h] = st * cd[:, hs] + _dot_tn(v[:, hs], kd[:, hs])


def _retention(p, cos_tab, sin_tab, dmat, rq, rk, cdec, nc_ctx):
    b, s, _ = p.shape
    nc = s // CHUNK
    w = RET_WIDTH
    base = 5 * HG_WIDTH // w

    def fwd(col):
        return pl.BlockSpec((1, CHUNK, w), lambda i, n: (i, n, col))

    def bwd(col):
        return pl.BlockSpec((1, CHUNK, w), lambda i, n: (i, _bwd_chunk(n, nc_ctx, nc), col))

    tab_f = pl.BlockSpec((CHUNK, w), lambda i, n: (n, 0))
    tab_b = pl.BlockSpec((CHUNK, w), lambda i, n: (_bwd_chunk(n, nc_ctx, nc), 0))

    def whole(a):
        return pl.BlockSpec(a.shape, lambda i, n: (0,) * a.ndim)

    out = jax.ShapeDtypeStruct((b, s, w), F32)
    return pl.pallas_call(
        _ret_kernel,
        grid=(b, nc),
        in_specs=[fwd(base), fwd(base + 1), fwd(base + 2), tab_f, tab_f,
                  bwd(base), bwd(base + 1), bwd(base + 2), tab_b, tab_b,
                  whole(dmat), whole(rq), whole(rk), whole(cdec)],
        out_specs=[fwd(0), bwd(0)],
        out_shape=[out, out],
        scratch_shapes=[pltpu.VMEM((2, RET_HEADS, RET_DK, RET_DK), F32)],
        compiler_params=_cparams(("arbitrary", "arbitrary")),
        name="retention_scan",
    )(p, p, p, cos_tab, sin_tab, p, p, p, cos_tab, sin_tab, dmat, rq, rk, cdec)


def _rope_tables(t_lat, t_ctx):
    m = RET_DK // 4
    inv = ROPE_BASE ** (-jnp.arange(m, dtype=F32) / m)
    rows = jnp.repeat(jnp.arange(t_lat // GRID_W, dtype=jnp.int32), GRID_W).astype(F32)
    cols = jnp.tile(jnp.arange(GRID_W, dtype=jnp.int32), t_lat // GRID_W).astype(F32)

    def half_tables(pos):
        ang = pos[:, None] * inv
        c, s = jnp.cos(ang), jnp.sin(ang)
        return jnp.concatenate([c, c], -1), jnp.concatenate([-s, s], -1)

    cr, sr = half_tables(rows)
    cc, sc = half_tables(cols)
    cos_h = jnp.concatenate([cr, cc], -1)
    sin_h = jnp.concatenate([sr, sc], -1)
    cos = jnp.tile(cos_h, (1, RET_HEADS))
    sin = jnp.tile(sin_h, (1, RET_HEADS))
    cos = jnp.concatenate([jnp.ones((t_ctx, RET_WIDTH), F32), cos], 0)
    sin = jnp.concatenate([jnp.zeros((t_ctx, RET_WIDTH), F32), sin], 0)
    return cos, sin


def _retention_decay_tables(log_gamma):
    c = CHUNK
    i = jnp.arange(c, dtype=F32)
    diff = i[:, None] - i[None, :]
    lg = log_gamma[:, :, None, None]
    d_f = jnp.where(diff >= 0, jnp.exp(lg[0] * diff), 0.0)
    d_b = jnp.where(diff <= 0, jnp.exp(lg[1] * (-diff)), 0.0)
    dmat = jnp.stack([d_f, d_b], 0)
    lane_lg = jnp.repeat(log_gamma, RET_DK, axis=1)
    rq = jnp.stack([jnp.exp(lane_lg[0][None, :] * (i[:, None] + 1.0)), jnp.exp(lane_lg[1][None, :] * (c - i[:, None]))], 0)
    rk = jnp.stack([jnp.exp(lane_lg[0][None, :] * (c - 1.0 - i[:, None])), jnp.exp(lane_lg[1][None, :] * i[:, None])], 0)
    cdec = jnp.exp(lane_lg * c)
    return dmat, rq, rk, cdec


def _s5_kernel(u_ref, w1_ref, w2_ref, ac_ref, y_ref, a_scr, hf_scr, hb_scr, *, n_ctx, n_all):
    width = S5_CHUNK * S5_GROUP_CH
    st2 = 2 * S5_STATE
    a_scr[...] = _dot(u_ref[0], w1_ref[0])
    ac = ac_ref[0]

    def step(n, carry):
        hf, hb = carry
        rf = pl.multiple_of(n * SUBLANES, SUBLANES)
        nb = jnp.where(n < n_ctx, n_ctx - 1 - n, n_all + n_ctx - 1 - n)
        rb = pl.multiple_of(nb * SUBLANES, SUBLANES)
        hf_scr[pl.ds(rf, SUBLANES), :] = hf
        hb_scr[pl.ds(rb, SUBLANES), :] = hb
        inj_f = a_scr[pl.ds(rf, SUBLANES), width : width + st2]
        inj_b = a_scr[pl.ds(rb, SUBLANES), width + st2 : width + 2 * st2]
        hf = hf * ac[0:1, :] + pltpu.roll(hf, S5_STATE, 1) * ac[1:2, :] + inj_f
        hb = hb * ac[2:3, :] + pltpu.roll(hb, S5_STATE, 1) * ac[3:4, :] + inj_b
        return hf, hb

    zero = jnp.zeros((SUBLANES, st2), F32)
    lax.fori_loop(0, n_all, step, (zero, zero))
    y_ref[0] = (a_scr[:, :width]
                + _dot(hf_scr[...].astype(BF16), w2_ref[0, 0])
                + _dot(hb_scr[...].astype(BF16), w2_ref[0, 1]))


def _s5_conv(u_g, w1, w2, ac, n_ctx):
    g, m, width = u_g.shape
    n_all = m // SUBLANES
    st2 = 2 * S5_STATE
    return pl.pallas_call(
        functools.partial(_s5_kernel, n_ctx=n_ctx, n_all=n_all),
        grid=(g,),
        in_specs=[
            pl.BlockSpec((1, m, width), lambda i: (i, 0, 0)),
            pl.BlockSpec((1, width, width + 2 * st2), lambda i: (i, 0, 0)),
            pl.BlockSpec((1, 2, st2, width), lambda i: (i, 0, 0, 0)),
            pl.BlockSpec((1, SUBLANES, st2), lambda i: (i, 0, 0)),
        ],
        out_specs=pl.BlockSpec((1, m, width), lambda i: (i, 0, 0)),
        out_shape=jax.ShapeDtypeStruct((g, m, width), F32),
        scratch_shapes=[pltpu.VMEM((m, width + 2 * st2), F32), pltpu.VMEM((m, st2), F32), pltpu.VMEM((m, st2), F32)],
        compiler_params=_cparams(("arbitrary",)),
        name="s5_conv",
    )(u_g, w1, w2, ac)


def _s5_weights(lam_re, lam_im, log_dt, b_re, b_im, c_re, c_im):
    cs, ch, p = S5_CHUNK, S5_GROUP_CH, S5_STATE
    lam_re = jnp.minimum(lam_re.astype(F32), -1e-4)
    lam_im = lam_im.astype(F32)
    dt = jnp.exp(log_dt.astype(F32))[..., None]
    mag = jnp.exp(dt * lam_re)
    abar_re, abar_im = mag * jnp.cos(dt * lam_im), mag * jnp.sin(dt * lam_im)
    den = jnp.square(lam_re) + jnp.square(lam_im)
    nr, ni = abar_re - 1.0, abar_im
    coef_re = ((nr * lam_re + ni * lam_im) / den)[..., None]
    coef_im = ((ni * lam_re - nr * lam_im) / den)[..., None]
    b_re, b_im = b_re.astype(F32), b_im.astype(F32)
    bb_re = coef_re * b_re - coef_im * b_im
    bb_im = coef_re * b_im + coef_im * b_re
    c_re, c_im = c_re.astype(F32), c_im.astype(F32)
    tau = jnp.arange(cs + 1, dtype=F32)[:, None, None, None]
    pw_mag = jnp.exp(tau * (dt * lam_re)[None])
    pw_re = pw_mag * jnp.cos(tau * (dt * lam_im)[None])
    pw_im = pw_mag * jnp.sin(tau * (dt * lam_im)[None])
    hi = lax.Precision.HIGHEST
    ab_re = pw_re[..., None] * bb_re[None] - pw_im[..., None] * bb_im[None]
    ab_im = pw_re[..., None] * bb_im[None] + pw_im[..., None] * bb_re[None]
    kern = (jnp.einsum('dgcp,tdgpe->tdgce', c_re, ab_re, precision=hi)
            - jnp.einsum('dgcp,tdgpe->tdgce', c_im, ab_im, precision=hi))
    s_i = jnp.arange(cs)[:, None]
    t_i = jnp.arange(cs)[None, :]
    lag_f = t_i - s_i
    lag_b = s_i - t_i
    kf = jnp.where((lag_f >= 0)[:, :, None, None, None], kern[jnp.clip(lag_f, 0, cs), 0], 0.0)
    kb = jnp.where((lag_b >= 0)[:, :, None, None, None], kern[jnp.clip(lag_b, 0, cs), 1], 0.0)
    toep = jnp.transpose(kf + kb, (2, 0, 4, 1, 3)).reshape(S5_GROUPS, cs * ch, cs * ch)
    inj_f = jnp.concatenate([ab_re[cs - 1 - jnp.arange(cs), 0], ab_im[cs - 1 - jnp.arange(cs), 0]], axis=2)
    inj_b = jnp.concatenate([ab_re[jnp.arange(cs), 1], ab_im[jnp.arange(cs), 1]], axis=2)
    inj_f = jnp.transpose(inj_f, (1, 0, 3, 2)).reshape(S5_GROUPS, cs * ch, 2 * p)
    inj_b = jnp.transpose(inj_b, (1, 0, 3, 2)).reshape(S5_GROUPS, cs * ch, 2 * p)
    w1 = jnp.concatenate([toep, inj_f, inj_b], axis=-1)
    def readout(d, expo):
        cr = c_re[d][None] * pw_re[expo, d][:, :, None, :] - c_im[d][None] * pw_im[expo, d][:, :, None, :]
        ci = c_re[d][None] * pw_im[expo, d][:, :, None, :] + c_im[d][None] * pw_re[expo, d][:, :, None, :]
        w = jnp.concatenate([cr, -ci], axis=-1)
        return jnp.transpose(w, (1, 3, 0, 2)).reshape(S5_GROUPS, 2 * p, cs * ch)
    w2 = jnp.stack([readout(0, jnp.arange(cs) + 1), readout(1, cs - jnp.arange(cs))], axis=1)
    ac_rows = []
    for d in range(2):
        ar, ai = pw_re[cs, d], pw_im[cs, d]
        ac_rows += [jnp.concatenate([ar, ar], -1), jnp.concatenate([-ai, ai], -1)]
    ac = jnp.stack(ac_rows + [jnp.zeros_like(ac_rows[0])] * (SUBLANES - 4), axis=1)
    return w1.astype(BF16), w2.astype(BF16), ac


def _s5_group_layout(u):
    b, s, _ = u.shape
    cs, ch = S5_CHUNK, S5_GROUP_CH
    x = u.reshape(b, s // cs, cs, S5_GROUPS, ch)
    x = jnp.transpose(x, (3, 1, 0, 2, 4))
    x = jnp.pad(x, ((0, 0), (0, 0), (0, SUBLANES - b), (0, 0), (0, 0)))
    return x.reshape(S5_GROUPS, (s // cs) * SUBLANES, cs * ch)


def _s5_token_layout(y, b, s):
    cs, ch = S5_CHUNK, S5_GROUP_CH
    x = y.reshape(S5_GROUPS, s // cs, SUBLANES, cs, ch)[:, :, :b]
    return jnp.transpose(x, (2, 1, 3, 0, 4)).reshape(b, s, S5_WIDTH)


def _layer_norm(x, g, b):
    mu = jnp.mean(x, -1, keepdims=True)
    xc = x - mu
    var = jnp.mean(xc * xc, -1, keepdims=True)
    return xc * lax.rsqrt(var + LN_EPS) * g + b


def _route(logits):
    col = lambda i: logits[:, i : i + 1]
    gl = [col(i) for i in range(N_GROUPS)]
    gmax = functools.reduce(jnp.maximum, gl)
    g_idx = jnp.full_like(gmax, N_GROUPS - 1).astype(jnp.int32)
    for i in reversed(range(N_GROUPS - 1)):
        g_idx = jnp.where(gl[i] == gmax, i, g_idx)
    g_p = 1.0 / functools.reduce(lambda a, b: a + b, [jnp.exp(x - gmax) for x in gl])
    el = []
    for e in range(EXPERTS_PER_GROUP):
        v = col(N_GROUPS + (N_GROUPS - 1) * EXPERTS_PER_GROUP + e)
        for g in reversed(range(N_GROUPS - 1)):
            v = jnp.where(g_idx == g, col(N_GROUPS + g * EXPERTS_PER_GROUP + e), v)
        el.append(v)
    m1 = functools.reduce(jnp.maximum, el)
    i1 = jnp.full_like(g_idx, EXPERTS_PER_GROUP - 1)
    for e in reversed(range(EXPERTS_PER_GROUP - 1)):
        i1 = jnp.where(el[e] == m1, e, i1)
    rest = [jnp.where(i1 == e, -jnp.inf, el[e]) for e in range(EXPERTS_PER_GROUP)]
    m2 = functools.reduce(jnp.maximum, rest)
    i2 = jnp.full_like(g_idx, EXPERTS_PER_GROUP - 1)
    for e in reversed(range(EXPERTS_PER_GROUP - 1)):
        i2 = jnp.where((rest[e] == m2) & (i1 != e), e, i2)
    t = jnp.exp(m2 - m1)
    w1 = g_p / (1.0 + t)
    w2 = g_p * t / (1.0 + t)
    lo = jnp.minimum(i1, i2)
    hi = jnp.maximum(i1, i2)
    pair = jnp.where(lo == 0, hi - 1, jnp.where(lo == 1, hi + 1, N_PAIRS - 1))
    bucket = g_idx * N_PAIRS + pair
    w_lo = jnp.where(i1 < i2, w1, w2)
    w_hi = jnp.where(i1 < i2, w2, w1)
    return bucket, w_lo, w_hi


def _merge_kernel(gate_ref, rg_ref, u_ref, hf_ref, hb_ref, rf_ref, rb_ref, y5_ref, x_ref, m_ref,
                  hgn_ref, gng_ref, gnb_ref, d_ref, gw_ref, gb_ref, wo_ref, lng_ref, lnb_ref,
                  wr_ref, br_ref, a128_ref, a64_ref, tri_ref, x1_ref, h2_ref, route_ref, cnt_ref, *, alpha):
    @pl.when((pl.program_id(0) == 0) & (pl.program_id(1) == 0))
    def _():
        cnt_ref[...] = jnp.zeros_like(cnt_ref)

    o_hg = hf_ref[0] + hb_ref[0]
    ms = _dot2(o_hg * o_hg, a128_ref[...])
    hg = o_hg * lax.rsqrt(ms + LN_EPS) * hgn_ref[...] * _silu(gate_ref[0])
    o_rt = rf_ref[0] + rb_ref[0]
    mu = _dot2(o_rt, a64_ref[...])
    xc = o_rt - mu
    var = _dot2(xc * xc, a64_ref[...])
    rt = (xc * lax.rsqrt(var + LN_EPS) * gng_ref[...] + gnb_ref[...]) * _silu(rg_ref[0])
    s5 = jax.nn.gelu(y5_ref[0] + d_ref[...] * u_ref[0])
    s5 = s5 * jax.nn.sigmoid(_dot(s5.astype(BF16), gw_ref[...]) + gb_ref[...])
    cat = jnp.concatenate([hg, rt, s5], axis=-1).astype(BF16)
    y = _dot(cat, wo_ref[...])
    m = m_ref[0, 0]
    x1 = _layer_norm(alpha * x_ref[0] + m[2:3, :] * y, lng_ref[...], lnb_ref[...])
    x1_ref[0] = x1
    h2 = x1 * (1.0 + m[4:5, :]) + m[3:4, :]
    logits = _dot3(h2, wr_ref[...]) + br_ref[...]
    bucket, w_lo, w_hi = _route(logits)
    lane = lax.broadcasted_iota(jnp.int32, logits.shape, 1)
    h2_ref[0, :, : h2.shape[1]] = h2
    h2_ref[0, :, h2.shape[1] :] = jnp.where(lane == 0, w_lo, jnp.where(lane == 1, w_hi, 0.0))
    onehot = lane == bucket
    before = _dot(tri_ref[...], onehot.astype(BF16)) + cnt_ref[...]
    rank = jnp.sum(jnp.where(onehot, before, 0.0), axis=-1, keepdims=True)
    route_ref[0] = jnp.where(lane == 0, bucket.astype(F32), jnp.where(lane == 1, rank, 0.0))
    cnt_ref[...] += jnp.sum(onehot.astype(F32), axis=0, keepdims=True)


def _merge(p, o_hf, o_hb, o_rf, o_rb, y5, x, modtab, prm, tm, nct, alpha):
    b, s, d = x.shape

    def tok(width, col):
        return pl.BlockSpec((1, tm, width), lambda i, j: (i, j, col))

    def whole(a):
        return pl.BlockSpec(a.shape, lambda i, j: (0,) * a.ndim)

    rbase = 5 * HG_WIDTH // RET_WIDTH
    in_specs = [tok(HG_WIDTH, 2), tok(RET_WIDTH, rbase + 3), tok(S5_WIDTH, rbase + 4),
                tok(HG_WIDTH, 0), tok(HG_WIDTH, 0), tok(RET_WIDTH, 0), tok(RET_WIDTH, 0), tok(S5_WIDTH, 0),
                tok(d, 0),
                pl.BlockSpec((1, 1, 6, d), lambda i, j: (i, jnp.minimum(j // nct, 1), 0, 0))]
    in_specs += [whole(a) for a in prm]
    return pl.pallas_call(
        functools.partial(_merge_kernel, alpha=alpha),
        grid=(b, s // tm),
        in_specs=in_specs,
        out_specs=[tok(d, 0), tok(d + LANES, 0), tok(LANES, 0), pl.BlockSpec((1, LANES), lambda i, j: (0, 0))],
        out_shape=[jax.ShapeDtypeStruct((b, s, d), F32), jax.ShapeDtypeStruct((b, s, d + LANES), F32),
                   jax.ShapeDtypeStruct((b, s, LANES), F32), jax.ShapeDtypeStruct((1, LANES), F32)],
        compiler_params=_cparams(("arbitrary", "arbitrary")),
        name="merge_ln1_router",
    )(p, p, p, o_hf, o_hb, o_rf, o_rb, y5, x, modtab, *prm)


def _routing_tables(route, counts, te, n_tiles):
    bucket = route[..., 0].astype(jnp.int32).reshape(-1)
    rank = route[..., 1].astype(jnp.int32).reshape(-1)
    cnt = counts[0, :N_BUCKETS].astype(jnp.int32)
    padded = (cnt + te - 1) // te * te
    ends = jnp.cumsum(padded)
    pos = (ends - padded)[bucket] + rank
    n_used = ends[-1] // te
    tile = jnp.arange(n_tiles, dtype=jnp.int32)
    tb = jnp.minimum(jnp.searchsorted(ends, tile * te, side="right"), N_BUCKETS - 1).astype(jnp.int32)
    tb = jnp.where(tile < n_used, tb, tb[jnp.maximum(n_used - 1, 0)])
    group, pair = tb // N_PAIRS, tb % N_PAIRS
    ea = group * EXPERTS_PER_GROUP + jnp.asarray(PAIR_LO, jnp.int32)[pair]
    eb = group * EXPERTS_PER_GROUP + jnp.asarray(PAIR_HI, jnp.int32)[pair]
    return pos, ea, eb, n_used.reshape(1).astype(jnp.int32)


def _dispatch_kernel(pos_ref, src_ref, init_ref, out_ref, sem, *, tm, tiles_per_row):
    del init_ref
    base = (pl.program_id(0) * tiles_per_row + pl.program_id(1)) * tm

    def row_copy(r):
        return pltpu.make_async_copy(src_ref.at[0, pl.ds(r, 1)], out_ref.at[pl.ds(pos_ref[base + r], 1)], sem)

    def start(r, carry):
        row_copy(r).start()
        return carry

    def wait(r, carry):
        row_copy(r).wait()
        return carry

    lax.fori_loop(0, tm, start, 0)
    lax.fori_loop(0, tm, wait, 0)


def _dispatch(pos, rows, n_sorted, tm):
    b, s, w = rows.shape
    return pl.pallas_call(
        functools.partial(_dispatch_kernel, tm=tm, tiles_per_row=s // tm),
        grid_spec=pltpu.PrefetchScalarGridSpec(
            num_scalar_prefetch=1,
            grid=(b, s // tm),
            in_specs=[pl.BlockSpec((1, tm, w), lambda i, j, pos_ref: (i, j, 0)),
                      pl.BlockSpec(memory_space=pl.ANY)],
            out_specs=pl.BlockSpec(memory_space=pl.ANY),
            scratch_shapes=[pltpu.SemaphoreType.DMA(())],
        ),
        out_shape=jax.ShapeDtypeStruct((n_sorted, w), F32),
        input_output_aliases={2: 0},
        compiler_params=_cparams(("arbitrary", "arbitrary")),
        name="moe_dispatch",
    )(pos, rows, jnp.zeros((n_sorted, w), F32))


def _expert_kernel(ea_ref, eb_ref, nused_ref, hs_ref, wga_ref, wua_ref, wda_ref, wgb_ref, wub_ref, wdb_ref,
                   o_ref, wg_scr, wu_scr, wd_scr):
    t = pl.program_id(0)
    prev = jnp.maximum(t - 1, 0)
    changed = (t == 0) | (ea_ref[t] != ea_ref[prev]) | (eb_ref[t] != eb_ref[prev])

    @pl.when(changed)
    def _():
        wg_scr[0] = wga_ref[0].astype(BF16)
        wu_scr[0] = wua_ref[0].astype(BF16)
        wd_scr[0] = wda_ref[0].astype(BF16)
        wg_scr[1] = wgb_ref[0].astype(BF16)
        wu_scr[1] = wub_ref[0].astype(BF16)
        wd_scr[1] = wdb_ref[0].astype(BF16)

    @pl.when(t < nused_ref[0])
    def _():
        d = o_ref.shape[1]
        h = hs_ref[:, :d].astype(BF16)
        y = jnp.zeros(o_ref.shape, F32)
        for e in range(2):
            act = (_silu(_dot(h, wg_scr[e])) * _dot(h, wu_scr[e])).astype(BF16)
            y = y + hs_ref[:, d + e : d + e + 1] * _dot(act, wd_scr[e])
        o_ref[...] = y

    @pl.when(t >= nused_ref[0])
    def _():
        o_ref[...] = jnp.zeros_like(o_ref)


def _experts(ea, eb, n_used, hs, wg, wu, wd, te):
    n_sorted, w = hs.shape
    _, d, eh = wg.shape

    def wspec(shape, which):
        return pl.BlockSpec((1,) + shape, lambda t, ea_ref, eb_ref, n_ref: ((ea_ref, eb_ref)[which][t], 0, 0))

    return pl.pallas_call(
        _expert_kernel,
        grid_spec=pltpu.PrefetchScalarGridSpec(
            num_scalar_prefetch=3,
            grid=(n_sorted // te,),
            in_specs=[pl.BlockSpec((te, w), lambda t, *_: (t, 0)),
                      wspec((d, eh), 0), wspec((d, eh), 0), wspec((eh, d), 0),
                      wspec((d, eh), 1), wspec((d, eh), 1), wspec((eh, d), 1)],
            out_specs=pl.BlockSpec((te, d), lambda t, *_: (t, 0)),
            scratch_shapes=[pltpu.VMEM((2, d, eh), BF16), pltpu.VMEM((2, d, eh), BF16), pltpu.VMEM((2, eh, d), BF16)],
        ),
        out_shape=jax.ShapeDtypeStruct((n_sorted, d), F32),
        compiler_params=_cparams(("arbitrary",)),
        name="moe_experts",
    )(ea, eb, n_used, hs, wg, wu, wd, wg, wu, wd)


def _combine_kernel(pos_ref, ys_ref, x1_ref, m_ref, lng_ref, lnb_ref, o_ref, buf, sem, *, tm, tiles_per_row, alpha):
    base = (pl.program_id(0) * tiles_per_row + pl.program_id(1)) * tm

    def row_copy(r):
        return pltpu.make_async_copy(ys_ref.at[pl.ds(pos_ref[base + r], 1)], buf.at[pl.ds(r, 1)], sem)

    def start(r, carry):
        row_copy(r).start()
        return carry

    def wait(r, carry):
        row_copy(r).wait()
        return carry

    lax.fori_loop(0, tm, start, 0)
    lax.fori_loop(0, tm, wait, 0)
    m = m_ref[0, 0]
    o_ref[0] = _layer_norm(alpha * x1_ref[0] + m[5:6, :] * buf[...], lng_ref[...], lnb_ref[...])


def _combine(pos, ys, x1, modtab, ln_g, ln_b, tm, nct, alpha):
    b, s, d = x1.shape
    return pl.pallas_call(
        functools.partial(_combine_kernel, tm=tm, tiles_per_row=s // tm, alpha=alpha),
        grid_spec=pltpu.PrefetchScalarGridSpec(
            num_scalar_prefetch=1,
            grid=(b, s // tm),
            in_specs=[pl.BlockSpec(memory_space=pl.ANY),
                      pl.BlockSpec((1, tm, d), lambda i, j, pos_ref: (i, j, 0)),
                      pl.BlockSpec((1, 1, 6, d), lambda i, j, pos_ref: (i, jnp.minimum(j // nct, 1), 0, 0)),
                      pl.BlockSpec((1, d), lambda i, j, pos_ref: (0, 0)),
                      pl.BlockSpec((1, d), lambda i, j, pos_ref: (0, 0))],
            out_specs=pl.BlockSpec((1, tm, d), lambda i, j, pos_ref: (i, j, 0)),
            scratch_shapes=[pltpu.VMEM((tm, d), F32), pltpu.SemaphoreType.DMA(())],
        ),
        out_shape=jax.ShapeDtypeStruct((b, s, d), F32),
        compiler_params=_cparams(("arbitrary", "arbitrary")),
        name="moe_combine_ln2",
    )(pos, ys, x1, modtab, ln_g, ln_b)


def _block_avg(width, group):
    idx = np.arange(width) // group
    return jnp.asarray((idx[:, None] == idx[None, :]).astype(np.float32) / group, dtype=BF16)


def _token_tile(t_ctx, t_lat):
    for tm in (256, 128, 64):
        if t_ctx % tm == 0 and t_lat % tm == 0:
            return tm
    raise ValueError("context and latent lengths must be multiples of 64")


def kernel(x, c, ctx, c_ctx, w_mod, b_mod, w_in, hg_lb_raw, hg_norm_g, ret_decay_raw, ret_gn_g, ret_gn_b, s5_lam_re, s5_lam_im, s5_log_dt, s5_b_re, s5_b_im, s5_c_re, s5_c_im, s5_d, s5_glu_w, s5_glu_b, w_out, ln1_g, ln1_b, ln2_g, ln2_b, rg_w, rg_b, re_w, re_b, exp_w_gate, exp_w_up, exp_w_down):
    bsz, t_lat, d = x.shape
    t_ctx = ctx.shape[1]
    depth = w_mod.shape[0]
    alpha = (2.0 * depth) ** 0.25
    assert d == D_MODEL and t_lat % GRID_W == 0 and t_ctx % CHUNK == 0 and bsz < SUBLANES
    assert t_ctx % S5_CHUNK == 0 and t_lat % S5_CHUNK == 0
    tm = _token_tile(t_ctx, t_lat)
    nct = t_ctx // tm
    s = t_ctx + t_lat

    rows = SUBLANES
    cvec = jnp.concatenate([c, c_ctx[None, :], jnp.zeros((rows - bsz - 1, d), F32)], 0)
    mod_all = _modulation(cvec, w_mod, b_mod)

    hg_lb = jnp.cumsum(jax.nn.softmax(hg_lb_raw.astype(F32), axis=0), axis=0)
    hg_lb = hg_lb - hg_lb[:1]
    log_gamma = jax.nn.log_sigmoid(ret_decay_raw.astype(F32))
    cos_tab, sin_tab = _rope_tables(t_lat, t_ctx)
    a128 = _block_avg(HG_WIDTH, HG_DK)
    a64 = _block_avg(RET_WIDTH, RET_DK)
    tri = jnp.asarray(np.tril(np.ones((tm, tm), np.float32), -1), dtype=BF16)
    te = min(EXPERT_TILE, tm)
    n_tiles = -(-(bsz * s + N_BUCKETS * (te - 1)) // te)

    xs = jnp.concatenate([ctx, x], axis=1)
    for l in range(depth):
        m = mod_all[l]
        lat = m[:bsz].reshape(bsz, 6, d)
        cm = jnp.broadcast_to(m[bsz].reshape(1, 6, d), (bsz, 6, d))
        modtab = jnp.stack([cm, lat], axis=1)

        p = _inproj(xs, modtab, w_in[l].astype(BF16), tm, nct)
        o_hf, o_hb = _hgrn(p, hg_lb[l], t_ctx // CHUNK)
        dmat, rq, rk, cdec = _retention_decay_tables(log_gamma[l])
        o_rf, o_rb = _retention(p, cos_tab, sin_tab, dmat, rq, rk, cdec, t_ctx // CHUNK)
        w1, w2, ac = _s5_weights(s5_lam_re[l], s5_lam_im[l], s5_log_dt[l], s5_b_re[l], s5_b_im[l], s5_c_re[l], s5_c_im[l])
        u_g = _s5_group_layout(p[:, :, IN_COLS - S5_WIDTH:]).astype(BF16)
        y5 = _s5_token_layout(_s5_conv(u_g, w1, w2, ac, t_ctx // S5_CHUNK), bsz, s)

        wr = jnp.concatenate([rg_w[l], re_w[l].reshape(d, N_EXPERTS),
                              jnp.zeros((d, LANES - N_GROUPS - N_EXPERTS), F32)], axis=1)
        br = jnp.concatenate([rg_b[l], re_b[l].reshape(N_EXPERTS),
                              jnp.zeros((LANES - N_GROUPS - N_EXPERTS,), F32)])[None, :]
        prm = [jnp.tile(hg_norm_g[l], HG_HEADS)[None, :], ret_gn_g[l][None, :], ret_gn_b[l][None, :],
               s5_d[l][None, :], s5_glu_w[l].astype(BF16), s5_glu_b[l][None, :], w_out[l].astype(BF16),
               ln1_g[l][None, :], ln1_b[l][None, :], wr, br, a128, a64, tri]
        x1, rows, route, counts = _merge(p, o_hf, o_hb, o_rf, o_rb, y5, xs, modtab, prm, tm, nct, alpha)
        pos, ea, eb, n_used = _routing_tables(route, counts, te, n_tiles)
        hs = _dispatch(pos, rows, n_tiles * te, tm)
        ys = _experts(ea, eb, n_used, hs, exp_w_gate[l], exp_w_up[l], exp_w_down[l], te)
        xs = _combine(pos, ys, x1, modtab, ln2_g[l][None, :], ln2_b[l][None, :], tm, nct, alpha)
    return xs[:, t_ctx:, :]
```

```python
import functools

import numpy as np
import jax
import jax.numpy as jnp
from jax import lax
from jax.experimental import pallas as pl
from jax.experimental.pallas import tpu as pltpu

F32 = jnp.float32
BF16 = jnp.bfloat16

D_MODEL = 1024
HG_WIDTH = 512
HG_HEADS = 4
HG_DK = HG_WIDTH // HG_HEADS
RET_WIDTH = 256
RET_HEADS = 4
RET_DK = RET_WIDTH // RET_HEADS
S5_WIDTH = 256
S5_GROUP_CH = 16
S5_GROUPS = S5_WIDTH // S5_GROUP_CH
S5_STATE = 64
IN_COLS = 5 * HG_WIDTH + 4 * RET_WIDTH + S5_WIDTH
CHUNK = 64
N_GROUPS = 4
EXPERTS_PER_GROUP = 4
N_EXPERTS = N_GROUPS * EXPERTS_PER_GROUP
N_PAIRS = EXPERTS_PER_GROUP * (EXPERTS_PER_GROUP - 1) // 2
N_BUCKETS = N_GROUPS * N_PAIRS
PAIR_LO = (0, 0, 0, 1, 1, 2)
PAIR_HI = (1, 2, 3, 2, 3, 3)
EXPERT_HIDDEN = D_MODEL // 2
LN_EPS = 1e-5
ROPE_BASE = 10000.0
GRID_W = 64

LANES = 128
SUBLANES = 8
TOKEN_TILE = 256
S5_CHUNK = 32
S5_FOLD = S5_CHUNK * S5_GROUP_CH
EXPERT_TILE = 256
DMA_UNROLL = 8
VMEM_LIMIT = 56 * 1024 * 1024


def _cparams(sem):
    return pltpu.CompilerParams(dimension_semantics=sem, vmem_limit_bytes=VMEM_LIMIT)


def _split_bf16(x):
    hi = x.astype(BF16)
    lo = (x - hi.astype(F32)).astype(BF16)
    return hi, lo


def _dot(a, b):
    return jnp.dot(a, b, preferred_element_type=F32)


def _dot3(a, b):
    ah, al = _split_bf16(a)
    bh, bl = _split_bf16(b)
    return _dot(ah, bh) + _dot(ah, bl) + _dot(al, bh)


def _dot2(a, b_bf16):
    ah, al = _split_bf16(a)
    return _dot(ah, b_bf16) + _dot(al, b_bf16)


def _dot_nt(a, b):
    return lax.dot_general(a, b, (((1,), (1,)), ((), ())), preferred_element_type=F32)


def _dot_tn(a, b):
    return lax.dot_general(a, b, (((0,), (0,)), ((), ())), preferred_element_type=F32)


def _silu(x):
    return x * jax.nn.sigmoid(x)


def _layer_spec(shape, l):
    zeros = (0,) * len(shape)
    return pl.BlockSpec((1,) + tuple(shape), lambda *_: (l,) + zeros)


def _mod_spec(l, nct):
    return pl.BlockSpec((1, 1, 1, 6, D_MODEL), lambda i, j, *_: (l, i, jnp.minimum(j // nct, 1), 0, 0))


def _swap_sublane_lanegroup(v):
    n = v.shape[0]
    r = lax.broadcasted_iota(jnp.int32, v.shape, 0)
    l = lax.broadcasted_iota(jnp.int32, v.shape, 1)
    for k in range(3):
        rb = (r >> k) & 1
        gb = (l >> (4 + k)) & 1
        sh = S5_GROUP_CH << k
        st = 1 << k
        a = pltpu.roll(pltpu.roll(v, LANES - sh, 1), st, 0)
        b = pltpu.roll(pltpu.roll(v, sh, 1), n - st, 0)
        v = jnp.where(rb == gb, v, jnp.where(rb == 1, a, b))
    return v


def _mod_kernel(c_ref, w_ref, b_ref, o_ref):
    sc = _silu(c_ref[...])
    o_ref[0] = _dot3(sc, w_ref[0]) + b_ref[0]


def _modulation(cvec, w_mod, b_mod):
    depth, d, n = w_mod.shape
    rows = cvec.shape[0]
    tn = 1536
    return pl.pallas_call(
        _mod_kernel,
        grid=(depth, n // tn),
        in_specs=[
            pl.BlockSpec((rows, d), lambda l, j: (0, 0)),
            pl.BlockSpec((1, d, tn), lambda l, j: (l, 0, j)),
            pl.BlockSpec((1, 1, tn), lambda l, j: (l, 0, j)),
        ],
        out_specs=pl.BlockSpec((1, rows, tn), lambda l, j: (l, 0, j)),
        out_shape=jax.ShapeDtypeStruct((depth, rows, n), F32),
        compiler_params=_cparams(("arbitrary", "arbitrary")),
        name="modulation",
    )(cvec, w_mod, b_mod.reshape(depth, 1, n))


def _inproj_kernel(x_ref, m_ref, w_ref, o_ref, ug_ref, wt_scr):
    m = m_ref[0, 0, 0]
    h = x_ref[0] * (1.0 + m[1:2, :]) + m[0:1, :]
    p = _dot(h.astype(BF16), w_ref[0])
    o_ref[0] = p
    rows_per_chunk = S5_CHUNK // SUBLANES
    n_chunks = p.shape[0] // S5_CHUNK
    for half in range(S5_WIDTH // LANES):
        lo = IN_COLS - S5_WIDTH + half * LANES
        wt_scr[...] = _swap_sublane_lanegroup(p[:, lo : lo + LANES])
        for g_lo in range(SUBLANES):
            for s_hi in range(rows_per_chunk):
                piece = wt_scr[pl.ds(s_hi * SUBLANES + g_lo, n_chunks, stride=S5_CHUNK), :]
                ug_ref[half * SUBLANES + g_lo, :, s_hi * LANES : (s_hi + 1) * LANES] = piece


def _inproj(x, modtab, w_in_bf16, l, tm, nct):
    b, s, d = x.shape
    n = w_in_bf16.shape[-1]
    cpt = tm // S5_CHUNK
    return pl.pallas_call(
        _inproj_kernel,
        grid=(b, s // tm),
        in_specs=[
            pl.BlockSpec((1, tm, d), lambda i, j: (i, j, 0)),
            _mod_spec(l, nct),
            _layer_spec((d, n), l),
        ],
        out_specs=[pl.BlockSpec((1, tm, n), lambda i, j: (i, j, 0)),
                   pl.BlockSpec((S5_GROUPS, cpt, S5_FOLD), lambda i, j: (0, j, i))],
        out_shape=[jax.ShapeDtypeStruct((b, s, n), F32),
                   jax.ShapeDtypeStruct((S5_GROUPS, s // S5_CHUNK, b * S5_FOLD), F32)],
        scratch_shapes=[pltpu.VMEM((tm, LANES), F32)],
        compiler_params=_cparams(("arbitrary", "arbitrary")),
        name="inproj",
    )(x, modtab, w_in_bf16)


def _block_sums(lf, reverse):
    c = lf.shape[0]
    row = lax.broadcasted_iota(jnp.int32, (c, 1), 0)
    a, z = lf, lf
    out = []
    s = 1
    while s < c:
        out.append((a, z - a))
        up = pltpu.roll(z, s, 0)
        dn = pltpu.roll(z, c - s, 0)
        odd = (row & s) != 0
        if reverse:
            a = a + jnp.where(odd, 0.0, dn)
        else:
            a = a + jnp.where(odd, up, 0.0)
        z = z + jnp.where(odd, up, dn)
        s *= 2
    out.append((a, z - a))
    return out, z


def _hgrn_kernel(qf_ref, vf_ref, zf_ref, qb_ref, vb_ref, zb_ref, lb_ref, of_ref, ob_ref, st_ref):
    c = CHUNK

    @pl.when(pl.program_id(1) == 0)
    def _():
        st_ref[...] = jnp.zeros_like(st_ref)

    ri = lax.broadcasted_iota(jnp.int32, (c, c), 0)
    ci = lax.broadcasted_iota(jnp.int32, (c, c), 1)
    for d, (q_ref, v_ref, z_ref, o_ref) in enumerate(
        ((qf_ref, vf_ref, zf_ref, of_ref), (qb_ref, vb_ref, zb_ref, ob_ref))
    ):
        reverse = d == 1
        q = _silu(q_ref[0])
        v = v_ref[0].astype(BF16)
        lb = lb_ref[0, d : d + 1, :]
        f = lb + (1.0 - lb) * jax.nn.sigmoid(z_ref[0])
        k = 1.0 - f
        lf = jnp.log(f)
        levels, tot = _block_sums(lf, reverse)
        qs = [(q * jnp.exp(a)).astype(BF16) for a, _ in levels]
        ks = [(k * jnp.exp(bb)).astype(BF16) for _, bb in levels]
        q0 = q.astype(BF16)
        k0 = k.astype(BF16)
        tail = jnp.exp(tot[0:1, :])
        causal = (ri < ci) if reverse else (ri > ci)
        for h in range(HG_HEADS):
            hs = slice(h * HG_DK, (h + 1) * HG_DK)
            scores = jnp.where(ri == ci, _dot_nt(q0[:, hs], k0[:, hs]), 0.0)
            for lvl in range(len(levels) - 1):
                partner = ((ri >> lvl) ^ (ci >> lvl)) == 1
                scores = scores + jnp.where(partner & causal, _dot_nt(qs[lvl][:, hs], ks[lvl][:, hs]), 0.0)
            st = st_ref[d, h]
            o = _dot(scores.astype(BF16), v[:, hs]) + _dot_nt(qs[-1][:, hs], st.astype(BF16))
            o_ref[0, :, hs] = o
            st_ref[d, h] = st * tail[:, hs] + _dot_tn(v[:, hs], ks[-1][:, hs])


def _bwd_chunk(n, nc_ctx, nc_all):
    return jnp.where(n < nc_ctx, nc_ctx - 1 - n, nc_all + nc_ctx - 1 - n)


def _hgrn(p, lb_all, l, nc_ctx):
    b, s, _ = p.shape
    nc = s // CHUNK
    w = HG_WIDTH

    def fwd(col):
        return pl.BlockSpec((1, CHUNK, w), lambda i, n: (i, n, col))

    def bwd(col):
        return pl.BlockSpec((1, CHUNK, w), lambda i, n: (i, _bwd_chunk(n, nc_ctx, nc), col))

    out = jax.ShapeDtypeStruct((b, s, w), F32)
    return pl.pallas_call(
        _hgrn_kernel,
        grid=(b, nc),
        in_specs=[fwd(0), fwd(1), fwd(3), bwd(0), bwd(1), bwd(4), _layer_spec((2, w), l)],
        out_specs=[fwd(0), bwd(0)],
        out_shape=[out, out],
        scratch_shapes=[pltpu.VMEM((2, HG_HEADS, HG_DK, HG_DK), F32)],
        compiler_params=_cparams(("arbitrary", "arbitrary")),
        name="hgrn2_scan",
    )(p, p, p, p, p, p, lb_all)


def _swap_halves(x, half):
    n = x.shape[-1]
    lane = lax.broadcasted_iota(jnp.int32, (1, n), 1)
    lower = (lane & half) == 0
    return jnp.where(lower, pltpu.roll(x, n - half, 1), pltpu.roll(x, half, 1))


def _ret_kernel(qf_ref, kf_ref, vf_ref, cf_ref, sf_ref, qb_ref, kb_ref, vb_ref, cb_ref, sb_ref,
                dmat_ref, rq_ref, rk_ref, cd_ref, of_ref, ob_ref, st_ref):
    @pl.when(pl.program_id(1) == 0)
    def _():
        st_ref[...] = jnp.zeros_like(st_ref)

    half = RET_DK // 4
    for d, (q_ref, k_ref, v_ref, c_ref, s_ref, o_ref) in enumerate(
        ((qf_ref, kf_ref, vf_ref, cf_ref, sf_ref, of_ref), (qb_ref, kb_ref, vb_ref, cb_ref, sb_ref, ob_ref))
    ):
        cos = c_ref[...]
        sin = s_ref[...]
        q = q_ref[0]
        k = k_ref[0] * (RET_DK ** -0.5)
        q = q * cos + _swap_halves(q, half) * sin
        k = k * cos + _swap_halves(k, half) * sin
        v = v_ref[0].astype(BF16)
        q0 = q.astype(BF16)
        k0 = k.astype(BF16)
        qd = (q * rq_ref[0, d]).astype(BF16)
        kd = (k * rk_ref[0, d]).astype(BF16)
        cd = cd_ref[0, d : d + 1, :]
        for h in range(RET_HEADS):
            hs = slice(h * RET_DK, (h + 1) * RET_DK)
            scores = _dot_nt(q0[:, hs], k0[:, hs]) * dmat_ref[0, d, h]
            st = st_ref[d, h]
            o_ref[0, :, hs] = _dot(scores.astype(BF16), v[:, hs]) + _dot_nt(qd[:, hs], st.astype(BF16))
            st_ref[d, h] = st * cd[:, hs] + _dot_tn(v[:, hs], kd[:, hs])


def _retention(p, cos_tab, sin_tab, tables, l, nc_ctx):
    b, s, _ = p.shape
    nc = s // CHUNK
    w = RET_WIDTH
    base = 5 * HG_WIDTH // w

    def fwd(col):
        return pl.BlockSpec((1, CHUNK, w), lambda i, n: (i, n, col))

    def bwd(col):
        return pl.BlockSpec((1, CHUNK, w), lambda i, n: (i, _bwd_chunk(n, nc_ctx, nc), col))

    tab_f = pl.BlockSpec((CHUNK, w), lambda i, n: (n, 0))
    tab_b = pl.BlockSpec((CHUNK, w), lambda i, n: (_bwd_chunk(n, nc_ctx, nc), 0))
    out = jax.ShapeDtypeStruct((b, s, w), F32)
    return pl.pallas_call(
        _ret_kernel,
        grid=(b, nc),
        in_specs=[fwd(base), fwd(base + 1), fwd(base + 2), tab_f, tab_f,
                  bwd(base), bwd(base + 1), bwd(base + 2), tab_b, tab_b]
                 + [_layer_spec(t.shape[1:], l) for t in tables],
        out_specs=[fwd(0), bwd(0)],
        out_shape=[out, out],
        scratch_shapes=[pltpu.VMEM((2, RET_HEADS, RET_DK, RET_DK), F32)],
        compiler_params=_cparams(("arbitrary", "arbitrary")),
        name="retention_scan",
    )(p, p, p, cos_tab, sin_tab, p, p, p, cos_tab, sin_tab, *tables)


def _rope_tables(t_lat, t_ctx):
    m = RET_DK // 4
    inv = ROPE_BASE ** (-jnp.arange(m, dtype=F32) / m)
    rows = jnp.repeat(jnp.arange(t_lat // GRID_W, dtype=jnp.int32), GRID_W).astype(F32)
    cols = jnp.tile(jnp.arange(GRID_W, dtype=jnp.int32), t_lat // GRID_W).astype(F32)

    def half_tables(pos):
        ang = pos[:, None] * inv
        c, s = jnp.cos(ang), jnp.sin(ang)
        return jnp.concatenate([c, c], -1), jnp.concatenate([-s, s], -1)

    cr, sr = half_tables(rows)
    cc, sc = half_tables(cols)
    cos_h = jnp.concatenate([cr, cc], -1)
    sin_h = jnp.concatenate([sr, sc], -1)
    cos = jnp.tile(cos_h, (1, RET_HEADS))
    sin = jnp.tile(sin_h, (1, RET_HEADS))
    cos = jnp.concatenate([jnp.ones((t_ctx, RET_WIDTH), F32), cos], 0)
    sin = jnp.concatenate([jnp.zeros((t_ctx, RET_WIDTH), F32), sin], 0)
    return cos, sin


def _retention_decay_tables(log_gamma):
    c = CHUNK
    i = jnp.arange(c, dtype=F32)
    diff = i[:, None] - i[None, :]
    lg = log_gamma[:, :, :, None, None]
    d_f = jnp.where(diff >= 0, jnp.exp(lg[:, 0] * diff), 0.0)
    d_b = jnp.where(diff <= 0, jnp.exp(lg[:, 1] * (-diff)), 0.0)
    dmat = jnp.stack([d_f, d_b], 1)
    lane_lg = jnp.repeat(log_gamma, RET_DK, axis=2)[:, :, None, :]
    col = i[None, :, None]
    rq = jnp.stack([jnp.exp(lane_lg[:, 0] * (col + 1.0)), jnp.exp(lane_lg[:, 1] * (c - col))], 1)
    rk = jnp.stack([jnp.exp(lane_lg[:, 0] * (c - 1.0 - col)), jnp.exp(lane_lg[:, 1] * col)], 1)
    cdec = jnp.exp(lane_lg[:, :, 0, :] * c)
    return dmat, rq, rk, cdec


def _s5_kernel(u_ref, w1_ref, w2_ref, ac_ref, y_ref, a_scr, hf_scr, hb_scr, *, n_ctx, n_all, bsz):
    width = S5_FOLD
    st2 = 2 * S5_STATE
    for b in range(bsz):
        a_scr[b] = _dot(u_ref[0, :, b * width : (b + 1) * width].astype(BF16), w1_ref[0, 0])
    ac = ac_ref[0, 0]

    state = [jnp.zeros((1, st2), F32)] * (2 * bsz)
    for n in range(n_all):
        nb = n_ctx - 1 - n if n < n_ctx else n_all + n_ctx - 1 - n
        for b in range(bsz):
            hf, hb = state[2 * b], state[2 * b + 1]
            hf_scr[b, n : n + 1, :] = hf
            hb_scr[b, nb : nb + 1, :] = hb
            inj_f = a_scr[b, n : n + 1, width : width + st2]
            inj_b = a_scr[b, nb : nb + 1, width + st2 : width + 2 * st2]
            state[2 * b] = hf * ac[0:1, :] + pltpu.roll(hf, S5_STATE, 1) * ac[1:2, :] + inj_f
            state[2 * b + 1] = hb * ac[2:3, :] + pltpu.roll(hb, S5_STATE, 1) * ac[3:4, :] + inj_b
    for b in range(bsz):
        y_ref[0, :, b * width : (b + 1) * width] = (
            a_scr[b, :, :width]
            + _dot(hf_scr[b].astype(BF16), w2_ref[0, 0, 0])
            + _dot(hb_scr[b].astype(BF16), w2_ref[0, 0, 1]))


def _s5_conv(ug, w1, w2, ac, l, bsz, n_ctx):
    g, n_all, _ = ug.shape
    width = S5_FOLD
    st2 = 2 * S5_STATE
    return pl.pallas_call(
        functools.partial(_s5_kernel, n_ctx=n_ctx, n_all=n_all, bsz=bsz),
        grid=(g,),
        in_specs=[
            pl.BlockSpec((1, n_all, bsz * width), lambda i: (i, 0, 0)),
            pl.BlockSpec((1, 1, width, width + 2 * st2), lambda i: (l, i, 0, 0)),
            pl.BlockSpec((1, 1, 2, st2, width), lambda i: (l, i, 0, 0, 0)),
            pl.BlockSpec((1, 1, SUBLANES, st2), lambda i: (l, i, 0, 0)),
        ],
        out_specs=pl.BlockSpec((1, n_all, bsz * width), lambda i: (i, 0, 0)),
        out_shape=jax.ShapeDtypeStruct((g, n_all, bsz * width), F32),
        scratch_shapes=[pltpu.VMEM((bsz, n_all, width + 2 * st2), F32),
                        pltpu.VMEM((bsz, n_all, st2), F32), pltpu.VMEM((bsz, n_all, st2), F32)],
        compiler_params=_cparams(("arbitrary",)),
        name="s5_conv",
    )(ug, w1, w2, ac)


def _s5_weights(lam_re, lam_im, log_dt, b_re, b_im, c_re, c_im):
    cs, ch, p, g = S5_CHUNK, S5_GROUP_CH, S5_STATE, S5_GROUPS
    nl = lam_re.shape[0]
    lam_re = jnp.minimum(lam_re.astype(F32), -1e-4)
    lam_im = lam_im.astype(F32)
    dt = jnp.exp(log_dt.astype(F32))[..., None]
    mag = jnp.exp(dt * lam_re)
    abar_re, abar_im = mag * jnp.cos(dt * lam_im), mag * jnp.sin(dt * lam_im)
    den = jnp.square(lam_re) + jnp.square(lam_im)
    nr, ni = abar_re - 1.0, abar_im
    coef_re = ((nr * lam_re + ni * lam_im) / den)[..., None]
    coef_im = ((ni * lam_re - nr * lam_im) / den)[..., None]
    b_re, b_im = b_re.astype(F32), b_im.astype(F32)
    bb_re = coef_re * b_re - coef_im * b_im
    bb_im = coef_re * b_im + coef_im * b_re
    c_re, c_im = c_re.astype(F32), c_im.astype(F32)
    tau = jnp.arange(cs + 1, dtype=F32)[:, None, None, None, None]
    pw_mag = jnp.exp(tau * (dt * lam_re)[None])
    pw_re = pw_mag * jnp.cos(tau * (dt * lam_im)[None])
    pw_im = pw_mag * jnp.sin(tau * (dt * lam_im)[None])
    ab_re = pw_re[..., None] * bb_re[None] - pw_im[..., None] * bb_im[None]
    ab_im = pw_re[..., None] * bb_im[None] + pw_im[..., None] * bb_re[None]
    hi = lax.Precision.HIGHEST
    kern = (jnp.einsum('ldgcp,tldgpe->ldgetc', c_re, ab_re[:cs], precision=hi)
            - jnp.einsum('ldgcp,tldgpe->ldgetc', c_im, ab_im[:cs], precision=hi))
    base_f = kern[:, 0].reshape(nl, g, ch, cs * ch)
    base_b = kern[:, 1][:, :, :, ::-1].reshape(nl, g, ch, cs * ch)
    rows_f, rows_b = [], []
    for s in range(cs):
        sh = s * ch
        rows_f.append(jnp.pad(base_f[..., : cs * ch - sh], ((0, 0), (0, 0), (0, 0), (sh, 0))))
        sh = (cs - 1 - s) * ch
        rows_b.append(jnp.pad(base_b[..., sh:], ((0, 0), (0, 0), (0, 0), (0, sh))))
    toep = (jnp.stack(rows_f, 2) + jnp.stack(rows_b, 2)).reshape(nl, g, cs * ch, cs * ch)

    def inject(d, order):
        re = jnp.transpose(ab_re[:, :, d][order], (1, 2, 0, 4, 3))
        im = jnp.transpose(ab_im[:, :, d][order], (1, 2, 0, 4, 3))
        return jnp.concatenate([re, im], -1).reshape(nl, g, cs * ch, 2 * p)

    steps = jnp.arange(cs)
    w1 = jnp.concatenate([toep, inject(0, cs - 1 - steps), inject(1, steps)], axis=-1)

    def readout(d, expo):
        pr, pi = pw_re[:, :, d][expo], pw_im[:, :, d][expo]
        cr = c_re[None, :, d] * pr[:, :, :, None, :] - c_im[None, :, d] * pi[:, :, :, None, :]
        ci = c_re[None, :, d] * pi[:, :, :, None, :] + c_im[None, :, d] * pr[:, :, :, None, :]
        w = jnp.concatenate([cr, -ci], axis=-1)
        return jnp.transpose(w, (1, 2, 4, 0, 3)).reshape(nl, g, 2 * p, cs * ch)

    w2 = jnp.stack([readout(0, steps + 1), readout(1, cs - steps)], axis=2)
    ac_rows = []
    for d in range(2):
        ar, ai = pw_re[cs, :, d], pw_im[cs, :, d]
        ac_rows += [jnp.concatenate([ar, ar], -1), jnp.concatenate([-ai, ai], -1)]
    ac = jnp.stack(ac_rows + [jnp.zeros_like(ac_rows[0])] * (SUBLANES - 4), axis=2)
    return w1.astype(BF16), w2.astype(BF16), ac


def _layer_norm(x, g, b):
    mu = jnp.mean(x, -1, keepdims=True)
    xc = x - mu
    var = jnp.mean(xc * xc, -1, keepdims=True)
    return xc * lax.rsqrt(var + LN_EPS) * g + b


def _route(logits):
    col = lambda i: logits[:, i : i + 1]
    gl = [col(i) for i in range(N_GROUPS)]
    gmax = functools.reduce(jnp.maximum, gl)
    g_idx = jnp.full_like(gmax, N_GROUPS - 1).astype(jnp.int32)
    for i in reversed(range(N_GROUPS - 1)):
        g_idx = jnp.where(gl[i] == gmax, i, g_idx)
    g_p = 1.0 / functools.reduce(lambda a, b: a + b, [jnp.exp(x - gmax) for x in gl])
    el = []
    for e in range(EXPERTS_PER_GROUP):
        v = col(N_GROUPS + (N_GROUPS - 1) * EXPERTS_PER_GROUP + e)
        for g in reversed(range(N_GROUPS - 1)):
            v = jnp.where(g_idx == g, col(N_GROUPS + g * EXPERTS_PER_GROUP + e), v)
        el.append(v)
    m1 = functools.reduce(jnp.maximum, el)
    i1 = jnp.full_like(g_idx, EXPERTS_PER_GROUP - 1)
    for e in reversed(range(EXPERTS_PER_GROUP - 1)):
        i1 = jnp.where(el[e] == m1, e, i1)
    rest = [jnp.where(i1 == e, -jnp.inf, el[e]) for e in range(EXPERTS_PER_GROUP)]
    m2 = functools.reduce(jnp.maximum, rest)
    i2 = jnp.full_like(g_idx, EXPERTS_PER_GROUP - 1)
    for e in reversed(range(EXPERTS_PER_GROUP - 1)):
        i2 = jnp.where((rest[e] == m2) & (i1 != e), e, i2)
    t = jnp.exp(m2 - m1)
    w1 = g_p / (1.0 + t)
    w2 = g_p * t / (1.0 + t)
    lo = jnp.minimum(i1, i2)
    hi = jnp.maximum(i1, i2)
    pair = jnp.where(lo == 0, hi - 1, jnp.where(lo == 1, hi + 1, N_PAIRS - 1))
    bucket = g_idx * N_PAIRS + pair
    w_lo = jnp.where(i1 < i2, w1, w2)
    w_hi = jnp.where(i1 < i2, w2, w1)
    return bucket, w_lo, w_hi


def _merge_kernel(gate_ref, rg_ref, u_ref, hf_ref, hb_ref, rf_ref, rb_ref, yg_ref, x_ref, m_ref,
                  hgn_ref, gng_ref, gnb_ref, d_ref, gw_ref, gb_ref, wo_ref, lng_ref, lnb_ref,
                  wr_ref, br_ref, a128_ref, a64_ref, tri_ref, x1_ref, h2_ref, route_ref, cnt_ref, wt_scr, *, alpha):
    @pl.when((pl.program_id(0) == 0) & (pl.program_id(1) == 0))
    def _():
        cnt_ref[...] = jnp.zeros_like(cnt_ref)

    o_hg = hf_ref[0] + hb_ref[0]
    ms = _dot2(o_hg * o_hg, a128_ref[...])
    hg = o_hg * lax.rsqrt(ms + LN_EPS) * hgn_ref[0] * _silu(gate_ref[0])
    o_rt = rf_ref[0] + rb_ref[0]
    mu = _dot2(o_rt, a64_ref[...])
    xc = o_rt - mu
    var = _dot2(xc * xc, a64_ref[...])
    rt = (xc * lax.rsqrt(var + LN_EPS) * gng_ref[0] + gnb_ref[0]) * _silu(rg_ref[0])
    rows_per_chunk = S5_CHUNK // SUBLANES
    n_chunks = yg_ref.shape[1]
    y5 = []
    for half in range(S5_WIDTH // LANES):
        for g_lo in range(SUBLANES):
            for s_hi in range(rows_per_chunk):
                wt_scr[pl.ds(s_hi * SUBLANES + g_lo, n_chunks, stride=S5_CHUNK), :] = (
                    yg_ref[half * SUBLANES + g_lo, :, s_hi * LANES : (s_hi + 1) * LANES])
        y5.append(_swap_sublane_lanegroup(wt_scr[...]))
    y5 = jnp.concatenate(y5, axis=-1)
    s5 = jax.nn.gelu(y5 + d_ref[0] * u_ref[0])
    s5 = s5 * jax.nn.sigmoid(_dot(s5.astype(BF16), gw_ref[0]) + gb_ref[0])
    cat = jnp.concatenate([hg, rt, s5], axis=-1).astype(BF16)
    y = _dot(cat, wo_ref[0])
    m = m_ref[0, 0, 0]
    x1 = _layer_norm(alpha * x_ref[0] + m[2:3, :] * y, lng_ref[0], lnb_ref[0])
    x1_ref[0] = x1
    h2 = x1 * (1.0 + m[4:5, :]) + m[3:4, :]
    logits = _dot3(h2, wr_ref[0]) + br_ref[0]
    bucket, w_lo, w_hi = _route(logits)
    lane = lax.broadcasted_iota(jnp.int32, logits.shape, 1)
    h2_ref[0, :, : h2.shape[1]] = h2
    h2_ref[0, :, h2.shape[1] :] = jnp.where(lane == 0, w_lo, jnp.where(lane == 1, w_hi, 0.0))
    onehot = lane == bucket
    before = _dot(tri_ref[...], onehot.astype(BF16)) + cnt_ref[...]
    rank = jnp.sum(jnp.where(onehot, before, 0.0), axis=-1, keepdims=True)
    route_ref[0] = jnp.where(lane == 0, bucket.astype(F32), jnp.where(lane == 1, rank, 0.0))
    cnt_ref[...] += jnp.sum(onehot.astype(F32), axis=0, keepdims=True)


def _merge(p, o_hf, o_hb, o_rf, o_rb, yg, x, modtab, layer_prm, const_prm, l, tm, nct, alpha):
    b, s, d = x.shape
    cpt = tm // S5_CHUNK

    def tok(width, col):
        return pl.BlockSpec((1, tm, width), lambda i, j: (i, j, col))

    def whole(a):
        return pl.BlockSpec(a.shape, lambda i, j: (0,) * a.ndim)

    rbase = 5 * HG_WIDTH // RET_WIDTH
    in_specs = [tok(HG_WIDTH, 2), tok(RET_WIDTH, rbase + 3), tok(S5_WIDTH, rbase + 4),
                tok(HG_WIDTH, 0), tok(HG_WIDTH, 0), tok(RET_WIDTH, 0), tok(RET_WIDTH, 0),
                pl.BlockSpec((S5_GROUPS, cpt, S5_FOLD), lambda i, j: (0, j, i)),
                tok(d, 0), _mod_spec(l, nct)]
    in_specs += [_layer_spec(a.shape[1:], l) for a in layer_prm]
    in_specs += [whole(a) for a in const_prm]
    return pl.pallas_call(
        functools.partial(_merge_kernel, alpha=alpha),
        grid=(b, s // tm),
        in_specs=in_specs,
        out_specs=[tok(d, 0), tok(d + LANES, 0), tok(LANES, 0), pl.BlockSpec((1, LANES), lambda i, j: (0, 0))],
        out_shape=[jax.ShapeDtypeStruct((b, s, d), F32), jax.ShapeDtypeStruct((b, s, d + LANES), F32),
                   jax.ShapeDtypeStruct((b, s, LANES), F32), jax.ShapeDtypeStruct((1, LANES), F32)],
        scratch_shapes=[pltpu.VMEM((tm, LANES), F32)],
        compiler_params=_cparams(("arbitrary", "arbitrary")),
        name="merge_ln1_router",
    )(p, p, p, o_hf, o_hb, o_rf, o_rb, yg, x, modtab, *layer_prm, *const_prm)


def _routing_tables(route, counts, te, n_tiles):
    bucket = route[..., 0].astype(jnp.int32).reshape(-1)
    rank = route[..., 1].astype(jnp.int32).reshape(-1)
    cnt = counts[0, :N_BUCKETS].astype(jnp.int32)
    padded = (cnt + te - 1) // te * te
    ends = jnp.cumsum(padded)
    pos = (ends - padded)[bucket] + rank
    n_used = ends[-1] // te
    tile = jnp.arange(n_tiles, dtype=jnp.int32)
    tb = jnp.minimum(jnp.searchsorted(ends, tile * te, side="right"), N_BUCKETS - 1).astype(jnp.int32)
    tb = jnp.where(tile < n_used, tb, tb[jnp.maximum(n_used - 1, 0)])
    group, pair = tb // N_PAIRS, tb % N_PAIRS
    ea = group * EXPERTS_PER_GROUP + jnp.asarray(PAIR_LO, jnp.int32)[pair]
    eb = group * EXPERTS_PER_GROUP + jnp.asarray(PAIR_HI, jnp.int32)[pair]
    return pos, ea, eb, n_used.reshape(1).astype(jnp.int32)


def _dispatch_kernel(pos_ref, src_ref, init_ref, out_ref, sem, *, tm, tiles_per_row):
    del init_ref
    base = (pl.program_id(0) * tiles_per_row + pl.program_id(1)) * tm

    def row_copy(r):
        return pltpu.make_async_copy(src_ref.at[0, pl.ds(r, 1)], out_ref.at[pl.ds(pos_ref[base + r], 1)], sem)

    def start(r, carry):
        row_copy(r).start()
        return carry

    def wait(r, carry):
        row_copy(r).wait()
        return carry

    lax.fori_loop(0, tm, start, 0, unroll=DMA_UNROLL)
    lax.fori_loop(0, tm, wait, 0, unroll=DMA_UNROLL)


def _dispatch(pos, rows, n_sorted, tm):
    b, s, w = rows.shape
    return pl.pallas_call(
        functools.partial(_dispatch_kernel, tm=tm, tiles_per_row=s // tm),
        grid_spec=pltpu.PrefetchScalarGridSpec(
            num_scalar_prefetch=1,
            grid=(b, s // tm),
            in_specs=[pl.BlockSpec((1, tm, w), lambda i, j, pos_ref: (i, j, 0)),
                      pl.BlockSpec(memory_space=pl.ANY)],
            out_specs=pl.BlockSpec(memory_space=pl.ANY),
            scratch_shapes=[pltpu.SemaphoreType.DMA(())],
        ),
        out_shape=jax.ShapeDtypeStruct((n_sorted, w), F32),
        input_output_aliases={2: 0},
        compiler_params=_cparams(("arbitrary", "arbitrary")),
        name="moe_dispatch",
    )(pos, rows, jnp.zeros((n_sorted, w), F32))


def _expert_kernel(ea_ref, eb_ref, nused_ref, hs_ref, wga_ref, wua_ref, wda_ref, wgb_ref, wub_ref, wdb_ref,
                   o_ref, wg_scr, wu_scr, wd_scr):
    t = pl.program_id(0)
    prev = jnp.maximum(t - 1, 0)
    changed = (t == 0) | (ea_ref[t] != ea_ref[prev]) | (eb_ref[t] != eb_ref[prev])

    @pl.when(changed)
    def _():
        wg_scr[0] = wga_ref[0, 0].astype(BF16)
        wu_scr[0] = wua_ref[0, 0].astype(BF16)
        wd_scr[0] = wda_ref[0, 0].astype(BF16)
        wg_scr[1] = wgb_ref[0, 0].astype(BF16)
        wu_scr[1] = wub_ref[0, 0].astype(BF16)
        wd_scr[1] = wdb_ref[0, 0].astype(BF16)

    @pl.when(t < nused_ref[0])
    def _():
        d = o_ref.shape[1]
        h = hs_ref[:, :d].astype(BF16)
        y = jnp.zeros(o_ref.shape, F32)
        for e in range(2):
            act = (_silu(_dot(h, wg_scr[e])) * _dot(h, wu_scr[e])).astype(BF16)
            y = y + hs_ref[:, d + e : d + e + 1] * _dot(act, wd_scr[e])
        o_ref[...] = y

    @pl.when(t >= nused_ref[0])
    def _():
        o_ref[...] = jnp.zeros_like(o_ref)


def _experts(ea, eb, n_used, hs, wg, wu, wd, l, te):
    n_sorted, w = hs.shape
    _, _, d, eh = wg.shape

    def wspec(shape, which):
        return pl.BlockSpec((1, 1) + shape, lambda t, ea_ref, eb_ref, n_ref: (l, (ea_ref, eb_ref)[which][t], 0, 0))

    return pl.pallas_call(
        _expert_kernel,
        grid_spec=pltpu.PrefetchScalarGridSpec(
            num_scalar_prefetch=3,
            grid=(n_sorted // te,),
            in_specs=[pl.BlockSpec((te, w), lambda t, *_: (t, 0)),
                      wspec((d, eh), 0), wspec((d, eh), 0), wspec((eh, d), 0),
                      wspec((d, eh), 1), wspec((d, eh), 1), wspec((eh, d), 1)],
            out_specs=pl.BlockSpec((te, d), lambda t, *_: (t, 0)),
            scratch_shapes=[pltpu.VMEM((2, d, eh), BF16), pltpu.VMEM((2, d, eh), BF16), pltpu.VMEM((2, eh, d), BF16)],
        ),
        out_shape=jax.ShapeDtypeStruct((n_sorted, d), F32),
        compiler_params=_cparams(("arbitrary",)),
        name="moe_experts",
    )(ea, eb, n_used, hs, wg, wu, wd, wg, wu, wd)


def _combine_kernel(pos_ref, ys_ref, x1_ref, m_ref, lng_ref, lnb_ref, o_ref, buf, sem, *, tm, tiles_per_row, alpha):
    base = (pl.program_id(0) * tiles_per_row + pl.program_id(1)) * tm

    def row_copy(r):
        return pltpu.make_async_copy(ys_ref.at[pl.ds(pos_ref[base + r], 1)], buf.at[pl.ds(r, 1)], sem)

    def start(r, carry):
        row_copy(r).start()
        return carry

    def wait(r, carry):
        row_copy(r).wait()
        return carry

    lax.fori_loop(0, tm, start, 0, unroll=DMA_UNROLL)
    lax.fori_loop(0, tm, wait, 0, unroll=DMA_UNROLL)
    m = m_ref[0, 0, 0]
    o_ref[0] = _layer_norm(alpha * x1_ref[0] + m[5:6, :] * buf[...], lng_ref[0], lnb_ref[0])


def _combine(pos, ys, x1, modtab, ln_g, ln_b, l, tm, nct, alpha):
    b, s, d = x1.shape
    return pl.pallas_call(
        functools.partial(_combine_kernel, tm=tm, tiles_per_row=s // tm, alpha=alpha),
        grid_spec=pltpu.PrefetchScalarGridSpec(
            num_scalar_prefetch=1,
            grid=(b, s // tm),
            in_specs=[pl.BlockSpec(memory_space=pl.ANY),
                      pl.BlockSpec((1, tm, d), lambda i, j, pos_ref: (i, j, 0)),
                      _mod_spec(l, nct), _layer_spec((1, d), l), _layer_spec((1, d), l)],
            out_specs=pl.BlockSpec((1, tm, d), lambda i, j, pos_ref: (i, j, 0)),
            scratch_shapes=[pltpu.VMEM((tm, d), F32), pltpu.SemaphoreType.DMA(())],
        ),
        out_shape=jax.ShapeDtypeStruct((b, s, d), F32),
        compiler_params=_cparams(("arbitrary", "arbitrary")),
        name="moe_combine_ln2",
    )(pos, ys, x1, modtab, ln_g, ln_b)


def _block_avg(width, group):
    idx = np.arange(width) // group
    return jnp.asarray((idx[:, None] == idx[None, :]).astype(np.float32) / group, dtype=BF16)


def kernel(x, c, ctx, c_ctx, w_mod, b_mod, w_in, hg_lb_raw, hg_norm_g, ret_decay_raw, ret_gn_g, ret_gn_b, s5_lam_re, s5_lam_im, s5_log_dt, s5_b_re, s5_b_im, s5_c_re, s5_c_im, s5_d, s5_glu_w, s5_glu_b, w_out, ln1_g, ln1_b, ln2_g, ln2_b, rg_w, rg_b, re_w, re_b, exp_w_gate, exp_w_up, exp_w_down):
    bsz, t_lat, d = x.shape
    t_ctx = ctx.shape[1]
    depth = w_mod.shape[0]
    alpha = (2.0 * depth) ** 0.25
    tm = TOKEN_TILE
    assert d == D_MODEL and t_lat % GRID_W == 0 and bsz < SUBLANES
    assert t_ctx % tm == 0 and t_lat % tm == 0, "context and latent lengths must be multiples of the token tile"
    nct = t_ctx // tm
    s = t_ctx + t_lat

    cvec = jnp.concatenate([c, c_ctx[None, :], jnp.zeros((SUBLANES - bsz - 1, d), F32)], 0)
    mod_all = _modulation(cvec, w_mod, b_mod)
    lat = mod_all[:, :bsz].reshape(depth, bsz, 6, d)
    cm = jnp.broadcast_to(mod_all[:, bsz].reshape(depth, 1, 6, d), (depth, bsz, 6, d))
    modtab = jnp.stack([cm, lat], axis=2)

    hg_lb = jnp.cumsum(jax.nn.softmax(hg_lb_raw.astype(F32), axis=0), axis=0)
    hg_lb = hg_lb - hg_lb[:1]
    ret_tables = _retention_decay_tables(jax.nn.log_sigmoid(ret_decay_raw.astype(F32)))
    cos_tab, sin_tab = _rope_tables(t_lat, t_ctx)
    s5_w1, s5_w2, s5_ac = _s5_weights(s5_lam_re, s5_lam_im, s5_log_dt, s5_b_re, s5_b_im, s5_c_re, s5_c_im)
    pad_r = LANES - N_GROUPS - N_EXPERTS
    wr = jnp.concatenate([rg_w, re_w.reshape(depth, d, N_EXPERTS), jnp.zeros((depth, d, pad_r), F32)], axis=2)
    br = jnp.concatenate([rg_b, re_b.reshape(depth, N_EXPERTS), jnp.zeros((depth, pad_r), F32)], axis=1)[:, None, :]
    row = lambda a: a[:, None, :]
    layer_prm = [row(jnp.tile(hg_norm_g, (1, HG_HEADS))), row(ret_gn_g), row(ret_gn_b), row(s5_d),
                 s5_glu_w.astype(BF16), row(s5_glu_b), w_out.astype(BF16), row(ln1_g), row(ln1_b), wr, br]
    tri = jnp.asarray(np.tril(np.ones((tm, tm), np.float32), -1), dtype=BF16)
    const_prm = [_block_avg(HG_WIDTH, HG_DK), _block_avg(RET_WIDTH, RET_DK), tri]
    w_in_bf16 = w_in.astype(BF16)
    ln2_g, ln2_b = row(ln2_g), row(ln2_b)
    te = EXPERT_TILE
    n_tiles = -(-(bsz * s + N_BUCKETS * (te - 1)) // te)

    xs = jnp.concatenate([ctx, x], axis=1)
    for l in range(depth):
        p, ug = _inproj(xs, modtab, w_in_bf16, l, tm, nct)
        o_hf, o_hb = _hgrn(p, hg_lb, l, t_ctx // CHUNK)
        o_rf, o_rb = _retention(p, cos_tab, sin_tab, ret_tables, l, t_ctx // CHUNK)
        yg = _s5_conv(ug, s5_w1, s5_w2, s5_ac, l, bsz, t_ctx // S5_CHUNK)
        x1, rows, route, counts = _merge(p, o_hf, o_hb, o_rf, o_rb, yg, xs, modtab, layer_prm, const_prm,
                                         l, tm, nct, alpha)
        pos, ea, eb, n_used = _routing_tables(route, counts, te, n_tiles)
        hs = _dispatch(pos, rows, n_tiles * te, tm)
        ys = _experts(ea, eb, n_used, hs, exp_w_gate, exp_w_up, exp_w_down, l, te)
        xs = _combine(pos, ys, x1, modtab, ln2_g, ln2_b, l, tm, nct, alpha)
    return xs[:, t_ctx:, :]
```

```python
import functools

import numpy as np
import jax
import jax.numpy as jnp
from jax import lax
from jax.experimental import pallas as pl
from jax.experimental.pallas import tpu as pltpu

F32 = jnp.float32
BF16 = jnp.bfloat16

D_MODEL = 1024
HG_WIDTH = 512
HG_HEADS = 4
HG_DK = HG_WIDTH // HG_HEADS
RET_WIDTH = 256
RET_HEADS = 4
RET_DK = RET_WIDTH // RET_HEADS
S5_WIDTH = 256
S5_GROUP_CH = 16
S5_GROUPS = S5_WIDTH // S5_GROUP_CH
S5_STATE = 64
IN_COLS = 5 * HG_WIDTH + 4 * RET_WIDTH + S5_WIDTH
CHUNK = 64
RET_CHUNK = 256
N_GROUPS = 4
EXPERTS_PER_GROUP = 4
N_EXPERTS = N_GROUPS * EXPERTS_PER_GROUP
N_PAIRS = EXPERTS_PER_GROUP * (EXPERTS_PER_GROUP - 1) // 2
N_BUCKETS = N_GROUPS * N_PAIRS
PAIR_LO = (0, 0, 0, 1, 1, 2)
PAIR_HI = (1, 2, 3, 2, 3, 3)
EXPERT_HIDDEN = D_MODEL // 2
LN_EPS = 1e-5
ROPE_BASE = 10000.0
GRID_W = 64

LANES = 128
SUBLANES = 8
TOKEN_TILE = 256
S5_CHUNK = 32
S5_FOLD = S5_CHUNK * S5_GROUP_CH
EXPERT_TILE = 256
DMA_UNROLL = 8
VMEM_LIMIT = 56 * 1024 * 1024


def _cparams(sem):
    return pltpu.CompilerParams(dimension_semantics=sem, vmem_limit_bytes=VMEM_LIMIT)


def _split_bf16(x):
    hi = x.astype(BF16)
    lo = (x - hi.astype(F32)).astype(BF16)
    return hi, lo


def _dot(a, b):
    return jnp.dot(a, b, preferred_element_type=F32)


def _dot3(a, b):
    ah, al = _split_bf16(a)
    bh, bl = _split_bf16(b)
    return _dot(ah, bh) + _dot(ah, bl) + _dot(al, bh)


def _dot2(a, b_bf16):
    ah, al = _split_bf16(a)
    return _dot(ah, b_bf16) + _dot(al, b_bf16)


def _dot_nt(a, b):
    return lax.dot_general(a, b, (((1,), (1,)), ((), ())), preferred_element_type=F32)


def _dot_tn(a, b):
    return lax.dot_general(a, b, (((0,), (0,)), ((), ())), preferred_element_type=F32)


def _silu(x):
    return x * jax.nn.sigmoid(x)


def _layer_spec(shape, l):
    zeros = (0,) * len(shape)
    return pl.BlockSpec((1,) + tuple(shape), lambda *_: (l,) + zeros)


def _mod_spec(l, nct):
    return pl.BlockSpec((1, 1, 1, 6, D_MODEL), lambda i, j, *_: (l, i, jnp.minimum(j // nct, 1), 0, 0))


def _swap_sublane_lanegroup(v):
    n = v.shape[0]
    r = lax.broadcasted_iota(jnp.int32, v.shape, 0)
    l = lax.broadcasted_iota(jnp.int32, v.shape, 1)
    for k in range(3):
        rb = (r >> k) & 1
        gb = (l >> (4 + k)) & 1
        sh = S5_GROUP_CH << k
        st = 1 << k
        a = pltpu.roll(pltpu.roll(v, LANES - sh, 1), st, 0)
        b = pltpu.roll(pltpu.roll(v, sh, 1), n - st, 0)
        v = jnp.where(rb == gb, v, jnp.where(rb == 1, a, b))
    return v


def _mod_kernel(c_ref, w_ref, b_ref, o_ref):
    sc = _silu(c_ref[...])
    o_ref[0] = _dot3(sc, w_ref[0]) + b_ref[0]


def _modulation(cvec, w_mod, b_mod):
    depth, d, n = w_mod.shape
    rows = cvec.shape[0]
    tn = 1536
    return pl.pallas_call(
        _mod_kernel,
        grid=(depth, n // tn),
        in_specs=[
            pl.BlockSpec((rows, d), lambda l, j: (0, 0)),
            pl.BlockSpec((1, d, tn), lambda l, j: (l, 0, j)),
            pl.BlockSpec((1, 1, tn), lambda l, j: (l, 0, j)),
        ],
        out_specs=pl.BlockSpec((1, rows, tn), lambda l, j: (l, 0, j)),
        out_shape=jax.ShapeDtypeStruct((depth, rows, n), F32),
        compiler_params=_cparams(("arbitrary", "arbitrary")),
        name="modulation",
    )(cvec, w_mod, b_mod.reshape(depth, 1, n))


def _inproj_kernel(x_ref, m_ref, w_ref, o_ref, ug_ref, wt_scr):
    m = m_ref[0, 0, 0]
    h = x_ref[0] * (1.0 + m[1:2, :]) + m[0:1, :]
    p = _dot(h.astype(BF16), w_ref[0])
    o_ref[0] = p
    rows_per_chunk = S5_CHUNK // SUBLANES
    n_chunks = p.shape[0] // S5_CHUNK
    for half in range(S5_WIDTH // LANES):
        lo = IN_COLS - S5_WIDTH + half * LANES
        wt_scr[...] = _swap_sublane_lanegroup(p[:, lo : lo + LANES])
        for g_lo in range(SUBLANES):
            for s_hi in range(rows_per_chunk):
                piece = wt_scr[pl.ds(s_hi * SUBLANES + g_lo, n_chunks, stride=S5_CHUNK), :]
                ug_ref[half * SUBLANES + g_lo, :, s_hi * LANES : (s_hi + 1) * LANES] = piece


def _inproj(x, modtab, w_in_bf16, l, tm, nct):
    b, s, d = x.shape
    n = w_in_bf16.shape[-1]
    cpt = tm // S5_CHUNK
    return pl.pallas_call(
        _inproj_kernel,
        grid=(b, s // tm),
        in_specs=[
            pl.BlockSpec((1, tm, d), lambda i, j: (i, j, 0)),
            _mod_spec(l, nct),
            _layer_spec((d, n), l),
        ],
        out_specs=[pl.BlockSpec((1, tm, n), lambda i, j: (i, j, 0)),
                   pl.BlockSpec((S5_GROUPS, cpt, S5_FOLD), lambda i, j: (0, j, i))],
        out_shape=[jax.ShapeDtypeStruct((b, s, n), F32),
                   jax.ShapeDtypeStruct((S5_GROUPS, s // S5_CHUNK, b * S5_FOLD), F32)],
        scratch_shapes=[pltpu.VMEM((tm, LANES), F32)],
        compiler_params=_cparams(("arbitrary", "arbitrary")),
        name="inproj",
    )(x, modtab, w_in_bf16)


def _block_gate_products(f, reverse):
    c = f.shape[0]
    row = lax.broadcasted_iota(jnp.int32, (c, 1), 0)
    a, z, b = f, f, jnp.ones_like(f)
    out = []
    s = 1
    while s < c:
        out.append((a, b))
        up = pltpu.roll(z, s, 0)
        dn = pltpu.roll(z, c - s, 0)
        odd = (row & s) != 0
        if reverse:
            a = a * jnp.where(odd, 1.0, dn)
            b = b * jnp.where(odd, up, 1.0)
        else:
            a = a * jnp.where(odd, up, 1.0)
            b = b * jnp.where(odd, 1.0, dn)
        z = z * jnp.where(odd, up, dn)
        s *= 2
    out.append((a, b))
    return out, z


def _hgrn_kernel(qf_ref, vf_ref, zf_ref, qb_ref, vb_ref, zb_ref, lb_ref, of_ref, ob_ref, st_ref):
    c = CHUNK

    @pl.when(pl.program_id(1) == 0)
    def _():
        st_ref[...] = jnp.zeros_like(st_ref)

    ri = lax.broadcasted_iota(jnp.int32, (c, c), 0)
    ci = lax.broadcasted_iota(jnp.int32, (c, c), 1)
    for d, (q_ref, v_ref, z_ref, o_ref) in enumerate(
        ((qf_ref, vf_ref, zf_ref, of_ref), (qb_ref, vb_ref, zb_ref, ob_ref))
    ):
        reverse = d == 1
        q = _silu(q_ref[0])
        v = v_ref[0].astype(BF16)
        lb = lb_ref[0, d : d + 1, :]
        f = lb + (1.0 - lb) * jax.nn.sigmoid(z_ref[0])
        k = 1.0 - f
        levels, tot = _block_gate_products(f, reverse)
        qs = [(q * a).astype(BF16) for a, _ in levels]
        ks = [(k * bb).astype(BF16) for _, bb in levels]
        q0 = q.astype(BF16)
        k0 = k.astype(BF16)
        tail = tot[0:1, :]
        causal = (ri < ci) if reverse else (ri > ci)
        masks = [(((ri >> lvl) ^ (ci >> lvl)) == 1) & causal for lvl in range(len(levels) - 1)]
        for h in range(HG_HEADS):
            hs = slice(h * HG_DK, (h + 1) * HG_DK)
            scores = jnp.where(ri == ci, _dot_nt(q0[:, hs], k0[:, hs]), 0.0)
            for lvl, mask in enumerate(masks):
                scores = scores + jnp.where(mask, _dot_nt(qs[lvl][:, hs], ks[lvl][:, hs]), 0.0)
            st = st_ref[d, h]
            o = _dot(scores.astype(BF16), v[:, hs]) + _dot_nt(qs[-1][:, hs], st.astype(BF16))
            o_ref[0, :, hs] = o
            st_ref[d, h] = st * tail[:, hs] + _dot_tn(v[:, hs], ks[-1][:, hs])


def _bwd_chunk(n, nc_ctx, nc_all):
    return jnp.where(n < nc_ctx, nc_ctx - 1 - n, nc_all + nc_ctx - 1 - n)


def _hgrn(p, lb_all, l, nc_ctx):
    b, s, _ = p.shape
    nc = s // CHUNK
    w = HG_WIDTH

    def fwd(col):
        return pl.BlockSpec((1, CHUNK, w), lambda i, n: (i, n, col))

    def bwd(col):
        return pl.BlockSpec((1, CHUNK, w), lambda i, n: (i, _bwd_chunk(n, nc_ctx, nc), col))

    out = jax.ShapeDtypeStruct((b, s, w), F32)
    return pl.pallas_call(
        _hgrn_kernel,
        grid=(b, nc),
        in_specs=[fwd(0), fwd(1), fwd(3), bwd(0), bwd(1), bwd(4), _layer_spec((2, w), l)],
        out_specs=[fwd(0), bwd(0)],
        out_shape=[out, out],
        scratch_shapes=[pltpu.VMEM((2, HG_HEADS, HG_DK, HG_DK), F32)],
        compiler_params=_cparams(("arbitrary", "arbitrary")),
        name="hgrn2_scan",
    )(p, p, p, p, p, p, lb_all)


def _swap_halves(x, half):
    n = x.shape[-1]
    lane = lax.broadcasted_iota(jnp.int32, (1, n), 1)
    lower = (lane & half) == 0
    return jnp.where(lower, pltpu.roll(x, n - half, 1), pltpu.roll(x, half, 1))


def _ret_kernel(qf_ref, kf_ref, vf_ref, cf_ref, sf_ref, qb_ref, kb_ref, vb_ref, cb_ref, sb_ref,
                dmat_ref, rq_ref, rk_ref, cd_ref, of_ref, ob_ref, st_ref):
    @pl.when(pl.program_id(1) == 0)
    def _():
        st_ref[...] = jnp.zeros_like(st_ref)

    half = RET_DK // 4
    for d, (q_ref, k_ref, v_ref, c_ref, s_ref, o_ref) in enumerate(
        ((qf_ref, kf_ref, vf_ref, cf_ref, sf_ref, of_ref), (qb_ref, kb_ref, vb_ref, cb_ref, sb_ref, ob_ref))
    ):
        cos = c_ref[...]
        sin = s_ref[...]
        q = q_ref[0]
        k = k_ref[0] * (RET_DK ** -0.5)
        q = q * cos + _swap_halves(q, half) * sin
        k = k * cos + _swap_halves(k, half) * sin
        v = v_ref[0].astype(BF16)
        q0 = q.astype(BF16)
        k0 = k.astype(BF16)
        qd = (q * rq_ref[0, d]).astype(BF16)
        kd = (k * rk_ref[0, d]).astype(BF16)
        cd = cd_ref[0, d : d + 1, :]
        for h in range(RET_HEADS):
            hs = slice(h * RET_DK, (h + 1) * RET_DK)
            scores = _dot_nt(q0[:, hs], k0[:, hs]) * dmat_ref[0, d, h]
            st = st_ref[d, h]
            o_ref[0, :, hs] = _dot(scores.astype(BF16), v[:, hs]) + _dot_nt(qd[:, hs], st.astype(BF16))
            st_ref[d, h] = st * cd[:, hs] + _dot_tn(v[:, hs], kd[:, hs])


def _retention(p, cos_tab, sin_tab, tables, l, nc_ctx):
    b, s, _ = p.shape
    nc = s // RET_CHUNK
    w = RET_WIDTH
    base = 5 * HG_WIDTH // w

    def fwd(col):
        return pl.BlockSpec((1, RET_CHUNK, w), lambda i, n: (i, n, col))

    def bwd(col):
        return pl.BlockSpec((1, RET_CHUNK, w), lambda i, n: (i, _bwd_chunk(n, nc_ctx, nc), col))

    tab_f = pl.BlockSpec((RET_CHUNK, w), lambda i, n: (n, 0))
    tab_b = pl.BlockSpec((RET_CHUNK, w), lambda i, n: (_bwd_chunk(n, nc_ctx, nc), 0))
    out = jax.ShapeDtypeStruct((b, s, w), F32)
    return pl.pallas_call(
        _ret_kernel,
        grid=(b, nc),
        in_specs=[fwd(base), fwd(base + 1), fwd(base + 2), tab_f, tab_f,
                  bwd(base), bwd(base + 1), bwd(base + 2), tab_b, tab_b]
                 + [_layer_spec(t.shape[1:], l) for t in tables],
        out_specs=[fwd(0), bwd(0)],
        out_shape=[out, out],
        scratch_shapes=[pltpu.VMEM((2, RET_HEADS, RET_DK, RET_DK), F32)],
        compiler_params=_cparams(("arbitrary", "arbitrary")),
        name="retention_scan",
    )(p, p, p, cos_tab, sin_tab, p, p, p, cos_tab, sin_tab, *tables)


def _rope_tables(t_lat, t_ctx):
    m = RET_DK // 4
    inv = ROPE_BASE ** (-jnp.arange(m, dtype=F32) / m)
    rows = jnp.repeat(jnp.arange(t_lat // GRID_W, dtype=jnp.int32), GRID_W).astype(F32)
    cols = jnp.tile(jnp.arange(GRID_W, dtype=jnp.int32), t_lat // GRID_W).astype(F32)

    def half_tables(pos):
        ang = pos[:, None] * inv
        c, s = jnp.cos(ang), jnp.sin(ang)
        return jnp.concatenate([c, c], -1), jnp.concatenate([-s, s], -1)

    cr, sr = half_tables(rows)
    cc, sc = half_tables(cols)
    cos_h = jnp.concatenate([cr, cc], -1)
    sin_h = jnp.concatenate([sr, sc], -1)
    cos = jnp.tile(cos_h, (1, RET_HEADS))
    sin = jnp.tile(sin_h, (1, RET_HEADS))
    cos = jnp.concatenate([jnp.ones((t_ctx, RET_WIDTH), F32), cos], 0)
    sin = jnp.concatenate([jnp.zeros((t_ctx, RET_WIDTH), F32), sin], 0)
    return cos, sin


def _retention_decay_tables(log_gamma):
    c = RET_CHUNK
    i = jnp.arange(c, dtype=F32)
    diff = i[:, None] - i[None, :]
    lg = log_gamma[:, :, :, None, None]
    d_f = jnp.where(diff >= 0, jnp.exp(lg[:, 0] * diff), 0.0)
    d_b = jnp.where(diff <= 0, jnp.exp(lg[:, 1] * (-diff)), 0.0)
    dmat = jnp.stack([d_f, d_b], 1)
    lane_lg = jnp.repeat(log_gamma, RET_DK, axis=2)[:, :, None, :]
    col = i[None, :, None]
    rq = jnp.stack([jnp.exp(lane_lg[:, 0] * (col + 1.0)), jnp.exp(lane_lg[:, 1] * (c - col))], 1)
    rk = jnp.stack([jnp.exp(lane_lg[:, 0] * (c - 1.0 - col)), jnp.exp(lane_lg[:, 1] * col)], 1)
    cdec = jnp.exp(lane_lg[:, :, 0, :] * c)
    return dmat, rq, rk, cdec


def _s5_kernel(u_ref, w1_ref, w2_ref, ac_ref, y_ref, a_scr, hf_scr, hb_scr, *, n_ctx, n_all, bsz):
    width = S5_FOLD
    st2 = 2 * S5_STATE
    for b in range(bsz):
        a_scr[b] = _dot(u_ref[0, :, b * width : (b + 1) * width].astype(BF16), w1_ref[0, 0])
    ac = ac_ref[0, 0]

    zero = jnp.zeros((1, st2), F32)
    state = [(zero, zero)] * (2 * bsz)
    for n in range(n_all):
        nb = n_ctx - 1 - n if n < n_ctx else n_all + n_ctx - 1 - n
        for b in range(bsz):
            for d, (row, scr) in enumerate(((n, hf_scr), (nb, hb_scr))):
                h, hsw = state[2 * b + d]
                scr[b, row : row + 1, :] = h
                lo = width + d * st2
                inj = a_scr[b, row : row + 1, lo : lo + st2]
                inj_sw = a_scr[b, row : row + 1, lo + 2 * st2 : lo + 3 * st2]
                ar, ai = ac[2 * d : 2 * d + 1, :], ac[2 * d + 1 : 2 * d + 2, :]
                state[2 * b + d] = (h * ar + hsw * ai + inj, hsw * ar - h * ai + inj_sw)
    for b in range(bsz):
        y_ref[0, :, b * width : (b + 1) * width] = (
            a_scr[b, :, :width]
            + _dot(hf_scr[b].astype(BF16), w2_ref[0, 0, 0])
            + _dot(hb_scr[b].astype(BF16), w2_ref[0, 0, 1]))


def _s5_conv(ug, w1, w2, ac, l, bsz, n_ctx):
    g, n_all, _ = ug.shape
    width = S5_FOLD
    st2 = 2 * S5_STATE
    return pl.pallas_call(
        functools.partial(_s5_kernel, n_ctx=n_ctx, n_all=n_all, bsz=bsz),
        grid=(g,),
        in_specs=[
            pl.BlockSpec((1, n_all, bsz * width), lambda i: (i, 0, 0)),
            pl.BlockSpec((1, 1, width, width + 4 * st2), lambda i: (l, i, 0, 0)),
            pl.BlockSpec((1, 1, 2, st2, width), lambda i: (l, i, 0, 0, 0)),
            pl.BlockSpec((1, 1, SUBLANES, st2), lambda i: (l, i, 0, 0)),
        ],
        out_specs=pl.BlockSpec((1, n_all, bsz * width), lambda i: (i, 0, 0)),
        out_shape=jax.ShapeDtypeStruct((g, n_all, bsz * width), F32),
        scratch_shapes=[pltpu.VMEM((bsz, n_all, width + 4 * st2), F32),
                        pltpu.VMEM((bsz, n_all, st2), F32), pltpu.VMEM((bsz, n_all, st2), F32)],
        compiler_params=_cparams(("arbitrary",)),
        name="s5_conv",
    )(ug, w1, w2, ac)


def _s5_weights(lam_re, lam_im, log_dt, b_re, b_im, c_re, c_im):
    cs, ch, p, g = S5_CHUNK, S5_GROUP_CH, S5_STATE, S5_GROUPS
    nl = lam_re.shape[0]
    lam_re = jnp.minimum(lam_re.astype(F32), -1e-4)
    lam_im = lam_im.astype(F32)
    dt = jnp.exp(log_dt.astype(F32))[..., None]
    mag = jnp.exp(dt * lam_re)
    abar_re, abar_im = mag * jnp.cos(dt * lam_im), mag * jnp.sin(dt * lam_im)
    den = jnp.square(lam_re) + jnp.square(lam_im)
    nr, ni = abar_re - 1.0, abar_im
    coef_re = ((nr * lam_re + ni * lam_im) / den)[..., None]
    coef_im = ((ni * lam_re - nr * lam_im) / den)[..., None]
    b_re, b_im = b_re.astype(F32), b_im.astype(F32)
    bb_re = coef_re * b_re - coef_im * b_im
    bb_im = coef_re * b_im + coef_im * b_re
    c_re, c_im = c_re.astype(F32), c_im.astype(F32)
    tau = jnp.arange(cs + 1, dtype=F32)[:, None, None, None, None]
    pw_mag = jnp.exp(tau * (dt * lam_re)[None])
    pw_re = pw_mag * jnp.cos(tau * (dt * lam_im)[None])
    pw_im = pw_mag * jnp.sin(tau * (dt * lam_im)[None])
    ab_re = pw_re[..., None] * bb_re[None] - pw_im[..., None] * bb_im[None]
    ab_im = pw_re[..., None] * bb_im[None] + pw_im[..., None] * bb_re[None]
    hi = lax.Precision.HIGHEST
    kern = (jnp.einsum('ldgcp,tldgpe->ldgetc', c_re, ab_re[:cs], precision=hi)
            - jnp.einsum('ldgcp,tldgpe->ldgetc', c_im, ab_im[:cs], precision=hi))
    base_f = kern[:, 0].reshape(nl, g, ch, cs * ch)
    base_b = kern[:, 1][:, :, :, ::-1].reshape(nl, g, ch, cs * ch)
    rows_f, rows_b = [], []
    for s in range(cs):
        sh = s * ch
        rows_f.append(jnp.pad(base_f[..., : cs * ch - sh], ((0, 0), (0, 0), (0, 0), (sh, 0))))
        sh = (cs - 1 - s) * ch
        rows_b.append(jnp.pad(base_b[..., sh:], ((0, 0), (0, 0), (0, 0), (0, sh))))
    toep = (jnp.stack(rows_f, 2) + jnp.stack(rows_b, 2)).reshape(nl, g, cs * ch, cs * ch)

    def inject(d, order, swapped):
        re = jnp.transpose(ab_re[:, :, d][order], (1, 2, 0, 4, 3))
        im = jnp.transpose(ab_im[:, :, d][order], (1, 2, 0, 4, 3))
        return jnp.concatenate([im, re] if swapped else [re, im], -1).reshape(nl, g, cs * ch, 2 * p)

    steps = jnp.arange(cs)
    w1 = jnp.concatenate([toep, inject(0, cs - 1 - steps, False), inject(1, steps, False),
                          inject(0, cs - 1 - steps, True), inject(1, steps, True)], axis=-1)

    def readout(d, expo):
        pr, pi = pw_re[:, :, d][expo], pw_im[:, :, d][expo]
        cr = c_re[None, :, d] * pr[:, :, :, None, :] - c_im[None, :, d] * pi[:, :, :, None, :]
        ci = c_re[None, :, d] * pi[:, :, :, None, :] + c_im[None, :, d] * pr[:, :, :, None, :]
        w = jnp.concatenate([cr, -ci], axis=-1)
        return jnp.transpose(w, (1, 2, 4, 0, 3)).reshape(nl, g, 2 * p, cs * ch)

    w2 = jnp.stack([readout(0, steps + 1), readout(1, cs - steps)], axis=2)
    ac_rows = []
    for d in range(2):
        ar, ai = pw_re[cs, :, d], pw_im[cs, :, d]
        ac_rows += [jnp.concatenate([ar, ar], -1), jnp.concatenate([-ai, ai], -1)]
    ac = jnp.stack(ac_rows + [jnp.zeros_like(ac_rows[0])] * (SUBLANES - 4), axis=2)
    return w1.astype(BF16), w2.astype(BF16), ac


def _layer_norm(x, g, b):
    mu = jnp.mean(x, -1, keepdims=True)
    xc = x - mu
    var = jnp.mean(xc * xc, -1, keepdims=True)
    return xc * lax.rsqrt(var + LN_EPS) * g + b


def _route(logits):
    col = lambda i: logits[:, i : i + 1]
    gl = [col(i) for i in range(N_GROUPS)]
    gmax = functools.reduce(jnp.maximum, gl)
    g_idx = jnp.full_like(gmax, N_GROUPS - 1).astype(jnp.int32)
    for i in reversed(range(N_GROUPS - 1)):
        g_idx = jnp.where(gl[i] == gmax, i, g_idx)
    g_p = 1.0 / functools.reduce(lambda a, b: a + b, [jnp.exp(x - gmax) for x in gl])
    el = []
    for e in range(EXPERTS_PER_GROUP):
        v = col(N_GROUPS + (N_GROUPS - 1) * EXPERTS_PER_GROUP + e)
        for g in reversed(range(N_GROUPS - 1)):
            v = jnp.where(g_idx == g, col(N_GROUPS + g * EXPERTS_PER_GROUP + e), v)
        el.append(v)
    m1 = functools.reduce(jnp.maximum, el)
    i1 = jnp.full_like(g_idx, EXPERTS_PER_GROUP - 1)
    for e in reversed(range(EXPERTS_PER_GROUP - 1)):
        i1 = jnp.where(el[e] == m1, e, i1)
    rest = [jnp.where(i1 == e, -jnp.inf, el[e]) for e in range(EXPERTS_PER_GROUP)]
    m2 = functools.reduce(jnp.maximum, rest)
    i2 = jnp.full_like(g_idx, EXPERTS_PER_GROUP - 1)
    for e in reversed(range(EXPERTS_PER_GROUP - 1)):
        i2 = jnp.where((rest[e] == m2) & (i1 != e), e, i2)
    t = jnp.exp(m2 - m1)
    w1 = g_p / (1.0 + t)
    w2 = g_p * t / (1.0 + t)
    lo = jnp.minimum(i1, i2)
    hi = jnp.maximum(i1, i2)
    pair = jnp.where(lo == 0, hi - 1, jnp.where(lo == 1, hi + 1, N_PAIRS - 1))
    bucket = g_idx * N_PAIRS + pair
    w_lo = jnp.where(i1 < i2, w1, w2)
    w_hi = jnp.where(i1 < i2, w2, w1)
    return bucket, w_lo, w_hi


def _merge_kernel(gate_ref, rg_ref, u_ref, hf_ref, hb_ref, rf_ref, rb_ref, yg_ref, x_ref, m_ref,
                  hgn_ref, gng_ref, gnb_ref, d_ref, gw_ref, gb_ref, wo_ref, lng_ref, lnb_ref,
                  wr_ref, br_ref, a128_ref, a64_ref, tri_ref, x1_ref, h2_ref, route_ref, cnt_ref, wt_scr, *, alpha):
    @pl.when((pl.program_id(0) == 0) & (pl.program_id(1) == 0))
    def _():
        cnt_ref[...] = jnp.zeros_like(cnt_ref)

    o_hg = hf_ref[0] + hb_ref[0]
    ms = _dot2(o_hg * o_hg, a128_ref[...])
    hg = o_hg * lax.rsqrt(ms + LN_EPS) * hgn_ref[0] * _silu(gate_ref[0])
    o_rt = rf_ref[0] + rb_ref[0]
    mu = _dot2(o_rt, a64_ref[...])
    xc = o_rt - mu
    var = _dot2(xc * xc, a64_ref[...])
    rt = (xc * lax.rsqrt(var + LN_EPS) * gng_ref[0] + gnb_ref[0]) * _silu(rg_ref[0])
    rows_per_chunk = S5_CHUNK // SUBLANES
    n_chunks = yg_ref.shape[1]
    y5 = []
    for half in range(S5_WIDTH // LANES):
        for g_lo in range(SUBLANES):
            for s_hi in range(rows_per_chunk):
                wt_scr[pl.ds(s_hi * SUBLANES + g_lo, n_chunks, stride=S5_CHUNK), :] = (
                    yg_ref[half * SUBLANES + g_lo, :, s_hi * LANES : (s_hi + 1) * LANES])
        y5.append(_swap_sublane_lanegroup(wt_scr[...]))
    y5 = jnp.concatenate(y5, axis=-1)
    s5 = jax.nn.gelu(y5 + d_ref[0] * u_ref[0])
    s5 = s5 * jax.nn.sigmoid(_dot(s5.astype(BF16), gw_ref[0]) + gb_ref[0])
    cat = jnp.concatenate([hg, rt, s5], axis=-1).astype(BF16)
    y = _dot(cat, wo_ref[0])
    m = m_ref[0, 0, 0]
    x1 = _layer_norm(alpha * x_ref[0] + m[2:3, :] * y, lng_ref[0], lnb_ref[0])
    x1_ref[0] = x1
    h2 = x1 * (1.0 + m[4:5, :]) + m[3:4, :]
    logits = _dot3(h2, wr_ref[0]) + br_ref[0]
    bucket, w_lo, w_hi = _route(logits)
    lane = lax.broadcasted_iota(jnp.int32, logits.shape, 1)
    h2_ref[0, :, : h2.shape[1]] = h2
    h2_ref[0, :, h2.shape[1] :] = jnp.where(lane == 0, w_lo, jnp.where(lane == 1, w_hi, 0.0))
    onehot = lane == bucket
    before = _dot(tri_ref[...], onehot.astype(BF16)) + cnt_ref[...]
    rank = jnp.sum(jnp.where(onehot, before, 0.0), axis=-1, keepdims=True)
    route_ref[0] = jnp.where(lane == 0, bucket.astype(F32), jnp.where(lane == 1, rank, 0.0))
    cnt_ref[...] += jnp.sum(onehot.astype(F32), axis=0, keepdims=True)


def _merge(p, o_hf, o_hb, o_rf, o_rb, yg, x, modtab, layer_prm, const_prm, l, tm, nct, alpha):
    b, s, d = x.shape
    cpt = tm // S5_CHUNK

    def tok(width, col):
        return pl.BlockSpec((1, tm, width), lambda i, j: (i, j, col))

    def whole(a):
        return pl.BlockSpec(a.shape, lambda i, j: (0,) * a.ndim)

    rbase = 5 * HG_WIDTH // RET_WIDTH
    in_specs = [tok(HG_WIDTH, 2), tok(RET_WIDTH, rbase + 3), tok(S5_WIDTH, rbase + 4),
                tok(HG_WIDTH, 0), tok(HG_WIDTH, 0), tok(RET_WIDTH, 0), tok(RET_WIDTH, 0),
                pl.BlockSpec((S5_GROUPS, cpt, S5_FOLD), lambda i, j: (0, j, i)),
                tok(d, 0), _mod_spec(l, nct)]
    in_specs += [_layer_spec(a.shape[1:], l) for a in layer_prm]
    in_specs += [whole(a) for a in const_prm]
    return pl.pallas_call(
        functools.partial(_merge_kernel, alpha=alpha),
        grid=(b, s // tm),
        in_specs=in_specs,
        out_specs=[tok(d, 0), tok(d + LANES, 0), tok(LANES, 0), pl.BlockSpec((1, LANES), lambda i, j: (0, 0))],
        out_shape=[jax.ShapeDtypeStruct((b, s, d), F32), jax.ShapeDtypeStruct((b, s, d + LANES), F32),
                   jax.ShapeDtypeStruct((b, s, LANES), F32), jax.ShapeDtypeStruct((1, LANES), F32)],
        scratch_shapes=[pltpu.VMEM((tm, LANES), F32)],
        compiler_params=_cparams(("arbitrary", "arbitrary")),
        name="merge_ln1_router",
    )(p, p, p, o_hf, o_hb, o_rf, o_rb, yg, x, modtab, *layer_prm, *const_prm)


def _routing_tables(route, counts, te, n_tiles):
    bucket = route[..., 0].astype(jnp.int32).reshape(-1)
    rank = route[..., 1].astype(jnp.int32).reshape(-1)
    cnt = counts[0, :N_BUCKETS].astype(jnp.int32)
    padded = (cnt + te - 1) // te * te
    ends = jnp.cumsum(padded)
    pos = (ends - padded)[bucket] + rank
    n_used = ends[-1] // te
    tile = jnp.arange(n_tiles, dtype=jnp.int32)
    tb = jnp.minimum(jnp.searchsorted(ends, tile * te, side="right"), N_BUCKETS - 1).astype(jnp.int32)
    tb = jnp.where(tile < n_used, tb, tb[jnp.maximum(n_used - 1, 0)])
    group, pair = tb // N_PAIRS, tb % N_PAIRS
    ea = group * EXPERTS_PER_GROUP + jnp.asarray(PAIR_LO, jnp.int32)[pair]
    eb = group * EXPERTS_PER_GROUP + jnp.asarray(PAIR_HI, jnp.int32)[pair]
    return pos, ea, eb, n_used.reshape(1).astype(jnp.int32)


def _dispatch_kernel(pos_ref, src_ref, init_ref, out_ref, sem, *, tm, tiles_per_row):
    del init_ref
    base = (pl.program_id(0) * tiles_per_row + pl.program_id(1)) * tm

    def row_copy(r):
        return pltpu.make_async_copy(src_ref.at[0, pl.ds(r, 1)], out_ref.at[pl.ds(pos_ref[base + r], 1)], sem)

    def start(r, carry):
        row_copy(r).start()
        return carry

    def wait(r, carry):
        row_copy(r).wait()
        return carry

    lax.fori_loop(0, tm, start, 0, unroll=DMA_UNROLL)
    lax.fori_loop(0, tm, wait, 0, unroll=DMA_UNROLL)


def _dispatch(pos, rows, n_sorted, tm):
    b, s, w = rows.shape
    return pl.pallas_call(
        functools.partial(_dispatch_kernel, tm=tm, tiles_per_row=s // tm),
        grid_spec=pltpu.PrefetchScalarGridSpec(
            num_scalar_prefetch=1,
            grid=(b, s // tm),
            in_specs=[pl.BlockSpec((1, tm, w), lambda i, j, pos_ref: (i, j, 0)),
                      pl.BlockSpec(memory_space=pl.ANY)],
            out_specs=pl.BlockSpec(memory_space=pl.ANY),
            scratch_shapes=[pltpu.SemaphoreType.DMA(())],
        ),
        out_shape=jax.ShapeDtypeStruct((n_sorted, w), F32),
        input_output_aliases={2: 0},
        compiler_params=_cparams(("arbitrary", "arbitrary")),
        name="moe_dispatch",
    )(pos, rows, jnp.zeros((n_sorted, w), F32))


def _expert_kernel(ea_ref, eb_ref, nused_ref, hs_ref, wga_ref, wua_ref, wda_ref, wgb_ref, wub_ref, wdb_ref,
                   o_ref, wg_scr, wu_scr, wd_scr):
    t = pl.program_id(0)
    prev = jnp.maximum(t - 1, 0)
    changed = (t == 0) | (ea_ref[t] != ea_ref[prev]) | (eb_ref[t] != eb_ref[prev])

    @pl.when(changed)
    def _():
        wg_scr[0] = wga_ref[0, 0].astype(BF16)
        wu_scr[0] = wua_ref[0, 0].astype(BF16)
        wd_scr[0] = wda_ref[0, 0].astype(BF16)
        wg_scr[1] = wgb_ref[0, 0].astype(BF16)
        wu_scr[1] = wub_ref[0, 0].astype(BF16)
        wd_scr[1] = wdb_ref[0, 0].astype(BF16)

    @pl.when(t < nused_ref[0])
    def _():
        d = o_ref.shape[1]
        h = hs_ref[:, :d].astype(BF16)
        y = jnp.zeros(o_ref.shape, F32)
        for e in range(2):
            act = (_silu(_dot(h, wg_scr[e])) * _dot(h, wu_scr[e])).astype(BF16)
            y = y + hs_ref[:, d + e : d + e + 1] * _dot(act, wd_scr[e])
        o_ref[...] = y

    @pl.when(t >= nused_ref[0])
    def _():
        o_ref[...] = jnp.zeros_like(o_ref)


def _experts(ea, eb, n_used, hs, wg, wu, wd, l, te):
    n_sorted, w = hs.shape
    _, _, d, eh = wg.shape

    def wspec(shape, which):
        return pl.BlockSpec((1, 1) + shape, lambda t, ea_ref, eb_ref, n_ref: (l, (ea_ref, eb_ref)[which][t], 0, 0))

    return pl.pallas_call(
        _expert_kernel,
        grid_spec=pltpu.PrefetchScalarGridSpec(
            num_scalar_prefetch=3,
            grid=(n_sorted // te,),
            in_specs=[pl.BlockSpec((te, w), lambda t, *_: (t, 0)),
                      wspec((d, eh), 0), wspec((d, eh), 0), wspec((eh, d), 0),
                      wspec((d, eh), 1), wspec((d, eh), 1), wspec((eh, d), 1)],
            out_specs=pl.BlockSpec((te, d), lambda t, *_: (t, 0)),
            scratch_shapes=[pltpu.VMEM((2, d, eh), BF16), pltpu.VMEM((2, d, eh), BF16), pltpu.VMEM((2, eh, d), BF16)],
        ),
        out_shape=jax.ShapeDtypeStruct((n_sorted, d), F32),
        compiler_params=_cparams(("arbitrary",)),
        name="moe_experts",
    )(ea, eb, n_used, hs, wg, wu, wd, wg, wu, wd)


def _combine_kernel(pos_ref, ys_ref, x1_ref, m_ref, lng_ref, lnb_ref, o_ref, buf, sem, *, tm, tiles_per_row, alpha):
    n_steps = pl.num_programs(0) * tiles_per_row
    step = pl.program_id(0) * tiles_per_row + pl.program_id(1)
    slot = step % 2

    def row_copy(tile, slot_, r):
        src = ys_ref.at[pl.ds(pos_ref[tile * tm + r], 1)]
        return pltpu.make_async_copy(src, buf.at[slot_, pl.ds(r, 1)], sem.at[slot_])

    def start_tile(tile, slot_):
        def body(r, carry):
            row_copy(tile, slot_, r).start()
            return carry

        lax.fori_loop(0, tm, body, 0, unroll=DMA_UNROLL)

    @pl.when(step == 0)
    def _():
        start_tile(0, 0)

    @pl.when(step + 1 < n_steps)
    def _():
        start_tile(step + 1, 1 - slot)

    def wait(r, carry):
        row_copy(step, slot, r).wait()
        return carry

    lax.fori_loop(0, tm, wait, 0, unroll=DMA_UNROLL)
    m = m_ref[0, 0, 0]
    o_ref[0] = _layer_norm(alpha * x1_ref[0] + m[5:6, :] * buf[slot], lng_ref[0], lnb_ref[0])


def _combine(pos, ys, x1, modtab, ln_g, ln_b, l, tm, nct, alpha):
    b, s, d = x1.shape
    return pl.pallas_call(
        functools.partial(_combine_kernel, tm=tm, tiles_per_row=s // tm, alpha=alpha),
        grid_spec=pltpu.PrefetchScalarGridSpec(
            num_scalar_prefetch=1,
            grid=(b, s // tm),
            in_specs=[pl.BlockSpec(memory_space=pl.ANY),
                      pl.BlockSpec((1, tm, d), lambda i, j, pos_ref: (i, j, 0)),
                      _mod_spec(l, nct), _layer_spec((1, d), l), _layer_spec((1, d), l)],
            out_specs=pl.BlockSpec((1, tm, d), lambda i, j, pos_ref: (i, j, 0)),
            scratch_shapes=[pltpu.VMEM((2, tm, d), F32), pltpu.SemaphoreType.DMA((2,))],
        ),
        out_shape=jax.ShapeDtypeStruct((b, s, d), F32),
        compiler_params=_cparams(("arbitrary", "arbitrary")),
        name="moe_combine_ln2",
    )(pos, ys, x1, modtab, ln_g, ln_b)


def _block_avg(width, group):
    idx = np.arange(width) // group
    return jnp.asarray((idx[:, None] == idx[None, :]).astype(np.float32) / group, dtype=BF16)


def kernel(x, c, ctx, c_ctx, w_mod, b_mod, w_in, hg_lb_raw, hg_norm_g, ret_decay_raw, ret_gn_g, ret_gn_b, s5_lam_re, s5_lam_im, s5_log_dt, s5_b_re, s5_b_im, s5_c_re, s5_c_im, s5_d, s5_glu_w, s5_glu_b, w_out, ln1_g, ln1_b, ln2_g, ln2_b, rg_w, rg_b, re_w, re_b, exp_w_gate, exp_w_up, exp_w_down):
    bsz, t_lat, d = x.shape
    t_ctx = ctx.shape[1]
    depth = w_mod.shape[0]
    alpha = (2.0 * depth) ** 0.25
    tm = TOKEN_TILE
    assert d == D_MODEL and t_lat % GRID_W == 0 and bsz < SUBLANES
    assert t_ctx % tm == 0 and t_lat % tm == 0, "context and latent lengths must be multiples of the token tile"
    nct = t_ctx // tm
    s = t_ctx + t_lat

    cvec = jnp.concatenate([c, c_ctx[None, :], jnp.zeros((SUBLANES - bsz - 1, d), F32)], 0)
    mod_all = _modulation(cvec, w_mod, b_mod)
    lat = mod_all[:, :bsz].reshape(depth, bsz, 6, d)
    cm = jnp.broadcast_to(mod_all[:, bsz].reshape(depth, 1, 6, d), (depth, bsz, 6, d))
    modtab = jnp.stack([cm, lat], axis=2)

    hg_lb = jnp.cumsum(jax.nn.softmax(hg_lb_raw.astype(F32), axis=0), axis=0)
    hg_lb = hg_lb - hg_lb[:1]
    ret_tables = _retention_decay_tables(jax.nn.log_sigmoid(ret_decay_raw.astype(F32)))
    cos_tab, sin_tab = _rope_tables(t_lat, t_ctx)
    s5_w1, s5_w2, s5_ac = _s5_weights(s5_lam_re, s5_lam_im, s5_log_dt, s5_b_re, s5_b_im, s5_c_re, s5_c_im)
    pad_r = LANES - N_GROUPS - N_EXPERTS
    wr = jnp.concatenate([rg_w, re_w.reshape(depth, d, N_EXPERTS), jnp.zeros((depth, d, pad_r), F32)], axis=2)
    br = jnp.concatenate([rg_b, re_b.reshape(depth, N_EXPERTS), jnp.zeros((depth, pad_r), F32)], axis=1)[:, None, :]
    row = lambda a: a[:, None, :]
    layer_prm = [row(jnp.tile(hg_norm_g, (1, HG_HEADS))), row(ret_gn_g), row(ret_gn_b), row(s5_d),
                 s5_glu_w.astype(BF16), row(s5_glu_b), w_out.astype(BF16), row(ln1_g), row(ln1_b), wr, br]
    tri = jnp.asarray(np.tril(np.ones((tm, tm), np.float32), -1), dtype=BF16)
    const_prm = [_block_avg(HG_WIDTH, HG_DK), _block_avg(RET_WIDTH, RET_DK), tri]
    w_in_bf16 = w_in.astype(BF16)
    ln2_g, ln2_b = row(ln2_g), row(ln2_b)
    te = EXPERT_TILE
    n_tiles = -(-(bsz * s + N_BUCKETS * (te - 1)) // te)

    xs = jnp.concatenate([ctx, x], axis=1)
    for l in range(depth):
        p, ug = _inproj(xs, modtab, w_in_bf16, l, tm, nct)
        o_hf, o_hb = _hgrn(p, hg_lb, l, t_ctx // CHUNK)
        o_rf, o_rb = _retention(p, cos_tab, sin_tab, ret_tables, l, t_ctx // RET_CHUNK)
        yg = _s5_conv(ug, s5_w1, s5_w2, s5_ac, l, bsz, t_ctx // S5_CHUNK)
        x1, rows, route, counts = _merge(p, o_hf, o_hb, o_rf, o_rb, yg, xs, modtab, layer_prm, const_prm,
                                         l, tm, nct, alpha)
        pos, ea, eb, n_used = _routing_tables(route, counts, te, n_tiles)
        hs = _dispatch(pos, rows, n_tiles * te, tm)
        ys = _experts(ea, eb, n_used, hs, exp_w_gate, exp_w_up, exp_w_down, l, te)
        xs = _combine(pos, ys, x1, modtab, ln2_g, ln2_b, l, tm, nct, alpha)
    return xs[:, t_ctx:, :]
```

```python
import functools

import numpy as np
import jax
import jax.numpy as jnp
from jax import lax
from jax.experimental import pallas as pl
from jax.experimental.pallas import tpu as pltpu

F32 = jnp.float32
BF16 = jnp.bfloat16

D_MODEL = 1024
HG_WIDTH = 512
HG_HEADS = 4
HG_DK = HG_WIDTH // HG_HEADS
RET_WIDTH = 256
RET_HEADS = 4
RET_DK = RET_WIDTH // RET_HEADS
S5_WIDTH = 256
S5_GROUP_CH = 16
S5_GROUPS = S5_WIDTH // S5_GROUP_CH
S5_STATE = 64
IN_COLS = 5 * HG_WIDTH + 4 * RET_WIDTH + S5_WIDTH
CHUNK = 64
HGRN_BATCH = 4
RET_CHUNK = 256
N_GROUPS = 4
EXPERTS_PER_GROUP = 4
N_EXPERTS = N_GROUPS * EXPERTS_PER_GROUP
N_PAIRS = EXPERTS_PER_GROUP * (EXPERTS_PER_GROUP - 1) // 2
N_BUCKETS = N_GROUPS * N_PAIRS
PAIR_LO = (0, 0, 0, 1, 1, 2)
PAIR_HI = (1, 2, 3, 2, 3, 3)
EXPERT_HIDDEN = D_MODEL // 2
LN_EPS = 1e-5
ROPE_BASE = 10000.0
GRID_W = 64

LANES = 128
SUBLANES = 8
TOKEN_TILE = 256
S5_CHUNK = 32
S5_FOLD = S5_CHUNK * S5_GROUP_CH
EXPERT_TILE = 256
DMA_UNROLL = 8
VMEM_LIMIT = 56 * 1024 * 1024


def _cparams(sem):
    return pltpu.CompilerParams(dimension_semantics=sem, vmem_limit_bytes=VMEM_LIMIT)


def _split_bf16(x):
    hi = x.astype(BF16)
    lo = (x - hi.astype(F32)).astype(BF16)
    return hi, lo


def _dot(a, b):
    return jnp.dot(a, b, preferred_element_type=F32)


def _dot3(a, b):
    ah, al = _split_bf16(a)
    bh, bl = _split_bf16(b)
    return _dot(ah, bh) + _dot(ah, bl) + _dot(al, bh)


def _dot2(a, b_bf16):
    ah, al = _split_bf16(a)
    return _dot(ah, b_bf16) + _dot(al, b_bf16)


def _dot_nt(a, b):
    return lax.dot_general(a, b, (((1,), (1,)), ((), ())), preferred_element_type=F32)


def _dot_tn(a, b):
    return lax.dot_general(a, b, (((0,), (0,)), ((), ())), preferred_element_type=F32)


def _silu(x):
    return x * jax.nn.sigmoid(x)


def _layer_spec(shape, l):
    zeros = (0,) * len(shape)
    return pl.BlockSpec((1,) + tuple(shape), lambda *_: (l,) + zeros)


def _mod_spec(l, nct):
    return pl.BlockSpec((1, 1, 1, 6, D_MODEL), lambda i, j, *_: (l, i, jnp.minimum(j // nct, 1), 0, 0))


def _swap_sublane_lanegroup(v):
    n = v.shape[0]
    r = lax.broadcasted_iota(jnp.int32, v.shape, 0)
    l = lax.broadcasted_iota(jnp.int32, v.shape, 1)
    for k in range(3):
        rb = (r >> k) & 1
        gb = (l >> (4 + k)) & 1
        sh = S5_GROUP_CH << k
        st = 1 << k
        a = pltpu.roll(pltpu.roll(v, LANES - sh, 1), st, 0)
        b = pltpu.roll(pltpu.roll(v, sh, 1), n - st, 0)
        v = jnp.where(rb == gb, v, jnp.where(rb == 1, a, b))
    return v


def _mod_kernel(c_ref, w_ref, b_ref, o_ref):
    sc = _silu(c_ref[...])
    o_ref[0] = _dot3(sc, w_ref[0]) + b_ref[0]


def _modulation(cvec, w_mod, b_mod):
    depth, d, n = w_mod.shape
    rows = cvec.shape[0]
    tn = 1536
    return pl.pallas_call(
        _mod_kernel,
        grid=(depth, n // tn),
        in_specs=[
            pl.BlockSpec((rows, d), lambda l, j: (0, 0)),
            pl.BlockSpec((1, d, tn), lambda l, j: (l, 0, j)),
            pl.BlockSpec((1, 1, tn), lambda l, j: (l, 0, j)),
        ],
        out_specs=pl.BlockSpec((1, rows, tn), lambda l, j: (l, 0, j)),
        out_shape=jax.ShapeDtypeStruct((depth, rows, n), F32),
        compiler_params=_cparams(("arbitrary", "arbitrary")),
        name="modulation",
    )(cvec, w_mod, b_mod.reshape(depth, 1, n))


def _inproj_kernel(x_ref, m_ref, w_ref, o_ref, ug_ref, wt_scr):
    m = m_ref[0, 0, 0]
    h = x_ref[0] * (1.0 + m[1:2, :]) + m[0:1, :]
    p = _dot(h.astype(BF16), w_ref[0])
    o_ref[0] = p
    rows_per_chunk = S5_CHUNK // SUBLANES
    n_chunks = p.shape[0] // S5_CHUNK
    for half in range(S5_WIDTH // LANES):
        lo = IN_COLS - S5_WIDTH + half * LANES
        wt_scr[...] = _swap_sublane_lanegroup(p[:, lo : lo + LANES])
        for g_lo in range(SUBLANES):
            for s_hi in range(rows_per_chunk):
                piece = wt_scr[pl.ds(s_hi * SUBLANES + g_lo, n_chunks, stride=S5_CHUNK), :]
                ug_ref[half * SUBLANES + g_lo, :, s_hi * LANES : (s_hi + 1) * LANES] = piece


def _inproj(x, modtab, w_in_bf16, l, tm, nct):
    b, s, d = x.shape
    n = w_in_bf16.shape[-1]
    cpt = tm // S5_CHUNK
    return pl.pallas_call(
        _inproj_kernel,
        grid=(b, s // tm),
        in_specs=[
            pl.BlockSpec((1, tm, d), lambda i, j: (i, j, 0)),
            _mod_spec(l, nct),
            _layer_spec((d, n), l),
        ],
        out_specs=[pl.BlockSpec((1, tm, n), lambda i, j: (i, j, 0)),
                   pl.BlockSpec((S5_GROUPS, cpt, S5_FOLD), lambda i, j: (0, j, i))],
        out_shape=[jax.ShapeDtypeStruct((b, s, n), F32),
                   jax.ShapeDtypeStruct((S5_GROUPS, s // S5_CHUNK, b * S5_FOLD), F32)],
        scratch_shapes=[pltpu.VMEM((tm, LANES), F32)],
        compiler_params=_cparams(("arbitrary", "arbitrary")),
        name="inproj",
    )(x, modtab, w_in_bf16)


def _block_gate_products(f, reverse):
    c = f.shape[0]
    row = lax.broadcasted_iota(jnp.int32, (c, 1), 0)
    a, z, b = f, f, jnp.ones_like(f)
    out = []
    s = 1
    while s < c:
        out.append((a, b))
        if s < SUBLANES:
            up = pltpu.roll(z, s, 0)
            dn = pltpu.roll(z, c - s, 0)
            odd = (row & s) != 0
            if reverse:
                a = a * jnp.where(odd, 1.0, dn)
                b = b * jnp.where(odd, up, 1.0)
            else:
                a = a * jnp.where(odd, up, 1.0)
                b = b * jnp.where(odd, 1.0, dn)
            z = z * jnp.where(odd, up, dn)
        else:
            na, nb, nz = [], [], []
            for lo in range(0, c, 2 * s):
                ev, od = slice(lo, lo + s), slice(lo + s, lo + 2 * s)
                zz = z[ev] * z[od]
                if reverse:
                    na += [a[ev] * z[od], a[od]]
                    nb += [b[ev], b[od] * z[ev]]
                else:
                    na += [a[ev], a[od] * z[ev]]
                    nb += [b[ev] * z[od], b[od]]
                nz += [zz, zz]
            a, b, z = (jnp.concatenate(t, axis=0) for t in (na, nb, nz))
        s *= 2
    out.append((a, b))
    return out, z


def _hgrn_kernel(qf_ref, vf_ref, zf_ref, qb_ref, vb_ref, zb_ref, lb_ref, of_ref, ob_ref, st_ref):
    c = CHUNK

    @pl.when(pl.program_id(1) == 0)
    def _():
        st_ref[...] = jnp.zeros_like(st_ref)

    ri = lax.broadcasted_iota(jnp.int32, (c, c), 0)
    ci = lax.broadcasted_iota(jnp.int32, (c, c), 1)
    for d, (q_ref, v_ref, z_ref, o_ref) in enumerate(
        ((qf_ref, vf_ref, zf_ref, of_ref), (qb_ref, vb_ref, zb_ref, ob_ref))
    ):
        reverse = d == 1
        causal = (ri < ci) if reverse else (ri > ci)
        n_levels = c.bit_length() - 1
        masks = [(((ri >> lvl) ^ (ci >> lvl)) == 1) & causal for lvl in range(n_levels)]
        for bi, h in [(bi, h) for bi in range(q_ref.shape[0]) for h in range(HG_HEADS)]:
            hs = slice(h * HG_DK, (h + 1) * HG_DK)
            q = _silu(q_ref[bi, :, hs])
            v = v_ref[bi, :, hs].astype(BF16)
            lb = lb_ref[0, d : d + 1, hs]
            f = lb + (1.0 - lb) * jax.nn.sigmoid(z_ref[bi, :, hs])
            k = 1.0 - f
            levels, tot = _block_gate_products(f, reverse)
            scores = jnp.where(ri == ci, _dot_nt(q.astype(BF16), k.astype(BF16)), 0.0)
            for mask, (a, bb) in zip(masks, levels):
                scores = scores + jnp.where(mask, _dot_nt((q * a).astype(BF16), (k * bb).astype(BF16)), 0.0)
            a_full, b_full = levels[-1]
            st = st_ref[bi, d, h]
            o_ref[bi, :, hs] = (_dot(scores.astype(BF16), v)
                                + _dot_nt((q * a_full).astype(BF16), st.astype(BF16)))
            st_ref[bi, d, h] = st * tot[0:1, :] + _dot_tn(v, (k * b_full).astype(BF16))


def _bwd_chunk(n, nc_ctx, nc_all):
    return jnp.where(n < nc_ctx, nc_ctx - 1 - n, nc_all + nc_ctx - 1 - n)


def _hgrn(p, lb_all, l, nc_ctx):
    b, s, _ = p.shape
    nc = s // CHUNK
    w = HG_WIDTH

    bb = HGRN_BATCH if b % HGRN_BATCH == 0 else 1

    def fwd(col):
        return pl.BlockSpec((bb, CHUNK, w), lambda i, n: (i, n, col))

    def bwd(col):
        return pl.BlockSpec((bb, CHUNK, w), lambda i, n: (i, _bwd_chunk(n, nc_ctx, nc), col))

    out = jax.ShapeDtypeStruct((b, s, w), F32)
    return pl.pallas_call(
        _hgrn_kernel,
        grid=(b // bb, nc),
        in_specs=[fwd(0), fwd(1), fwd(3), bwd(0), bwd(1), bwd(4), _layer_spec((2, w), l)],
        out_specs=[fwd(0), bwd(0)],
        out_shape=[out, out],
        scratch_shapes=[pltpu.VMEM((bb, 2, HG_HEADS, HG_DK, HG_DK), F32)],
        compiler_params=_cparams(("arbitrary", "arbitrary")),
        name="hgrn2_scan",
    )(p, p, p, p, p, p, lb_all)


def _swap_halves(x, half):
    n = x.shape[-1]
    lane = lax.broadcasted_iota(jnp.int32, (1, n), 1)
    lower = (lane & half) == 0
    return jnp.where(lower, pltpu.roll(x, n - half, 1), pltpu.roll(x, half, 1))


def _ret_kernel(qf_ref, kf_ref, vf_ref, cf_ref, sf_ref, qb_ref, kb_ref, vb_ref, cb_ref, sb_ref,
                dmat_ref, rq_ref, rk_ref, cd_ref, of_ref, ob_ref, st_ref):
    @pl.when(pl.program_id(1) == 0)
    def _():
        st_ref[...] = jnp.zeros_like(st_ref)

    half = RET_DK // 4
    for d, (q_ref, k_ref, v_ref, c_ref, s_ref, o_ref) in enumerate(
        ((qf_ref, kf_ref, vf_ref, cf_ref, sf_ref, of_ref), (qb_ref, kb_ref, vb_ref, cb_ref, sb_ref, ob_ref))
    ):
        cos = c_ref[...]
        sin = s_ref[...]
        q = q_ref[0]
        k = k_ref[0] * (RET_DK ** -0.5)
        q = q * cos + _swap_halves(q, half) * sin
        k = k * cos + _swap_halves(k, half) * sin
        v = v_ref[0].astype(BF16)
        q0 = q.astype(BF16)
        k0 = k.astype(BF16)
        qd = (q * rq_ref[0, d]).astype(BF16)
        kd = (k * rk_ref[0, d]).astype(BF16)
        cd = cd_ref[0, d : d + 1, :]
        for h in range(RET_HEADS):
            hs = slice(h * RET_DK, (h + 1) * RET_DK)
            scores = _dot_nt(q0[:, hs], k0[:, hs]) * dmat_ref[0, d, h]
            st = st_ref[d, h]
            o_ref[0, :, hs] = _dot(scores.astype(BF16), v[:, hs]) + _dot_nt(qd[:, hs], st.astype(BF16))
            st_ref[d, h] = st * cd[:, hs] + _dot_tn(v[:, hs], kd[:, hs])


def _retention(p, cos_tab, sin_tab, tables, l, nc_ctx):
    b, s, _ = p.shape
    nc = s // RET_CHUNK
    w = RET_WIDTH
    base = 5 * HG_WIDTH // w

    def fwd(col):
        return pl.BlockSpec((1, RET_CHUNK, w), lambda i, n: (i, n, col))

    def bwd(col):
        return pl.BlockSpec((1, RET_CHUNK, w), lambda i, n: (i, _bwd_chunk(n, nc_ctx, nc), col))

    tab_f = pl.BlockSpec((RET_CHUNK, w), lambda i, n: (n, 0))
    tab_b = pl.BlockSpec((RET_CHUNK, w), lambda i, n: (_bwd_chunk(n, nc_ctx, nc), 0))
    out = jax.ShapeDtypeStruct((b, s, w), F32)
    return pl.pallas_call(
        _ret_kernel,
        grid=(b, nc),
        in_specs=[fwd(base), fwd(base + 1), fwd(base + 2), tab_f, tab_f,
                  bwd(base), bwd(base + 1), bwd(base + 2), tab_b, tab_b]
                 + [_layer_spec(t.shape[1:], l) for t in tables],
        out_specs=[fwd(0), bwd(0)],
        out_shape=[out, out],
        scratch_shapes=[pltpu.VMEM((2, RET_HEADS, RET_DK, RET_DK), F32)],
        compiler_params=_cparams(("arbitrary", "arbitrary")),
        name="retention_scan",
    )(p, p, p, cos_tab, sin_tab, p, p, p, cos_tab, sin_tab, *tables)


def _rope_tables(t_lat, t_ctx):
    m = RET_DK // 4
    inv = ROPE_BASE ** (-jnp.arange(m, dtype=F32) / m)
    rows = jnp.repeat(jnp.arange(t_lat // GRID_W, dtype=jnp.int32), GRID_W).astype(F32)
    cols = jnp.tile(jnp.arange(GRID_W, dtype=jnp.int32), t_lat // GRID_W).astype(F32)

    def half_tables(pos):
        ang = pos[:, None] * inv
        c, s = jnp.cos(ang), jnp.sin(ang)
        return jnp.concatenate([c, c], -1), jnp.concatenate([-s, s], -1)

    cr, sr = half_tables(rows)
    cc, sc = half_tables(cols)
    cos_h = jnp.concatenate([cr, cc], -1)
    sin_h = jnp.concatenate([sr, sc], -1)
    cos = jnp.tile(cos_h, (1, RET_HEADS))
    sin = jnp.tile(sin_h, (1, RET_HEADS))
    cos = jnp.concatenate([jnp.ones((t_ctx, RET_WIDTH), F32), cos], 0)
    sin = jnp.concatenate([jnp.zeros((t_ctx, RET_WIDTH), F32), sin], 0)
    return cos, sin


def _retention_decay_tables(log_gamma):
    c = RET_CHUNK
    i = jnp.arange(c, dtype=F32)
    diff = i[:, None] - i[None, :]
    lg = log_gamma[:, :, :, None, None]
    d_f = jnp.where(diff >= 0, jnp.exp(lg[:, 0] * diff), 0.0)
    d_b = jnp.where(diff <= 0, jnp.exp(lg[:, 1] * (-diff)), 0.0)
    dmat = jnp.stack([d_f, d_b], 1)
    lane_lg = jnp.repeat(log_gamma, RET_DK, axis=2)[:, :, None, :]
    col = i[None, :, None]
    rq = jnp.stack([jnp.exp(lane_lg[:, 0] * (col + 1.0)), jnp.exp(lane_lg[:, 1] * (c - col))], 1)
    rk = jnp.stack([jnp.exp(lane_lg[:, 0] * (c - 1.0 - col)), jnp.exp(lane_lg[:, 1] * col)], 1)
    cdec = jnp.exp(lane_lg[:, :, 0, :] * c)
    return dmat, rq, rk, cdec


def _s5_kernel(u_ref, base_ref, inj_ref, w2_ref, ac_ref, y_ref, toep_scr, a_scr, i_scr, hf_scr, hb_scr,
               *, n_ctx, n_all, bsz):
    width = S5_FOLD
    st2 = 2 * S5_STATE
    ch = S5_GROUP_CH
    lane = lax.broadcasted_iota(jnp.int32, (ch, width), 1)
    base_f = base_ref[0, 0, 0]
    base_b = base_ref[0, 0, 1]
    for s in range(S5_CHUNK):
        sh_f = s * ch
        sh_b = (S5_CHUNK - 1 - s) * ch
        part_f = jnp.where(lane >= sh_f, pltpu.roll(base_f, sh_f, 1), 0.0) if sh_f else base_f
        part_b = jnp.where(lane < width - sh_b, pltpu.roll(base_b, width - sh_b, 1), 0.0) if sh_b else base_b
        toep_scr[s * ch : (s + 1) * ch, :] = (part_f + part_b).astype(BF16)
    for b in range(bsz):
        ub = u_ref[0, :, b * width : (b + 1) * width].astype(BF16)
        a_scr[b] = _dot(ub, toep_scr[...])
        i_scr[b] = _dot(ub, inj_ref[0, 0])
    ac = ac_ref[0, 0]

    zero = jnp.zeros((1, st2), F32)
    state = [(zero, zero)] * (2 * bsz)
    for n in range(n_all):
        nb = n_ctx - 1 - n if n < n_ctx else n_all + n_ctx - 1 - n
        for b in range(bsz):
            for d, (row, scr) in enumerate(((n, hf_scr), (nb, hb_scr))):
                h, hsw = state[2 * b + d]
                scr[b, row : row + 1, :] = h
                inj = i_scr[b, row : row + 1, d * st2 : (d + 1) * st2]
                inj_sw = i_scr[b, row : row + 1, (d + 2) * st2 : (d + 3) * st2]
                ar, ai = ac[2 * d : 2 * d + 1, :], ac[2 * d + 1 : 2 * d + 2, :]
                state[2 * b + d] = (h * ar + hsw * ai + inj, hsw * ar - h * ai + inj_sw)
    for b in range(bsz):
        y_ref[0, :, b * width : (b + 1) * width] = (
            a_scr[b]
            + _dot(hf_scr[b].astype(BF16), w2_ref[0, 0, 0])
            + _dot(hb_scr[b].astype(BF16), w2_ref[0, 0, 1]))


def _s5_conv(ug, base, inj, w2, ac, l, bsz, n_ctx):
    g, n_all, _ = ug.shape
    width = S5_FOLD
    st2 = 2 * S5_STATE
    return pl.pallas_call(
        functools.partial(_s5_kernel, n_ctx=n_ctx, n_all=n_all, bsz=bsz),
        grid=(g,),
        in_specs=[
            pl.BlockSpec((1, n_all, bsz * width), lambda i: (i, 0, 0)),
            pl.BlockSpec((1, 1, 2, S5_GROUP_CH, width), lambda i: (l, i, 0, 0, 0)),
            pl.BlockSpec((1, 1, width, 4 * st2), lambda i: (l, i, 0, 0)),
            pl.BlockSpec((1, 1, 2, st2, width), lambda i: (l, i, 0, 0, 0)),
            pl.BlockSpec((1, 1, SUBLANES, st2), lambda i: (l, i, 0, 0)),
        ],
        out_specs=pl.BlockSpec((1, n_all, bsz * width), lambda i: (i, 0, 0)),
        out_shape=jax.ShapeDtypeStruct((g, n_all, bsz * width), F32),
        scratch_shapes=[pltpu.VMEM((width, width), BF16),
                        pltpu.VMEM((bsz, n_all, width), F32), pltpu.VMEM((bsz, n_all, 4 * st2), F32),
                        pltpu.VMEM((bsz, n_all, st2), F32), pltpu.VMEM((bsz, n_all, st2), F32)],
        compiler_params=_cparams(("arbitrary",)),
        name="s5_conv",
    )(ug, base, inj, w2, ac)


def _s5_weights(lam_re, lam_im, log_dt, b_re, b_im, c_re, c_im):
    cs, ch, p, g = S5_CHUNK, S5_GROUP_CH, S5_STATE, S5_GROUPS
    nl = lam_re.shape[0]
    lam_re = jnp.minimum(lam_re.astype(F32), -1e-4)
    lam_im = lam_im.astype(F32)
    dt = jnp.exp(log_dt.astype(F32))[..., None]
    mag = jnp.exp(dt * lam_re)
    abar_re, abar_im = mag * jnp.cos(dt * lam_im), mag * jnp.sin(dt * lam_im)
    den = jnp.square(lam_re) + jnp.square(lam_im)
    nr, ni = abar_re - 1.0, abar_im
    coef_re = ((nr * lam_re + ni * lam_im) / den)[..., None]
    coef_im = ((ni * lam_re - nr * lam_im) / den)[..., None]
    b_re, b_im = b_re.astype(F32), b_im.astype(F32)
    bb_re = coef_re * b_re - coef_im * b_im
    bb_im = coef_re * b_im + coef_im * b_re
    ct_re = jnp.swapaxes(c_re.astype(F32), -1, -2)
    ct_im = jnp.swapaxes(c_im.astype(F32), -1, -2)

    def powers(expo):
        e = expo.astype(F32)[None, :, None, None, :]
        m = jnp.exp(e * (dt * lam_re)[..., None])
        return m * jnp.cos(e * (dt * lam_im)[..., None]), m * jnp.sin(e * (dt * lam_im)[..., None])

    slots = jnp.arange(cs + 1)
    pr, pi = powers(jnp.stack([slots, cs - slots]))
    r_re = (pr[..., None] * ct_re[..., None, :] - pi[..., None] * ct_im[..., None, :]).reshape(nl, 2, g, p, (cs + 1) * ch)
    r_im = (pr[..., None] * ct_im[..., None, :] + pi[..., None] * ct_re[..., None, :]).reshape(nl, 2, g, p, (cs + 1) * ch)
    hi = lax.Precision.HIGHEST
    kern = (jnp.einsum('ldgpe,ldgpn->ldgen', bb_re, r_re, precision=hi)
            - jnp.einsum('ldgpe,ldgpn->ldgen', bb_im, r_im, precision=hi))
    w = cs * ch
    base = jnp.stack([kern[:, 0, :, :, :w], kern[:, 1, :, :, ch:]], axis=2)
    w2 = jnp.stack([jnp.concatenate([r_re[:, 0, :, :, ch:], -r_im[:, 0, :, :, ch:]], axis=2),
                    jnp.concatenate([r_re[:, 1, :, :, :w], -r_im[:, 1, :, :, :w]], axis=2)], axis=2)
    steps = jnp.arange(cs)
    qr, qi = powers(jnp.stack([cs - 1 - steps, steps]))
    qr = jnp.swapaxes(qr, -1, -2)[..., :, None, :]
    qi = jnp.swapaxes(qi, -1, -2)[..., :, None, :]
    bt_re = jnp.swapaxes(bb_re, -1, -2)[:, :, :, None]
    bt_im = jnp.swapaxes(bb_im, -1, -2)[:, :, :, None]
    ab_re = (qr * bt_re - qi * bt_im).reshape(nl, 2, g, w, p)
    ab_im = (qr * bt_im + qi * bt_re).reshape(nl, 2, g, w, p)
    inj = jnp.concatenate([ab_re[:, 0], ab_im[:, 0], ab_re[:, 1], ab_im[:, 1],
                           ab_im[:, 0], ab_re[:, 0], ab_im[:, 1], ab_re[:, 1]], axis=-1)
    ac_rows = []
    for d, slot in enumerate((cs, 0)):
        ar, ai = pr[:, d, :, :, slot], pi[:, d, :, :, slot]
        ac_rows += [jnp.concatenate([ar, ar], -1), jnp.concatenate([-ai, ai], -1)]
    ac = jnp.stack(ac_rows + [jnp.zeros_like(ac_rows[0])] * (SUBLANES - 4), axis=2)
    return base, inj.astype(BF16), w2.astype(BF16), ac


def _layer_norm(x, g, b):
    mu = jnp.mean(x, -1, keepdims=True)
    xc = x - mu
    var = jnp.mean(xc * xc, -1, keepdims=True)
    return xc * lax.rsqrt(var + LN_EPS) * g + b


def _route(logits):
    col = lambda i: logits[:, i : i + 1]
    gl = [col(i) for i in range(N_GROUPS)]
    gmax = functools.reduce(jnp.maximum, gl)
    g_idx = jnp.full_like(gmax, N_GROUPS - 1).astype(jnp.int32)
    for i in reversed(range(N_GROUPS - 1)):
        g_idx = jnp.where(gl[i] == gmax, i, g_idx)
    g_p = 1.0 / functools.reduce(lambda a, b: a + b, [jnp.exp(x - gmax) for x in gl])
    el = []
    for e in range(EXPERTS_PER_GROUP):
        v = col(N_GROUPS + (N_GROUPS - 1) * EXPERTS_PER_GROUP + e)
        for g in reversed(range(N_GROUPS - 1)):
            v = jnp.where(g_idx == g, col(N_GROUPS + g * EXPERTS_PER_GROUP + e), v)
        el.append(v)
    m1 = functools.reduce(jnp.maximum, el)
    i1 = jnp.full_like(g_idx, EXPERTS_PER_GROUP - 1)
    for e in reversed(range(EXPERTS_PER_GROUP - 1)):
        i1 = jnp.where(el[e] == m1, e, i1)
    rest = [jnp.where(i1 == e, -jnp.inf, el[e]) for e in range(EXPERTS_PER_GROUP)]
    m2 = functools.reduce(jnp.maximum, rest)
    i2 = jnp.full_like(g_idx, EXPERTS_PER_GROUP - 1)
    for e in reversed(range(EXPERTS_PER_GROUP - 1)):
        i2 = jnp.where((rest[e] == m2) & (i1 != e), e, i2)
    t = jnp.exp(m2 - m1)
    w1 = g_p / (1.0 + t)
    w2 = g_p * t / (1.0 + t)
    lo = jnp.minimum(i1, i2)
    hi = jnp.maximum(i1, i2)
    pair = jnp.where(lo == 0, hi - 1, jnp.where(lo == 1, hi + 1, N_PAIRS - 1))
    bucket = g_idx * N_PAIRS + pair
    w_lo = jnp.where(i1 < i2, w1, w2)
    w_hi = jnp.where(i1 < i2, w2, w1)
    return bucket, w_lo, w_hi


def _merge_kernel(gate_ref, rg_ref, u_ref, hf_ref, hb_ref, rf_ref, rb_ref, yg_ref, x_ref, m_ref,
                  hgn_ref, gng_ref, gnb_ref, d_ref, gw_ref, gb_ref, wo_ref, lng_ref, lnb_ref,
                  wr_ref, br_ref, a128_ref, a64_ref, tri_ref, x1_ref, h2_ref, route_ref, cnt_ref, wt_scr, *, alpha):
    @pl.when((pl.program_id(0) == 0) & (pl.program_id(1) == 0))
    def _():
        cnt_ref[...] = jnp.zeros_like(cnt_ref)

    o_hg = hf_ref[0] + hb_ref[0]
    ms = _dot2(o_hg * o_hg, a128_ref[...])
    hg = o_hg * lax.rsqrt(ms + LN_EPS) * hgn_ref[0] * _silu(gate_ref[0])
    o_rt = rf_ref[0] + rb_ref[0]
    mu = _dot2(o_rt, a64_ref[...])
    xc = o_rt - mu
    var = _dot2(xc * xc, a64_ref[...])
    rt = (xc * lax.rsqrt(var + LN_EPS) * gng_ref[0] + gnb_ref[0]) * _silu(rg_ref[0])
    rows_per_chunk = S5_CHUNK // SUBLANES
    n_chunks = yg_ref.shape[1]
    y5 = []
    for half in range(S5_WIDTH // LANES):
        for g_lo in range(SUBLANES):
            for s_hi in range(rows_per_chunk):
                wt_scr[pl.ds(s_hi * SUBLANES + g_lo, n_chunks, stride=S5_CHUNK), :] = (
                    yg_ref[half * SUBLANES + g_lo, :, s_hi * LANES : (s_hi + 1) * LANES])
        y5.append(_swap_sublane_lanegroup(wt_scr[...]))
    y5 = jnp.concatenate(y5, axis=-1)
    s5 = jax.nn.gelu(y5 + d_ref[0] * u_ref[0])
    s5 = s5 * jax.nn.sigmoid(_dot(s5.astype(BF16), gw_ref[0]) + gb_ref[0])
    cat = jnp.concatenate([hg, rt, s5], axis=-1).astype(BF16)
    y = _dot(cat, wo_ref[0])
    m = m_ref[0, 0, 0]
    x1 = _layer_norm(alpha * x_ref[0] + m[2:3, :] * y, lng_ref[0], lnb_ref[0])
    x1_ref[0] = x1
    h2 = x1 * (1.0 + m[4:5, :]) + m[3:4, :]
    logits = _dot3(h2, wr_ref[0]) + br_ref[0]
    bucket, w_lo, w_hi = _route(logits)
    lane = lax.broadcasted_iota(jnp.int32, logits.shape, 1)
    h2_ref[0, :, : h2.shape[1]] = h2
    h2_ref[0, :, h2.shape[1] :] = jnp.where(lane == 0, w_lo, jnp.where(lane == 1, w_hi, 0.0))
    onehot = lane == bucket
    before = _dot(tri_ref[...], onehot.astype(BF16)) + cnt_ref[...]
    rank = jnp.sum(jnp.where(onehot, before, 0.0), axis=-1, keepdims=True)
    route_ref[0] = jnp.where(lane == 0, bucket.astype(F32), jnp.where(lane == 1, rank, 0.0))
    cnt_ref[...] += jnp.sum(onehot.astype(F32), axis=0, keepdims=True)


def _merge(p, o_hf, o_hb, o_rf, o_rb, yg, x, modtab, layer_prm, const_prm, l, tm, nct, alpha):
    b, s, d = x.shape
    cpt = tm // S5_CHUNK

    def tok(width, col):
        return pl.BlockSpec((1, tm, width), lambda i, j: (i, j, col))

    def whole(a):
        return pl.BlockSpec(a.shape, lambda i, j: (0,) * a.ndim)

    rbase = 5 * HG_WIDTH // RET_WIDTH
    in_specs = [tok(HG_WIDTH, 2), tok(RET_WIDTH, rbase + 3), tok(S5_WIDTH, rbase + 4),
                tok(HG_WIDTH, 0), tok(HG_WIDTH, 0), tok(RET_WIDTH, 0), tok(RET_WIDTH, 0),
                pl.BlockSpec((S5_GROUPS, cpt, S5_FOLD), lambda i, j: (0, j, i)),
                tok(d, 0), _mod_spec(l, nct)]
    in_specs += [_layer_spec(a.shape[1:], l) for a in layer_prm]
    in_specs += [whole(a) for a in const_prm]
    return pl.pallas_call(
        functools.partial(_merge_kernel, alpha=alpha),
        grid=(b, s // tm),
        in_specs=in_specs,
        out_specs=[tok(d, 0), tok(d + LANES, 0), tok(LANES, 0), pl.BlockSpec((1, LANES), lambda i, j: (0, 0))],
        out_shape=[jax.ShapeDtypeStruct((b, s, d), F32), jax.ShapeDtypeStruct((b, s, d + LANES), F32),
                   jax.ShapeDtypeStruct((b, s, LANES), F32), jax.ShapeDtypeStruct((1, LANES), F32)],
        scratch_shapes=[pltpu.VMEM((tm, LANES), F32)],
        compiler_params=_cparams(("arbitrary", "arbitrary")),
        name="merge_ln1_router",
    )(p, p, p, o_hf, o_hb, o_rf, o_rb, yg, x, modtab, *layer_prm, *const_prm)


def _routing_tables(route, counts, te, n_tiles):
    bucket = route[..., 0].astype(jnp.int32).reshape(-1)
    rank = route[..., 1].astype(jnp.int32).reshape(-1)
    cnt = counts[0, :N_BUCKETS].astype(jnp.int32)
    padded = (cnt + te - 1) // te * te
    ends = jnp.cumsum(padded)
    pos = (ends - padded)[bucket] + rank
    n_used = ends[-1] // te
    tile = jnp.arange(n_tiles, dtype=jnp.int32)
    tb = jnp.minimum(jnp.searchsorted(ends, tile * te, side="right"), N_BUCKETS - 1).astype(jnp.int32)
    tb = jnp.where(tile < n_used, tb, tb[jnp.maximum(n_used - 1, 0)])
    group, pair = tb // N_PAIRS, tb % N_PAIRS
    ea = group * EXPERTS_PER_GROUP + jnp.asarray(PAIR_LO, jnp.int32)[pair]
    eb = group * EXPERTS_PER_GROUP + jnp.asarray(PAIR_HI, jnp.int32)[pair]
    return pos, ea, eb, n_used.reshape(1).astype(jnp.int32)


def _dispatch_kernel(pos_ref, src_ref, init_ref, out_ref, sem, *, tm, tiles_per_row):
    del init_ref
    base = (pl.program_id(0) * tiles_per_row + pl.program_id(1)) * tm

    def row_copy(r):
        return pltpu.make_async_copy(src_ref.at[0, pl.ds(r, 1)], out_ref.at[pl.ds(pos_ref[base + r], 1)], sem)

    def start(r, carry):
        row_copy(r).start()
        return carry

    def wait(r, carry):
        row_copy(r).wait()
        return carry

    lax.fori_loop(0, tm, start, 0, unroll=DMA_UNROLL)
    lax.fori_loop(0, tm, wait, 0, unroll=DMA_UNROLL)


def _dispatch(pos, rows, n_sorted, tm):
    b, s, w = rows.shape
    return pl.pallas_call(
        functools.partial(_dispatch_kernel, tm=tm, tiles_per_row=s // tm),
        grid_spec=pltpu.PrefetchScalarGridSpec(
            num_scalar_prefetch=1,
            grid=(b, s // tm),
            in_specs=[pl.BlockSpec((1, tm, w), lambda i, j, pos_ref: (i, j, 0)),
                      pl.BlockSpec(memory_space=pl.ANY)],
            out_specs=pl.BlockSpec(memory_space=pl.ANY),
            scratch_shapes=[pltpu.SemaphoreType.DMA(())],
        ),
        out_shape=jax.ShapeDtypeStruct((n_sorted, w), F32),
        input_output_aliases={2: 0},
        compiler_params=_cparams(("arbitrary", "arbitrary")),
        name="moe_dispatch",
    )(pos, rows, jnp.zeros((n_sorted, w), F32))


def _expert_kernel(ea_ref, eb_ref, nused_ref, hs_ref, wga_ref, wua_ref, wda_ref, wgb_ref, wub_ref, wdb_ref,
                   o_ref, wg_scr, wu_scr, wd_scr):
    t = pl.program_id(0)
    prev = jnp.maximum(t - 1, 0)
    changed = (t == 0) | (ea_ref[t] != ea_ref[prev]) | (eb_ref[t] != eb_ref[prev])

    @pl.when(changed)
    def _():
        wg_scr[0] = wga_ref[0, 0].astype(BF16)
        wu_scr[0] = wua_ref[0, 0].astype(BF16)
        wd_scr[0] = wda_ref[0, 0].astype(BF16)
        wg_scr[1] = wgb_ref[0, 0].astype(BF16)
        wu_scr[1] = wub_ref[0, 0].astype(BF16)
        wd_scr[1] = wdb_ref[0, 0].astype(BF16)

    @pl.when(t < nused_ref[0])
    def _():
        d = o_ref.shape[1]
        h = hs_ref[:, :d].astype(BF16)
        y = jnp.zeros(o_ref.shape, F32)
        for e in range(2):
            act = (_silu(_dot(h, wg_scr[e])) * _dot(h, wu_scr[e])).astype(BF16)
            y = y + hs_ref[:, d + e : d + e + 1] * _dot(act, wd_scr[e])
        o_ref[...] = y

    @pl.when(t >= nused_ref[0])
    def _():
        o_ref[...] = jnp.zeros_like(o_ref)


def _experts(ea, eb, n_used, hs, wg, wu, wd, l, te):
    n_sorted, w = hs.shape
    _, _, d, eh = wg.shape

    def wspec(shape, which):
        return pl.BlockSpec((1, 1) + shape, lambda t, ea_ref, eb_ref, n_ref: (l, (ea_ref, eb_ref)[which][t], 0, 0))

    return pl.pallas_call(
        _expert_kernel,
        grid_spec=pltpu.PrefetchScalarGridSpec(
            num_scalar_prefetch=3,
            grid=(n_sorted // te,),
            in_specs=[pl.BlockSpec((te, w), lambda t, *_: (t, 0)),
                      wspec((d, eh), 0), wspec((d, eh), 0), wspec((eh, d), 0),
                      wspec((d, eh), 1), wspec((d, eh), 1), wspec((eh, d), 1)],
            out_specs=pl.BlockSpec((te, d), lambda t, *_: (t, 0)),
            scratch_shapes=[pltpu.VMEM((2, d, eh), BF16), pltpu.VMEM((2, d, eh), BF16), pltpu.VMEM((2, eh, d), BF16)],
        ),
        out_shape=jax.ShapeDtypeStruct((n_sorted, d), F32),
        compiler_params=_cparams(("arbitrary",)),
        name="moe_experts",
    )(ea, eb, n_used, hs, wg, wu, wd, wg, wu, wd)


def _combine_kernel(pos_ref, ys_ref, x1_ref, m_ref, lng_ref, lnb_ref, o_ref, buf, sem,
                    *, tm, tiles_per_row, first_tile, alpha):
    cols = pl.num_programs(1)
    n_steps = pl.num_programs(0) * cols
    step = pl.program_id(0) * cols + pl.program_id(1)
    slot = step % 2

    def row_copy(step_, slot_, r):
        tile = (step_ // cols) * tiles_per_row + first_tile + step_ % cols
        src = ys_ref.at[pl.ds(pos_ref[tile * tm + r], 1)]
        return pltpu.make_async_copy(src, buf.at[slot_, pl.ds(r, 1)], sem.at[slot_])

    def start_tile(step_, slot_):
        def body(r, carry):
            row_copy(step_, slot_, r).start()
            return carry

        lax.fori_loop(0, tm, body, 0, unroll=DMA_UNROLL)

    @pl.when(step == 0)
    def _():
        start_tile(0, 0)

    @pl.when(step + 1 < n_steps)
    def _():
        start_tile(step + 1, 1 - slot)

    def wait(r, carry):
        row_copy(step, slot, r).wait()
        return carry

    lax.fori_loop(0, tm, wait, 0, unroll=DMA_UNROLL)
    m = m_ref[0, 0, 0]
    o_ref[0] = _layer_norm(alpha * x1_ref[0] + m[5:6, :] * buf[slot], lng_ref[0], lnb_ref[0])


def _combine(pos, ys, x1, modtab, ln_g, ln_b, l, tm, nct, alpha, first_tile):
    b, s, d = x1.shape
    cols = s // tm - first_tile
    mod_spec = pl.BlockSpec((1, 1, 1, 6, d),
                            lambda i, j, pos_ref: (l, i, jnp.minimum((j + first_tile) // nct, 1), 0, 0))
    return pl.pallas_call(
        functools.partial(_combine_kernel, tm=tm, tiles_per_row=s // tm, first_tile=first_tile, alpha=alpha),
        grid_spec=pltpu.PrefetchScalarGridSpec(
            num_scalar_prefetch=1,
            grid=(b, cols),
            in_specs=[pl.BlockSpec(memory_space=pl.ANY),
                      pl.BlockSpec((1, tm, d), lambda i, j, pos_ref: (i, j + first_tile, 0)),
                      mod_spec, _layer_spec((1, d), l), _layer_spec((1, d), l)],
            out_specs=pl.BlockSpec((1, tm, d), lambda i, j, pos_ref: (i, j, 0)),
            scratch_shapes=[pltpu.VMEM((2, tm, d), F32), pltpu.SemaphoreType.DMA((2,))],
        ),
        out_shape=jax.ShapeDtypeStruct((b, cols * tm, d), F32),
        compiler_params=_cparams(("arbitrary", "arbitrary")),
        name="moe_combine_ln2",
    )(pos, ys, x1, modtab, ln_g, ln_b)


def _block_avg(width, group):
    idx = np.arange(width) // group
    return jnp.asarray((idx[:, None] == idx[None, :]).astype(np.float32) / group, dtype=BF16)


def kernel(x, c, ctx, c_ctx, w_mod, b_mod, w_in, hg_lb_raw, hg_norm_g, ret_decay_raw, ret_gn_g, ret_gn_b, s5_lam_re, s5_lam_im, s5_log_dt, s5_b_re, s5_b_im, s5_c_re, s5_c_im, s5_d, s5_glu_w, s5_glu_b, w_out, ln1_g, ln1_b, ln2_g, ln2_b, rg_w, rg_b, re_w, re_b, exp_w_gate, exp_w_up, exp_w_down):
    bsz, t_lat, d = x.shape
    t_ctx = ctx.shape[1]
    depth = w_mod.shape[0]
    alpha = (2.0 * depth) ** 0.25
    tm = TOKEN_TILE
    assert d == D_MODEL and t_lat % GRID_W == 0 and bsz < SUBLANES
    assert t_ctx % tm == 0 and t_lat % tm == 0, "context and latent lengths must be multiples of the token tile"
    nct = t_ctx // tm
    s = t_ctx + t_lat

    cvec = jnp.concatenate([c, c_ctx[None, :], jnp.zeros((SUBLANES - bsz - 1, d), F32)], 0)
    mod_all = _modulation(cvec, w_mod, b_mod)
    lat = mod_all[:, :bsz].reshape(depth, bsz, 6, d)
    cm = jnp.broadcast_to(mod_all[:, bsz].reshape(depth, 1, 6, d), (depth, bsz, 6, d))
    modtab = jnp.stack([cm, lat], axis=2)

    hg_lb = jnp.cumsum(jax.nn.softmax(hg_lb_raw.astype(F32), axis=0), axis=0)
    hg_lb = hg_lb - hg_lb[:1]
    ret_tables = _retention_decay_tables(jax.nn.log_sigmoid(ret_decay_raw.astype(F32)))
    cos_tab, sin_tab = _rope_tables(t_lat, t_ctx)
    s5_tabs = _s5_weights(s5_lam_re, s5_lam_im, s5_log_dt, s5_b_re, s5_b_im, s5_c_re, s5_c_im)
    pad_r = LANES - N_GROUPS - N_EXPERTS
    wr = jnp.concatenate([rg_w, re_w.reshape(depth, d, N_EXPERTS), jnp.zeros((depth, d, pad_r), F32)], axis=2)
    br = jnp.concatenate([rg_b, re_b.reshape(depth, N_EXPERTS), jnp.zeros((depth, pad_r), F32)], axis=1)[:, None, :]
    row = lambda a: a[:, None, :]
    layer_prm = [row(jnp.tile(hg_norm_g, (1, HG_HEADS))), row(ret_gn_g), row(ret_gn_b), row(s5_d),
                 s5_glu_w.astype(BF16), row(s5_glu_b), w_out.astype(BF16), row(ln1_g), row(ln1_b), wr, br]
    tri = jnp.asarray(np.tril(np.ones((tm, tm), np.float32), -1), dtype=BF16)
    const_prm = [_block_avg(HG_WIDTH, HG_DK), _block_avg(RET_WIDTH, RET_DK), tri]
    w_in_bf16 = w_in.astype(BF16)
    ln2_g, ln2_b = row(ln2_g), row(ln2_b)
    te = EXPERT_TILE
    n_tiles = -(-(bsz * s + N_BUCKETS * (te - 1)) // te)

    xs = jnp.concatenate([ctx, x], axis=1)
    for l in range(depth):
        p, ug = _inproj(xs, modtab, w_in_bf16, l, tm, nct)
        o_hf, o_hb = _hgrn(p, hg_lb, l, t_ctx // CHUNK)
        o_rf, o_rb = _retention(p, cos_tab, sin_tab, ret_tables, l, t_ctx // RET_CHUNK)
        yg = _s5_conv(ug, *s5_tabs, l, bsz, t_ctx // S5_CHUNK)
        x1, rows, route, counts = _merge(p, o_hf, o_hb, o_rf, o_rb, yg, xs, modtab, layer_prm, const_prm,
                                         l, tm, nct, alpha)
        pos, ea, eb, n_used = _routing_tables(route, counts, te, n_tiles)
        hs = _dispatch(pos, rows, n_tiles * te, tm)
        ys = _experts(ea, eb, n_used, hs, exp_w_gate, exp_w_up, exp_w_down, l, te)
        xs = _combine(pos, ys, x1, modtab, ln2_g, ln2_b, l, tm, nct, alpha, nct if l == depth - 1 else 0)
    return xs
```

```python
import functools

import numpy as np
import jax
import jax.numpy as jnp
from jax import lax
from jax.experimental import pallas as pl
from jax.experimental.pallas import tpu as pltpu

F32 = jnp.float32
BF16 = jnp.bfloat16

D_MODEL = 1024
HG_WIDTH = 512
HG_HEADS = 4
HG_DK = HG_WIDTH // HG_HEADS
RET_WIDTH = 256
RET_HEADS = 4
RET_DK = RET_WIDTH // RET_HEADS
S5_WIDTH = 256
S5_GROUP_CH = 16
S5_GROUPS = S5_WIDTH // S5_GROUP_CH
S5_STATE = 64
IN_COLS = 5 * HG_WIDTH + 4 * RET_WIDTH + S5_WIDTH
CHUNK = 64
HGRN_BATCH = 4
RET_CHUNK = 256
N_GROUPS = 4
EXPERTS_PER_GROUP = 4
N_EXPERTS = N_GROUPS * EXPERTS_PER_GROUP
N_PAIRS = EXPERTS_PER_GROUP * (EXPERTS_PER_GROUP - 1) // 2
N_BUCKETS = N_GROUPS * N_PAIRS
PAIR_LO = (0, 0, 0, 1, 1, 2)
PAIR_HI = (1, 2, 3, 2, 3, 3)
EXPERT_HIDDEN = D_MODEL // 2
LN_EPS = 1e-5
ROPE_BASE = 10000.0
GRID_W = 64

LANES = 128
SUBLANES = 8
TOKEN_TILE = 256
S5_CHUNK = 32
S5_FOLD = S5_CHUNK * S5_GROUP_CH
EXPERT_TILE = 256
DMA_UNROLL = 8
VMEM_LIMIT = 56 * 1024 * 1024


def _cparams(sem):
    return pltpu.CompilerParams(dimension_semantics=sem, vmem_limit_bytes=VMEM_LIMIT)


def _split_bf16(x):
    hi = x.astype(BF16)
    lo = (x - hi.astype(F32)).astype(BF16)
    return hi, lo


def _dot(a, b):
    return jnp.dot(a, b, preferred_element_type=F32)


def _dot3(a, b):
    ah, al = _split_bf16(a)
    bh, bl = _split_bf16(b)
    return _dot(ah, bh) + _dot(ah, bl) + _dot(al, bh)


def _dot2(a, b_bf16):
    ah, al = _split_bf16(a)
    return _dot(ah, b_bf16) + _dot(al, b_bf16)


def _dot_nt(a, b):
    return lax.dot_general(a, b, (((1,), (1,)), ((), ())), preferred_element_type=F32)


def _dot_tn(a, b):
    return lax.dot_general(a, b, (((0,), (0,)), ((), ())), preferred_element_type=F32)


def _silu(x):
    return x * jax.nn.sigmoid(x)


def _layer_spec(shape, l):
    zeros = (0,) * len(shape)
    return pl.BlockSpec((1,) + tuple(shape), lambda *_: (l,) + zeros)


def _mod_spec(l, nct):
    return pl.BlockSpec((1, 1, 1, 6, D_MODEL), lambda i, j, *_: (l, i, jnp.minimum(j // nct, 1), 0, 0))


def _swap_sublane_lanegroup(v):
    n = v.shape[0]
    r = lax.broadcasted_iota(jnp.int32, v.shape, 0)
    l = lax.broadcasted_iota(jnp.int32, v.shape, 1)
    for k in range(3):
        rb = (r >> k) & 1
        gb = (l >> (4 + k)) & 1
        sh = S5_GROUP_CH << k
        st = 1 << k
        a = pltpu.roll(pltpu.roll(v, LANES - sh, 1), st, 0)
        b = pltpu.roll(pltpu.roll(v, sh, 1), n - st, 0)
        v = jnp.where(rb == gb, v, jnp.where(rb == 1, a, b))
    return v


def _mod_kernel(c_ref, w_ref, b_ref, o_ref):
    sc = _silu(c_ref[...])
    o_ref[0] = _dot3(sc, w_ref[0]) + b_ref[0]


def _modulation(cvec, w_mod, b_mod):
    depth, d, n = w_mod.shape
    rows = cvec.shape[0]
    tn = 1536
    return pl.pallas_call(
        _mod_kernel,
        grid=(depth, n // tn),
        in_specs=[
            pl.BlockSpec((rows, d), lambda l, j: (0, 0)),
            pl.BlockSpec((1, d, tn), lambda l, j: (l, 0, j)),
            pl.BlockSpec((1, 1, tn), lambda l, j: (l, 0, j)),
        ],
        out_specs=pl.BlockSpec((1, rows, tn), lambda l, j: (l, 0, j)),
        out_shape=jax.ShapeDtypeStruct((depth, rows, n), F32),
        compiler_params=_cparams(("arbitrary", "arbitrary")),
        name="modulation",
    )(cvec, w_mod, b_mod.reshape(depth, 1, n))


def _inproj_kernel(x_ref, m_ref, w_ref, o_ref, ug_ref, wt_scr):
    m = m_ref[0, 0, 0]
    h = x_ref[0] * (1.0 + m[1:2, :]) + m[0:1, :]
    p = _dot(h.astype(BF16), w_ref[0])
    o_ref[0] = p
    rows_per_chunk = S5_CHUNK // SUBLANES
    n_chunks = p.shape[0] // S5_CHUNK
    for half in range(S5_WIDTH // LANES):
        lo = IN_COLS - S5_WIDTH + half * LANES
        wt_scr[...] = _swap_sublane_lanegroup(p[:, lo : lo + LANES])
        for g_lo in range(SUBLANES):
            for s_hi in range(rows_per_chunk):
                piece = wt_scr[pl.ds(s_hi * SUBLANES + g_lo, n_chunks, stride=S5_CHUNK), :]
                ug_ref[half * SUBLANES + g_lo, :, s_hi * LANES : (s_hi + 1) * LANES] = piece


def _inproj(x, modtab, w_in_bf16, l, tm, nct):
    b, s, d = x.shape
    n = w_in_bf16.shape[-1]
    cpt = tm // S5_CHUNK
    return pl.pallas_call(
        _inproj_kernel,
        grid=(b, s // tm),
        in_specs=[
            pl.BlockSpec((1, tm, d), lambda i, j: (i, j, 0)),
            _mod_spec(l, nct),
            _layer_spec((d, n), l),
        ],
        out_specs=[pl.BlockSpec((1, tm, n), lambda i, j: (i, j, 0)),
                   pl.BlockSpec((S5_GROUPS, cpt, S5_FOLD), lambda i, j: (0, j, i))],
        out_shape=[jax.ShapeDtypeStruct((b, s, n), F32),
                   jax.ShapeDtypeStruct((S5_GROUPS, s // S5_CHUNK, b * S5_FOLD), F32)],
        scratch_shapes=[pltpu.VMEM((tm, LANES), F32)],
        compiler_params=_cparams(("arbitrary", "arbitrary")),
        name="inproj",
    )(x, modtab, w_in_bf16)


def _block_gate_products(f, reverse):
    c = f.shape[0]
    row = lax.broadcasted_iota(jnp.int32, (c, 1), 0)
    a, z, b = f, f, jnp.ones_like(f)
    out = []
    s = 1
    while s < c:
        out.append((a, b))
        if s < SUBLANES:
            z3 = z.reshape(c // SUBLANES, SUBLANES, z.shape[1])
            up = pltpu.roll(z3, s, 1).reshape(z.shape)
            dn = pltpu.roll(z3, SUBLANES - s, 1).reshape(z.shape)
            odd = (row & s) != 0
            if reverse:
                a = a * jnp.where(odd, 1.0, dn)
                b = b * jnp.where(odd, up, 1.0)
            else:
                a = a * jnp.where(odd, up, 1.0)
                b = b * jnp.where(odd, 1.0, dn)
            z = z * jnp.where(odd, up, dn)
        else:
            na, nb, nz = [], [], []
            for lo in range(0, c, 2 * s):
                ev, od = slice(lo, lo + s), slice(lo + s, lo + 2 * s)
                zz = z[ev] * z[od]
                if reverse:
                    na += [a[ev] * z[od], a[od]]
                    nb += [b[ev], b[od] * z[ev]]
                else:
                    na += [a[ev], a[od] * z[ev]]
                    nb += [b[ev] * z[od], b[od]]
                nz += [zz, zz]
            a, b, z = (jnp.concatenate(t, axis=0) for t in (na, nb, nz))
        s *= 2
    out.append((a, b))
    return out, z


def _hgrn_kernel(qf_ref, vf_ref, zf_ref, qb_ref, vb_ref, zb_ref, lb_ref, of_ref, ob_ref, st_ref):
    c = CHUNK

    @pl.when(pl.program_id(1) == 0)
    def _():
        st_ref[...] = jnp.zeros_like(st_ref)

    ri = lax.broadcasted_iota(jnp.int32, (c, c), 0)
    ci = lax.broadcasted_iota(jnp.int32, (c, c), 1)
    for d, (q_ref, v_ref, z_ref, o_ref) in enumerate(
        ((qf_ref, vf_ref, zf_ref, of_ref), (qb_ref, vb_ref, zb_ref, ob_ref))
    ):
        reverse = d == 1
        causal = (ri < ci) if reverse else (ri > ci)
        n_levels = c.bit_length() - 1
        masks = [(((ri >> lvl) ^ (ci >> lvl)) == 1) & causal for lvl in range(n_levels)]
        for bi, h in [(bi, h) for bi in range(q_ref.shape[0]) for h in range(HG_HEADS)]:
            hs = slice(h * HG_DK, (h + 1) * HG_DK)
            q = _silu(q_ref[bi, :, hs])
            v = v_ref[bi, :, hs].astype(BF16)
            lb = lb_ref[0, d : d + 1, hs]
            f = lb + (1.0 - lb) * jax.nn.sigmoid(z_ref[bi, :, hs])
            k = 1.0 - f
            levels, tot = _block_gate_products(f, reverse)
            scores = jnp.where(ri == ci, _dot_nt(q.astype(BF16), k.astype(BF16)), 0.0)
            for mask, (a, bb) in zip(masks, levels):
                scores = scores + jnp.where(mask, _dot_nt((q * a).astype(BF16), (k * bb).astype(BF16)), 0.0)
            a_full, b_full = levels[-1]
            st = st_ref[bi, d, h]
            o_ref[bi, :, hs] = (_dot(scores.astype(BF16), v)
                                + _dot_nt((q * a_full).astype(BF16), st.astype(BF16)))
            st_ref[bi, d, h] = st * tot[0:1, :] + _dot_tn(v, (k * b_full).astype(BF16))


def _bwd_chunk(n, nc_ctx, nc_all):
    return jnp.where(n < nc_ctx, nc_ctx - 1 - n, nc_all + nc_ctx - 1 - n)


def _hgrn(p, lb_all, l, nc_ctx):
    b, s, _ = p.shape
    nc = s // CHUNK
    w = HG_WIDTH

    bb = HGRN_BATCH if b % HGRN_BATCH == 0 else 1

    def fwd(col):
        return pl.BlockSpec((bb, CHUNK, w), lambda i, n: (i, n, col))

    def bwd(col):
        return pl.BlockSpec((bb, CHUNK, w), lambda i, n: (i, _bwd_chunk(n, nc_ctx, nc), col))

    out = jax.ShapeDtypeStruct((b, s, w), F32)
    return pl.pallas_call(
        _hgrn_kernel,
        grid=(b // bb, nc),
        in_specs=[fwd(0), fwd(1), fwd(3), bwd(0), bwd(1), bwd(4), _layer_spec((2, w), l)],
        out_specs=[fwd(0), bwd(0)],
        out_shape=[out, out],
        scratch_shapes=[pltpu.VMEM((bb, 2, HG_HEADS, HG_DK, HG_DK), F32)],
        compiler_params=_cparams(("arbitrary", "arbitrary")),
        name="hgrn2_scan",
    )(p, p, p, p, p, p, lb_all)


def _swap_halves(x, half):
    n = x.shape[-1]
    lane = lax.broadcasted_iota(jnp.int32, (1, n), 1)
    lower = (lane & half) == 0
    return jnp.where(lower, pltpu.roll(x, n - half, 1), pltpu.roll(x, half, 1))


def _ret_kernel(qf_ref, kf_ref, vf_ref, cf_ref, sf_ref, qb_ref, kb_ref, vb_ref, cb_ref, sb_ref,
                dmat_ref, rq_ref, rk_ref, cd_ref, of_ref, ob_ref, st_ref):
    @pl.when(pl.program_id(1) == 0)
    def _():
        st_ref[...] = jnp.zeros_like(st_ref)

    half = RET_DK // 4
    for d, (q_ref, k_ref, v_ref, c_ref, s_ref, o_ref) in enumerate(
        ((qf_ref, kf_ref, vf_ref, cf_ref, sf_ref, of_ref), (qb_ref, kb_ref, vb_ref, cb_ref, sb_ref, ob_ref))
    ):
        cos = c_ref[...]
        sin = s_ref[...]
        q = q_ref[0]
        k = k_ref[0] * (RET_DK ** -0.5)
        q = q * cos + _swap_halves(q, half) * sin
        k = k * cos + _swap_halves(k, half) * sin
        v = v_ref[0].astype(BF16)
        q0 = q.astype(BF16)
        k0 = k.astype(BF16)
        qd = (q * rq_ref[0, d]).astype(BF16)
        kd = (k * rk_ref[0, d]).astype(BF16)
        cd = cd_ref[0, d : d + 1, :]
        for h in range(RET_HEADS):
            hs = slice(h * RET_DK, (h + 1) * RET_DK)
            scores = _dot_nt(q0[:, hs], k0[:, hs]) * dmat_ref[0, d, h]
            st = st_ref[d, h]
            o_ref[0, :, hs] = _dot(scores.astype(BF16), v[:, hs]) + _dot_nt(qd[:, hs], st.astype(BF16))
            st_ref[d, h] = st * cd[:, hs] + _dot_tn(v[:, hs], kd[:, hs])


def _retention(p, cos_tab, sin_tab, tables, l, nc_ctx):
    b, s, _ = p.shape
    nc = s // RET_CHUNK
    w = RET_WIDTH
    base = 5 * HG_WIDTH // w

    def fwd(col):
        return pl.BlockSpec((1, RET_CHUNK, w), lambda i, n: (i, n, col))

    def bwd(col):
        return pl.BlockSpec((1, RET_CHUNK, w), lambda i, n: (i, _bwd_chunk(n, nc_ctx, nc), col))

    tab_f = pl.BlockSpec((RET_CHUNK, w), lambda i, n: (n, 0))
    tab_b = pl.BlockSpec((RET_CHUNK, w), lambda i, n: (_bwd_chunk(n, nc_ctx, nc), 0))
    out = jax.ShapeDtypeStruct((b, s, w), F32)
    return pl.pallas_call(
        _ret_kernel,
        grid=(b, nc),
        in_specs=[fwd(base), fwd(base + 1), fwd(base + 2), tab_f, tab_f,
                  bwd(base), bwd(base + 1), bwd(base + 2), tab_b, tab_b]
                 + [_layer_spec(t.shape[1:], l) for t in tables],
        out_specs=[fwd(0), bwd(0)],
        out_shape=[out, out],
        scratch_shapes=[pltpu.VMEM((2, RET_HEADS, RET_DK, RET_DK), F32)],
        compiler_params=_cparams(("arbitrary", "arbitrary")),
        name="retention_scan",
    )(p, p, p, cos_tab, sin_tab, p, p, p, cos_tab, sin_tab, *tables)


def _rope_tables(t_lat, t_ctx):
    m = RET_DK // 4
    inv = ROPE_BASE ** (-jnp.arange(m, dtype=F32) / m)
    rows = jnp.repeat(jnp.arange(t_lat // GRID_W, dtype=jnp.int32), GRID_W).astype(F32)
    cols = jnp.tile(jnp.arange(GRID_W, dtype=jnp.int32), t_lat // GRID_W).astype(F32)

    def half_tables(pos):
        ang = pos[:, None] * inv
        c, s = jnp.cos(ang), jnp.sin(ang)
        return jnp.concatenate([c, c], -1), jnp.concatenate([-s, s], -1)

    cr, sr = half_tables(rows)
    cc, sc = half_tables(cols)
    cos_h = jnp.concatenate([cr, cc], -1)
    sin_h = jnp.concatenate([sr, sc], -1)
    cos = jnp.tile(cos_h, (1, RET_HEADS))
    sin = jnp.tile(sin_h, (1, RET_HEADS))
    cos = jnp.concatenate([jnp.ones((t_ctx, RET_WIDTH), F32), cos], 0)
    sin = jnp.concatenate([jnp.zeros((t_ctx, RET_WIDTH), F32), sin], 0)
    return cos, sin


def _retention_decay_tables(log_gamma):
    c = RET_CHUNK
    i = jnp.arange(c, dtype=F32)
    diff = i[:, None] - i[None, :]
    lg = log_gamma[:, :, :, None, None]
    d_f = jnp.where(diff >= 0, jnp.exp(lg[:, 0] * diff), 0.0)
    d_b = jnp.where(diff <= 0, jnp.exp(lg[:, 1] * (-diff)), 0.0)
    dmat = jnp.stack([d_f, d_b], 1)
    lane_lg = jnp.repeat(log_gamma, RET_DK, axis=2)[:, :, None, :]
    col = i[None, :, None]
    rq = jnp.stack([jnp.exp(lane_lg[:, 0] * (col + 1.0)), jnp.exp(lane_lg[:, 1] * (c - col))], 1)
    rk = jnp.stack([jnp.exp(lane_lg[:, 0] * (c - 1.0 - col)), jnp.exp(lane_lg[:, 1] * col)], 1)
    cdec = jnp.exp(lane_lg[:, :, 0, :] * c)
    return dmat, rq, rk, cdec


def _s5_kernel(u_ref, base_ref, inj_ref, w2_ref, ac_ref, y_ref, toep_scr, a_scr, i_scr, hf_scr, hb_scr,
               *, n_ctx, n_all, bsz):
    width = S5_FOLD
    st2 = 2 * S5_STATE
    ch = S5_GROUP_CH
    lane = lax.broadcasted_iota(jnp.int32, (ch, width), 1)
    base_f = base_ref[0, 0, 0]
    base_b = base_ref[0, 0, 1]
    for s in range(S5_CHUNK):
        sh_f = s * ch
        sh_b = (S5_CHUNK - 1 - s) * ch
        part_f = jnp.where(lane >= sh_f, pltpu.roll(base_f, sh_f, 1), 0.0) if sh_f else base_f
        part_b = jnp.where(lane < width - sh_b, pltpu.roll(base_b, width - sh_b, 1), 0.0) if sh_b else base_b
        toep_scr[s * ch : (s + 1) * ch, :] = (part_f + part_b).astype(BF16)
    for b in range(bsz):
        ub = u_ref[0, :, b * width : (b + 1) * width].astype(BF16)
        a_scr[b] = _dot(ub, toep_scr[...])
        i_scr[b] = _dot(ub, inj_ref[0, 0])
    ac = ac_ref[0, 0]

    zero = jnp.zeros((1, st2), F32)
    state = [(zero, zero)] * (2 * bsz)
    for n in range(n_all):
        nb = n_ctx - 1 - n if n < n_ctx else n_all + n_ctx - 1 - n
        for b in range(bsz):
            for d, (row, scr) in enumerate(((n, hf_scr), (nb, hb_scr))):
                h, hsw = state[2 * b + d]
                scr[b, row : row + 1, :] = h
                inj = i_scr[b, row : row + 1, d * st2 : (d + 1) * st2]
                inj_sw = i_scr[b, row : row + 1, (d + 2) * st2 : (d + 3) * st2]
                ar, ai = ac[2 * d : 2 * d + 1, :], ac[2 * d + 1 : 2 * d + 2, :]
                state[2 * b + d] = (h * ar + hsw * ai + inj, hsw * ar - h * ai + inj_sw)
    for b in range(bsz):
        y_ref[0, :, b * width : (b + 1) * width] = (
            a_scr[b]
            + _dot(hf_scr[b].astype(BF16), w2_ref[0, 0, 0])
            + _dot(hb_scr[b].astype(BF16), w2_ref[0, 0, 1]))


def _s5_conv(ug, base, inj, w2, ac, l, bsz, n_ctx):
    g, n_all, _ = ug.shape
    width = S5_FOLD
    st2 = 2 * S5_STATE
    return pl.pallas_call(
        functools.partial(_s5_kernel, n_ctx=n_ctx, n_all=n_all, bsz=bsz),
        grid=(g,),
        in_specs=[
            pl.BlockSpec((1, n_all, bsz * width), lambda i: (i, 0, 0)),
            pl.BlockSpec((1, 1, 2, S5_GROUP_CH, width), lambda i: (l, i, 0, 0, 0)),
            pl.BlockSpec((1, 1, width, 4 * st2), lambda i: (l, i, 0, 0)),
            pl.BlockSpec((1, 1, 2, st2, width), lambda i: (l, i, 0, 0, 0)),
            pl.BlockSpec((1, 1, SUBLANES, st2), lambda i: (l, i, 0, 0)),
        ],
        out_specs=pl.BlockSpec((1, n_all, bsz * width), lambda i: (i, 0, 0)),
        out_shape=jax.ShapeDtypeStruct((g, n_all, bsz * width), F32),
        scratch_shapes=[pltpu.VMEM((width, width), BF16),
                        pltpu.VMEM((bsz, n_all, width), F32), pltpu.VMEM((bsz, n_all, 4 * st2), F32),
                        pltpu.VMEM((bsz, n_all, st2), F32), pltpu.VMEM((bsz, n_all, st2), F32)],
        compiler_params=_cparams(("arbitrary",)),
        name="s5_conv",
    )(ug, base, inj, w2, ac)


def _s5_weights(lam_re, lam_im, log_dt, b_re, b_im, c_re, c_im):
    cs, ch, p, g = S5_CHUNK, S5_GROUP_CH, S5_STATE, S5_GROUPS
    nl = lam_re.shape[0]
    lam_re = jnp.minimum(lam_re.astype(F32), -1e-4)
    lam_im = lam_im.astype(F32)
    dt = jnp.exp(log_dt.astype(F32))[..., None]
    mag = jnp.exp(dt * lam_re)
    abar_re, abar_im = mag * jnp.cos(dt * lam_im), mag * jnp.sin(dt * lam_im)
    den = jnp.square(lam_re) + jnp.square(lam_im)
    nr, ni = abar_re - 1.0, abar_im
    coef_re = ((nr * lam_re + ni * lam_im) / den)[..., None]
    coef_im = ((ni * lam_re - nr * lam_im) / den)[..., None]
    b_re, b_im = b_re.astype(F32), b_im.astype(F32)
    bb_re = coef_re * b_re - coef_im * b_im
    bb_im = coef_re * b_im + coef_im * b_re
    ct_re = jnp.swapaxes(c_re.astype(F32), -1, -2)
    ct_im = jnp.swapaxes(c_im.astype(F32), -1, -2)

    def powers(expo):
        e = expo.astype(F32)[None, :, None, None, :]
        m = jnp.exp(e * (dt * lam_re)[..., None])
        return m * jnp.cos(e * (dt * lam_im)[..., None]), m * jnp.sin(e * (dt * lam_im)[..., None])

    slots = jnp.arange(cs + 1)
    pr, pi = powers(jnp.stack([slots, cs - slots]))
    r_re = (pr[..., None] * ct_re[..., None, :] - pi[..., None] * ct_im[..., None, :]).reshape(nl, 2, g, p, (cs + 1) * ch)
    r_im = (pr[..., None] * ct_im[..., None, :] + pi[..., None] * ct_re[..., None, :]).reshape(nl, 2, g, p, (cs + 1) * ch)
    hi = lax.Precision.HIGHEST
    kern = (jnp.einsum('ldgpe,ldgpn->ldgen', bb_re, r_re, precision=hi)
            - jnp.einsum('ldgpe,ldgpn->ldgen', bb_im, r_im, precision=hi))
    w = cs * ch
    base = jnp.stack([kern[:, 0, :, :, :w], kern[:, 1, :, :, ch:]], axis=2)
    w2 = jnp.stack([jnp.concatenate([r_re[:, 0, :, :, ch:], -r_im[:, 0, :, :, ch:]], axis=2),
                    jnp.concatenate([r_re[:, 1, :, :, :w], -r_im[:, 1, :, :, :w]], axis=2)], axis=2)
    steps = jnp.arange(cs)
    qr, qi = powers(jnp.stack([cs - 1 - steps, steps]))
    qr = jnp.swapaxes(qr, -1, -2)[..., :, None, :]
    qi = jnp.swapaxes(qi, -1, -2)[..., :, None, :]
    bt_re = jnp.swapaxes(bb_re, -1, -2)[:, :, :, None]
    bt_im = jnp.swapaxes(bb_im, -1, -2)[:, :, :, None]
    ab_re = (qr * bt_re - qi * bt_im).reshape(nl, 2, g, w, p)
    ab_im = (qr * bt_im + qi * bt_re).reshape(nl, 2, g, w, p)
    inj = jnp.concatenate([ab_re[:, 0], ab_im[:, 0], ab_re[:, 1], ab_im[:, 1],
                           ab_im[:, 0], ab_re[:, 0], ab_im[:, 1], ab_re[:, 1]], axis=-1)
    ac_rows = []
    for d, slot in enumerate((cs, 0)):
        ar, ai = pr[:, d, :, :, slot], pi[:, d, :, :, slot]
        ac_rows += [jnp.concatenate([ar, ar], -1), jnp.concatenate([-ai, ai], -1)]
    ac = jnp.stack(ac_rows + [jnp.zeros_like(ac_rows[0])] * (SUBLANES - 4), axis=2)
    return base, inj.astype(BF16), w2.astype(BF16), ac


def _layer_norm(x, g, b):
    mu = jnp.mean(x, -1, keepdims=True)
    xc = x - mu
    var = jnp.mean(xc * xc, -1, keepdims=True)
    return xc * lax.rsqrt(var + LN_EPS) * g + b


def _route(logits_t):
    col = lambda i: logits_t[i : i + 1, :]
    gl = [col(i) for i in range(N_GROUPS)]
    gmax = functools.reduce(jnp.maximum, gl)
    g_idx = jnp.full_like(gmax, N_GROUPS - 1).astype(jnp.int32)
    for i in reversed(range(N_GROUPS - 1)):
        g_idx = jnp.where(gl[i] == gmax, i, g_idx)
    g_p = 1.0 / functools.reduce(lambda a, b: a + b, [jnp.exp(x - gmax) for x in gl])
    el = []
    for e in range(EXPERTS_PER_GROUP):
        v = col(N_GROUPS + (N_GROUPS - 1) * EXPERTS_PER_GROUP + e)
        for g in reversed(range(N_GROUPS - 1)):
            v = jnp.where(g_idx == g, col(N_GROUPS + g * EXPERTS_PER_GROUP + e), v)
        el.append(v)
    m1 = functools.reduce(jnp.maximum, el)
    i1 = jnp.full_like(g_idx, EXPERTS_PER_GROUP - 1)
    for e in reversed(range(EXPERTS_PER_GROUP - 1)):
        i1 = jnp.where(el[e] == m1, e, i1)
    rest = [jnp.where(i1 == e, -jnp.inf, el[e]) for e in range(EXPERTS_PER_GROUP)]
    m2 = functools.reduce(jnp.maximum, rest)
    i2 = jnp.full_like(g_idx, EXPERTS_PER_GROUP - 1)
    for e in reversed(range(EXPERTS_PER_GROUP - 1)):
        i2 = jnp.where((rest[e] == m2) & (i1 != e), e, i2)
    t = jnp.exp(m2 - m1)
    w1 = g_p / (1.0 + t)
    w2 = g_p * t / (1.0 + t)
    lo = jnp.minimum(i1, i2)
    hi = jnp.maximum(i1, i2)
    pair = jnp.where(lo == 0, hi - 1, jnp.where(lo == 1, hi + 1, N_PAIRS - 1))
    bucket = g_idx * N_PAIRS + pair
    w_lo = jnp.where(i1 < i2, w1, w2)
    w_hi = jnp.where(i1 < i2, w2, w1)
    return bucket, w_lo, w_hi


def _merge_kernel(gate_ref, rg_ref, u_ref, hf_ref, hb_ref, rf_ref, rb_ref, yg_ref, x_ref, m_ref,
                  hgn_ref, gng_ref, gnb_ref, d_ref, gw_ref, gb_ref, wo_ref, lng_ref, lnb_ref,
                  wrh_ref, wrl_ref, br_ref, a128_ref, a64_ref, tri_ref, x1_ref, h2_ref, route_ref, cnt_ref, wt_scr,
                  *, alpha):
    @pl.when((pl.program_id(0) == 0) & (pl.program_id(1) == 0))
    def _():
        cnt_ref[...] = jnp.zeros_like(cnt_ref)

    o_hg = hf_ref[0] + hb_ref[0]
    ms = _dot2(o_hg * o_hg, a128_ref[...])
    hg = o_hg * lax.rsqrt(ms + LN_EPS) * hgn_ref[0] * _silu(gate_ref[0])
    o_rt = rf_ref[0] + rb_ref[0]
    mu = _dot2(o_rt, a64_ref[...])
    xc = o_rt - mu
    var = _dot2(xc * xc, a64_ref[...])
    rt = (xc * lax.rsqrt(var + LN_EPS) * gng_ref[0] + gnb_ref[0]) * _silu(rg_ref[0])
    rows_per_chunk = S5_CHUNK // SUBLANES
    n_chunks = yg_ref.shape[1]
    y5 = []
    for half in range(S5_WIDTH // LANES):
        for g_lo in range(SUBLANES):
            for s_hi in range(rows_per_chunk):
                wt_scr[pl.ds(s_hi * SUBLANES + g_lo, n_chunks, stride=S5_CHUNK), :] = (
                    yg_ref[half * SUBLANES + g_lo, :, s_hi * LANES : (s_hi + 1) * LANES])
        y5.append(_swap_sublane_lanegroup(wt_scr[...]))
    y5 = jnp.concatenate(y5, axis=-1)
    s5 = jax.nn.gelu(y5 + d_ref[0] * u_ref[0])
    s5 = s5 * jax.nn.sigmoid(_dot(s5.astype(BF16), gw_ref[0]) + gb_ref[0])
    cat = jnp.concatenate([hg, rt, s5], axis=-1).astype(BF16)
    y = _dot(cat, wo_ref[0])
    m = m_ref[0, 0, 0]
    x1 = _layer_norm(alpha * x_ref[0] + m[2:3, :] * y, lng_ref[0], lnb_ref[0])
    x1_ref[0] = x1
    h2 = x1 * (1.0 + m[4:5, :]) + m[3:4, :]
    h_hi, h_lo = _split_bf16(h2)
    logits_t = (_dot_nt(wrh_ref[0], h_hi) + _dot_nt(wrh_ref[0], h_lo) + _dot_nt(wrl_ref[0], h_hi)) + br_ref[0]
    bucket, w_lo, w_hi = _route(logits_t)
    sub = lax.broadcasted_iota(jnp.int32, logits_t.shape, 0)
    h2_ref[0, :, : h2.shape[1]] = h2
    h2_ref[0, :, h2.shape[1] :] = jnp.where(sub == 0, w_lo, jnp.where(sub == 1, w_hi, 0.0)).T
    onehot = sub == bucket
    before = _dot(onehot.astype(BF16), tri_ref[...]) + cnt_ref[...]
    rank = jnp.sum(jnp.where(onehot, before, 0.0), axis=0, keepdims=True)
    rsub = lax.broadcasted_iota(jnp.int32, route_ref.shape[1:], 0)
    route_ref[0] = jnp.where(rsub == 0, bucket.astype(F32), jnp.where(rsub == 1, rank, 0.0))
    cnt_ref[...] += jnp.sum(onehot.astype(F32), axis=1, keepdims=True)


def _merge(p, o_hf, o_hb, o_rf, o_rb, yg, x, modtab, layer_prm, const_prm, l, tm, nct, alpha):
    b, s, d = x.shape
    cpt = tm // S5_CHUNK

    def tok(width, col):
        return pl.BlockSpec((1, tm, width), lambda i, j: (i, j, col))

    def whole(a):
        return pl.BlockSpec(a.shape, lambda i, j: (0,) * a.ndim)

    rbase = 5 * HG_WIDTH // RET_WIDTH
    in_specs = [tok(HG_WIDTH, 2), tok(RET_WIDTH, rbase + 3), tok(S5_WIDTH, rbase + 4),
                tok(HG_WIDTH, 0), tok(HG_WIDTH, 0), tok(RET_WIDTH, 0), tok(RET_WIDTH, 0),
                pl.BlockSpec((S5_GROUPS, cpt, S5_FOLD), lambda i, j: (0, j, i)),
                tok(d, 0), _mod_spec(l, nct)]
    in_specs += [_layer_spec(a.shape[1:], l) for a in layer_prm]
    in_specs += [whole(a) for a in const_prm]
    return pl.pallas_call(
        functools.partial(_merge_kernel, alpha=alpha),
        grid=(b, s // tm),
        in_specs=in_specs,
        out_specs=[tok(d, 0), tok(d + LANES, 0), pl.BlockSpec((1, SUBLANES, tm), lambda i, j: (i, 0, j)),
                   pl.BlockSpec((LANES, 1), lambda i, j: (0, 0))],
        out_shape=[jax.ShapeDtypeStruct((b, s, d), F32), jax.ShapeDtypeStruct((b, s, d + LANES), F32),
                   jax.ShapeDtypeStruct((b, SUBLANES, s), F32), jax.ShapeDtypeStruct((LANES, 1), F32)],
        scratch_shapes=[pltpu.VMEM((tm, LANES), F32)],
        compiler_params=_cparams(("arbitrary", "arbitrary")),
        name="merge_ln1_router",
    )(p, p, p, o_hf, o_hb, o_rf, o_rb, yg, x, modtab, *layer_prm, *const_prm)


def _routing_tables(route, counts, te, n_tiles):
    bucket = route[:, 0, :].astype(jnp.int32).reshape(-1)
    rank = route[:, 1, :].astype(jnp.int32).reshape(-1)
    cnt = counts[:N_BUCKETS, 0].astype(jnp.int32)
    padded = (cnt + te - 1) // te * te
    ends = jnp.cumsum(padded)
    pos = (ends - padded)[bucket] + rank
    n_used = ends[-1] // te
    tile = jnp.arange(n_tiles, dtype=jnp.int32)
    tb = jnp.minimum(jnp.searchsorted(ends, tile * te, side="right"), N_BUCKETS - 1).astype(jnp.int32)
    tb = jnp.where(tile < n_used, tb, tb[jnp.maximum(n_used - 1, 0)])
    group, pair = tb // N_PAIRS, tb % N_PAIRS
    ea = group * EXPERTS_PER_GROUP + jnp.asarray(PAIR_LO, jnp.int32)[pair]
    eb = group * EXPERTS_PER_GROUP + jnp.asarray(PAIR_HI, jnp.int32)[pair]
    return pos, ea, eb, n_used.reshape(1).astype(jnp.int32)


def _dispatch_kernel(pos_ref, src_ref, init_ref, out_ref, sem, *, tm, tiles_per_row):
    del init_ref
    base = (pl.program_id(0) * tiles_per_row + pl.program_id(1)) * tm

    def row_copy(r):
        return pltpu.make_async_copy(src_ref.at[0, pl.ds(r, 1)], out_ref.at[pl.ds(pos_ref[base + r], 1)], sem)

    def start(g, carry):
        first = pl.multiple_of(g * DMA_UNROLL, DMA_UNROLL)
        for k in range(DMA_UNROLL):
            row_copy(first + k).start(priority=k % 2)
        return carry

    def wait(r, carry):
        row_copy(r).wait()
        return carry

    lax.fori_loop(0, tm // DMA_UNROLL, start, 0)
    lax.fori_loop(0, tm, wait, 0, unroll=DMA_UNROLL)


def _dispatch(pos, rows, init, tm):
    b, s, w = rows.shape
    n_sorted = init.shape[0]
    return pl.pallas_call(
        functools.partial(_dispatch_kernel, tm=tm, tiles_per_row=s // tm),
        grid_spec=pltpu.PrefetchScalarGridSpec(
            num_scalar_prefetch=1,
            grid=(b, s // tm),
            in_specs=[pl.BlockSpec((1, tm, w), lambda i, j, pos_ref: (i, j, 0)),
                      pl.BlockSpec(memory_space=pl.ANY)],
            out_specs=pl.BlockSpec(memory_space=pl.ANY),
            scratch_shapes=[pltpu.SemaphoreType.DMA(())],
        ),
        out_shape=jax.ShapeDtypeStruct((n_sorted, w), F32),
        input_output_aliases={2: 0},
        compiler_params=_cparams(("arbitrary", "arbitrary")),
        name="moe_dispatch",
    )(pos, rows, init)


def _expert_kernel(ea_ref, eb_ref, nused_ref, hs_ref, wga_ref, wua_ref, wda_ref, wgb_ref, wub_ref, wdb_ref,
                   o_ref, wg_scr, wu_scr, wd_scr):
    t = pl.program_id(0)
    prev = jnp.maximum(t - 1, 0)
    changed = (t == 0) | (ea_ref[t] != ea_ref[prev]) | (eb_ref[t] != eb_ref[prev])

    @pl.when(changed)
    def _():
        wg_scr[0] = wga_ref[0, 0].astype(BF16)
        wu_scr[0] = wua_ref[0, 0].astype(BF16)
        wd_scr[0] = wda_ref[0, 0].astype(BF16)
        wg_scr[1] = wgb_ref[0, 0].astype(BF16)
        wu_scr[1] = wub_ref[0, 0].astype(BF16)
        wd_scr[1] = wdb_ref[0, 0].astype(BF16)

    @pl.when(t < nused_ref[0])
    def _():
        d = o_ref.shape[1]
        h = hs_ref[:, :d].astype(BF16)
        y = jnp.zeros(o_ref.shape, F32)
        for e in range(2):
            act = (_silu(_dot(h, wg_scr[e])) * _dot(h, wu_scr[e])).astype(BF16)
            y = y + hs_ref[:, d + e : d + e + 1] * _dot(act, wd_scr[e])
        o_ref[...] = y

    @pl.when(t >= nused_ref[0])
    def _():
        o_ref[...] = jnp.zeros_like(o_ref)


def _experts(ea, eb, n_used, hs, wg, wu, wd, l, te):
    n_sorted, w = hs.shape
    _, _, d, eh = wg.shape

    def wspec(shape, which):
        return pl.BlockSpec((1, 1) + shape, lambda t, ea_ref, eb_ref, n_ref: (l, (ea_ref, eb_ref)[which][t], 0, 0))

    return pl.pallas_call(
        _expert_kernel,
        grid_spec=pltpu.PrefetchScalarGridSpec(
            num_scalar_prefetch=3,
            grid=(n_sorted // te,),
            in_specs=[pl.BlockSpec((te, w), lambda t, *_: (t, 0)),
                      wspec((d, eh), 0), wspec((d, eh), 0), wspec((eh, d), 0),
                      wspec((d, eh), 1), wspec((d, eh), 1), wspec((eh, d), 1)],
            out_specs=pl.BlockSpec((te, d), lambda t, *_: (t, 0)),
            scratch_shapes=[pltpu.VMEM((2, d, eh), BF16), pltpu.VMEM((2, d, eh), BF16), pltpu.VMEM((2, eh, d), BF16)],
        ),
        out_shape=jax.ShapeDtypeStruct((n_sorted, d), F32),
        compiler_params=_cparams(("arbitrary",)),
        name="moe_experts",
    )(ea, eb, n_used, hs, wg, wu, wd, wg, wu, wd)


def _combine_kernel(pos_ref, ys_ref, x1_ref, m_ref, lng_ref, lnb_ref, o_ref, buf, sem,
                    *, tm, tiles_per_row, first_tile, alpha):
    cols = pl.num_programs(1)
    n_steps = pl.num_programs(0) * cols
    step = pl.program_id(0) * cols + pl.program_id(1)
    slot = step % 2

    def row_copy(step_, slot_, r):
        tile = (step_ // cols) * tiles_per_row + first_tile + step_ % cols
        src = ys_ref.at[pl.ds(pos_ref[tile * tm + r], 1)]
        return pltpu.make_async_copy(src, buf.at[slot_, pl.ds(r, 1)], sem.at[slot_])

    def start_tile(step_, slot_):
        def body(g, carry):
            first = pl.multiple_of(g * DMA_UNROLL, DMA_UNROLL)
            for k in range(DMA_UNROLL):
                row_copy(step_, slot_, first + k).start(priority=k % 2)
            return carry

        lax.fori_loop(0, tm // DMA_UNROLL, body, 0)

    @pl.when(step == 0)
    def _():
        start_tile(0, 0)

    @pl.when(step + 1 < n_steps)
    def _():
        start_tile(step + 1, 1 - slot)

    def wait(r, carry):
        row_copy(step, slot, r).wait()
        return carry

    lax.fori_loop(0, tm, wait, 0, unroll=DMA_UNROLL)
    m = m_ref[0, 0, 0]
    o_ref[0] = _layer_norm(alpha * x1_ref[0] + m[5:6, :] * buf[slot], lng_ref[0], lnb_ref[0])


def _combine(pos, ys, x1, modtab, ln_g, ln_b, l, tm, nct, alpha, first_tile):
    b, s, d = x1.shape
    cols = s // tm - first_tile
    mod_spec = pl.BlockSpec((1, 1, 1, 6, d),
                            lambda i, j, pos_ref: (l, i, jnp.minimum((j + first_tile) // nct, 1), 0, 0))
    return pl.pallas_call(
        functools.partial(_combine_kernel, tm=tm, tiles_per_row=s // tm, first_tile=first_tile, alpha=alpha),
        grid_spec=pltpu.PrefetchScalarGridSpec(
            num_scalar_prefetch=1,
            grid=(b, cols),
            in_specs=[pl.BlockSpec(memory_space=pl.ANY),
                      pl.BlockSpec((1, tm, d), lambda i, j, pos_ref: (i, j + first_tile, 0)),
                      mod_spec, _layer_spec((1, d), l), _layer_spec((1, d), l)],
            out_specs=pl.BlockSpec((1, tm, d), lambda i, j, pos_ref: (i, j, 0)),
            scratch_shapes=[pltpu.VMEM((2, tm, d), F32), pltpu.SemaphoreType.DMA((2,))],
        ),
        out_shape=jax.ShapeDtypeStruct((b, cols * tm, d), F32),
        compiler_params=_cparams(("arbitrary", "arbitrary")),
        name="moe_combine_ln2",
    )(pos, ys, x1, modtab, ln_g, ln_b)


def _block_avg(width, group):
    idx = np.arange(width) // group
    return jnp.asarray((idx[:, None] == idx[None, :]).astype(np.float32) / group, dtype=BF16)


def kernel(x, c, ctx, c_ctx, w_mod, b_mod, w_in, hg_lb_raw, hg_norm_g, ret_decay_raw, ret_gn_g, ret_gn_b, s5_lam_re, s5_lam_im, s5_log_dt, s5_b_re, s5_b_im, s5_c_re, s5_c_im, s5_d, s5_glu_w, s5_glu_b, w_out, ln1_g, ln1_b, ln2_g, ln2_b, rg_w, rg_b, re_w, re_b, exp_w_gate, exp_w_up, exp_w_down):
    bsz, t_lat, d = x.shape
    t_ctx = ctx.shape[1]
    depth = w_mod.shape[0]
    alpha = (2.0 * depth) ** 0.25
    tm = TOKEN_TILE
    assert d == D_MODEL and t_lat % GRID_W == 0 and bsz < SUBLANES
    assert t_ctx % tm == 0 and t_lat % tm == 0, "context and latent lengths must be multiples of the token tile"
    nct = t_ctx // tm
    s = t_ctx + t_lat

    cvec = jnp.concatenate([c, c_ctx[None, :], jnp.zeros((SUBLANES - bsz - 1, d), F32)], 0)
    mod_all = _modulation(cvec, w_mod, b_mod)
    lat = mod_all[:, :bsz].reshape(depth, bsz, 6, d)
    cm = jnp.broadcast_to(mod_all[:, bsz].reshape(depth, 1, 6, d), (depth, bsz, 6, d))
    modtab = jnp.stack([cm, lat], axis=2)

    hg_lb = jnp.cumsum(jax.nn.softmax(hg_lb_raw.astype(F32), axis=0), axis=0)
    hg_lb = hg_lb - hg_lb[:1]
    ret_tables = _retention_decay_tables(jax.nn.log_sigmoid(ret_decay_raw.astype(F32)))
    cos_tab, sin_tab = _rope_tables(t_lat, t_ctx)
    s5_tabs = _s5_weights(s5_lam_re, s5_lam_im, s5_log_dt, s5_b_re, s5_b_im, s5_c_re, s5_c_im)
    pad_r = LANES - N_GROUPS - N_EXPERTS
    wr = jnp.concatenate([rg_w, re_w.reshape(depth, d, N_EXPERTS), jnp.zeros((depth, d, pad_r), F32)], axis=2)
    wr = jnp.swapaxes(wr, 1, 2)
    wr_hi, wr_lo = _split_bf16(wr)
    br = jnp.concatenate([rg_b, re_b.reshape(depth, N_EXPERTS), jnp.zeros((depth, pad_r), F32)], axis=1)[:, :, None]
    row = lambda a: a[:, None, :]
    layer_prm = [row(jnp.tile(hg_norm_g, (1, HG_HEADS))), row(ret_gn_g), row(ret_gn_b), row(s5_d),
                 s5_glu_w.astype(BF16), row(s5_glu_b), w_out.astype(BF16), row(ln1_g), row(ln1_b),
                 wr_hi, wr_lo, br]
    tri = jnp.asarray(np.triu(np.ones((tm, tm), np.float32), 1), dtype=BF16)
    const_prm = [_block_avg(HG_WIDTH, HG_DK), _block_avg(RET_WIDTH, RET_DK), tri]
    w_in_bf16 = w_in.astype(BF16)
    ln2_g, ln2_b = row(ln2_g), row(ln2_b)
    te = EXPERT_TILE
    n_tiles = -(-(bsz * s + N_BUCKETS * (te - 1)) // te)

    hs = jnp.zeros((n_tiles * te, d + LANES), F32)
    xs = jnp.concatenate([ctx, x], axis=1)
    for l in range(depth):
        p, ug = _inproj(xs, modtab, w_in_bf16, l, tm, nct)
        o_hf, o_hb = _hgrn(p, hg_lb, l, t_ctx // CHUNK)
        o_rf, o_rb = _retention(p, cos_tab, sin_tab, ret_tables, l, t_ctx // RET_CHUNK)
        yg = _s5_conv(ug, *s5_tabs, l, bsz, t_ctx // S5_CHUNK)
        x1, rows, route, counts = _merge(p, o_hf, o_hb, o_rf, o_rb, yg, xs, modtab, layer_prm, const_prm,
                                         l, tm, nct, alpha)
        pos, ea, eb, n_used = _routing_tables(route, counts, te, n_tiles)
        hs = _dispatch(pos, rows, hs, tm)
        ys = _experts(ea, eb, n_used, hs, exp_w_gate, exp_w_up, exp_w_down, l, te)
        xs = _combine(pos, ys, x1, modtab, ln2_g, ln2_b, l, tm, nct, alpha, nct if l == depth - 1 else 0)
    return xs
```

```python
import functools

import numpy as np
import jax
import jax.numpy as jnp
from jax import lax
from jax.experimental import pallas as pl
from jax.experimental.pallas import tpu as pltpu

F32 = jnp.float32
BF16 = jnp.bfloat16

D_MODEL = 1024
HG_WIDTH = 512
HG_HEADS = 4
HG_DK = HG_WIDTH // HG_HEADS
RET_WIDTH = 256
RET_HEADS = 4
RET_DK = RET_WIDTH // RET_HEADS
S5_WIDTH = 256
S5_GROUP_CH = 16
S5_GROUPS = S5_WIDTH // S5_GROUP_CH
S5_STATE = 64
IN_COLS = 5 * HG_WIDTH + 4 * RET_WIDTH + S5_WIDTH
CHUNK = 64
HGRN_BATCH = 4
RET_CHUNK = 256
N_GROUPS = 4
EXPERTS_PER_GROUP = 4
N_EXPERTS = N_GROUPS * EXPERTS_PER_GROUP
N_PAIRS = EXPERTS_PER_GROUP * (EXPERTS_PER_GROUP - 1) // 2
N_BUCKETS = N_GROUPS * N_PAIRS
PAIR_SLOT_A = (0, 2, 2, 3, 3, 3)
PAIR_SLOT_B = (1, 1, 0, 0, 1, 2)
EXPERT_HIDDEN = D_MODEL // 2
LN_EPS = 1e-5
ROPE_BASE = 10000.0
GRID_W = 64

LANES = 128
SUBLANES = 8
TOKEN_TILE = 256
S5_CHUNK = 32
S5_FOLD = S5_CHUNK * S5_GROUP_CH
EXPERT_TILE = 256
DMA_UNROLL = 8
VMEM_LIMIT = 56 * 1024 * 1024


def _cparams(sem):
    return pltpu.CompilerParams(dimension_semantics=sem, vmem_limit_bytes=VMEM_LIMIT)


def _split_bf16(x):
    hi = x.astype(BF16)
    lo = (x - hi.astype(F32)).astype(BF16)
    return hi, lo


def _dot(a, b):
    return jnp.dot(a, b, preferred_element_type=F32)


def _dot3(a, b):
    ah, al = _split_bf16(a)
    bh, bl = _split_bf16(b)
    return _dot(ah, bh) + _dot(ah, bl) + _dot(al, bh)


def _dot2(a, b_bf16):
    ah, al = _split_bf16(a)
    return _dot(ah, b_bf16) + _dot(al, b_bf16)


def _dot_nt(a, b):
    return lax.dot_general(a, b, (((1,), (1,)), ((), ())), preferred_element_type=F32)


def _dot_tn(a, b):
    return lax.dot_general(a, b, (((0,), (0,)), ((), ())), preferred_element_type=F32)


def _silu(x):
    return x * jax.nn.sigmoid(x)


def _layer_spec(shape, l):
    zeros = (0,) * len(shape)
    return pl.BlockSpec((1,) + tuple(shape), lambda *_: (l,) + zeros)


def _mod_spec(l, nct):
    return pl.BlockSpec((1, 1, 1, 6, D_MODEL), lambda i, j, *_: (l, i, jnp.minimum(j // nct, 1), 0, 0))


def _swap_sublane_lanegroup(v):
    n = v.shape[0]
    r = lax.broadcasted_iota(jnp.int32, v.shape, 0)
    l = lax.broadcasted_iota(jnp.int32, v.shape, 1)
    for k in range(3):
        rb = (r >> k) & 1
        gb = (l >> (4 + k)) & 1
        sh = S5_GROUP_CH << k
        st = 1 << k
        a = pltpu.roll(pltpu.roll(v, LANES - sh, 1), st, 0)
        b = pltpu.roll(pltpu.roll(v, sh, 1), n - st, 0)
        v = jnp.where(rb == gb, v, jnp.where(rb == 1, a, b))
    return v


def _mod_kernel(c_ref, w_ref, b_ref, o_ref):
    sc = _silu(c_ref[...])
    o_ref[0] = _dot3(sc, w_ref[0]) + b_ref[0]


def _modulation(cvec, w_mod, b_mod):
    depth, d, n = w_mod.shape
    rows = cvec.shape[0]
    tn = 1536
    return pl.pallas_call(
        _mod_kernel,
        grid=(depth, n // tn),
        in_specs=[
            pl.BlockSpec((rows, d), lambda l, j: (0, 0)),
            pl.BlockSpec((1, d, tn), lambda l, j: (l, 0, j)),
            pl.BlockSpec((1, 1, tn), lambda l, j: (l, 0, j)),
        ],
        out_specs=pl.BlockSpec((1, rows, tn), lambda l, j: (l, 0, j)),
        out_shape=jax.ShapeDtypeStruct((depth, rows, n), F32),
        compiler_params=_cparams(("arbitrary", "arbitrary")),
        name="modulation",
    )(cvec, w_mod, b_mod.reshape(depth, 1, n))


def _inproj_kernel(x_ref, m_ref, w_ref, o_ref, ug_ref, wt_scr):
    m = m_ref[0, 0, 0]
    h = x_ref[0] * (1.0 + m[1:2, :]) + m[0:1, :]
    p = _dot(h.astype(BF16), w_ref[0])
    o_ref[0] = p
    rows_per_chunk = S5_CHUNK // SUBLANES
    n_chunks = p.shape[0] // S5_CHUNK
    for half in range(S5_WIDTH // LANES):
        lo = IN_COLS - S5_WIDTH + half * LANES
        wt_scr[...] = _swap_sublane_lanegroup(p[:, lo : lo + LANES])
        for g_lo in range(SUBLANES):
            for s_hi in range(rows_per_chunk):
                piece = wt_scr[pl.ds(s_hi * SUBLANES + g_lo, n_chunks, stride=S5_CHUNK), :]
                ug_ref[half * SUBLANES + g_lo, :, s_hi * LANES : (s_hi + 1) * LANES] = piece


def _inproj(x, modtab, w_in_bf16, l, tm, nct):
    b, s, d = x.shape
    n = w_in_bf16.shape[-1]
    cpt = tm // S5_CHUNK
    return pl.pallas_call(
        _inproj_kernel,
        grid=(b, s // tm),
        in_specs=[
            pl.BlockSpec((1, tm, d), lambda i, j: (i, j, 0)),
            _mod_spec(l, nct),
            _layer_spec((d, n), l),
        ],
        out_specs=[pl.BlockSpec((1, tm, n), lambda i, j: (i, j, 0)),
                   pl.BlockSpec((S5_GROUPS, cpt, S5_FOLD), lambda i, j: (0, j, i))],
        out_shape=[jax.ShapeDtypeStruct((b, s, n), F32),
                   jax.ShapeDtypeStruct((S5_GROUPS, s // S5_CHUNK, b * S5_FOLD), F32)],
        scratch_shapes=[pltpu.VMEM((tm, LANES), F32)],
        compiler_params=_cparams(("arbitrary", "arbitrary")),
        name="inproj",
    )(x, modtab, w_in_bf16)


def _block_gate_products(f, reverse):
    c = f.shape[0]
    row = lax.broadcasted_iota(jnp.int32, (c, 1), 0)
    a, z, b = f, f, jnp.ones_like(f)
    out = []
    s = 1
    while s < c:
        out.append((a, b))
        if s < SUBLANES:
            z3 = z.reshape(c // SUBLANES, SUBLANES, z.shape[1])
            up = pltpu.roll(z3, s, 1).reshape(z.shape)
            dn = pltpu.roll(z3, SUBLANES - s, 1).reshape(z.shape)
            odd = (row & s) != 0
            if reverse:
                a = a * jnp.where(odd, 1.0, dn)
                b = b * jnp.where(odd, up, 1.0)
            else:
                a = a * jnp.where(odd, up, 1.0)
                b = b * jnp.where(odd, 1.0, dn)
            z = z * jnp.where(odd, up, dn)
        else:
            na, nb, nz = [], [], []
            for lo in range(0, c, 2 * s):
                ev, od = slice(lo, lo + s), slice(lo + s, lo + 2 * s)
                zz = z[ev] * z[od]
                if reverse:
                    na += [a[ev] * z[od], a[od]]
                    nb += [b[ev], b[od] * z[ev]]
                else:
                    na += [a[ev], a[od] * z[ev]]
                    nb += [b[ev] * z[od], b[od]]
                nz += [zz, zz]
            a, b, z = (jnp.concatenate(t, axis=0) for t in (na, nb, nz))
        s *= 2
    out.append((a, b))
    return out, z


def _hgrn_kernel(qf_ref, vf_ref, zf_ref, qb_ref, vb_ref, zb_ref, lb_ref, of_ref, ob_ref, st_ref):
    c = CHUNK

    @pl.when(pl.program_id(1) == 0)
    def _():
        st_ref[...] = jnp.zeros_like(st_ref)

    ri = lax.broadcasted_iota(jnp.int32, (c, c), 0)
    ci = lax.broadcasted_iota(jnp.int32, (c, c), 1)
    for d, (q_ref, v_ref, z_ref, o_ref) in enumerate(
        ((qf_ref, vf_ref, zf_ref, of_ref), (qb_ref, vb_ref, zb_ref, ob_ref))
    ):
        reverse = d == 1
        causal = (ri < ci) if reverse else (ri > ci)
        n_levels = c.bit_length() - 1
        masks = [(((ri >> lvl) ^ (ci >> lvl)) == 1) & causal for lvl in range(n_levels)]
        for bi, h in [(bi, h) for bi in range(q_ref.shape[0]) for h in range(HG_HEADS)]:
            hs = slice(h * HG_DK, (h + 1) * HG_DK)
            q = _silu(q_ref[bi, :, hs])
            v = v_ref[bi, :, hs].astype(BF16)
            lb = lb_ref[0, d : d + 1, hs]
            f = lb + (1.0 - lb) * jax.nn.sigmoid(z_ref[bi, :, hs])
            k = 1.0 - f
            levels, tot = _block_gate_products(f, reverse)
            scores = jnp.where(ri == ci, _dot_nt(q.astype(BF16), k.astype(BF16)), 0.0)
            for mask, (a, bb) in zip(masks, levels):
                scores = scores + jnp.where(mask, _dot_nt((q * a).astype(BF16), (k * bb).astype(BF16)), 0.0)
            a_full, b_full = levels[-1]
            st = st_ref[bi, d, h]
            o_ref[bi, :, hs] = (_dot(scores.astype(BF16), v)
                                + _dot_nt((q * a_full).astype(BF16), st.astype(BF16)))
            st_ref[bi, d, h] = st * tot[0:1, :] + _dot_tn(v, (k * b_full).astype(BF16))


def _bwd_chunk(n, nc_ctx, nc_all):
    return jnp.where(n < nc_ctx, nc_ctx - 1 - n, nc_all + nc_ctx - 1 - n)


def _hgrn(p, lb_all, l, nc_ctx):
    b, s, _ = p.shape
    nc = s // CHUNK
    w = HG_WIDTH

    bb = HGRN_BATCH if b % HGRN_BATCH == 0 else 1

    def fwd(col):
        return pl.BlockSpec((bb, CHUNK, w), lambda i, n: (i, n, col))

    def bwd(col):
        return pl.BlockSpec((bb, CHUNK, w), lambda i, n: (i, _bwd_chunk(n, nc_ctx, nc), col))

    out = jax.ShapeDtypeStruct((b, s, w), F32)
    return pl.pallas_call(
        _hgrn_kernel,
        grid=(b // bb, nc),
        in_specs=[fwd(0), fwd(1), fwd(3), bwd(0), bwd(1), bwd(4), _layer_spec((2, w), l)],
        out_specs=[fwd(0), bwd(0)],
        out_shape=[out, out],
        scratch_shapes=[pltpu.VMEM((bb, 2, HG_HEADS, HG_DK, HG_DK), F32)],
        compiler_params=_cparams(("arbitrary", "arbitrary")),
        name="hgrn2_scan",
    )(p, p, p, p, p, p, lb_all)


def _swap_halves(x, half):
    n = x.shape[-1]
    lane = lax.broadcasted_iota(jnp.int32, (1, n), 1)
    lower = (lane & half) == 0
    return jnp.where(lower, pltpu.roll(x, n - half, 1), pltpu.roll(x, half, 1))


def _ret_kernel(qf_ref, kf_ref, vf_ref, cf_ref, sf_ref, qb_ref, kb_ref, vb_ref, cb_ref, sb_ref,
                dmat_ref, rq_ref, rk_ref, cd_ref, of_ref, ob_ref, st_ref):
    @pl.when(pl.program_id(1) == 0)
    def _():
        st_ref[...] = jnp.zeros_like(st_ref)

    half = RET_DK // 4
    for d, (q_ref, k_ref, v_ref, c_ref, s_ref, o_ref) in enumerate(
        ((qf_ref, kf_ref, vf_ref, cf_ref, sf_ref, of_ref), (qb_ref, kb_ref, vb_ref, cb_ref, sb_ref, ob_ref))
    ):
        cos = c_ref[...]
        sin = s_ref[...]
        q = q_ref[0]
        k = k_ref[0] * (RET_DK ** -0.5)
        q = q * cos + _swap_halves(q, half) * sin
        k = k * cos + _swap_halves(k, half) * sin
        v = v_ref[0]
        vt = v.T.astype(BF16)
        v = v.astype(BF16)
        q0 = q.astype(BF16)
        k0 = k.astype(BF16)
        qd = (q * rq_ref[0, d]).astype(BF16)
        kd = (k * rk_ref[0, d]).astype(BF16)
        cd = cd_ref[0, d : d + 1, :]
        lane = lax.broadcasted_iota(jnp.int32, (1, LANES), 1)
        left = lane < RET_DK
        same_head = (lax.broadcasted_iota(jnp.int32, (LANES, 1), 0) < RET_DK) == left
        zero = jnp.zeros((), BF16)
        for p in range(RET_HEADS // 2):
            ps = slice(p * LANES, (p + 1) * LANES)
            kp, vp = k0[:, ps], v[:, ps]
            k_blk = jnp.concatenate([jnp.where(left, kp, zero), jnp.where(left, zero, kp)], axis=0)
            v_blk = jnp.concatenate([jnp.where(left, vp, zero), jnp.where(left, zero, vp)], axis=0)
            scores = _dot_nt(q0[:, ps], k_blk) * dmat_ref[0, d, p]
            st = st_ref[d, p]
            o_ref[0, :, ps] = _dot(scores.astype(BF16), v_blk) + _dot_nt(qd[:, ps], st.astype(BF16))
            st_ref[d, p] = st * cd[:, ps] + jnp.where(same_head, _dot(vt[ps, :], kd[:, ps]), 0.0)


def _retention(p, cos_tab, sin_tab, tables, l, nc_ctx):
    b, s, _ = p.shape
    nc = s // RET_CHUNK
    w = RET_WIDTH
    base = 5 * HG_WIDTH // w

    def fwd(col):
        return pl.BlockSpec((1, RET_CHUNK, w), lambda i, n: (i, n, col))

    def bwd(col):
        return pl.BlockSpec((1, RET_CHUNK, w), lambda i, n: (i, _bwd_chunk(n, nc_ctx, nc), col))

    tab_f = pl.BlockSpec((RET_CHUNK, w), lambda i, n: (n, 0))
    tab_b = pl.BlockSpec((RET_CHUNK, w), lambda i, n: (_bwd_chunk(n, nc_ctx, nc), 0))
    out = jax.ShapeDtypeStruct((b, s, w), F32)
    return pl.pallas_call(
        _ret_kernel,
        grid=(b, nc),
        in_specs=[fwd(base), fwd(base + 1), fwd(base + 2), tab_f, tab_f,
                  bwd(base), bwd(base + 1), bwd(base + 2), tab_b, tab_b]
                 + [_layer_spec(t.shape[1:], l) for t in tables],
        out_specs=[fwd(0), bwd(0)],
        out_shape=[out, out],
        scratch_shapes=[pltpu.VMEM((2, RET_HEADS // 2, LANES, LANES), F32)],
        compiler_params=_cparams(("arbitrary", "arbitrary")),
        name="retention_scan",
    )(p, p, p, cos_tab, sin_tab, p, p, p, cos_tab, sin_tab, *tables)


def _rope_tables(t_lat, t_ctx):
    m = RET_DK // 4
    inv = ROPE_BASE ** (-jnp.arange(m, dtype=F32) / m)
    rows = jnp.repeat(jnp.arange(t_lat // GRID_W, dtype=jnp.int32), GRID_W).astype(F32)
    cols = jnp.tile(jnp.arange(GRID_W, dtype=jnp.int32), t_lat // GRID_W).astype(F32)

    def half_tables(pos):
        ang = pos[:, None] * inv
        c, s = jnp.cos(ang), jnp.sin(ang)
        return jnp.concatenate([c, c], -1), jnp.concatenate([-s, s], -1)

    cr, sr = half_tables(rows)
    cc, sc = half_tables(cols)
    cos_h = jnp.concatenate([cr, cc], -1)
    sin_h = jnp.concatenate([sr, sc], -1)
    cos = jnp.tile(cos_h, (1, RET_HEADS))
    sin = jnp.tile(sin_h, (1, RET_HEADS))
    cos = jnp.concatenate([jnp.ones((t_ctx, RET_WIDTH), F32), cos], 0)
    sin = jnp.concatenate([jnp.zeros((t_ctx, RET_WIDTH), F32), sin], 0)
    return cos, sin


def _retention_decay_tables(log_gamma):
    c = RET_CHUNK
    i = jnp.arange(c, dtype=F32)
    diff = i[:, None] - i[None, :]
    lg = log_gamma[:, :, :, None, None]
    d_f = jnp.where(diff >= 0, jnp.exp(lg[:, 0] * diff), 0.0)
    d_b = jnp.where(diff <= 0, jnp.exp(lg[:, 1] * (-diff)), 0.0)
    dmat = jnp.stack([d_f, d_b], 1)
    dmat = jnp.concatenate([dmat[:, :, 0::2], dmat[:, :, 1::2]], axis=-1)
    lane_lg = jnp.repeat(log_gamma, RET_DK, axis=2)[:, :, None, :]
    col = i[None, :, None]
    rq = jnp.stack([jnp.exp(lane_lg[:, 0] * (col + 1.0)), jnp.exp(lane_lg[:, 1] * (c - col))], 1)
    rk = jnp.stack([jnp.exp(lane_lg[:, 0] * (c - 1.0 - col)), jnp.exp(lane_lg[:, 1] * col)], 1)
    cdec = jnp.exp(lane_lg[:, :, 0, :] * c)
    return dmat, rq, rk, cdec


def _s5_kernel(u_ref, base_ref, inj_ref, w2_ref, ac_ref, y_ref, toep_scr, a_scr, i_scr, hf_scr, hb_scr,
               *, n_ctx, n_all, bsz):
    width = S5_FOLD
    st2 = 2 * S5_STATE
    ch = S5_GROUP_CH
    lane = lax.broadcasted_iota(jnp.int32, (ch, width), 1)
    base_f = base_ref[0, 0, 0]
    base_b = base_ref[0, 0, 1]
    for s in range(S5_CHUNK):
        sh_f = s * ch
        sh_b = (S5_CHUNK - 1 - s) * ch
        part_f = jnp.where(lane >= sh_f, pltpu.roll(base_f, sh_f, 1), 0.0) if sh_f else base_f
        part_b = jnp.where(lane < width - sh_b, pltpu.roll(base_b, width - sh_b, 1), 0.0) if sh_b else base_b
        toep_scr[s * ch : (s + 1) * ch, :] = (part_f + part_b).astype(BF16)
    for b in range(bsz):
        ub = u_ref[0, :, b * width : (b + 1) * width].astype(BF16)
        a_scr[b] = _dot(ub, toep_scr[...])
        i_scr[b] = _dot(ub, inj_ref[0, 0])
    ac = ac_ref[0, 0]

    zero = jnp.zeros((1, st2), F32)
    state = [(zero, zero)] * (2 * bsz)
    for n in range(n_all):
        nb = n_ctx - 1 - n if n < n_ctx else n_all + n_ctx - 1 - n
        for b in range(bsz):
            for d, (row, scr) in enumerate(((n, hf_scr), (nb, hb_scr))):
                h, hsw = state[2 * b + d]
                scr[b, row : row + 1, :] = h
                inj = i_scr[b, row : row + 1, d * st2 : (d + 1) * st2]
                inj_sw = i_scr[b, row : row + 1, (d + 2) * st2 : (d + 3) * st2]
                ar, ai = ac[2 * d : 2 * d + 1, :], ac[2 * d + 1 : 2 * d + 2, :]
                state[2 * b + d] = (h * ar + hsw * ai + inj, hsw * ar - h * ai + inj_sw)
    for b in range(bsz):
        y_ref[0, :, b * width : (b + 1) * width] = (
            a_scr[b]
            + _dot(hf_scr[b].astype(BF16), w2_ref[0, 0, 0])
            + _dot(hb_scr[b].astype(BF16), w2_ref[0, 0, 1]))


def _s5_conv(ug, base, inj, w2, ac, l, bsz, n_ctx):
    g, n_all, _ = ug.shape
    width = S5_FOLD
    st2 = 2 * S5_STATE
    return pl.pallas_call(
        functools.partial(_s5_kernel, n_ctx=n_ctx, n_all=n_all, bsz=bsz),
        grid=(g,),
        in_specs=[
            pl.BlockSpec((1, n_all, bsz * width), lambda i: (i, 0, 0)),
            pl.BlockSpec((1, 1, 2, S5_GROUP_CH, width), lambda i: (l, i, 0, 0, 0)),
            pl.BlockSpec((1, 1, width, 4 * st2), lambda i: (l, i, 0, 0)),
            pl.BlockSpec((1, 1, 2, st2, width), lambda i: (l, i, 0, 0, 0)),
            pl.BlockSpec((1, 1, SUBLANES, st2), lambda i: (l, i, 0, 0)),
        ],
        out_specs=pl.BlockSpec((1, n_all, bsz * width), lambda i: (i, 0, 0)),
        out_shape=jax.ShapeDtypeStruct((g, n_all, bsz * width), F32),
        scratch_shapes=[pltpu.VMEM((width, width), BF16),
                        pltpu.VMEM((bsz, n_all, width), F32), pltpu.VMEM((bsz, n_all, 4 * st2), F32),
                        pltpu.VMEM((bsz, n_all, st2), F32), pltpu.VMEM((bsz, n_all, st2), F32)],
        compiler_params=_cparams(("arbitrary",)),
        name="s5_conv",
    )(ug, base, inj, w2, ac)


def _s5_weights(lam_re, lam_im, log_dt, b_re, b_im, c_re, c_im):
    cs, ch, p, g = S5_CHUNK, S5_GROUP_CH, S5_STATE, S5_GROUPS
    nl = lam_re.shape[0]
    lam_re = jnp.minimum(lam_re.astype(F32), -1e-4)
    lam_im = lam_im.astype(F32)
    dt = jnp.exp(log_dt.astype(F32))[..., None]
    mag = jnp.exp(dt * lam_re)
    abar_re, abar_im = mag * jnp.cos(dt * lam_im), mag * jnp.sin(dt * lam_im)
    den = jnp.square(lam_re) + jnp.square(lam_im)
    nr, ni = abar_re - 1.0, abar_im
    coef_re = ((nr * lam_re + ni * lam_im) / den)[..., None]
    coef_im = ((ni * lam_re - nr * lam_im) / den)[..., None]
    b_re, b_im = b_re.astype(F32), b_im.astype(F32)
    bb_re = coef_re * b_re - coef_im * b_im
    bb_im = coef_re * b_im + coef_im * b_re
    ct_re = jnp.swapaxes(c_re.astype(F32), -1, -2)
    ct_im = jnp.swapaxes(c_im.astype(F32), -1, -2)

    def powers(expo):
        e = expo.astype(F32)[None, :, None, None, :]
        m = jnp.exp(e * (dt * lam_re)[..., None])
        return m * jnp.cos(e * (dt * lam_im)[..., None]), m * jnp.sin(e * (dt * lam_im)[..., None])

    slots = jnp.arange(cs + 1)
    pr, pi = powers(jnp.stack([slots, cs - slots]))
    r_re = (pr[..., None] * ct_re[..., None, :] - pi[..., None] * ct_im[..., None, :]).reshape(nl, 2, g, p, (cs + 1) * ch)
    r_im = (pr[..., None] * ct_im[..., None, :] + pi[..., None] * ct_re[..., None, :]).reshape(nl, 2, g, p, (cs + 1) * ch)
    hi = lax.Precision.HIGHEST
    kern = (jnp.einsum('ldgpe,ldgpn->ldgen', bb_re, r_re, precision=hi)
            - jnp.einsum('ldgpe,ldgpn->ldgen', bb_im, r_im, precision=hi))
    w = cs * ch
    base = jnp.stack([kern[:, 0, :, :, :w], kern[:, 1, :, :, ch:]], axis=2)
    w2 = jnp.stack([jnp.concatenate([r_re[:, 0, :, :, ch:], -r_im[:, 0, :, :, ch:]], axis=2),
                    jnp.concatenate([r_re[:, 1, :, :, :w], -r_im[:, 1, :, :, :w]], axis=2)], axis=2)
    steps = jnp.arange(cs)
    qr, qi = powers(jnp.stack([cs - 1 - steps, steps]))
    qr = jnp.swapaxes(qr, -1, -2)[..., :, None, :]
    qi = jnp.swapaxes(qi, -1, -2)[..., :, None, :]
    bt_re = jnp.swapaxes(bb_re, -1, -2)[:, :, :, None]
    bt_im = jnp.swapaxes(bb_im, -1, -2)[:, :, :, None]
    ab_re = (qr * bt_re - qi * bt_im).reshape(nl, 2, g, w, p)
    ab_im = (qr * bt_im + qi * bt_re).reshape(nl, 2, g, w, p)
    inj = jnp.concatenate([ab_re[:, 0], ab_im[:, 0], ab_re[:, 1], ab_im[:, 1],
                           ab_im[:, 0], ab_re[:, 0], ab_im[:, 1], ab_re[:, 1]], axis=-1)
    ac_rows = []
    for d, slot in enumerate((cs, 0)):
        ar, ai = pr[:, d, :, :, slot], pi[:, d, :, :, slot]
        ac_rows += [jnp.concatenate([ar, ar], -1), jnp.concatenate([-ai, ai], -1)]
    ac = jnp.stack(ac_rows + [jnp.zeros_like(ac_rows[0])] * (SUBLANES - 4), axis=2)
    return base, inj.astype(BF16), w2.astype(BF16), ac


def _layer_norm(x, g, b):
    mu = jnp.mean(x, -1, keepdims=True)
    xc = x - mu
    var = jnp.mean(xc * xc, -1, keepdims=True)
    return xc * lax.rsqrt(var + LN_EPS) * g + b


def _route(logits_t):
    col = lambda i: logits_t[i : i + 1, :]
    gl = [col(i) for i in range(N_GROUPS)]
    gmax = functools.reduce(jnp.maximum, gl)
    g_idx = jnp.full_like(gmax, N_GROUPS - 1).astype(jnp.int32)
    for i in reversed(range(N_GROUPS - 1)):
        g_idx = jnp.where(gl[i] == gmax, i, g_idx)
    g_p = 1.0 / functools.reduce(lambda a, b: a + b, [jnp.exp(x - gmax) for x in gl])
    el = []
    for e in range(EXPERTS_PER_GROUP):
        v = col(N_GROUPS + (N_GROUPS - 1) * EXPERTS_PER_GROUP + e)
        for g in reversed(range(N_GROUPS - 1)):
            v = jnp.where(g_idx == g, col(N_GROUPS + g * EXPERTS_PER_GROUP + e), v)
        el.append(v)
    m1 = functools.reduce(jnp.maximum, el)
    i1 = jnp.full_like(g_idx, EXPERTS_PER_GROUP - 1)
    for e in reversed(range(EXPERTS_PER_GROUP - 1)):
        i1 = jnp.where(el[e] == m1, e, i1)
    rest = [jnp.where(i1 == e, -jnp.inf, el[e]) for e in range(EXPERTS_PER_GROUP)]
    m2 = functools.reduce(jnp.maximum, rest)
    i2 = jnp.full_like(g_idx, EXPERTS_PER_GROUP - 1)
    for e in reversed(range(EXPERTS_PER_GROUP - 1)):
        i2 = jnp.where((rest[e] == m2) & (i1 != e), e, i2)
    t = jnp.exp(m2 - m1)
    w1 = g_p / (1.0 + t)
    w2 = g_p * t / (1.0 + t)
    lo = jnp.minimum(i1, i2)
    hi = jnp.maximum(i1, i2)
    pair = jnp.where(hi == 1, 0, jnp.where(hi == 2, jnp.where(lo == 1, 1, 2), 3 + lo))
    bucket = g_idx * N_PAIRS + pair
    w_lower = jnp.where(i1 < i2, w1, w2)
    w_higher = jnp.where(i1 < i2, w2, w1)
    w_a = jnp.where(pair == 0, w_lower, w_higher)
    w_b = jnp.where(pair == 0, w_higher, w_lower)
    return bucket, w_a, w_b


def _merge_kernel(gate_ref, rg_ref, u_ref, hf_ref, hb_ref, rf_ref, rb_ref, yg_ref, x_ref, m_ref,
                  hgn_ref, gng_ref, gnb_ref, d_ref, gw_ref, gb_ref, wo_ref, lng_ref, lnb_ref,
                  wrh_ref, wrl_ref, br_ref, a128_ref, a64_ref, tri_ref, x1_ref, h2_ref, route_ref, cnt_ref, wt_scr,
                  *, alpha):
    @pl.when((pl.program_id(0) == 0) & (pl.program_id(1) == 0))
    def _():
        cnt_ref[...] = jnp.zeros_like(cnt_ref)

    o_hg = hf_ref[0] + hb_ref[0]
    ms = _dot2(o_hg * o_hg, a128_ref[...])
    hg = o_hg * lax.rsqrt(ms + LN_EPS) * hgn_ref[0] * _silu(gate_ref[0])
    o_rt = rf_ref[0] + rb_ref[0]
    mu = _dot2(o_rt, a64_ref[...])
    xc = o_rt - mu
    var = _dot2(xc * xc, a64_ref[...])
    rt = (xc * lax.rsqrt(var + LN_EPS) * gng_ref[0] + gnb_ref[0]) * _silu(rg_ref[0])
    rows_per_chunk = S5_CHUNK // SUBLANES
    n_chunks = yg_ref.shape[1]
    y5 = []
    for half in range(S5_WIDTH // LANES):
        for g_lo in range(SUBLANES):
            for s_hi in range(rows_per_chunk):
                wt_scr[pl.ds(s_hi * SUBLANES + g_lo, n_chunks, stride=S5_CHUNK), :] = (
                    yg_ref[half * SUBLANES + g_lo, :, s_hi * LANES : (s_hi + 1) * LANES])
        y5.append(_swap_sublane_lanegroup(wt_scr[...]))
    y5 = jnp.concatenate(y5, axis=-1)
    s5 = jax.nn.gelu(y5 + d_ref[0] * u_ref[0])
    s5 = s5 * jax.nn.sigmoid(_dot(s5.astype(BF16), gw_ref[0]) + gb_ref[0])
    cat = jnp.concatenate([hg, rt, s5], axis=-1).astype(BF16)
    y = _dot(cat, wo_ref[0])
    m = m_ref[0, 0, 0]
    x1 = _layer_norm(alpha * x_ref[0] + m[2:3, :] * y, lng_ref[0], lnb_ref[0])
    x1_ref[0] = x1
    h2 = x1 * (1.0 + m[4:5, :]) + m[3:4, :]
    h_hi, h_lo = _split_bf16(h2)
    logits_t = (_dot_nt(wrh_ref[0], h_hi) + _dot_nt(wrh_ref[0], h_lo) + _dot_nt(wrl_ref[0], h_hi)) + br_ref[0]
    bucket, w_a, w_b = _route(logits_t)
    sub = lax.broadcasted_iota(jnp.int32, logits_t.shape, 0)
    h2_ref[0, :, : h2.shape[1]] = h2
    h2_ref[0, :, h2.shape[1] :] = jnp.where(sub == 0, w_a, jnp.where(sub == 1, w_b, 0.0)).T
    onehot = sub == bucket
    before = _dot(onehot.astype(BF16), tri_ref[...]) + cnt_ref[...]
    rank = jnp.sum(jnp.where(onehot, before, 0.0), axis=0, keepdims=True)
    rsub = lax.broadcasted_iota(jnp.int32, route_ref.shape[1:], 0)
    route_ref[0] = jnp.where(rsub == 0, bucket.astype(F32), jnp.where(rsub == 1, rank, 0.0))
    cnt_ref[...] += jnp.sum(onehot.astype(F32), axis=1, keepdims=True)


def _merge(p, o_hf, o_hb, o_rf, o_rb, yg, x, modtab, layer_prm, const_prm, l, tm, nct, alpha):
    b, s, d = x.shape
    cpt = tm // S5_CHUNK

    def tok(width, col):
        return pl.BlockSpec((1, tm, width), lambda i, j: (i, j, col))

    def whole(a):
        return pl.BlockSpec(a.shape, lambda i, j: (0,) * a.ndim)

    rbase = 5 * HG_WIDTH // RET_WIDTH
    in_specs = [tok(HG_WIDTH, 2), tok(RET_WIDTH, rbase + 3), tok(S5_WIDTH, rbase + 4),
                tok(HG_WIDTH, 0), tok(HG_WIDTH, 0), tok(RET_WIDTH, 0), tok(RET_WIDTH, 0),
                pl.BlockSpec((S5_GROUPS, cpt, S5_FOLD), lambda i, j: (0, j, i)),
                tok(d, 0), _mod_spec(l, nct)]
    in_specs += [_layer_spec(a.shape[1:], l) for a in layer_prm]
    in_specs += [whole(a) for a in const_prm]
    return pl.pallas_call(
        functools.partial(_merge_kernel, alpha=alpha),
        grid=(b, s // tm),
        in_specs=in_specs,
        out_specs=[tok(d, 0), tok(d + LANES, 0), pl.BlockSpec((1, SUBLANES, tm), lambda i, j: (i, 0, j)),
                   pl.BlockSpec((LANES, 1), lambda i, j: (0, 0))],
        out_shape=[jax.ShapeDtypeStruct((b, s, d), F32), jax.ShapeDtypeStruct((b, s, d + LANES), F32),
                   jax.ShapeDtypeStruct((b, SUBLANES, s), F32), jax.ShapeDtypeStruct((LANES, 1), F32)],
        scratch_shapes=[pltpu.VMEM((tm, LANES), F32)],
        compiler_params=_cparams(("arbitrary", "arbitrary")),
        name="merge_ln1_router",
    )(p, p, p, o_hf, o_hb, o_rf, o_rb, yg, x, modtab, *layer_prm, *const_prm)


def _routing_tables(route, counts, te, n_tiles):
    bucket = route[:, 0, :].astype(jnp.int32).reshape(-1)
    rank = route[:, 1, :].astype(jnp.int32).reshape(-1)
    cnt = counts[:N_BUCKETS, 0].astype(jnp.int32)
    padded = (cnt + te - 1) // te * te
    ends = jnp.cumsum(padded)
    pos = (ends - padded)[bucket] + rank
    n_used = ends[-1] // te
    tile = jnp.arange(n_tiles, dtype=jnp.int32)
    tb = jnp.minimum(jnp.searchsorted(ends, tile * te, side="right"), N_BUCKETS - 1).astype(jnp.int32)
    tb = jnp.where(tile < n_used, tb, tb[jnp.maximum(n_used - 1, 0)])
    group, pair = tb // N_PAIRS, tb % N_PAIRS
    ea = group * EXPERTS_PER_GROUP + jnp.asarray(PAIR_SLOT_A, jnp.int32)[pair]
    eb = group * EXPERTS_PER_GROUP + jnp.asarray(PAIR_SLOT_B, jnp.int32)[pair]
    return pos, ea, eb, n_used.reshape(1).astype(jnp.int32)


def _dispatch_kernel(pos_ref, src_ref, init_ref, out_ref, sem, *, tm, tiles_per_row):
    del init_ref
    base = (pl.program_id(0) * tiles_per_row + pl.program_id(1)) * tm

    def row_copy(r):
        return pltpu.make_async_copy(src_ref.at[0, pl.ds(r, 1)], out_ref.at[pl.ds(pos_ref[base + r], 1)], sem)

    def start(g, carry):
        first = pl.multiple_of(g * DMA_UNROLL, DMA_UNROLL)
        for k in range(DMA_UNROLL):
            row_copy(first + k).start(priority=k % 2)
        return carry

    def wait(r, carry):
        row_copy(r).wait()
        return carry

    lax.fori_loop(0, tm // DMA_UNROLL, start, 0)
    lax.fori_loop(0, tm, wait, 0, unroll=DMA_UNROLL)


def _dispatch(pos, rows, init, tm):
    b, s, w = rows.shape
    n_sorted = init.shape[0]
    return pl.pallas_call(
        functools.partial(_dispatch_kernel, tm=tm, tiles_per_row=s // tm),
        grid_spec=pltpu.PrefetchScalarGridSpec(
            num_scalar_prefetch=1,
            grid=(b, s // tm),
            in_specs=[pl.BlockSpec((1, tm, w), lambda i, j, pos_ref: (i, j, 0)),
                      pl.BlockSpec(memory_space=pl.ANY)],
            out_specs=pl.BlockSpec(memory_space=pl.ANY),
            scratch_shapes=[pltpu.SemaphoreType.DMA(())],
        ),
        out_shape=jax.ShapeDtypeStruct((n_sorted, w), F32),
        input_output_aliases={2: 0},
        compiler_params=_cparams(("arbitrary", "arbitrary")),
        name="moe_dispatch",
    )(pos, rows, init)


def _expert_kernel(ea_ref, eb_ref, nused_ref, hs_ref, wga_ref, wua_ref, wda_ref, wgb_ref, wub_ref, wdb_ref,
                   o_ref, wg_scr, wu_scr, wd_scr):
    t = pl.program_id(0)
    prev = jnp.maximum(t - 1, 0)
    @pl.when((t == 0) | (ea_ref[t] != ea_ref[prev]))
    def _():
        wg_scr[0] = wga_ref[0, 0].astype(BF16)
        wu_scr[0] = wua_ref[0, 0].astype(BF16)
        wd_scr[0] = wda_ref[0, 0].astype(BF16)

    @pl.when((t == 0) | (eb_ref[t] != eb_ref[prev]))
    def _():
        wg_scr[1] = wgb_ref[0, 0].astype(BF16)
        wu_scr[1] = wub_ref[0, 0].astype(BF16)
        wd_scr[1] = wdb_ref[0, 0].astype(BF16)

    @pl.when(t < nused_ref[0])
    def _():
        d = o_ref.shape[1]
        h = hs_ref[:, :d].astype(BF16)
        y = jnp.zeros(o_ref.shape, F32)
        for e in range(2):
            act = (_silu(_dot(h, wg_scr[e])) * _dot(h, wu_scr[e])).astype(BF16)
            y = y + hs_ref[:, d + e : d + e + 1] * _dot(act, wd_scr[e])
        o_ref[...] = y

    @pl.when(t >= nused_ref[0])
    def _():
        o_ref[...] = jnp.zeros_like(o_ref)


def _experts(ea, eb, n_used, hs, wg, wu, wd, l, te):
    n_sorted, w = hs.shape
    _, _, d, eh = wg.shape

    def wspec(shape, which):
        return pl.BlockSpec((1, 1) + shape, lambda t, ea_ref, eb_ref, n_ref: (l, (ea_ref, eb_ref)[which][t], 0, 0))

    return pl.pallas_call(
        _expert_kernel,
        grid_spec=pltpu.PrefetchScalarGridSpec(
            num_scalar_prefetch=3,
            grid=(n_sorted // te,),
            in_specs=[pl.BlockSpec((te, w), lambda t, *_: (t, 0)),
                      wspec((d, eh), 0), wspec((d, eh), 0), wspec((eh, d), 0),
                      wspec((d, eh), 1), wspec((d, eh), 1), wspec((eh, d), 1)],
            out_specs=pl.BlockSpec((te, d), lambda t, *_: (t, 0)),
            scratch_shapes=[pltpu.VMEM((2, d, eh), BF16), pltpu.VMEM((2, d, eh), BF16), pltpu.VMEM((2, eh, d), BF16)],
        ),
        out_shape=jax.ShapeDtypeStruct((n_sorted, d), F32),
        compiler_params=_cparams(("arbitrary",)),
        name="moe_experts",
    )(ea, eb, n_used, hs, wg, wu, wd, wg, wu, wd)


def _combine_kernel(pos_ref, ys_ref, x1_ref, m_ref, lng_ref, lnb_ref, o_ref, buf, sem,
                    *, tm, tiles_per_row, first_tile, alpha):
    cols = pl.num_programs(1)
    n_steps = pl.num_programs(0) * cols
    step = pl.program_id(0) * cols + pl.program_id(1)
    slot = step % 2

    def row_copy(step_, slot_, r):
        tile = (step_ // cols) * tiles_per_row + first_tile + step_ % cols
        src = ys_ref.at[pl.ds(pos_ref[tile * tm + r], 1)]
        return pltpu.make_async_copy(src, buf.at[slot_, pl.ds(r, 1)], sem.at[slot_])

    def start_tile(step_, slot_):
        def body(g, carry):
            first = pl.multiple_of(g * DMA_UNROLL, DMA_UNROLL)
            for k in range(DMA_UNROLL):
                row_copy(step_, slot_, first + k).start(priority=k % 2)
            return carry

        lax.fori_loop(0, tm // DMA_UNROLL, body, 0)

    @pl.when(step == 0)
    def _():
        start_tile(0, 0)

    @pl.when(step + 1 < n_steps)
    def _():
        start_tile(step + 1, 1 - slot)

    def wait(r, carry):
        row_copy(step, slot, r).wait()
        return carry

    lax.fori_loop(0, tm, wait, 0, unroll=DMA_UNROLL)
    m = m_ref[0, 0, 0]
    o_ref[0] = _layer_norm(alpha * x1_ref[0] + m[5:6, :] * buf[slot], lng_ref[0], lnb_ref[0])


def _combine(pos, ys, x1, modtab, ln_g, ln_b, l, tm, nct, alpha, first_tile):
    b, s, d = x1.shape
    cols = s // tm - first_tile
    mod_spec = pl.BlockSpec((1, 1, 1, 6, d),
                            lambda i, j, pos_ref: (l, i, jnp.minimum((j + first_tile) // nct, 1), 0, 0))
    return pl.pallas_call(
        functools.partial(_combine_kernel, tm=tm, tiles_per_row=s // tm, first_tile=first_tile, alpha=alpha),
        grid_spec=pltpu.PrefetchScalarGridSpec(
            num_scalar_prefetch=1,
            grid=(b, cols),
            in_specs=[pl.BlockSpec(memory_space=pl.ANY),
                      pl.BlockSpec((1, tm, d), lambda i, j, pos_ref: (i, j + first_tile, 0)),
                      mod_spec, _layer_spec((1, d), l), _layer_spec((1, d), l)],
            out_specs=pl.BlockSpec((1, tm, d), lambda i, j, pos_ref: (i, j, 0)),
            scratch_shapes=[pltpu.VMEM((2, tm, d), F32), pltpu.SemaphoreType.DMA((2,))],
        ),
        out_shape=jax.ShapeDtypeStruct((b, cols * tm, d), F32),
        compiler_params=_cparams(("arbitrary", "arbitrary")),
        name="moe_combine_ln2",
    )(pos, ys, x1, modtab, ln_g, ln_b)


def _block_avg(width, group):
    idx = np.arange(width) // group
    return jnp.asarray((idx[:, None] == idx[None, :]).astype(np.float32) / group, dtype=BF16)


def kernel(x, c, ctx, c_ctx, w_mod, b_mod, w_in, hg_lb_raw, hg_norm_g, ret_decay_raw, ret_gn_g, ret_gn_b, s5_lam_re, s5_lam_im, s5_log_dt, s5_b_re, s5_b_im, s5_c_re, s5_c_im, s5_d, s5_glu_w, s5_glu_b, w_out, ln1_g, ln1_b, ln2_g, ln2_b, rg_w, rg_b, re_w, re_b, exp_w_gate, exp_w_up, exp_w_down):
    bsz, t_lat, d = x.shape
    t_ctx = ctx.shape[1]
    depth = w_mod.shape[0]
    alpha = (2.0 * depth) ** 0.25
    tm = TOKEN_TILE
    assert d == D_MODEL and t_lat % GRID_W == 0 and bsz < SUBLANES
    assert t_ctx % tm == 0 and t_lat % tm == 0, "context and latent lengths must be multiples of the token tile"
    nct = t_ctx // tm
    s = t_ctx + t_lat

    cvec = jnp.concatenate([c, c_ctx[None, :], jnp.zeros((SUBLANES - bsz - 1, d), F32)], 0)
    mod_all = _modulation(cvec, w_mod, b_mod)
    lat = mod_all[:, :bsz].reshape(depth, bsz, 6, d)
    cm = jnp.broadcast_to(mod_all[:, bsz].reshape(depth, 1, 6, d), (depth, bsz, 6, d))
    modtab = jnp.stack([cm, lat], axis=2)

    hg_lb = jnp.cumsum(jax.nn.softmax(hg_lb_raw.astype(F32), axis=0), axis=0)
    hg_lb = hg_lb - hg_lb[:1]
    ret_tables = _retention_decay_tables(jax.nn.log_sigmoid(ret_decay_raw.astype(F32)))
    cos_tab, sin_tab = _rope_tables(t_lat, t_ctx)
    s5_tabs = _s5_weights(s5_lam_re, s5_lam_im, s5_log_dt, s5_b_re, s5_b_im, s5_c_re, s5_c_im)
    pad_r = LANES - N_GROUPS - N_EXPERTS
    wr = jnp.concatenate([rg_w, re_w.reshape(depth, d, N_EXPERTS), jnp.zeros((depth, d, pad_r), F32)], axis=2)
    wr = jnp.swapaxes(wr, 1, 2)
    wr_hi, wr_lo = _split_bf16(wr)
    br = jnp.concatenate([rg_b, re_b.reshape(depth, N_EXPERTS), jnp.zeros((depth, pad_r), F32)], axis=1)[:, :, None]
    row = lambda a: a[:, None, :]
    layer_prm = [row(jnp.tile(hg_norm_g, (1, HG_HEADS))), row(ret_gn_g), row(ret_gn_b), row(s5_d),
                 s5_glu_w.astype(BF16), row(s5_glu_b), w_out.astype(BF16), row(ln1_g), row(ln1_b),
                 wr_hi, wr_lo, br]
    tri = jnp.asarray(np.triu(np.ones((tm, tm), np.float32), 1), dtype=BF16)
    const_prm = [_block_avg(HG_WIDTH, HG_DK), _block_avg(RET_WIDTH, RET_DK), tri]
    w_in_bf16 = w_in.astype(BF16)
    ln2_g, ln2_b = row(ln2_g), row(ln2_b)
    te = EXPERT_TILE
    n_tiles = -(-(bsz * s + N_BUCKETS * (te - 1)) // te)

    hs = jnp.zeros((n_tiles * te, d + LANES), F32)
    xs = jnp.concatenate([ctx, x], axis=1)
    for l in range(depth):
        p, ug = _inproj(xs, modtab, w_in_bf16, l, tm, nct)
        o_hf, o_hb = _hgrn(p, hg_lb, l, t_ctx // CHUNK)
        o_rf, o_rb = _retention(p, cos_tab, sin_tab, ret_tables, l, t_ctx // RET_CHUNK)
        yg = _s5_conv(ug, *s5_tabs, l, bsz, t_ctx // S5_CHUNK)
        x1, rows, route, counts = _merge(p, o_hf, o_hb, o_rf, o_rb, yg, xs, modtab, layer_prm, const_prm,
                                         l, tm, nct, alpha)
        pos, ea, eb, n_used = _routing_tables(route, counts, te, n_tiles)
        hs = _dispatch(pos, rows, hs, tm)
        ys = _experts(ea, eb, n_used, hs, exp_w_gate, exp_w_up, exp_w_down, l, te)
        xs = _combine(pos, ys, x1, modtab, ln2_g, ln2_b, l, tm, nct, alpha, nct if l == depth - 1 else 0)
    return xs
```

```python
import functools

import numpy as np
import jax
import jax.numpy as jnp
from jax import lax
from jax.experimental import pallas as pl
from jax.experimental.pallas import tpu as pltpu

F32 = jnp.float32
BF16 = jnp.bfloat16

D_MODEL = 1024
HG_WIDTH = 512
HG_HEADS = 4
HG_DK = HG_WIDTH // HG_HEADS
RET_WIDTH = 256
RET_HEADS = 4
RET_DK = RET_WIDTH // RET_HEADS
S5_WIDTH = 256
S5_GROUP_CH = 16
S5_GROUPS = S5_WIDTH // S5_GROUP_CH
S5_STATE = 64
IN_COLS = 5 * HG_WIDTH + 4 * RET_WIDTH + S5_WIDTH
CHUNK = 64
HGRN_BATCH = 4
RET_CHUNK = 256
N_GROUPS = 4
EXPERTS_PER_GROUP = 4
N_EXPERTS = N_GROUPS * EXPERTS_PER_GROUP
N_PAIRS = EXPERTS_PER_GROUP * (EXPERTS_PER_GROUP - 1) // 2
N_BUCKETS = N_GROUPS * N_PAIRS
PAIR_SLOT_A = (0, 2, 2, 3, 3, 3)
PAIR_SLOT_B = (1, 1, 0, 0, 1, 2)
EXPERT_HIDDEN = D_MODEL // 2
LN_EPS = 1e-5
ROPE_BASE = 10000.0
GRID_W = 64

LANES = 128
SUBLANES = 8
TOKEN_TILE = 256
S5_CHUNK = 32
S5_FOLD = S5_CHUNK * S5_GROUP_CH
EXPERT_TILE = 256
DMA_UNROLL = 8
VMEM_LIMIT = 56 * 1024 * 1024


def _cparams(sem):
    return pltpu.CompilerParams(dimension_semantics=sem, vmem_limit_bytes=VMEM_LIMIT)


def _split_bf16(x):
    hi = x.astype(BF16)
    lo = (x - hi.astype(F32)).astype(BF16)
    return hi, lo


def _dot(a, b):
    return jnp.dot(a, b, preferred_element_type=F32)


def _dot3(a, b):
    ah, al = _split_bf16(a)
    bh, bl = _split_bf16(b)
    return _dot(ah, bh) + _dot(ah, bl) + _dot(al, bh)


def _dot2(a, b_bf16):
    ah, al = _split_bf16(a)
    return _dot(ah, b_bf16) + _dot(al, b_bf16)


def _dot_nt(a, b):
    return lax.dot_general(a, b, (((1,), (1,)), ((), ())), preferred_element_type=F32)


def _dot_tn(a, b):
    return lax.dot_general(a, b, (((0,), (0,)), ((), ())), preferred_element_type=F32)


def _silu(x):
    return x * jax.nn.sigmoid(x)


def _layer_spec(shape, l):
    zeros = (0,) * len(shape)
    return pl.BlockSpec((1,) + tuple(shape), lambda *_: (l,) + zeros)


def _mod_spec(l, nct):
    return pl.BlockSpec((1, 1, 1, 6, D_MODEL), lambda i, j, *_: (l, i, jnp.minimum(j // nct, 1), 0, 0))


def _swap_sublane_lanegroup(v):
    n = v.shape[0]
    r = lax.broadcasted_iota(jnp.int32, v.shape, 0)
    l = lax.broadcasted_iota(jnp.int32, v.shape, 1)
    for k in range(3):
        rb = (r >> k) & 1
        gb = (l >> (4 + k)) & 1
        sh = S5_GROUP_CH << k
        st = 1 << k
        a = pltpu.roll(pltpu.roll(v, LANES - sh, 1), st, 0)
        b = pltpu.roll(pltpu.roll(v, sh, 1), n - st, 0)
        v = jnp.where(rb == gb, v, jnp.where(rb == 1, a, b))
    return v


def _mod_kernel(c_ref, w_ref, b_ref, o_ref):
    sc = _silu(c_ref[...])
    o_ref[0] = _dot3(sc, w_ref[0]) + b_ref[0]


def _modulation(cvec, w_mod, b_mod):
    depth, d, n = w_mod.shape
    rows = cvec.shape[0]
    tn = 1536
    return pl.pallas_call(
        _mod_kernel,
        grid=(depth, n // tn),
        in_specs=[
            pl.BlockSpec((rows, d), lambda l, j: (0, 0)),
            pl.BlockSpec((1, d, tn), lambda l, j: (l, 0, j)),
            pl.BlockSpec((1, 1, tn), lambda l, j: (l, 0, j)),
        ],
        out_specs=pl.BlockSpec((1, rows, tn), lambda l, j: (l, 0, j)),
        out_shape=jax.ShapeDtypeStruct((depth, rows, n), F32),
        compiler_params=_cparams(("arbitrary", "arbitrary")),
        name="modulation",
    )(cvec, w_mod, b_mod.reshape(depth, 1, n))


def _inproj_kernel(x_ref, m_ref, w_ref, o_ref, ug_ref, wt_scr):
    _project(x_ref[0], m_ref, w_ref, o_ref, ug_ref, wt_scr)


def _ln2_inproj_kernel(pos_ref, ys_ref, x1_ref, mprev_ref, lng_ref, lnb_ref, m_ref, w_ref,
                       x2_ref, o_ref, ug_ref, buf, sem, wt_scr, *, tm, tiles_per_row, alpha):
    y = _gather_expert_rows(pos_ref, ys_ref, buf, sem, tm=tm, tiles_per_row=tiles_per_row, first_tile=0)
    g2 = mprev_ref[0, 0, 0][5:6, :]
    x2 = _layer_norm(alpha * x1_ref[0] + g2 * y, lng_ref[0], lnb_ref[0])
    x2_ref[0] = x2
    _project(x2, m_ref, w_ref, o_ref, ug_ref, wt_scr)


def _project(x, m_ref, w_ref, o_ref, ug_ref, wt_scr):
    m = m_ref[0, 0, 0]
    h = x * (1.0 + m[1:2, :]) + m[0:1, :]
    p = _dot(h.astype(BF16), w_ref[0])
    o_ref[0] = p
    rows_per_chunk = S5_CHUNK // SUBLANES
    n_chunks = p.shape[0] // S5_CHUNK
    for half in range(S5_WIDTH // LANES):
        lo = IN_COLS - S5_WIDTH + half * LANES
        wt_scr[...] = _swap_sublane_lanegroup(p[:, lo : lo + LANES])
        for g_lo in range(SUBLANES):
            for s_hi in range(rows_per_chunk):
                piece = wt_scr[pl.ds(s_hi * SUBLANES + g_lo, n_chunks, stride=S5_CHUNK), :]
                ug_ref[half * SUBLANES + g_lo, :, s_hi * LANES : (s_hi + 1) * LANES] = piece


def _inproj(x, modtab, w_in_bf16, l, tm, nct):
    b, s, d = x.shape
    n = w_in_bf16.shape[-1]
    cpt = tm // S5_CHUNK
    return pl.pallas_call(
        _inproj_kernel,
        grid=(b, s // tm),
        in_specs=[
            pl.BlockSpec((1, tm, d), lambda i, j: (i, j, 0)),
            _mod_spec(l, nct),
            _layer_spec((d, n), l),
        ],
        out_specs=[pl.BlockSpec((1, tm, n), lambda i, j: (i, j, 0)),
                   pl.BlockSpec((S5_GROUPS, cpt, S5_FOLD), lambda i, j: (0, j, i))],
        out_shape=[jax.ShapeDtypeStruct((b, s, n), F32),
                   jax.ShapeDtypeStruct((S5_GROUPS, s // S5_CHUNK, b * S5_FOLD), F32)],
        scratch_shapes=[pltpu.VMEM((tm, LANES), F32)],
        compiler_params=_cparams(("arbitrary", "arbitrary")),
        name="inproj",
    )(x, modtab, w_in_bf16)


def _ln2_inproj(pos, ys, x1, modtab, ln_g, ln_b, w_in_bf16, l, tm, nct, alpha):
    b, s, d = x1.shape
    n = w_in_bf16.shape[-1]
    cpt = tm // S5_CHUNK
    tok = lambda width: pl.BlockSpec((1, tm, width), lambda i, j, pos_ref: (i, j, 0))
    return pl.pallas_call(
        functools.partial(_ln2_inproj_kernel, tm=tm, tiles_per_row=s // tm, alpha=alpha),
        grid_spec=pltpu.PrefetchScalarGridSpec(
            num_scalar_prefetch=1,
            grid=(b, s // tm),
            in_specs=[pl.BlockSpec(memory_space=pl.ANY), tok(d),
                      _mod_spec(l - 1, nct), _layer_spec((1, d), l - 1), _layer_spec((1, d), l - 1),
                      _mod_spec(l, nct), _layer_spec((d, n), l)],
            out_specs=[tok(d), tok(n),
                       pl.BlockSpec((S5_GROUPS, cpt, S5_FOLD), lambda i, j, pos_ref: (0, j, i))],
            scratch_shapes=[pltpu.VMEM((2, tm, d), F32), pltpu.SemaphoreType.DMA((2,)),
                            pltpu.VMEM((tm, LANES), F32)],
        ),
        out_shape=[jax.ShapeDtypeStruct((b, s, d), F32), jax.ShapeDtypeStruct((b, s, n), F32),
                   jax.ShapeDtypeStruct((S5_GROUPS, s // S5_CHUNK, b * S5_FOLD), F32)],
        compiler_params=_cparams(("arbitrary", "arbitrary")),
        name="ln2_inproj",
    )(pos, ys, x1, modtab, ln_g, ln_b, modtab, w_in_bf16)


def _block_gate_products(f, reverse):
    c = f.shape[0]
    row = lax.broadcasted_iota(jnp.int32, (c, 1), 0)
    a, z, b = f, f, jnp.ones_like(f)
    out = []
    s = 1
    while s < c:
        out.append((a, b))
        if s < SUBLANES:
            z3 = z.reshape(c // SUBLANES, SUBLANES, z.shape[1])
            up = pltpu.roll(z3, s, 1).reshape(z.shape)
            dn = pltpu.roll(z3, SUBLANES - s, 1).reshape(z.shape)
            odd = (row & s) != 0
            if reverse:
                a = a * jnp.where(odd, 1.0, dn)
                b = b * jnp.where(odd, up, 1.0)
            else:
                a = a * jnp.where(odd, up, 1.0)
                b = b * jnp.where(odd, 1.0, dn)
            z = z * jnp.where(odd, up, dn)
        else:
            na, nb, nz = [], [], []
            for lo in range(0, c, 2 * s):
                ev, od = slice(lo, lo + s), slice(lo + s, lo + 2 * s)
                zz = z[ev] * z[od]
                if reverse:
                    na += [a[ev] * z[od], a[od]]
                    nb += [b[ev], b[od] * z[ev]]
                else:
                    na += [a[ev], a[od] * z[ev]]
                    nb += [b[ev] * z[od], b[od]]
                nz += [zz, zz]
            a, b, z = (jnp.concatenate(t, axis=0) for t in (na, nb, nz))
        s *= 2
    out.append((a, b))
    return out, z


def _hgrn_kernel(qf_ref, vf_ref, zf_ref, qb_ref, vb_ref, zb_ref, lb_ref, of_ref, ob_ref, st_ref):
    c = CHUNK

    @pl.when(pl.program_id(1) == 0)
    def _():
        st_ref[...] = jnp.zeros_like(st_ref)

    ri = lax.broadcasted_iota(jnp.int32, (c, c), 0)
    ci = lax.broadcasted_iota(jnp.int32, (c, c), 1)
    for d, (q_ref, v_ref, z_ref, o_ref) in enumerate(
        ((qf_ref, vf_ref, zf_ref, of_ref), (qb_ref, vb_ref, zb_ref, ob_ref))
    ):
        reverse = d == 1
        causal = (ri < ci) if reverse else (ri > ci)
        n_levels = c.bit_length() - 1
        masks = [(((ri >> lvl) ^ (ci >> lvl)) == 1) & causal for lvl in range(n_levels)]
        for bi, h in [(bi, h) for bi in range(q_ref.shape[0]) for h in range(HG_HEADS)]:
            hs = slice(h * HG_DK, (h + 1) * HG_DK)
            q = _silu(q_ref[bi, :, hs])
            v = v_ref[bi, :, hs].astype(BF16)
            lb = lb_ref[0, d : d + 1, hs]
            f = lb + (1.0 - lb) * jax.nn.sigmoid(z_ref[bi, :, hs])
            k = 1.0 - f
            levels, tot = _block_gate_products(f, reverse)
            scores = jnp.where(ri == ci, _dot_nt(q.astype(BF16), k.astype(BF16)), 0.0)
            for mask, (a, bb) in zip(masks, levels):
                scores = scores + jnp.where(mask, _dot_nt((q * a).astype(BF16), (k * bb).astype(BF16)), 0.0)
            a_full, b_full = levels[-1]
            st = st_ref[bi, d, h]
            o_ref[bi, :, hs] = (_dot(scores.astype(BF16), v)
                                + _dot_nt((q * a_full).astype(BF16), st.astype(BF16)))
            st_ref[bi, d, h] = st * tot[0:1, :] + _dot_tn(v, (k * b_full).astype(BF16))


def _bwd_chunk(n, nc_ctx, nc_all):
    return jnp.where(n < nc_ctx, nc_ctx - 1 - n, nc_all + nc_ctx - 1 - n)


def _hgrn(p, lb_all, l, nc_ctx):
    b, s, _ = p.shape
    nc = s // CHUNK
    w = HG_WIDTH

    bb = HGRN_BATCH if b % HGRN_BATCH == 0 else 1

    def fwd(col):
        return pl.BlockSpec((bb, CHUNK, w), lambda i, n: (i, n, col))

    def bwd(col):
        return pl.BlockSpec((bb, CHUNK, w), lambda i, n: (i, _bwd_chunk(n, nc_ctx, nc), col))

    out = jax.ShapeDtypeStruct((b, s, w), F32)
    return pl.pallas_call(
        _hgrn_kernel,
        grid=(b // bb, nc),
        in_specs=[fwd(0), fwd(1), fwd(3), bwd(0), bwd(1), bwd(4), _layer_spec((2, w), l)],
        out_specs=[fwd(0), bwd(0)],
        out_shape=[out, out],
        scratch_shapes=[pltpu.VMEM((bb, 2, HG_HEADS, HG_DK, HG_DK), F32)],
        compiler_params=_cparams(("arbitrary", "arbitrary")),
        name="hgrn2_scan",
    )(p, p, p, p, p, p, lb_all)


def _swap_halves(x, half):
    n = x.shape[-1]
    lane = lax.broadcasted_iota(jnp.int32, (1, n), 1)
    lower = (lane & half) == 0
    return jnp.where(lower, pltpu.roll(x, n - half, 1), pltpu.roll(x, half, 1))


def _ret_kernel(qf_ref, kf_ref, vf_ref, cf_ref, sf_ref, qb_ref, kb_ref, vb_ref, cb_ref, sb_ref,
                dmat_ref, rq_ref, rk_ref, cd_ref, of_ref, ob_ref, st_ref):
    @pl.when(pl.program_id(1) == 0)
    def _():
        st_ref[...] = jnp.zeros_like(st_ref)

    half = RET_DK // 4
    for d, (q_ref, k_ref, v_ref, c_ref, s_ref, o_ref) in enumerate(
        ((qf_ref, kf_ref, vf_ref, cf_ref, sf_ref, of_ref), (qb_ref, kb_ref, vb_ref, cb_ref, sb_ref, ob_ref))
    ):
        cos = c_ref[...]
        sin = s_ref[...]
        q = q_ref[0]
        k = k_ref[0] * (RET_DK ** -0.5)
        q = q * cos + _swap_halves(q, half) * sin
        k = k * cos + _swap_halves(k, half) * sin
        v = v_ref[0]
        vt = v.T.astype(BF16)
        v = v.astype(BF16)
        q0 = q.astype(BF16)
        k0 = k.astype(BF16)
        qd = (q * rq_ref[0, d]).astype(BF16)
        kd = (k * rk_ref[0, d]).astype(BF16)
        cd = cd_ref[0, d : d + 1, :]
        lane = lax.broadcasted_iota(jnp.int32, (1, LANES), 1)
        left = lane < RET_DK
        same_head = (lax.broadcasted_iota(jnp.int32, (LANES, 1), 0) < RET_DK) == left
        zero = jnp.zeros((), BF16)
        for p in range(RET_HEADS // 2):
            ps = slice(p * LANES, (p + 1) * LANES)
            kp, vp = k0[:, ps], v[:, ps]
            k_blk = jnp.concatenate([jnp.where(left, kp, zero), jnp.where(left, zero, kp)], axis=0)
            v_blk = jnp.concatenate([jnp.where(left, vp, zero), jnp.where(left, zero, vp)], axis=0)
            scores = _dot_nt(q0[:, ps], k_blk) * dmat_ref[0, d, p]
            st = st_ref[d, p]
            o_ref[0, :, ps] = _dot(scores.astype(BF16), v_blk) + _dot_nt(qd[:, ps], st.astype(BF16))
            st_ref[d, p] = st * cd[:, ps] + jnp.where(same_head, _dot(vt[ps, :], kd[:, ps]), 0.0)


def _retention(p, cos_tab, sin_tab, tables, l, nc_ctx):
    b, s, _ = p.shape
    nc = s // RET_CHUNK
    w = RET_WIDTH
    base = 5 * HG_WIDTH // w

    def fwd(col):
        return pl.BlockSpec((1, RET_CHUNK, w), lambda i, n: (i, n, col))

    def bwd(col):
        return pl.BlockSpec((1, RET_CHUNK, w), lambda i, n: (i, _bwd_chunk(n, nc_ctx, nc), col))

    tab_f = pl.BlockSpec((RET_CHUNK, w), lambda i, n: (n, 0))
    tab_b = pl.BlockSpec((RET_CHUNK, w), lambda i, n: (_bwd_chunk(n, nc_ctx, nc), 0))
    out = jax.ShapeDtypeStruct((b, s, w), F32)
    return pl.pallas_call(
        _ret_kernel,
        grid=(b, nc),
        in_specs=[fwd(base), fwd(base + 1), fwd(base + 2), tab_f, tab_f,
                  bwd(base), bwd(base + 1), bwd(base + 2), tab_b, tab_b]
                 + [_layer_spec(t.shape[1:], l) for t in tables],
        out_specs=[fwd(0), bwd(0)],
        out_shape=[out, out],
        scratch_shapes=[pltpu.VMEM((2, RET_HEADS // 2, LANES, LANES), F32)],
        compiler_params=_cparams(("arbitrary", "arbitrary")),
        name="retention_scan",
    )(p, p, p, cos_tab, sin_tab, p, p, p, cos_tab, sin_tab, *tables)


def _rope_tables(t_lat, t_ctx):
    m = RET_DK // 4
    inv = ROPE_BASE ** (-jnp.arange(m, dtype=F32) / m)
    rows = jnp.repeat(jnp.arange(t_lat // GRID_W, dtype=jnp.int32), GRID_W).astype(F32)
    cols = jnp.tile(jnp.arange(GRID_W, dtype=jnp.int32), t_lat // GRID_W).astype(F32)

    def half_tables(pos):
        ang = pos[:, None] * inv
        c, s = jnp.cos(ang), jnp.sin(ang)
        return jnp.concatenate([c, c], -1), jnp.concatenate([-s, s], -1)

    cr, sr = half_tables(rows)
    cc, sc = half_tables(cols)
    cos_h = jnp.concatenate([cr, cc], -1)
    sin_h = jnp.concatenate([sr, sc], -1)
    cos = jnp.tile(cos_h, (1, RET_HEADS))
    sin = jnp.tile(sin_h, (1, RET_HEADS))
    cos = jnp.concatenate([jnp.ones((t_ctx, RET_WIDTH), F32), cos], 0)
    sin = jnp.concatenate([jnp.zeros((t_ctx, RET_WIDTH), F32), sin], 0)
    return cos, sin


def _retention_decay_tables(log_gamma):
    c = RET_CHUNK
    i = jnp.arange(c, dtype=F32)
    diff = i[:, None] - i[None, :]
    lg = log_gamma[:, :, :, None, None]
    d_f = jnp.where(diff >= 0, jnp.exp(lg[:, 0] * diff), 0.0)
    d_b = jnp.where(diff <= 0, jnp.exp(lg[:, 1] * (-diff)), 0.0)
    dmat = jnp.stack([d_f, d_b], 1)
    dmat = jnp.concatenate([dmat[:, :, 0::2], dmat[:, :, 1::2]], axis=-1)
    lane_lg = jnp.repeat(log_gamma, RET_DK, axis=2)[:, :, None, :]
    col = i[None, :, None]
    rq = jnp.stack([jnp.exp(lane_lg[:, 0] * (col + 1.0)), jnp.exp(lane_lg[:, 1] * (c - col))], 1)
    rk = jnp.stack([jnp.exp(lane_lg[:, 0] * (c - 1.0 - col)), jnp.exp(lane_lg[:, 1] * col)], 1)
    cdec = jnp.exp(lane_lg[:, :, 0, :] * c)
    return dmat, rq, rk, cdec


def _s5_kernel(u_ref, base_ref, inj_ref, w2_ref, ac_ref, y_ref, toep_scr, a_scr, i_scr, hf_scr, hb_scr,
               *, n_ctx, n_all, bsz):
    width = S5_FOLD
    st2 = 2 * S5_STATE
    ch = S5_GROUP_CH
    lane = lax.broadcasted_iota(jnp.int32, (ch, width), 1)
    base_f = base_ref[0, 0, 0]
    base_b = base_ref[0, 0, 1]
    for s in range(S5_CHUNK):
        sh_f = s * ch
        sh_b = (S5_CHUNK - 1 - s) * ch
        part_f = jnp.where(lane >= sh_f, pltpu.roll(base_f, sh_f, 1), 0.0) if sh_f else base_f
        part_b = jnp.where(lane < width - sh_b, pltpu.roll(base_b, width - sh_b, 1), 0.0) if sh_b else base_b
        toep_scr[s * ch : (s + 1) * ch, :] = (part_f + part_b).astype(BF16)
    for b in range(bsz):
        ub = u_ref[0, :, b * width : (b + 1) * width].astype(BF16)
        a_scr[b] = _dot(ub, toep_scr[...])
        i_scr[b] = _dot(ub, inj_ref[0, 0])
    ac = ac_ref[0, 0]

    zero = jnp.zeros((1, st2), F32)
    state = [(zero, zero)] * (2 * bsz)
    for n in range(n_all):
        nb = n_ctx - 1 - n if n < n_ctx else n_all + n_ctx - 1 - n
        for b in range(bsz):
            for d, (row, scr) in enumerate(((n, hf_scr), (nb, hb_scr))):
                h, hsw = state[2 * b + d]
                scr[b, row : row + 1, :] = h
                inj = i_scr[b, row : row + 1, d * st2 : (d + 1) * st2]
                inj_sw = i_scr[b, row : row + 1, (d + 2) * st2 : (d + 3) * st2]
                ar, ai = ac[2 * d : 2 * d + 1, :], ac[2 * d + 1 : 2 * d + 2, :]
                state[2 * b + d] = (h * ar + hsw * ai + inj, hsw * ar - h * ai + inj_sw)
    for b in range(bsz):
        y_ref[0, :, b * width : (b + 1) * width] = (
            a_scr[b]
            + _dot(hf_scr[b].astype(BF16), w2_ref[0, 0, 0])
            + _dot(hb_scr[b].astype(BF16), w2_ref[0, 0, 1]))


def _s5_conv(ug, base, inj, w2, ac, l, bsz, n_ctx):
    g, n_all, _ = ug.shape
    width = S5_FOLD
    st2 = 2 * S5_STATE
    return pl.pallas_call(
        functools.partial(_s5_kernel, n_ctx=n_ctx, n_all=n_all, bsz=bsz),
        grid=(g,),
        in_specs=[
            pl.BlockSpec((1, n_all, bsz * width), lambda i: (i, 0, 0)),
            pl.BlockSpec((1, 1, 2, S5_GROUP_CH, width), lambda i: (l, i, 0, 0, 0)),
            pl.BlockSpec((1, 1, width, 4 * st2), lambda i: (l, i, 0, 0)),
            pl.BlockSpec((1, 1, 2, st2, width), lambda i: (l, i, 0, 0, 0)),
            pl.BlockSpec((1, 1, SUBLANES, st2), lambda i: (l, i, 0, 0)),
        ],
        out_specs=pl.BlockSpec((1, n_all, bsz * width), lambda i: (i, 0, 0)),
        out_shape=jax.ShapeDtypeStruct((g, n_all, bsz * width), F32),
        scratch_shapes=[pltpu.VMEM((width, width), BF16),
                        pltpu.VMEM((bsz, n_all, width), F32), pltpu.VMEM((bsz, n_all, 4 * st2), F32),
                        pltpu.VMEM((bsz, n_all, st2), F32), pltpu.VMEM((bsz, n_all, st2), F32)],
        compiler_params=_cparams(("arbitrary",)),
        name="s5_conv",
    )(ug, base, inj, w2, ac)


def _s5_weights(lam_re, lam_im, log_dt, b_re, b_im, c_re, c_im):
    cs, ch, p, g = S5_CHUNK, S5_GROUP_CH, S5_STATE, S5_GROUPS
    nl = lam_re.shape[0]
    lam_re = jnp.minimum(lam_re.astype(F32), -1e-4)
    lam_im = lam_im.astype(F32)
    dt = jnp.exp(log_dt.astype(F32))[..., None]
    mag = jnp.exp(dt * lam_re)
    abar_re, abar_im = mag * jnp.cos(dt * lam_im), mag * jnp.sin(dt * lam_im)
    den = jnp.square(lam_re) + jnp.square(lam_im)
    nr, ni = abar_re - 1.0, abar_im
    coef_re = ((nr * lam_re + ni * lam_im) / den)[..., None]
    coef_im = ((ni * lam_re - nr * lam_im) / den)[..., None]
    b_re, b_im = b_re.astype(F32), b_im.astype(F32)
    bb_re = coef_re * b_re - coef_im * b_im
    bb_im = coef_re * b_im + coef_im * b_re
    ct_re = jnp.swapaxes(c_re.astype(F32), -1, -2)
    ct_im = jnp.swapaxes(c_im.astype(F32), -1, -2)

    def powers(expo):
        e = expo.astype(F32)[None, :, None, None, :]
        m = jnp.exp(e * (dt * lam_re)[..., None])
        return m * jnp.cos(e * (dt * lam_im)[..., None]), m * jnp.sin(e * (dt * lam_im)[..., None])

    slots = jnp.arange(cs + 1)
    pr, pi = powers(jnp.stack([slots, cs - slots]))
    r_re = (pr[..., None] * ct_re[..., None, :] - pi[..., None] * ct_im[..., None, :]).reshape(nl, 2, g, p, (cs + 1) * ch)
    r_im = (pr[..., None] * ct_im[..., None, :] + pi[..., None] * ct_re[..., None, :]).reshape(nl, 2, g, p, (cs + 1) * ch)
    hi = lax.Precision.HIGHEST
    kern = (jnp.einsum('ldgpe,ldgpn->ldgen', bb_re, r_re, precision=hi)
            - jnp.einsum('ldgpe,ldgpn->ldgen', bb_im, r_im, precision=hi))
    w = cs * ch
    base = jnp.stack([kern[:, 0, :, :, :w], kern[:, 1, :, :, ch:]], axis=2)
    w2 = jnp.stack([jnp.concatenate([r_re[:, 0, :, :, ch:], -r_im[:, 0, :, :, ch:]], axis=2),
                    jnp.concatenate([r_re[:, 1, :, :, :w], -r_im[:, 1, :, :, :w]], axis=2)], axis=2)
    steps = jnp.arange(cs)
    qr, qi = powers(jnp.stack([cs - 1 - steps, steps]))
    qr = jnp.swapaxes(qr, -1, -2)[..., :, None, :]
    qi = jnp.swapaxes(qi, -1, -2)[..., :, None, :]
    bt_re = jnp.swapaxes(bb_re, -1, -2)[:, :, :, None]
    bt_im = jnp.swapaxes(bb_im, -1, -2)[:, :, :, None]
    ab_re = (qr * bt_re - qi * bt_im).reshape(nl, 2, g, w, p)
    ab_im = (qr * bt_im + qi * bt_re).reshape(nl, 2, g, w, p)
    inj = jnp.concatenate([ab_re[:, 0], ab_im[:, 0], ab_re[:, 1], ab_im[:, 1],
                           ab_im[:, 0], ab_re[:, 0], ab_im[:, 1], ab_re[:, 1]], axis=-1)
    ac_rows = []
    for d, slot in enumerate((cs, 0)):
        ar, ai = pr[:, d, :, :, slot], pi[:, d, :, :, slot]
        ac_rows += [jnp.concatenate([ar, ar], -1), jnp.concatenate([-ai, ai], -1)]
    ac = jnp.stack(ac_rows + [jnp.zeros_like(ac_rows[0])] * (SUBLANES - 4), axis=2)
    return base, inj.astype(BF16), w2.astype(BF16), ac


def _layer_norm(x, g, b):
    mu = jnp.mean(x, -1, keepdims=True)
    xc = x - mu
    var = jnp.mean(xc * xc, -1, keepdims=True)
    return xc * lax.rsqrt(var + LN_EPS) * g + b


def _route(logits_t):
    col = lambda i: logits_t[i : i + 1, :]
    gl = [col(i) for i in range(N_GROUPS)]
    gmax = functools.reduce(jnp.maximum, gl)
    g_idx = jnp.full_like(gmax, N_GROUPS - 1).astype(jnp.int32)
    for i in reversed(range(N_GROUPS - 1)):
        g_idx = jnp.where(gl[i] == gmax, i, g_idx)
    g_p = 1.0 / functools.reduce(lambda a, b: a + b, [jnp.exp(x - gmax) for x in gl])
    el = []
    for e in range(EXPERTS_PER_GROUP):
        v = col(N_GROUPS + (N_GROUPS - 1) * EXPERTS_PER_GROUP + e)
        for g in reversed(range(N_GROUPS - 1)):
            v = jnp.where(g_idx == g, col(N_GROUPS + g * EXPERTS_PER_GROUP + e), v)
        el.append(v)
    m1 = functools.reduce(jnp.maximum, el)
    i1 = jnp.full_like(g_idx, EXPERTS_PER_GROUP - 1)
    for e in reversed(range(EXPERTS_PER_GROUP - 1)):
        i1 = jnp.where(el[e] == m1, e, i1)
    rest = [jnp.where(i1 == e, -jnp.inf, el[e]) for e in range(EXPERTS_PER_GROUP)]
    m2 = functools.reduce(jnp.maximum, rest)
    i2 = jnp.full_like(g_idx, EXPERTS_PER_GROUP - 1)
    for e in reversed(range(EXPERTS_PER_GROUP - 1)):
        i2 = jnp.where((rest[e] == m2) & (i1 != e), e, i2)
    t = jnp.exp(m2 - m1)
    w1 = g_p / (1.0 + t)
    w2 = g_p * t / (1.0 + t)
    lo = jnp.minimum(i1, i2)
    hi = jnp.maximum(i1, i2)
    pair = jnp.where(hi == 1, 0, jnp.where(hi == 2, jnp.where(lo == 1, 1, 2), 3 + lo))
    bucket = g_idx * N_PAIRS + pair
    w_lower = jnp.where(i1 < i2, w1, w2)
    w_higher = jnp.where(i1 < i2, w2, w1)
    w_a = jnp.where(pair == 0, w_lower, w_higher)
    w_b = jnp.where(pair == 0, w_higher, w_lower)
    return bucket, w_a, w_b


def _merge_kernel(gate_ref, rg_ref, u_ref, hf_ref, hb_ref, rf_ref, rb_ref, yg_ref, x_ref, m_ref,
                  hgn_ref, gng_ref, gnb_ref, d_ref, gw_ref, gb_ref, wo_ref, lng_ref, lnb_ref,
                  wrh_ref, wrl_ref, br_ref, a128_ref, a64_ref, tri_ref, x1_ref, h2_ref, route_ref, cnt_ref, wt_scr,
                  *, alpha):
    @pl.when((pl.program_id(0) == 0) & (pl.program_id(1) == 0))
    def _():
        cnt_ref[...] = jnp.zeros_like(cnt_ref)

    o_hg = hf_ref[0] + hb_ref[0]
    ms = _dot2(o_hg * o_hg, a128_ref[...])
    hg = o_hg * lax.rsqrt(ms + LN_EPS) * hgn_ref[0] * _silu(gate_ref[0])
    o_rt = rf_ref[0] + rb_ref[0]
    mu = _dot2(o_rt, a64_ref[...])
    xc = o_rt - mu
    var = _dot2(xc * xc, a64_ref[...])
    rt = (xc * lax.rsqrt(var + LN_EPS) * gng_ref[0] + gnb_ref[0]) * _silu(rg_ref[0])
    rows_per_chunk = S5_CHUNK // SUBLANES
    n_chunks = yg_ref.shape[1]
    y5 = []
    for half in range(S5_WIDTH // LANES):
        for g_lo in range(SUBLANES):
            for s_hi in range(rows_per_chunk):
                wt_scr[pl.ds(s_hi * SUBLANES + g_lo, n_chunks, stride=S5_CHUNK), :] = (
                    yg_ref[half * SUBLANES + g_lo, :, s_hi * LANES : (s_hi + 1) * LANES])
        y5.append(_swap_sublane_lanegroup(wt_scr[...]))
    y5 = jnp.concatenate(y5, axis=-1)
    s5 = jax.nn.gelu(y5 + d_ref[0] * u_ref[0])
    s5 = s5 * jax.nn.sigmoid(_dot(s5.astype(BF16), gw_ref[0]) + gb_ref[0])
    cat = jnp.concatenate([hg, rt, s5], axis=-1).astype(BF16)
    y = _dot(cat, wo_ref[0])
    m = m_ref[0, 0, 0]
    x1 = _layer_norm(alpha * x_ref[0] + m[2:3, :] * y, lng_ref[0], lnb_ref[0])
    x1_ref[0] = x1
    h2 = x1 * (1.0 + m[4:5, :]) + m[3:4, :]
    h_hi, h_lo = _split_bf16(h2)
    logits_t = (_dot_nt(wrh_ref[0], h_hi) + _dot_nt(wrh_ref[0], h_lo) + _dot_nt(wrl_ref[0], h_hi)) + br_ref[0]
    bucket, w_a, w_b = _route(logits_t)
    sub = lax.broadcasted_iota(jnp.int32, logits_t.shape, 0)
    h2_ref[0, :, : h2.shape[1]] = h2
    h2_ref[0, :, h2.shape[1] :] = jnp.where(sub == 0, w_a, jnp.where(sub == 1, w_b, 0.0)).T
    onehot = sub == bucket
    before = _dot(onehot.astype(BF16), tri_ref[...]) + cnt_ref[...]
    rank = jnp.sum(jnp.where(onehot, before, 0.0), axis=0, keepdims=True)
    rsub = lax.broadcasted_iota(jnp.int32, route_ref.shape[1:], 0)
    route_ref[0] = jnp.where(rsub == 0, bucket.astype(F32), jnp.where(rsub == 1, rank, 0.0))
    cnt_ref[...] += jnp.sum(onehot.astype(F32), axis=1, keepdims=True)


def _merge(p, o_hf, o_hb, o_rf, o_rb, yg, x, modtab, layer_prm, const_prm, l, tm, nct, alpha):
    b, s, d = x.shape
    cpt = tm // S5_CHUNK

    def tok(width, col):
        return pl.BlockSpec((1, tm, width), lambda i, j: (i, j, col))

    def whole(a):
        return pl.BlockSpec(a.shape, lambda i, j: (0,) * a.ndim)

    rbase = 5 * HG_WIDTH // RET_WIDTH
    in_specs = [tok(HG_WIDTH, 2), tok(RET_WIDTH, rbase + 3), tok(S5_WIDTH, rbase + 4),
                tok(HG_WIDTH, 0), tok(HG_WIDTH, 0), tok(RET_WIDTH, 0), tok(RET_WIDTH, 0),
                pl.BlockSpec((S5_GROUPS, cpt, S5_FOLD), lambda i, j: (0, j, i)),
                tok(d, 0), _mod_spec(l, nct)]
    in_specs += [_layer_spec(a.shape[1:], l) for a in layer_prm]
    in_specs += [whole(a) for a in const_prm]
    return pl.pallas_call(
        functools.partial(_merge_kernel, alpha=alpha),
        grid=(b, s // tm),
        in_specs=in_specs,
        out_specs=[tok(d, 0), tok(d + LANES, 0), pl.BlockSpec((1, SUBLANES, tm), lambda i, j: (i, 0, j)),
                   pl.BlockSpec((LANES, 1), lambda i, j: (0, 0))],
        out_shape=[jax.ShapeDtypeStruct((b, s, d), F32), jax.ShapeDtypeStruct((b, s, d + LANES), F32),
                   jax.ShapeDtypeStruct((b, SUBLANES, s), F32), jax.ShapeDtypeStruct((LANES, 1), F32)],
        scratch_shapes=[pltpu.VMEM((tm, LANES), F32)],
        compiler_params=_cparams(("arbitrary", "arbitrary")),
        name="merge_ln1_router",
    )(p, p, p, o_hf, o_hb, o_rf, o_rb, yg, x, modtab, *layer_prm, *const_prm)


def _routing_tables(route, counts, te, n_tiles):
    bucket = route[:, 0, :].astype(jnp.int32).reshape(-1)
    rank = route[:, 1, :].astype(jnp.int32).reshape(-1)
    cnt = counts[:N_BUCKETS, 0].astype(jnp.int32)
    padded = (cnt + te - 1) // te * te
    ends = jnp.cumsum(padded)
    pos = (ends - padded)[bucket] + rank
    n_used = ends[-1] // te
    tile = jnp.arange(n_tiles, dtype=jnp.int32)
    tb = jnp.minimum(jnp.searchsorted(ends, tile * te, side="right"), N_BUCKETS - 1).astype(jnp.int32)
    tb = jnp.where(tile < n_used, tb, tb[jnp.maximum(n_used - 1, 0)])
    group, pair = tb // N_PAIRS, tb % N_PAIRS
    ea = group * EXPERTS_PER_GROUP + jnp.asarray(PAIR_SLOT_A, jnp.int32)[pair]
    eb = group * EXPERTS_PER_GROUP + jnp.asarray(PAIR_SLOT_B, jnp.int32)[pair]
    return pos, ea, eb, n_used.reshape(1).astype(jnp.int32)


def _dispatch_kernel(pos_ref, src_ref, init_ref, out_ref, sem, *, tm, tiles_per_row):
    del init_ref
    base = (pl.program_id(0) * tiles_per_row + pl.program_id(1)) * tm

    def row_copy(r):
        return pltpu.make_async_copy(src_ref.at[0, pl.ds(r, 1)], out_ref.at[pl.ds(pos_ref[base + r], 1)], sem)

    def start(g, carry):
        first = pl.multiple_of(g * DMA_UNROLL, DMA_UNROLL)
        for k in range(DMA_UNROLL):
            row_copy(first + k).start(priority=k % 2)
        return carry

    def wait(r, carry):
        row_copy(r).wait()
        return carry

    lax.fori_loop(0, tm // DMA_UNROLL, start, 0)
    lax.fori_loop(0, tm, wait, 0, unroll=DMA_UNROLL)


def _dispatch(pos, rows, init, tm):
    b, s, w = rows.shape
    n_sorted = init.shape[0]
    return pl.pallas_call(
        functools.partial(_dispatch_kernel, tm=tm, tiles_per_row=s // tm),
        grid_spec=pltpu.PrefetchScalarGridSpec(
            num_scalar_prefetch=1,
            grid=(b, s // tm),
            in_specs=[pl.BlockSpec((1, tm, w), lambda i, j, pos_ref: (i, j, 0)),
                      pl.BlockSpec(memory_space=pl.ANY)],
            out_specs=pl.BlockSpec(memory_space=pl.ANY),
            scratch_shapes=[pltpu.SemaphoreType.DMA(())],
        ),
        out_shape=jax.ShapeDtypeStruct((n_sorted, w), F32),
        input_output_aliases={2: 0},
        compiler_params=_cparams(("arbitrary", "arbitrary")),
        name="moe_dispatch",
    )(pos, rows, init)


def _expert_kernel(ea_ref, eb_ref, nused_ref, hs_ref, wga_ref, wua_ref, wda_ref, wgb_ref, wub_ref, wdb_ref,
                   o_ref, wg_scr, wu_scr, wd_scr):
    t = pl.program_id(0)
    prev = jnp.maximum(t - 1, 0)
    @pl.when((t == 0) | (ea_ref[t] != ea_ref[prev]))
    def _():
        wg_scr[0] = wga_ref[0, 0].astype(BF16)
        wu_scr[0] = wua_ref[0, 0].astype(BF16)
        wd_scr[0] = wda_ref[0, 0].astype(BF16)

    @pl.when((t == 0) | (eb_ref[t] != eb_ref[prev]))
    def _():
        wg_scr[1] = wgb_ref[0, 0].astype(BF16)
        wu_scr[1] = wub_ref[0, 0].astype(BF16)
        wd_scr[1] = wdb_ref[0, 0].astype(BF16)

    @pl.when(t < nused_ref[0])
    def _():
        d = o_ref.shape[1]
        h = hs_ref[:, :d].astype(BF16)
        y = jnp.zeros(o_ref.shape, F32)
        for e in range(2):
            act = (_silu(_dot(h, wg_scr[e])) * _dot(h, wu_scr[e])).astype(BF16)
            y = y + hs_ref[:, d + e : d + e + 1] * _dot(act, wd_scr[e])
        o_ref[...] = y

    @pl.when(t >= nused_ref[0])
    def _():
        o_ref[...] = jnp.zeros_like(o_ref)


def _experts(ea, eb, n_used, hs, wg, wu, wd, l, te):
    n_sorted, w = hs.shape
    _, _, d, eh = wg.shape

    def wspec(shape, which):
        return pl.BlockSpec((1, 1) + shape, lambda t, ea_ref, eb_ref, n_ref: (l, (ea_ref, eb_ref)[which][t], 0, 0))

    return pl.pallas_call(
        _expert_kernel,
        grid_spec=pltpu.PrefetchScalarGridSpec(
            num_scalar_prefetch=3,
            grid=(n_sorted // te,),
            in_specs=[pl.BlockSpec((te, w), lambda t, *_: (t, 0)),
                      wspec((d, eh), 0), wspec((d, eh), 0), wspec((eh, d), 0),
                      wspec((d, eh), 1), wspec((d, eh), 1), wspec((eh, d), 1)],
            out_specs=pl.BlockSpec((te, d), lambda t, *_: (t, 0)),
            scratch_shapes=[pltpu.VMEM((2, d, eh), BF16), pltpu.VMEM((2, d, eh), BF16), pltpu.VMEM((2, eh, d), BF16)],
        ),
        out_shape=jax.ShapeDtypeStruct((n_sorted, d), F32),
        compiler_params=_cparams(("arbitrary",)),
        name="moe_experts",
    )(ea, eb, n_used, hs, wg, wu, wd, wg, wu, wd)


def _gather_expert_rows(pos_ref, ys_ref, buf, sem, *, tm, tiles_per_row, first_tile):
    cols = pl.num_programs(1)
    n_steps = pl.num_programs(0) * cols
    step = pl.program_id(0) * cols + pl.program_id(1)
    slot = step % 2

    def row_copy(step_, slot_, r):
        tile = (step_ // cols) * tiles_per_row + first_tile + step_ % cols
        src = ys_ref.at[pl.ds(pos_ref[tile * tm + r], 1)]
        return pltpu.make_async_copy(src, buf.at[slot_, pl.ds(r, 1)], sem.at[slot_])

    def start_tile(step_, slot_):
        def body(g, carry):
            first = pl.multiple_of(g * DMA_UNROLL, DMA_UNROLL)
            for k in range(DMA_UNROLL):
                row_copy(step_, slot_, first + k).start(priority=k % 2)
            return carry

        lax.fori_loop(0, tm // DMA_UNROLL, body, 0)

    @pl.when(step == 0)
    def _():
        start_tile(0, 0)

    @pl.when(step + 1 < n_steps)
    def _():
        start_tile(step + 1, 1 - slot)

    def wait(r, carry):
        row_copy(step, slot, r).wait()
        return carry

    lax.fori_loop(0, tm, wait, 0, unroll=DMA_UNROLL)
    return buf[slot]


def _combine_kernel(pos_ref, ys_ref, x1_ref, m_ref, lng_ref, lnb_ref, o_ref, buf, sem,
                    *, tm, tiles_per_row, first_tile, alpha):
    y = _gather_expert_rows(pos_ref, ys_ref, buf, sem, tm=tm, tiles_per_row=tiles_per_row, first_tile=first_tile)
    m = m_ref[0, 0, 0]
    o_ref[0] = _layer_norm(alpha * x1_ref[0] + m[5:6, :] * y, lng_ref[0], lnb_ref[0])


def _combine(pos, ys, x1, modtab, ln_g, ln_b, l, tm, nct, alpha, first_tile):
    b, s, d = x1.shape
    cols = s // tm - first_tile
    mod_spec = pl.BlockSpec((1, 1, 1, 6, d),
                            lambda i, j, pos_ref: (l, i, jnp.minimum((j + first_tile) // nct, 1), 0, 0))
    return pl.pallas_call(
        functools.partial(_combine_kernel, tm=tm, tiles_per_row=s // tm, first_tile=first_tile, alpha=alpha),
        grid_spec=pltpu.PrefetchScalarGridSpec(
            num_scalar_prefetch=1,
            grid=(b, cols),
            in_specs=[pl.BlockSpec(memory_space=pl.ANY),
                      pl.BlockSpec((1, tm, d), lambda i, j, pos_ref: (i, j + first_tile, 0)),
                      mod_spec, _layer_spec((1, d), l), _layer_spec((1, d), l)],
            out_specs=pl.BlockSpec((1, tm, d), lambda i, j, pos_ref: (i, j, 0)),
            scratch_shapes=[pltpu.VMEM((2, tm, d), F32), pltpu.SemaphoreType.DMA((2,))],
        ),
        out_shape=jax.ShapeDtypeStruct((b, cols * tm, d), F32),
        compiler_params=_cparams(("arbitrary", "arbitrary")),
        name="moe_combine_ln2",
    )(pos, ys, x1, modtab, ln_g, ln_b)


def _block_avg(width, group):
    idx = np.arange(width) // group
    return jnp.asarray((idx[:, None] == idx[None, :]).astype(np.float32) / group, dtype=BF16)


def kernel(x, c, ctx, c_ctx, w_mod, b_mod, w_in, hg_lb_raw, hg_norm_g, ret_decay_raw, ret_gn_g, ret_gn_b, s5_lam_re, s5_lam_im, s5_log_dt, s5_b_re, s5_b_im, s5_c_re, s5_c_im, s5_d, s5_glu_w, s5_glu_b, w_out, ln1_g, ln1_b, ln2_g, ln2_b, rg_w, rg_b, re_w, re_b, exp_w_gate, exp_w_up, exp_w_down):
    bsz, t_lat, d = x.shape
    t_ctx = ctx.shape[1]
    depth = w_mod.shape[0]
    alpha = (2.0 * depth) ** 0.25
    tm = TOKEN_TILE
    assert d == D_MODEL and t_lat % GRID_W == 0 and bsz < SUBLANES
    assert t_ctx % tm == 0 and t_lat % tm == 0, "context and latent lengths must be multiples of the token tile"
    nct = t_ctx // tm
    s = t_ctx + t_lat

    cvec = jnp.concatenate([c, c_ctx[None, :], jnp.zeros((SUBLANES - bsz - 1, d), F32)], 0)
    mod_all = _modulation(cvec, w_mod, b_mod)
    lat = mod_all[:, :bsz].reshape(depth, bsz, 6, d)
    cm = jnp.broadcast_to(mod_all[:, bsz].reshape(depth, 1, 6, d), (depth, bsz, 6, d))
    modtab = jnp.stack([cm, lat], axis=2)

    hg_lb = jnp.cumsum(jax.nn.softmax(hg_lb_raw.astype(F32), axis=0), axis=0)
    hg_lb = hg_lb - hg_lb[:1]
    ret_tables = _retention_decay_tables(jax.nn.log_sigmoid(ret_decay_raw.astype(F32)))
    cos_tab, sin_tab = _rope_tables(t_lat, t_ctx)
    s5_tabs = _s5_weights(s5_lam_re, s5_lam_im, s5_log_dt, s5_b_re, s5_b_im, s5_c_re, s5_c_im)
    pad_r = LANES - N_GROUPS - N_EXPERTS
    wr = jnp.concatenate([rg_w, re_w.reshape(depth, d, N_EXPERTS), jnp.zeros((depth, d, pad_r), F32)], axis=2)
    wr = jnp.swapaxes(wr, 1, 2)
    wr_hi, wr_lo = _split_bf16(wr)
    br = jnp.concatenate([rg_b, re_b.reshape(depth, N_EXPERTS), jnp.zeros((depth, pad_r), F32)], axis=1)[:, :, None]
    row = lambda a: a[:, None, :]
    layer_prm = [row(jnp.tile(hg_norm_g, (1, HG_HEADS))), row(ret_gn_g), row(ret_gn_b), row(s5_d),
                 s5_glu_w.astype(BF16), row(s5_glu_b), w_out.astype(BF16), row(ln1_g), row(ln1_b),
                 wr_hi, wr_lo, br]
    tri = jnp.asarray(np.triu(np.ones((tm, tm), np.float32), 1), dtype=BF16)
    const_prm = [_block_avg(HG_WIDTH, HG_DK), _block_avg(RET_WIDTH, RET_DK), tri]
    w_in_bf16 = w_in.astype(BF16)
    ln2_g, ln2_b = row(ln2_g), row(ln2_b)
    te = EXPERT_TILE
    n_tiles = -(-(bsz * s + N_BUCKETS * (te - 1)) // te)

    hs = jnp.zeros((n_tiles * te, d + LANES), F32)
    xs = jnp.concatenate([ctx, x], axis=1)
    for l in range(depth):
        if l == 0:
            p, ug = _inproj(xs, modtab, w_in_bf16, l, tm, nct)
        else:
            xs, p, ug = _ln2_inproj(pos, ys, x1, modtab, ln2_g, ln2_b, w_in_bf16, l, tm, nct, alpha)
        o_hf, o_hb = _hgrn(p, hg_lb, l, t_ctx // CHUNK)
        o_rf, o_rb = _retention(p, cos_tab, sin_tab, ret_tables, l, t_ctx // RET_CHUNK)
        yg = _s5_conv(ug, *s5_tabs, l, bsz, t_ctx // S5_CHUNK)
        x1, rows, route, counts = _merge(p, o_hf, o_hb, o_rf, o_rb, yg, xs, modtab, layer_prm, const_prm,
                                         l, tm, nct, alpha)
        pos, ea, eb, n_used = _routing_tables(route, counts, te, n_tiles)
        hs = _dispatch(pos, rows, hs, tm)
        ys = _experts(ea, eb, n_used, hs, exp_w_gate, exp_w_up, exp_w_down, l, te)
    return _combine(pos, ys, x1, modtab, ln2_g, ln2_b, depth - 1, tm, nct, alpha, nct)
```

```python
import functools

import numpy as np
import jax
import jax.numpy as jnp
from jax import lax
from jax.experimental import pallas as pl
from jax.experimental.pallas import tpu as pltpu

F32 = jnp.float32
BF16 = jnp.bfloat16

D_MODEL = 1024
HG_WIDTH = 512
HG_HEADS = 4
HG_DK = HG_WIDTH // HG_HEADS
RET_WIDTH = 256
RET_HEADS = 4
RET_DK = RET_WIDTH // RET_HEADS
S5_WIDTH = 256
S5_GROUP_CH = 16
S5_GROUPS = S5_WIDTH // S5_GROUP_CH
S5_STATE = 64
IN_COLS = 5 * HG_WIDTH + 4 * RET_WIDTH + S5_WIDTH
CHUNK = 64
HGRN_BATCH = 4
RET_CHUNK = 256
N_GROUPS = 4
EXPERTS_PER_GROUP = 4
N_EXPERTS = N_GROUPS * EXPERTS_PER_GROUP
N_PAIRS = EXPERTS_PER_GROUP * (EXPERTS_PER_GROUP - 1) // 2
N_BUCKETS = N_GROUPS * N_PAIRS
PAIR_SLOT_A = (0, 2, 2, 3, 3, 3)
PAIR_SLOT_B = (1, 1, 0, 0, 1, 2)
EXPERT_HIDDEN = D_MODEL // 2
LN_EPS = 1e-5
ROPE_BASE = 10000.0
GRID_W = 64

LANES = 128
SUBLANES = 8
TOKEN_TILE = 256
S5_CHUNK = 32
S5_FOLD = S5_CHUNK * S5_GROUP_CH
EXPERT_TILE = 256
DMA_UNROLL = 8
VMEM_LIMIT = 56 * 1024 * 1024


def _cparams(sem):
    return pltpu.CompilerParams(dimension_semantics=sem, vmem_limit_bytes=VMEM_LIMIT)


def _split_bf16(x):
    hi = x.astype(BF16)
    lo = (x - hi.astype(F32)).astype(BF16)
    return hi, lo


def _dot(a, b):
    return jnp.dot(a, b, preferred_element_type=F32)


def _dot3(a, b):
    ah, al = _split_bf16(a)
    bh, bl = _split_bf16(b)
    return _dot(ah, bh) + _dot(ah, bl) + _dot(al, bh)


def _dot2(a, b_bf16):
    ah, al = _split_bf16(a)
    return _dot(ah, b_bf16) + _dot(al, b_bf16)


def _dot_nt(a, b):
    return lax.dot_general(a, b, (((1,), (1,)), ((), ())), preferred_element_type=F32)


def _dot_tn(a, b):
    return lax.dot_general(a, b, (((0,), (0,)), ((), ())), preferred_element_type=F32)


def _silu(x):
    return x * jax.nn.sigmoid(x)


def _layer_spec(shape, l):
    zeros = (0,) * len(shape)
    return pl.BlockSpec((1,) + tuple(shape), lambda *_: (l,) + zeros)


def _mod_spec(l, nct):
    return pl.BlockSpec((1, 1, 1, 6, D_MODEL), lambda i, j, *_: (l, i, jnp.minimum(j // nct, 1), 0, 0))


def _swap_sublane_lanegroup(v):
    n = v.shape[0]
    r = lax.broadcasted_iota(jnp.int32, v.shape, 0)
    l = lax.broadcasted_iota(jnp.int32, v.shape, 1)
    for k in range(3):
        rb = (r >> k) & 1
        gb = (l >> (4 + k)) & 1
        sh = S5_GROUP_CH << k
        st = 1 << k
        a = pltpu.roll(pltpu.roll(v, LANES - sh, 1), st, 0)
        b = pltpu.roll(pltpu.roll(v, sh, 1), n - st, 0)
        v = jnp.where(rb == gb, v, jnp.where(rb == 1, a, b))
    return v


def _mod_kernel(c_ref, w_ref, b_ref, o_ref):
    sc = _silu(c_ref[...])
    o_ref[0] = _dot3(sc, w_ref[0]) + b_ref[0]


def _modulation(cvec, w_mod, b_mod):
    depth, d, n = w_mod.shape
    rows = cvec.shape[0]
    tn = 1536
    return pl.pallas_call(
        _mod_kernel,
        grid=(depth, n // tn),
        in_specs=[
            pl.BlockSpec((rows, d), lambda l, j: (0, 0)),
            pl.BlockSpec((1, d, tn), lambda l, j: (l, 0, j)),
            pl.BlockSpec((1, 1, tn), lambda l, j: (l, 0, j)),
        ],
        out_specs=pl.BlockSpec((1, rows, tn), lambda l, j: (l, 0, j)),
        out_shape=jax.ShapeDtypeStruct((depth, rows, n), F32),
        compiler_params=_cparams(("arbitrary", "arbitrary")),
        name="modulation",
    )(cvec, w_mod, b_mod.reshape(depth, 1, n))


def _inproj_kernel(x_ref, m_ref, w_ref, o_ref, ug_ref, wt_scr):
    _project(x_ref[0], m_ref, w_ref, o_ref, ug_ref, wt_scr)


def _ln2_inproj_kernel(pos_ref, ys_ref, x1_ref, mprev_ref, lng_ref, lnb_ref, m_ref, w_ref,
                       x2_ref, o_ref, ug_ref, buf, sem, wt_scr, *, tm, tiles_per_row, alpha):
    y = _gather_expert_rows(pos_ref, ys_ref, buf, sem, tm=tm, tiles_per_row=tiles_per_row, first_tile=0)
    g2 = mprev_ref[0, 0, 0][5:6, :]
    x2 = _layer_norm(alpha * x1_ref[0] + g2 * y, lng_ref[0], lnb_ref[0])
    x2_ref[0] = x2
    _project(x2, m_ref, w_ref, o_ref, ug_ref, wt_scr)


def _project(x, m_ref, w_ref, o_ref, ug_ref, wt_scr):
    m = m_ref[0, 0, 0]
    h = x * (1.0 + m[1:2, :]) + m[0:1, :]
    p = _dot(h.astype(BF16), w_ref[0])
    o_ref[0] = p
    rows_per_chunk = S5_CHUNK // SUBLANES
    n_chunks = p.shape[0] // S5_CHUNK
    for half in range(S5_WIDTH // LANES):
        lo = IN_COLS - S5_WIDTH + half * LANES
        wt_scr[...] = _swap_sublane_lanegroup(p[:, lo : lo + LANES])
        for g_lo in range(SUBLANES):
            for s_hi in range(rows_per_chunk):
                piece = wt_scr[pl.ds(s_hi * SUBLANES + g_lo, n_chunks, stride=S5_CHUNK), :]
                ug_ref[half * SUBLANES + g_lo, :, s_hi * LANES : (s_hi + 1) * LANES] = piece


def _inproj(x, modtab, w_in_bf16, l, tm, nct):
    b, s, d = x.shape
    n = w_in_bf16.shape[-1]
    cpt = tm // S5_CHUNK
    return pl.pallas_call(
        _inproj_kernel,
        grid=(b, s // tm),
        in_specs=[
            pl.BlockSpec((1, tm, d), lambda i, j: (i, j, 0)),
            _mod_spec(l, nct),
            _layer_spec((d, n), l),
        ],
        out_specs=[pl.BlockSpec((1, tm, n), lambda i, j: (i, j, 0)),
                   pl.BlockSpec((S5_GROUPS, cpt, S5_FOLD), lambda i, j: (0, j, i))],
        out_shape=[jax.ShapeDtypeStruct((b, s, n), F32),
                   jax.ShapeDtypeStruct((S5_GROUPS, s // S5_CHUNK, b * S5_FOLD), F32)],
        scratch_shapes=[pltpu.VMEM((tm, LANES), F32)],
        compiler_params=_cparams(("arbitrary", "arbitrary")),
        name="inproj",
    )(x, modtab, w_in_bf16)


def _ln2_inproj(pos, ys, x1, modtab, ln_g, ln_b, w_in_bf16, l, tm, nct, alpha):
    b, s, d = x1.shape
    n = w_in_bf16.shape[-1]
    cpt = tm // S5_CHUNK
    tok = lambda width: pl.BlockSpec((1, tm, width), lambda i, j, pos_ref: (i, j, 0))
    return pl.pallas_call(
        functools.partial(_ln2_inproj_kernel, tm=tm, tiles_per_row=s // tm, alpha=alpha),
        grid_spec=pltpu.PrefetchScalarGridSpec(
            num_scalar_prefetch=1,
            grid=(b, s // tm),
            in_specs=[pl.BlockSpec(memory_space=pl.ANY), tok(d),
                      _mod_spec(l - 1, nct), _layer_spec((1, d), l - 1), _layer_spec((1, d), l - 1),
                      _mod_spec(l, nct), _layer_spec((d, n), l)],
            out_specs=[tok(d), tok(n),
                       pl.BlockSpec((S5_GROUPS, cpt, S5_FOLD), lambda i, j, pos_ref: (0, j, i))],
            scratch_shapes=[pltpu.VMEM((2, tm, d // LANES, LANES), F32), pltpu.SemaphoreType.DMA((2,)),
                            pltpu.VMEM((tm, LANES), F32)],
        ),
        out_shape=[jax.ShapeDtypeStruct((b, s, d), F32), jax.ShapeDtypeStruct((b, s, n), F32),
                   jax.ShapeDtypeStruct((S5_GROUPS, s // S5_CHUNK, b * S5_FOLD), F32)],
        compiler_params=_cparams(("arbitrary", "arbitrary")),
        name="ln2_inproj",
    )(pos, ys, x1, modtab, ln_g, ln_b, modtab, w_in_bf16)


def _block_gate_products(f, reverse):
    c = f.shape[0]
    row = lax.broadcasted_iota(jnp.int32, (c, 1), 0)
    a, z, b = f, f, jnp.ones_like(f)
    out = []
    s = 1
    while s < c:
        out.append((a, b))
        if s < SUBLANES:
            z3 = z.reshape(c // SUBLANES, SUBLANES, z.shape[1])
            up = pltpu.roll(z3, s, 1).reshape(z.shape)
            dn = pltpu.roll(z3, SUBLANES - s, 1).reshape(z.shape)
            odd = (row & s) != 0
            if reverse:
                a = a * jnp.where(odd, 1.0, dn)
                b = b * jnp.where(odd, up, 1.0)
            else:
                a = a * jnp.where(odd, up, 1.0)
                b = b * jnp.where(odd, 1.0, dn)
            z = z * jnp.where(odd, up, dn)
        else:
            na, nb, nz = [], [], []
            for lo in range(0, c, 2 * s):
                ev, od = slice(lo, lo + s), slice(lo + s, lo + 2 * s)
                zz = z[ev] * z[od]
                if reverse:
                    na += [a[ev] * z[od], a[od]]
                    nb += [b[ev], b[od] * z[ev]]
                else:
                    na += [a[ev], a[od] * z[ev]]
                    nb += [b[ev] * z[od], b[od]]
                nz += [zz, zz]
            a, b, z = (jnp.concatenate(t, axis=0) for t in (na, nb, nz))
        s *= 2
    out.append((a, b))
    return out, z


def _hgrn_kernel(qf_ref, vf_ref, zf_ref, qb_ref, vb_ref, zb_ref, lb_ref, of_ref, ob_ref, st_ref):
    c = CHUNK

    @pl.when(pl.program_id(1) == 0)
    def _():
        st_ref[...] = jnp.zeros_like(st_ref)

    ri = lax.broadcasted_iota(jnp.int32, (c, c), 0)
    ci = lax.broadcasted_iota(jnp.int32, (c, c), 1)
    for d, (q_ref, v_ref, z_ref, o_ref) in enumerate(
        ((qf_ref, vf_ref, zf_ref, of_ref), (qb_ref, vb_ref, zb_ref, ob_ref))
    ):
        reverse = d == 1
        causal = (ri < ci) if reverse else (ri > ci)
        n_levels = c.bit_length() - 1
        masks = [(((ri >> lvl) ^ (ci >> lvl)) == 1) & causal for lvl in range(n_levels)]
        for bi, h in [(bi, h) for bi in range(q_ref.shape[0]) for h in range(HG_HEADS)]:
            hs = slice(h * HG_DK, (h + 1) * HG_DK)
            q = _silu(q_ref[bi, :, hs])
            v = v_ref[bi, :, hs].astype(BF16)
            lb = lb_ref[0, d : d + 1, hs]
            f = lb + (1.0 - lb) * jax.nn.sigmoid(z_ref[bi, :, hs])
            k = 1.0 - f
            levels, tot = _block_gate_products(f, reverse)
            scores = jnp.where(ri == ci, _dot_nt(q.astype(BF16), k.astype(BF16)), 0.0)
            for mask, (a, bb) in zip(masks, levels):
                scores = scores + jnp.where(mask, _dot_nt((q * a).astype(BF16), (k * bb).astype(BF16)), 0.0)
            a_full, b_full = levels[-1]
            st = st_ref[bi, d, h]
            o_ref[bi, :, hs] = (_dot(scores.astype(BF16), v)
                                + _dot_nt((q * a_full).astype(BF16), st.astype(BF16)))
            st_ref[bi, d, h] = st * tot[0:1, :] + _dot_tn(v, (k * b_full).astype(BF16))


def _bwd_chunk(n, nc_ctx, nc_all):
    return jnp.where(n < nc_ctx, nc_ctx - 1 - n, nc_all + nc_ctx - 1 - n)


def _hgrn(p, lb_all, l, nc_ctx):
    b, s, _ = p.shape
    nc = s // CHUNK
    w = HG_WIDTH

    bb = HGRN_BATCH if b % HGRN_BATCH == 0 else 1

    def fwd(col):
        return pl.BlockSpec((bb, CHUNK, w), lambda i, n: (i, n, col))

    def bwd(col):
        return pl.BlockSpec((bb, CHUNK, w), lambda i, n: (i, _bwd_chunk(n, nc_ctx, nc), col))

    out = jax.ShapeDtypeStruct((b, s, w), F32)
    return pl.pallas_call(
        _hgrn_kernel,
        grid=(b // bb, nc),
        in_specs=[fwd(0), fwd(1), fwd(3), bwd(0), bwd(1), bwd(4), _layer_spec((2, w), l)],
        out_specs=[fwd(0), bwd(0)],
        out_shape=[out, out],
        scratch_shapes=[pltpu.VMEM((bb, 2, HG_HEADS, HG_DK, HG_DK), F32)],
        compiler_params=_cparams(("arbitrary", "arbitrary")),
        name="hgrn2_scan",
    )(p, p, p, p, p, p, lb_all)


def _swap_halves(x, half):
    n = x.shape[-1]
    lane = lax.broadcasted_iota(jnp.int32, (1, n), 1)
    lower = (lane & half) == 0
    return jnp.where(lower, pltpu.roll(x, n - half, 1), pltpu.roll(x, half, 1))


def _ret_kernel(qf_ref, kf_ref, vf_ref, cf_ref, sf_ref, qb_ref, kb_ref, vb_ref, cb_ref, sb_ref,
                dmat_ref, rq_ref, rk_ref, cd_ref, of_ref, ob_ref, st_ref):
    @pl.when(pl.program_id(1) == 0)
    def _():
        st_ref[...] = jnp.zeros_like(st_ref)

    half = RET_DK // 4
    for d, (q_ref, k_ref, v_ref, c_ref, s_ref, o_ref) in enumerate(
        ((qf_ref, kf_ref, vf_ref, cf_ref, sf_ref, of_ref), (qb_ref, kb_ref, vb_ref, cb_ref, sb_ref, ob_ref))
    ):
        cos = c_ref[...]
        sin = s_ref[...]
        q = q_ref[0]
        k = k_ref[0] * (RET_DK ** -0.5)
        q = q * cos + _swap_halves(q, half) * sin
        k = k * cos + _swap_halves(k, half) * sin
        v = v_ref[0]
        vt = v.T.astype(BF16)
        v = v.astype(BF16)
        q0 = q.astype(BF16)
        k0 = k.astype(BF16)
        qd = (q * rq_ref[0, d]).astype(BF16)
        kd = (k * rk_ref[0, d]).astype(BF16)
        cd = cd_ref[0, d : d + 1, :]
        lane = lax.broadcasted_iota(jnp.int32, (1, LANES), 1)
        left = lane < RET_DK
        same_head = (lax.broadcasted_iota(jnp.int32, (LANES, 1), 0) < RET_DK) == left
        zero = jnp.zeros((), BF16)
        for p in range(RET_HEADS // 2):
            ps = slice(p * LANES, (p + 1) * LANES)
            kp, vp = k0[:, ps], v[:, ps]
            k_blk = jnp.concatenate([jnp.where(left, kp, zero), jnp.where(left, zero, kp)], axis=0)
            v_blk = jnp.concatenate([jnp.where(left, vp, zero), jnp.where(left, zero, vp)], axis=0)
            scores = _dot_nt(q0[:, ps], k_blk) * dmat_ref[0, d, p]
            st = st_ref[d, p]
            o_ref[0, :, ps] = _dot(scores.astype(BF16), v_blk) + _dot_nt(qd[:, ps], st.astype(BF16))
            st_ref[d, p] = st * cd[:, ps] + jnp.where(same_head, _dot(vt[ps, :], kd[:, ps]), 0.0)


def _retention(p, cos_tab, sin_tab, tables, l, nc_ctx):
    b, s, _ = p.shape
    nc = s // RET_CHUNK
    w = RET_WIDTH
    base = 5 * HG_WIDTH // w

    def fwd(col):
        return pl.BlockSpec((1, RET_CHUNK, w), lambda i, n: (i, n, col))

    def bwd(col):
        return pl.BlockSpec((1, RET_CHUNK, w), lambda i, n: (i, _bwd_chunk(n, nc_ctx, nc), col))

    tab_f = pl.BlockSpec((RET_CHUNK, w), lambda i, n: (n, 0))
    tab_b = pl.BlockSpec((RET_CHUNK, w), lambda i, n: (_bwd_chunk(n, nc_ctx, nc), 0))
    out = jax.ShapeDtypeStruct((b, s, w), F32)
    return pl.pallas_call(
        _ret_kernel,
        grid=(b, nc),
        in_specs=[fwd(base), fwd(base + 1), fwd(base + 2), tab_f, tab_f,
                  bwd(base), bwd(base + 1), bwd(base + 2), tab_b, tab_b]
                 + [_layer_spec(t.shape[1:], l) for t in tables],
        out_specs=[fwd(0), bwd(0)],
        out_shape=[out, out],
        scratch_shapes=[pltpu.VMEM((2, RET_HEADS // 2, LANES, LANES), F32)],
        compiler_params=_cparams(("arbitrary", "arbitrary")),
        name="retention_scan",
    )(p, p, p, cos_tab, sin_tab, p, p, p, cos_tab, sin_tab, *tables)


def _rope_tables(t_lat, t_ctx):
    m = RET_DK // 4
    inv = ROPE_BASE ** (-jnp.arange(m, dtype=F32) / m)
    rows = jnp.repeat(jnp.arange(t_lat // GRID_W, dtype=jnp.int32), GRID_W).astype(F32)
    cols = jnp.tile(jnp.arange(GRID_W, dtype=jnp.int32), t_lat // GRID_W).astype(F32)

    def half_tables(pos):
        ang = pos[:, None] * inv
        c, s = jnp.cos(ang), jnp.sin(ang)
        return jnp.concatenate([c, c], -1), jnp.concatenate([-s, s], -1)

    cr, sr = half_tables(rows)
    cc, sc = half_tables(cols)
    cos_h = jnp.concatenate([cr, cc], -1)
    sin_h = jnp.concatenate([sr, sc], -1)
    cos = jnp.tile(cos_h, (1, RET_HEADS))
    sin = jnp.tile(sin_h, (1, RET_HEADS))
    cos = jnp.concatenate([jnp.ones((t_ctx, RET_WIDTH), F32), cos], 0)
    sin = jnp.concatenate([jnp.zeros((t_ctx, RET_WIDTH), F32), sin], 0)
    return cos, sin


def _retention_decay_tables(log_gamma):
    c = RET_CHUNK
    i = jnp.arange(c, dtype=F32)
    diff = i[:, None] - i[None, :]
    lg = log_gamma[:, :, :, None, None]
    d_f = jnp.where(diff >= 0, jnp.exp(lg[:, 0] * diff), 0.0)
    d_b = jnp.where(diff <= 0, jnp.exp(lg[:, 1] * (-diff)), 0.0)
    dmat = jnp.stack([d_f, d_b], 1)
    dmat = jnp.concatenate([dmat[:, :, 0::2], dmat[:, :, 1::2]], axis=-1)
    lane_lg = jnp.repeat(log_gamma, RET_DK, axis=2)[:, :, None, :]
    col = i[None, :, None]
    rq = jnp.stack([jnp.exp(lane_lg[:, 0] * (col + 1.0)), jnp.exp(lane_lg[:, 1] * (c - col))], 1)
    rk = jnp.stack([jnp.exp(lane_lg[:, 0] * (c - 1.0 - col)), jnp.exp(lane_lg[:, 1] * col)], 1)
    cdec = jnp.exp(lane_lg[:, :, 0, :] * c)
    return dmat, rq, rk, cdec


def _s5_kernel(u_ref, base_ref, inj_ref, w2_ref, ac_ref, y_ref, toep_scr, a_scr, i_scr, hf_scr, hb_scr,
               *, n_ctx, n_all, bsz):
    width = S5_FOLD
    st2 = 2 * S5_STATE
    ch = S5_GROUP_CH
    lane = lax.broadcasted_iota(jnp.int32, (ch, width), 1)
    base_f = base_ref[0, 0, 0]
    base_b = base_ref[0, 0, 1]
    for s in range(S5_CHUNK):
        sh_f = s * ch
        sh_b = (S5_CHUNK - 1 - s) * ch
        part_f = jnp.where(lane >= sh_f, pltpu.roll(base_f, sh_f, 1), 0.0) if sh_f else base_f
        part_b = jnp.where(lane < width - sh_b, pltpu.roll(base_b, width - sh_b, 1), 0.0) if sh_b else base_b
        toep_scr[s * ch : (s + 1) * ch, :] = (part_f + part_b).astype(BF16)
    for b in range(bsz):
        ub = u_ref[0, :, b * width : (b + 1) * width].astype(BF16)
        a_scr[b] = _dot(ub, toep_scr[...])
        i_scr[b] = _dot(ub, inj_ref[0, 0])
    ac = ac_ref[0, 0]

    zero = jnp.zeros((1, st2), F32)
    state = [(zero, zero)] * (2 * bsz)
    for n in range(n_all):
        nb = n_ctx - 1 - n if n < n_ctx else n_all + n_ctx - 1 - n
        for b in range(bsz):
            for d, (row, scr) in enumerate(((n, hf_scr), (nb, hb_scr))):
                h, hsw = state[2 * b + d]
                scr[b, row : row + 1, :] = h
                inj = i_scr[b, row : row + 1, d * st2 : (d + 1) * st2]
                inj_sw = i_scr[b, row : row + 1, (d + 2) * st2 : (d + 3) * st2]
                ar, ai = ac[2 * d : 2 * d + 1, :], ac[2 * d + 1 : 2 * d + 2, :]
                state[2 * b + d] = (h * ar + hsw * ai + inj, hsw * ar - h * ai + inj_sw)
    for b in range(bsz):
        y_ref[0, :, b * width : (b + 1) * width] = (
            a_scr[b]
            + _dot(hf_scr[b].astype(BF16), w2_ref[0, 0, 0])
            + _dot(hb_scr[b].astype(BF16), w2_ref[0, 0, 1]))


def _s5_conv(ug, base, inj, w2, ac, l, bsz, n_ctx):
    g, n_all, _ = ug.shape
    width = S5_FOLD
    st2 = 2 * S5_STATE
    return pl.pallas_call(
        functools.partial(_s5_kernel, n_ctx=n_ctx, n_all=n_all, bsz=bsz),
        grid=(g,),
        in_specs=[
            pl.BlockSpec((1, n_all, bsz * width), lambda i: (i, 0, 0)),
            pl.BlockSpec((1, 1, 2, S5_GROUP_CH, width), lambda i: (l, i, 0, 0, 0)),
            pl.BlockSpec((1, 1, width, 4 * st2), lambda i: (l, i, 0, 0)),
            pl.BlockSpec((1, 1, 2, st2, width), lambda i: (l, i, 0, 0, 0)),
            pl.BlockSpec((1, 1, SUBLANES, st2), lambda i: (l, i, 0, 0)),
        ],
        out_specs=pl.BlockSpec((1, n_all, bsz * width), lambda i: (i, 0, 0)),
        out_shape=jax.ShapeDtypeStruct((g, n_all, bsz * width), F32),
        scratch_shapes=[pltpu.VMEM((width, width), BF16),
                        pltpu.VMEM((bsz, n_all, width), F32), pltpu.VMEM((bsz, n_all, 4 * st2), F32),
                        pltpu.VMEM((bsz, n_all, st2), F32), pltpu.VMEM((bsz, n_all, st2), F32)],
        compiler_params=_cparams(("arbitrary",)),
        name="s5_conv",
    )(ug, base, inj, w2, ac)


def _s5_weights(lam_re, lam_im, log_dt, b_re, b_im, c_re, c_im):
    cs, ch, p, g = S5_CHUNK, S5_GROUP_CH, S5_STATE, S5_GROUPS
    nl = lam_re.shape[0]
    lam_re = jnp.minimum(lam_re.astype(F32), -1e-4)
    lam_im = lam_im.astype(F32)
    dt = jnp.exp(log_dt.astype(F32))[..., None]
    mag = jnp.exp(dt * lam_re)
    abar_re, abar_im = mag * jnp.cos(dt * lam_im), mag * jnp.sin(dt * lam_im)
    den = jnp.square(lam_re) + jnp.square(lam_im)
    nr, ni = abar_re - 1.0, abar_im
    coef_re = ((nr * lam_re + ni * lam_im) / den)[..., None]
    coef_im = ((ni * lam_re - nr * lam_im) / den)[..., None]
    b_re, b_im = b_re.astype(F32), b_im.astype(F32)
    bb_re = coef_re * b_re - coef_im * b_im
    bb_im = coef_re * b_im + coef_im * b_re
    ct_re = jnp.swapaxes(c_re.astype(F32), -1, -2)
    ct_im = jnp.swapaxes(c_im.astype(F32), -1, -2)

    def powers(expo):
        e = expo.astype(F32)[None, :, None, None, :]
        m = jnp.exp(e * (dt * lam_re)[..., None])
        return m * jnp.cos(e * (dt * lam_im)[..., None]), m * jnp.sin(e * (dt * lam_im)[..., None])

    slots = jnp.arange(cs + 1)
    pr, pi = powers(jnp.stack([slots, cs - slots]))
    r_re = (pr[..., None] * ct_re[..., None, :] - pi[..., None] * ct_im[..., None, :]).reshape(nl, 2, g, p, (cs + 1) * ch)
    r_im = (pr[..., None] * ct_im[..., None, :] + pi[..., None] * ct_re[..., None, :]).reshape(nl, 2, g, p, (cs + 1) * ch)
    hi = lax.Precision.HIGHEST
    kern = (jnp.einsum('ldgpe,ldgpn->ldgen', bb_re, r_re, precision=hi)
            - jnp.einsum('ldgpe,ldgpn->ldgen', bb_im, r_im, precision=hi))
    w = cs * ch
    base = jnp.stack([kern[:, 0, :, :, :w], kern[:, 1, :, :, ch:]], axis=2)
    w2 = jnp.stack([jnp.concatenate([r_re[:, 0, :, :, ch:], -r_im[:, 0, :, :, ch:]], axis=2),
                    jnp.concatenate([r_re[:, 1, :, :, :w], -r_im[:, 1, :, :, :w]], axis=2)], axis=2)
    steps = jnp.arange(cs)
    qr, qi = powers(jnp.stack([cs - 1 - steps, steps]))
    qr = jnp.swapaxes(qr, -1, -2)[..., :, None, :]
    qi = jnp.swapaxes(qi, -1, -2)[..., :, None, :]
    bt_re = jnp.swapaxes(bb_re, -1, -2)[:, :, :, None]
    bt_im = jnp.swapaxes(bb_im, -1, -2)[:, :, :, None]
    ab_re = (qr * bt_re - qi * bt_im).reshape(nl, 2, g, w, p)
    ab_im = (qr * bt_im + qi * bt_re).reshape(nl, 2, g, w, p)
    inj = jnp.concatenate([ab_re[:, 0], ab_im[:, 0], ab_re[:, 1], ab_im[:, 1],
                           ab_im[:, 0], ab_re[:, 0], ab_im[:, 1], ab_re[:, 1]], axis=-1)
    ac_rows = []
    for d, slot in enumerate((cs, 0)):
        ar, ai = pr[:, d, :, :, slot], pi[:, d, :, :, slot]
        ac_rows += [jnp.concatenate([ar, ar], -1), jnp.concatenate([-ai, ai], -1)]
    ac = jnp.stack(ac_rows + [jnp.zeros_like(ac_rows[0])] * (SUBLANES - 4), axis=2)
    return base, inj.astype(BF16), w2.astype(BF16), ac


def _layer_norm(x, g, b):
    mu = jnp.mean(x, -1, keepdims=True)
    xc = x - mu
    var = jnp.mean(xc * xc, -1, keepdims=True)
    return xc * lax.rsqrt(var + LN_EPS) * g + b


def _route(logits_t):
    col = lambda i: logits_t[i : i + 1, :]
    gl = [col(i) for i in range(N_GROUPS)]
    gmax = functools.reduce(jnp.maximum, gl)
    g_idx = jnp.full_like(gmax, N_GROUPS - 1).astype(jnp.int32)
    for i in reversed(range(N_GROUPS - 1)):
        g_idx = jnp.where(gl[i] == gmax, i, g_idx)
    g_p = 1.0 / functools.reduce(lambda a, b: a + b, [jnp.exp(x - gmax) for x in gl])
    el = []
    for e in range(EXPERTS_PER_GROUP):
        v = col(N_GROUPS + (N_GROUPS - 1) * EXPERTS_PER_GROUP + e)
        for g in reversed(range(N_GROUPS - 1)):
            v = jnp.where(g_idx == g, col(N_GROUPS + g * EXPERTS_PER_GROUP + e), v)
        el.append(v)
    m1 = functools.reduce(jnp.maximum, el)
    i1 = jnp.full_like(g_idx, EXPERTS_PER_GROUP - 1)
    for e in reversed(range(EXPERTS_PER_GROUP - 1)):
        i1 = jnp.where(el[e] == m1, e, i1)
    rest = [jnp.where(i1 == e, -jnp.inf, el[e]) for e in range(EXPERTS_PER_GROUP)]
    m2 = functools.reduce(jnp.maximum, rest)
    i2 = jnp.full_like(g_idx, EXPERTS_PER_GROUP - 1)
    for e in reversed(range(EXPERTS_PER_GROUP - 1)):
        i2 = jnp.where((rest[e] == m2) & (i1 != e), e, i2)
    t = jnp.exp(m2 - m1)
    w1 = g_p / (1.0 + t)
    w2 = g_p * t / (1.0 + t)
    lo = jnp.minimum(i1, i2)
    hi = jnp.maximum(i1, i2)
    pair = jnp.where(hi == 1, 0, jnp.where(hi == 2, jnp.where(lo == 1, 1, 2), 3 + lo))
    bucket = g_idx * N_PAIRS + pair
    w_lower = jnp.where(i1 < i2, w1, w2)
    w_higher = jnp.where(i1 < i2, w2, w1)
    w_a = jnp.where(pair == 0, w_lower, w_higher)
    w_b = jnp.where(pair == 0, w_higher, w_lower)
    return bucket, w_a, w_b


def _merge_kernel(gate_ref, rg_ref, u_ref, hf_ref, hb_ref, rf_ref, rb_ref, yg_ref, x_ref, m_ref,
                  hgn_ref, gng_ref, gnb_ref, d_ref, gw_ref, gb_ref, wo_ref, lng_ref, lnb_ref,
                  wrh_ref, wrl_ref, br_ref, a128_ref, a64_ref, tri_ref, x1_ref, h2_ref, route_ref, cnt_ref, wt_scr,
                  *, alpha):
    @pl.when((pl.program_id(0) == 0) & (pl.program_id(1) == 0))
    def _():
        cnt_ref[...] = jnp.zeros_like(cnt_ref)

    o_hg = hf_ref[0] + hb_ref[0]
    ms = _dot2(o_hg * o_hg, a128_ref[...])
    hg = o_hg * lax.rsqrt(ms + LN_EPS) * hgn_ref[0] * _silu(gate_ref[0])
    o_rt = rf_ref[0] + rb_ref[0]
    mu = _dot2(o_rt, a64_ref[...])
    xc = o_rt - mu
    var = _dot2(xc * xc, a64_ref[...])
    rt = (xc * lax.rsqrt(var + LN_EPS) * gng_ref[0] + gnb_ref[0]) * _silu(rg_ref[0])
    rows_per_chunk = S5_CHUNK // SUBLANES
    n_chunks = yg_ref.shape[1]
    y5 = []
    for half in range(S5_WIDTH // LANES):
        for g_lo in range(SUBLANES):
            for s_hi in range(rows_per_chunk):
                wt_scr[pl.ds(s_hi * SUBLANES + g_lo, n_chunks, stride=S5_CHUNK), :] = (
                    yg_ref[half * SUBLANES + g_lo, :, s_hi * LANES : (s_hi + 1) * LANES])
        y5.append(_swap_sublane_lanegroup(wt_scr[...]))
    y5 = jnp.concatenate(y5, axis=-1)
    s5 = jax.nn.gelu(y5 + d_ref[0] * u_ref[0])
    s5 = s5 * jax.nn.sigmoid(_dot(s5.astype(BF16), gw_ref[0]) + gb_ref[0])
    cat = jnp.concatenate([hg, rt, s5], axis=-1).astype(BF16)
    y = _dot(cat, wo_ref[0])
    m = m_ref[0, 0, 0]
    x1 = _layer_norm(alpha * x_ref[0] + m[2:3, :] * y, lng_ref[0], lnb_ref[0])
    x1_ref[0] = x1
    h2 = x1 * (1.0 + m[4:5, :]) + m[3:4, :]
    h_hi, h_lo = _split_bf16(h2)
    logits_t = (_dot_nt(wrh_ref[0], h_hi) + _dot_nt(wrh_ref[0], h_lo) + _dot_nt(wrl_ref[0], h_hi)) + br_ref[0]
    bucket, w_a, w_b = _route(logits_t)
    sub = lax.broadcasted_iota(jnp.int32, logits_t.shape, 0)
    h2_ref[0, :, : h2.shape[1]] = h2
    h2_ref[0, :, h2.shape[1] :] = jnp.where(sub == 0, w_a, jnp.where(sub == 1, w_b, 0.0)).T
    onehot = sub == bucket
    before = _dot(onehot.astype(BF16), tri_ref[...]) + cnt_ref[...]
    rank = jnp.sum(jnp.where(onehot, before, 0.0), axis=0, keepdims=True)
    rsub = lax.broadcasted_iota(jnp.int32, route_ref.shape[1:], 0)
    route_ref[0] = jnp.where(rsub == 0, bucket.astype(F32), jnp.where(rsub == 1, rank, 0.0))
    cnt_ref[...] += jnp.sum(onehot.astype(F32), axis=1, keepdims=True)


def _merge(p, o_hf, o_hb, o_rf, o_rb, yg, x, modtab, layer_prm, const_prm, l, tm, nct, alpha):
    b, s, d = x.shape
    cpt = tm // S5_CHUNK

    def tok(width, col):
        return pl.BlockSpec((1, tm, width), lambda i, j: (i, j, col))

    def whole(a):
        return pl.BlockSpec(a.shape, lambda i, j: (0,) * a.ndim)

    rbase = 5 * HG_WIDTH // RET_WIDTH
    in_specs = [tok(HG_WIDTH, 2), tok(RET_WIDTH, rbase + 3), tok(S5_WIDTH, rbase + 4),
                tok(HG_WIDTH, 0), tok(HG_WIDTH, 0), tok(RET_WIDTH, 0), tok(RET_WIDTH, 0),
                pl.BlockSpec((S5_GROUPS, cpt, S5_FOLD), lambda i, j: (0, j, i)),
                tok(d, 0), _mod_spec(l, nct)]
    in_specs += [_layer_spec(a.shape[1:], l) for a in layer_prm]
    in_specs += [whole(a) for a in const_prm]
    return pl.pallas_call(
        functools.partial(_merge_kernel, alpha=alpha),
        grid=(b, s // tm),
        in_specs=in_specs,
        out_specs=[tok(d, 0), tok(d + LANES, 0), pl.BlockSpec((1, SUBLANES, tm), lambda i, j: (i, 0, j)),
                   pl.BlockSpec((LANES, 1), lambda i, j: (0, 0))],
        out_shape=[jax.ShapeDtypeStruct((b, s, d), F32), jax.ShapeDtypeStruct((b, s, d + LANES), F32),
                   jax.ShapeDtypeStruct((b, SUBLANES, s), F32), jax.ShapeDtypeStruct((LANES, 1), F32)],
        scratch_shapes=[pltpu.VMEM((tm, LANES), F32)],
        compiler_params=_cparams(("arbitrary", "arbitrary")),
        name="merge_ln1_router",
    )(p, p, p, o_hf, o_hb, o_rf, o_rb, yg, x, modtab, *layer_prm, *const_prm)


def _routing_tables(route, counts, te, n_tiles):
    bucket = route[:, 0, :].astype(jnp.int32).reshape(-1)
    rank = route[:, 1, :].astype(jnp.int32).reshape(-1)
    cnt = counts[:N_BUCKETS, 0].astype(jnp.int32)
    padded = (cnt + te - 1) // te * te
    ends = jnp.cumsum(padded)
    pos = (ends - padded)[bucket] + rank
    n_used = ends[-1] // te
    tile = jnp.arange(n_tiles, dtype=jnp.int32)
    tb = jnp.minimum(jnp.searchsorted(ends, tile * te, side="right"), N_BUCKETS - 1).astype(jnp.int32)
    tb = jnp.where(tile < n_used, tb, tb[jnp.maximum(n_used - 1, 0)])
    group, pair = tb // N_PAIRS, tb % N_PAIRS
    ea = group * EXPERTS_PER_GROUP + jnp.asarray(PAIR_SLOT_A, jnp.int32)[pair]
    eb = group * EXPERTS_PER_GROUP + jnp.asarray(PAIR_SLOT_B, jnp.int32)[pair]
    return pos, ea, eb, n_used.reshape(1).astype(jnp.int32)


def _dispatch_kernel(pos_ref, src_ref, init_ref, out_ref, sem, *, tm, tiles_per_row):
    del init_ref
    base = (pl.program_id(0) * tiles_per_row + pl.program_id(1)) * tm

    def row_copy(r):
        return pltpu.make_async_copy(src_ref.at[0, pl.ds(r, 1)], out_ref.at[pl.ds(pos_ref[base + r], 1)], sem)

    def start(g, carry):
        first = pl.multiple_of(g * DMA_UNROLL, DMA_UNROLL)
        for k in range(DMA_UNROLL):
            row_copy(first + k).start(priority=k % 2)
        return carry

    def wait(r, carry):
        row_copy(r).wait()
        return carry

    lax.fori_loop(0, tm // DMA_UNROLL, start, 0)
    lax.fori_loop(0, tm, wait, 0, unroll=DMA_UNROLL)


def _dispatch(pos, rows, init, tm):
    b, s, w = rows.shape
    n_sorted = init.shape[0]
    return pl.pallas_call(
        functools.partial(_dispatch_kernel, tm=tm, tiles_per_row=s // tm),
        grid_spec=pltpu.PrefetchScalarGridSpec(
            num_scalar_prefetch=1,
            grid=(b, s // tm),
            in_specs=[pl.BlockSpec((1, tm, w), lambda i, j, pos_ref: (i, j, 0)),
                      pl.BlockSpec(memory_space=pl.ANY)],
            out_specs=pl.BlockSpec(memory_space=pl.ANY),
            scratch_shapes=[pltpu.SemaphoreType.DMA(())],
        ),
        out_shape=jax.ShapeDtypeStruct((n_sorted, w), F32),
        input_output_aliases={2: 0},
        compiler_params=_cparams(("arbitrary", "arbitrary")),
        name="moe_dispatch",
    )(pos, rows, init)


def _expert_kernel(ea_ref, eb_ref, nused_ref, hs_ref, wga_ref, wua_ref, wda_ref, wgb_ref, wub_ref, wdb_ref,
                   o_ref, wg_scr, wu_scr, wd_scr):
    t = pl.program_id(0)
    prev = jnp.maximum(t - 1, 0)
    @pl.when((t == 0) | (ea_ref[t] != ea_ref[prev]))
    def _():
        wg_scr[0] = wga_ref[0, 0].astype(BF16)
        wu_scr[0] = wua_ref[0, 0].astype(BF16)
        wd_scr[0] = wda_ref[0, 0].astype(BF16)

    @pl.when((t == 0) | (eb_ref[t] != eb_ref[prev]))
    def _():
        wg_scr[1] = wgb_ref[0, 0].astype(BF16)
        wu_scr[1] = wub_ref[0, 0].astype(BF16)
        wd_scr[1] = wdb_ref[0, 0].astype(BF16)

    @pl.when(t < nused_ref[0])
    def _():
        d = wg_scr.shape[1]
        h = hs_ref[:, :d].astype(BF16)
        y = jnp.zeros((hs_ref.shape[0], d), F32)
        for e in range(2):
            act = (_silu(_dot(h, wg_scr[e])) * _dot(h, wu_scr[e])).astype(BF16)
            y = y + hs_ref[:, d + e : d + e + 1] * _dot(act, wd_scr[e])
        o_ref[...] = y.reshape(o_ref.shape)

    @pl.when(t >= nused_ref[0])
    def _():
        o_ref[...] = jnp.zeros_like(o_ref)


def _experts(ea, eb, n_used, hs, wg, wu, wd, l, te):
    n_sorted, w = hs.shape
    _, _, d, eh = wg.shape

    def wspec(shape, which):
        return pl.BlockSpec((1, 1) + shape, lambda t, ea_ref, eb_ref, n_ref: (l, (ea_ref, eb_ref)[which][t], 0, 0))

    return pl.pallas_call(
        _expert_kernel,
        grid_spec=pltpu.PrefetchScalarGridSpec(
            num_scalar_prefetch=3,
            grid=(n_sorted // te,),
            in_specs=[pl.BlockSpec((te, w), lambda t, *_: (t, 0)),
                      wspec((d, eh), 0), wspec((d, eh), 0), wspec((eh, d), 0),
                      wspec((d, eh), 1), wspec((d, eh), 1), wspec((eh, d), 1)],
            out_specs=pl.BlockSpec((te, d // LANES, LANES), lambda t, *_: (t, 0, 0)),
            scratch_shapes=[pltpu.VMEM((2, d, eh), BF16), pltpu.VMEM((2, d, eh), BF16), pltpu.VMEM((2, eh, d), BF16)],
        ),
        out_shape=jax.ShapeDtypeStruct((n_sorted, d // LANES, LANES), F32),
        compiler_params=_cparams(("arbitrary",)),
        name="moe_experts",
    )(ea, eb, n_used, hs, wg, wu, wd, wg, wu, wd)


def _gather_expert_rows(pos_ref, ys_ref, buf, sem, *, tm, tiles_per_row, first_tile):
    cols = pl.num_programs(1)
    n_steps = pl.num_programs(0) * cols
    step = pl.program_id(0) * cols + pl.program_id(1)
    slot = step % 2

    def row_copy(step_, slot_, r):
        tile = (step_ // cols) * tiles_per_row + first_tile + step_ % cols
        return pltpu.make_async_copy(ys_ref.at[pos_ref[tile * tm + r]], buf.at[slot_, r], sem.at[slot_])

    def start_tile(step_, slot_):
        def body(g, carry):
            first = pl.multiple_of(g * DMA_UNROLL, DMA_UNROLL)
            for k in range(DMA_UNROLL):
                row_copy(step_, slot_, first + k).start(priority=k % 2)
            return carry

        lax.fori_loop(0, tm // DMA_UNROLL, body, 0)

    @pl.when(step == 0)
    def _():
        start_tile(0, 0)

    @pl.when(step + 1 < n_steps)
    def _():
        start_tile(step + 1, 1 - slot)

    def wait(r, carry):
        row_copy(step, slot, r).wait()
        return carry

    lax.fori_loop(0, tm, wait, 0, unroll=DMA_UNROLL)
    return buf[slot].reshape(tm, buf.shape[2] * buf.shape[3])


def _combine_kernel(pos_ref, ys_ref, x1_ref, m_ref, lng_ref, lnb_ref, o_ref, buf, sem,
                    *, tm, tiles_per_row, first_tile, alpha):
    y = _gather_expert_rows(pos_ref, ys_ref, buf, sem, tm=tm, tiles_per_row=tiles_per_row, first_tile=first_tile)
    m = m_ref[0, 0, 0]
    o_ref[0] = _layer_norm(alpha * x1_ref[0] + m[5:6, :] * y, lng_ref[0], lnb_ref[0])


def _combine(pos, ys, x1, modtab, ln_g, ln_b, l, tm, nct, alpha, first_tile):
    b, s, d = x1.shape
    cols = s // tm - first_tile
    mod_spec = pl.BlockSpec((1, 1, 1, 6, d),
                            lambda i, j, pos_ref: (l, i, jnp.minimum((j + first_tile) // nct, 1), 0, 0))
    return pl.pallas_call(
        functools.partial(_combine_kernel, tm=tm, tiles_per_row=s // tm, first_tile=first_tile, alpha=alpha),
        grid_spec=pltpu.PrefetchScalarGridSpec(
            num_scalar_prefetch=1,
            grid=(b, cols),
            in_specs=[pl.BlockSpec(memory_space=pl.ANY),
                      pl.BlockSpec((1, tm, d), lambda i, j, pos_ref: (i, j + first_tile, 0)),
                      mod_spec, _layer_spec((1, d), l), _layer_spec((1, d), l)],
            out_specs=pl.BlockSpec((1, tm, d), lambda i, j, pos_ref: (i, j, 0)),
            scratch_shapes=[pltpu.VMEM((2, tm, d // LANES, LANES), F32), pltpu.SemaphoreType.DMA((2,))],
        ),
        out_shape=jax.ShapeDtypeStruct((b, cols * tm, d), F32),
        compiler_params=_cparams(("arbitrary", "arbitrary")),
        name="moe_combine_ln2",
    )(pos, ys, x1, modtab, ln_g, ln_b)


def _block_avg(width, group):
    idx = np.arange(width) // group
    return jnp.asarray((idx[:, None] == idx[None, :]).astype(np.float32) / group, dtype=BF16)


def kernel(x, c, ctx, c_ctx, w_mod, b_mod, w_in, hg_lb_raw, hg_norm_g, ret_decay_raw, ret_gn_g, ret_gn_b, s5_lam_re, s5_lam_im, s5_log_dt, s5_b_re, s5_b_im, s5_c_re, s5_c_im, s5_d, s5_glu_w, s5_glu_b, w_out, ln1_g, ln1_b, ln2_g, ln2_b, rg_w, rg_b, re_w, re_b, exp_w_gate, exp_w_up, exp_w_down):
    bsz, t_lat, d = x.shape
    t_ctx = ctx.shape[1]
    depth = w_mod.shape[0]
    alpha = (2.0 * depth) ** 0.25
    tm = TOKEN_TILE
    assert d == D_MODEL and t_lat % GRID_W == 0 and bsz < SUBLANES
    assert t_ctx % tm == 0 and t_lat % tm == 0, "context and latent lengths must be multiples of the token tile"
    nct = t_ctx // tm
    s = t_ctx + t_lat

    cvec = jnp.concatenate([c, c_ctx[None, :], jnp.zeros((SUBLANES - bsz - 1, d), F32)], 0)
    mod_all = _modulation(cvec, w_mod, b_mod)
    lat = mod_all[:, :bsz].reshape(depth, bsz, 6, d)
    cm = jnp.broadcast_to(mod_all[:, bsz].reshape(depth, 1, 6, d), (depth, bsz, 6, d))
    modtab = jnp.stack([cm, lat], axis=2)

    hg_lb = jnp.cumsum(jax.nn.softmax(hg_lb_raw.astype(F32), axis=0), axis=0)
    hg_lb = hg_lb - hg_lb[:1]
    ret_tables = _retention_decay_tables(jax.nn.log_sigmoid(ret_decay_raw.astype(F32)))
    cos_tab, sin_tab = _rope_tables(t_lat, t_ctx)
    s5_tabs = _s5_weights(s5_lam_re, s5_lam_im, s5_log_dt, s5_b_re, s5_b_im, s5_c_re, s5_c_im)
    pad_r = LANES - N_GROUPS - N_EXPERTS
    wr = jnp.concatenate([rg_w, re_w.reshape(depth, d, N_EXPERTS), jnp.zeros((depth, d, pad_r), F32)], axis=2)
    wr = jnp.swapaxes(wr, 1, 2)
    wr_hi, wr_lo = _split_bf16(wr)
    br = jnp.concatenate([rg_b, re_b.reshape(depth, N_EXPERTS), jnp.zeros((depth, pad_r), F32)], axis=1)[:, :, None]
    row = lambda a: a[:, None, :]
    layer_prm = [row(jnp.tile(hg_norm_g, (1, HG_HEADS))), row(ret_gn_g), row(ret_gn_b), row(s5_d),
                 s5_glu_w.astype(BF16), row(s5_glu_b), w_out.astype(BF16), row(ln1_g), row(ln1_b),
                 wr_hi, wr_lo, br]
    tri = jnp.asarray(np.triu(np.ones((tm, tm), np.float32), 1), dtype=BF16)
    const_prm = [_block_avg(HG_WIDTH, HG_DK), _block_avg(RET_WIDTH, RET_DK), tri]
    w_in_bf16 = w_in.astype(BF16)
    ln2_g, ln2_b = row(ln2_g), row(ln2_b)
    te = EXPERT_TILE
    n_tiles = -(-(bsz * s + N_BUCKETS * (te - 1)) // te)

    hs = jnp.zeros((n_tiles * te, d + LANES), F32)
    xs = jnp.concatenate([ctx, x], axis=1)
    for l in range(depth):
        if l == 0:
            p, ug = _inproj(xs, modtab, w_in_bf16, l, tm, nct)
        else:
            xs, p, ug = _ln2_inproj(pos, ys, x1, modtab, ln2_g, ln2_b, w_in_bf16, l, tm, nct, alpha)
        o_hf, o_hb = _hgrn(p, hg_lb, l, t_ctx // CHUNK)
        o_rf, o_rb = _retention(p, cos_tab, sin_tab, ret_tables, l, t_ctx // RET_CHUNK)
        yg = _s5_conv(ug, *s5_tabs, l, bsz, t_ctx // S5_CHUNK)
        x1, rows, route, counts = _merge(p, o_hf, o_hb, o_rf, o_rb, yg, xs, modtab, layer_prm, const_prm,
                                         l, tm, nct, alpha)
        pos, ea, eb, n_used = _routing_tables(route, counts, te, n_tiles)
        hs = _dispatch(pos, rows, hs, tm)
        ys = _experts(ea, eb, n_used, hs, exp_w_gate, exp_w_up, exp_w_down, l, te)
    return _combine(pos, ys, x1, modtab, ln2_g, ln2_b, depth - 1, tm, nct, alpha, nct)
```

```python
import functools

import numpy as np
import jax
import jax.numpy as jnp
from jax import lax
from jax.experimental import pallas as pl
from jax.experimental.pallas import tpu as pltpu

F32 = jnp.float32
BF16 = jnp.bfloat16

D_MODEL = 1024
HG_WIDTH = 512
HG_HEADS = 4
HG_DK = HG_WIDTH // HG_HEADS
RET_WIDTH = 256
RET_HEADS = 4
RET_DK = RET_WIDTH // RET_HEADS
S5_WIDTH = 256
S5_GROUP_CH = 16
S5_GROUPS = S5_WIDTH // S5_GROUP_CH
S5_STATE = 64
IN_COLS = 5 * HG_WIDTH + 4 * RET_WIDTH + S5_WIDTH
CHUNK = 64
HGRN_BATCH = 4
RET_CHUNK = 256
N_GROUPS = 4
EXPERTS_PER_GROUP = 4
N_EXPERTS = N_GROUPS * EXPERTS_PER_GROUP
N_PAIRS = EXPERTS_PER_GROUP * (EXPERTS_PER_GROUP - 1) // 2
N_BUCKETS = N_GROUPS * N_PAIRS
PAIR_SLOT_A = (0, 2, 2, 3, 3, 3)
PAIR_SLOT_B = (1, 1, 0, 0, 1, 2)
EXPERT_HIDDEN = D_MODEL // 2
LN_EPS = 1e-5
ROPE_BASE = 10000.0
GRID_W = 64

LANES = 128
SUBLANES = 8
TOKEN_TILE = 256
S5_CHUNK = 32
S5_FOLD = S5_CHUNK * S5_GROUP_CH
EXPERT_TILE = 256
DMA_UNROLL = 8
VMEM_LIMIT = 56 * 1024 * 1024


def _cparams(sem):
    return pltpu.CompilerParams(dimension_semantics=sem, vmem_limit_bytes=VMEM_LIMIT)


def _split_bf16(x):
    hi = x.astype(BF16)
    lo = (x - hi.astype(F32)).astype(BF16)
    return hi, lo


def _dot(a, b):
    return jnp.dot(a, b, preferred_element_type=F32)


def _dot3(a, b):
    ah, al = _split_bf16(a)
    bh, bl = _split_bf16(b)
    return _dot(ah, bh) + _dot(ah, bl) + _dot(al, bh)


def _dot2(a, b_bf16):
    ah, al = _split_bf16(a)
    return _dot(ah, b_bf16) + _dot(al, b_bf16)


def _dot_nt(a, b):
    return lax.dot_general(a, b, (((1,), (1,)), ((), ())), preferred_element_type=F32)


def _dot_tn(a, b):
    return lax.dot_general(a, b, (((0,), (0,)), ((), ())), preferred_element_type=F32)


def _silu(x):
    return x * jax.nn.sigmoid(x)


def _layer_spec(shape, l):
    zeros = (0,) * len(shape)
    return pl.BlockSpec((1,) + tuple(shape), lambda *_: (l,) + zeros)


def _mod_spec(l, nct):
    return pl.BlockSpec((1, 1, 1, 6, D_MODEL), lambda i, j, *_: (l, i, jnp.minimum(j // nct, 1), 0, 0))


def _swap_sublane_lanegroup(v):
    n = v.shape[0]
    r = lax.broadcasted_iota(jnp.int32, v.shape, 0)
    l = lax.broadcasted_iota(jnp.int32, v.shape, 1)
    for k in range(3):
        rb = (r >> k) & 1
        gb = (l >> (4 + k)) & 1
        sh = S5_GROUP_CH << k
        st = 1 << k
        a = pltpu.roll(pltpu.roll(v, LANES - sh, 1), st, 0)
        b = pltpu.roll(pltpu.roll(v, sh, 1), n - st, 0)
        v = jnp.where(rb == gb, v, jnp.where(rb == 1, a, b))
    return v


def _mod_kernel(c_ref, w_ref, b_ref, o_ref):
    sc = _silu(c_ref[...])
    o_ref[0] = _dot3(sc, w_ref[0]) + b_ref[0]


def _modulation(cvec, w_mod, b_mod):
    depth, d, n = w_mod.shape
    rows = cvec.shape[0]
    tn = 1536
    return pl.pallas_call(
        _mod_kernel,
        grid=(depth, n // tn),
        in_specs=[
            pl.BlockSpec((rows, d), lambda l, j: (0, 0)),
            pl.BlockSpec((1, d, tn), lambda l, j: (l, 0, j)),
            pl.BlockSpec((1, 1, tn), lambda l, j: (l, 0, j)),
        ],
        out_specs=pl.BlockSpec((1, rows, tn), lambda l, j: (l, 0, j)),
        out_shape=jax.ShapeDtypeStruct((depth, rows, n), F32),
        compiler_params=_cparams(("arbitrary", "arbitrary")),
        name="modulation",
    )(cvec, w_mod, b_mod.reshape(depth, 1, n))


def _inproj_kernel(x_ref, m_ref, w_ref, o_ref, ug_ref, wt_scr):
    _project(x_ref[0], m_ref, w_ref, o_ref, ug_ref, wt_scr)


def _ln2_inproj_kernel(pos_ref, ys_ref, x1_ref, mprev_ref, lng_ref, lnb_ref, m_ref, w_ref,
                       x2_ref, o_ref, ug_ref, buf, sem, wt_scr, *, tm, tiles_per_row, alpha):
    y = _gather_expert_rows(pos_ref, ys_ref, buf, sem, tm=tm, tiles_per_row=tiles_per_row, first_tile=0)
    g2 = mprev_ref[0, 0, 0][5:6, :]
    x2 = _layer_norm(alpha * x1_ref[0] + g2 * y, lng_ref[0], lnb_ref[0])
    x2_ref[0] = x2
    _project(x2, m_ref, w_ref, o_ref, ug_ref, wt_scr)


def _project(x, m_ref, w_ref, o_ref, ug_ref, wt_scr):
    m = m_ref[0, 0, 0]
    h = x * (1.0 + m[1:2, :]) + m[0:1, :]
    p = _dot(h.astype(BF16), w_ref[0])
    o_ref[0] = p
    rows_per_chunk = S5_CHUNK // SUBLANES
    n_chunks = p.shape[0] // S5_CHUNK
    for half in range(S5_WIDTH // LANES):
        lo = IN_COLS - S5_WIDTH + half * LANES
        wt_scr[...] = _swap_sublane_lanegroup(p[:, lo : lo + LANES])
        for g_lo in range(SUBLANES):
            for s_hi in range(rows_per_chunk):
                piece = wt_scr[pl.ds(s_hi * SUBLANES + g_lo, n_chunks, stride=S5_CHUNK), :]
                ug_ref[half * SUBLANES + g_lo, :, s_hi * LANES : (s_hi + 1) * LANES] = piece


def _inproj(x, modtab, w_in_bf16, l, tm, nct):
    b, s, d = x.shape
    n = w_in_bf16.shape[-1]
    cpt = tm // S5_CHUNK
    return pl.pallas_call(
        _inproj_kernel,
        grid=(b, s // tm),
        in_specs=[
            pl.BlockSpec((1, tm, d), lambda i, j: (i, j, 0)),
            _mod_spec(l, nct),
            _layer_spec((d, n), l),
        ],
        out_specs=[pl.BlockSpec((1, tm, n), lambda i, j: (i, j, 0)),
                   pl.BlockSpec((S5_GROUPS, cpt, S5_FOLD), lambda i, j: (0, j, i))],
        out_shape=[jax.ShapeDtypeStruct((b, s, n), F32),
                   jax.ShapeDtypeStruct((S5_GROUPS, s // S5_CHUNK, b * S5_FOLD), F32)],
        scratch_shapes=[pltpu.VMEM((tm, LANES), F32)],
        compiler_params=_cparams(("arbitrary", "arbitrary")),
        name="inproj",
    )(x, modtab, w_in_bf16)


def _ln2_inproj(pos, ys, x1, modtab, ln_g, ln_b, w_in_bf16, l, tm, nct, alpha):
    b, s, d = x1.shape
    n = w_in_bf16.shape[-1]
    cpt = tm // S5_CHUNK
    tok = lambda width: pl.BlockSpec((1, tm, width), lambda i, j, pos_ref: (i, j, 0))
    return pl.pallas_call(
        functools.partial(_ln2_inproj_kernel, tm=tm, tiles_per_row=s // tm, alpha=alpha),
        grid_spec=pltpu.PrefetchScalarGridSpec(
            num_scalar_prefetch=1,
            grid=(b, s // tm),
            in_specs=[pl.BlockSpec(memory_space=pl.ANY), tok(d),
                      _mod_spec(l - 1, nct), _layer_spec((1, d), l - 1), _layer_spec((1, d), l - 1),
                      _mod_spec(l, nct), _layer_spec((d, n), l)],
            out_specs=[tok(d), tok(n),
                       pl.BlockSpec((S5_GROUPS, cpt, S5_FOLD), lambda i, j, pos_ref: (0, j, i))],
            scratch_shapes=[pltpu.VMEM((2, tm, d // LANES, LANES), F32), pltpu.SemaphoreType.DMA((2,)),
                            pltpu.VMEM((tm, LANES), F32)],
        ),
        out_shape=[jax.ShapeDtypeStruct((b, s, d), F32), jax.ShapeDtypeStruct((b, s, n), F32),
                   jax.ShapeDtypeStruct((S5_GROUPS, s // S5_CHUNK, b * S5_FOLD), F32)],
        compiler_params=_cparams(("arbitrary", "arbitrary")),
        name="ln2_inproj",
    )(pos, ys, x1, modtab, ln_g, ln_b, modtab, w_in_bf16)


def _block_gate_products(f, reverse):
    c = f.shape[0]
    row = lax.broadcasted_iota(jnp.int32, (c, 1), 0)
    a, z, b = f, f, jnp.ones_like(f)
    out = []
    s = 1
    while s < c:
        out.append((a, b))
        if s < SUBLANES:
            z3 = z.reshape(c // SUBLANES, SUBLANES, z.shape[1])
            up = pltpu.roll(z3, s, 1).reshape(z.shape)
            dn = pltpu.roll(z3, SUBLANES - s, 1).reshape(z.shape)
            odd = (row & s) != 0
            if reverse:
                a = a * jnp.where(odd, 1.0, dn)
                b = b * jnp.where(odd, up, 1.0)
            else:
                a = a * jnp.where(odd, up, 1.0)
                b = b * jnp.where(odd, 1.0, dn)
            z = z * jnp.where(odd, up, dn)
        else:
            na, nb, nz = [], [], []
            for lo in range(0, c, 2 * s):
                ev, od = slice(lo, lo + s), slice(lo + s, lo + 2 * s)
                zz = z[ev] * z[od]
                if reverse:
                    na += [a[ev] * z[od], a[od]]
                    nb += [b[ev], b[od] * z[ev]]
                else:
                    na += [a[ev], a[od] * z[ev]]
                    nb += [b[ev] * z[od], b[od]]
                nz += [zz, zz]
            a, b, z = (jnp.concatenate(t, axis=0) for t in (na, nb, nz))
        s *= 2
    out.append((a, b))
    return out, z


def _hgrn_kernel(qf_ref, vf_ref, zf_ref, qb_ref, vb_ref, zb_ref, lb_ref, of_ref, ob_ref, st_ref):
    c = CHUNK

    @pl.when(pl.program_id(1) == 0)
    def _():
        st_ref[...] = jnp.zeros_like(st_ref)

    ri = lax.broadcasted_iota(jnp.int32, (c, c), 0)
    ci = lax.broadcasted_iota(jnp.int32, (c, c), 1)
    for d, (q_ref, v_ref, z_ref, o_ref) in enumerate(
        ((qf_ref, vf_ref, zf_ref, of_ref), (qb_ref, vb_ref, zb_ref, ob_ref))
    ):
        reverse = d == 1
        causal = (ri < ci) if reverse else (ri > ci)
        n_levels = c.bit_length() - 1
        masks = [(((ri >> lvl) ^ (ci >> lvl)) == 1) & causal for lvl in range(n_levels)]
        for bi, h in [(bi, h) for bi in range(q_ref.shape[0]) for h in range(HG_HEADS)]:
            hs = slice(h * HG_DK, (h + 1) * HG_DK)
            q = _silu(q_ref[bi, :, hs])
            v = v_ref[bi, :, hs].astype(BF16)
            lb = lb_ref[0, d : d + 1, hs]
            f = lb + (1.0 - lb) * jax.nn.sigmoid(z_ref[bi, :, hs])
            k = 1.0 - f
            levels, tot = _block_gate_products(f, reverse)
            scores = jnp.where(ri == ci, _dot_nt(q.astype(BF16), k.astype(BF16)), 0.0)
            for mask, (a, bb) in zip(masks, levels):
                scores = scores + jnp.where(mask, _dot_nt((q * a).astype(BF16), (k * bb).astype(BF16)), 0.0)
            a_full, b_full = levels[-1]
            st = st_ref[bi, d, h]
            o_ref[bi, :, hs] = (_dot(scores.astype(BF16), v)
                                + _dot_nt((q * a_full).astype(BF16), st.astype(BF16)))
            st_ref[bi, d, h] = st * tot[0:1, :] + _dot_tn(v, (k * b_full).astype(BF16))


def _bwd_chunk(n, nc_ctx, nc_all):
    return jnp.where(n < nc_ctx, nc_ctx - 1 - n, nc_all + nc_ctx - 1 - n)


def _hgrn(p, lb_all, l, nc_ctx):
    b, s, _ = p.shape
    nc = s // CHUNK
    w = HG_WIDTH

    bb = HGRN_BATCH if b % HGRN_BATCH == 0 else 1

    def fwd(col):
        return pl.BlockSpec((bb, CHUNK, w), lambda i, n: (i, n, col))

    def bwd(col):
        return pl.BlockSpec((bb, CHUNK, w), lambda i, n: (i, _bwd_chunk(n, nc_ctx, nc), col))

    out = jax.ShapeDtypeStruct((b, s, w), F32)
    return pl.pallas_call(
        _hgrn_kernel,
        grid=(b // bb, nc),
        in_specs=[fwd(0), fwd(1), fwd(3), bwd(0), bwd(1), bwd(4), _layer_spec((2, w), l)],
        out_specs=[fwd(0), bwd(0)],
        out_shape=[out, out],
        scratch_shapes=[pltpu.VMEM((bb, 2, HG_HEADS, HG_DK, HG_DK), F32)],
        compiler_params=_cparams(("arbitrary", "arbitrary")),
        name="hgrn2_scan",
    )(p, p, p, p, p, p, lb_all)


def _swap_halves(x, half):
    n = x.shape[-1]
    lane = lax.broadcasted_iota(jnp.int32, (1, n), 1)
    lower = (lane & half) == 0
    return jnp.where(lower, pltpu.roll(x, n - half, 1), pltpu.roll(x, half, 1))


def _ret_kernel(qf_ref, kf_ref, vf_ref, cf_ref, sf_ref, qb_ref, kb_ref, vb_ref, cb_ref, sb_ref,
                dmat_ref, rq_ref, rk_ref, cd_ref, of_ref, ob_ref, st_ref):
    @pl.when(pl.program_id(1) == 0)
    def _():
        st_ref[...] = jnp.zeros_like(st_ref)

    half = RET_DK // 4
    for d, (q_ref, k_ref, v_ref, c_ref, s_ref, o_ref) in enumerate(
        ((qf_ref, kf_ref, vf_ref, cf_ref, sf_ref, of_ref), (qb_ref, kb_ref, vb_ref, cb_ref, sb_ref, ob_ref))
    ):
        cos = c_ref[...]
        sin = s_ref[...]
        q = q_ref[0]
        k = k_ref[0] * (RET_DK ** -0.5)
        q = q * cos + _swap_halves(q, half) * sin
        k = k * cos + _swap_halves(k, half) * sin
        v = v_ref[0]
        vt = v.T.astype(BF16)
        v = v.astype(BF16)
        q0 = q.astype(BF16)
        k0 = k.astype(BF16)
        qd = (q * rq_ref[0, d]).astype(BF16)
        kd = (k * rk_ref[0, d]).astype(BF16)
        cd = cd_ref[0, d : d + 1, :]
        lane = lax.broadcasted_iota(jnp.int32, (1, LANES), 1)
        left = lane < RET_DK
        same_head = (lax.broadcasted_iota(jnp.int32, (LANES, 1), 0) < RET_DK) == left
        zero = jnp.zeros((), BF16)
        for p in range(RET_HEADS // 2):
            ps = slice(p * LANES, (p + 1) * LANES)
            kp, vp = k0[:, ps], v[:, ps]
            k_blk = jnp.concatenate([jnp.where(left, kp, zero), jnp.where(left, zero, kp)], axis=0)
            v_blk = jnp.concatenate([jnp.where(left, vp, zero), jnp.where(left, zero, vp)], axis=0)
            scores = _dot_nt(q0[:, ps], k_blk) * dmat_ref[0, d, p]
            st = st_ref[d, p]
            o_ref[0, :, ps] = _dot(scores.astype(BF16), v_blk) + _dot_nt(qd[:, ps], st.astype(BF16))
            st_ref[d, p] = st * cd[:, ps] + jnp.where(same_head, _dot(vt[ps, :], kd[:, ps]), 0.0)


def _retention(p, cos_tab, sin_tab, tables, l, nc_ctx):
    b, s, _ = p.shape
    nc = s // RET_CHUNK
    w = RET_WIDTH
    base = 5 * HG_WIDTH // w

    def fwd(col):
        return pl.BlockSpec((1, RET_CHUNK, w), lambda i, n: (i, n, col))

    def bwd(col):
        return pl.BlockSpec((1, RET_CHUNK, w), lambda i, n: (i, _bwd_chunk(n, nc_ctx, nc), col))

    tab_f = pl.BlockSpec((RET_CHUNK, w), lambda i, n: (n, 0))
    tab_b = pl.BlockSpec((RET_CHUNK, w), lambda i, n: (_bwd_chunk(n, nc_ctx, nc), 0))
    out = jax.ShapeDtypeStruct((b, s, w), F32)
    return pl.pallas_call(
        _ret_kernel,
        grid=(b, nc),
        in_specs=[fwd(base), fwd(base + 1), fwd(base + 2), tab_f, tab_f,
                  bwd(base), bwd(base + 1), bwd(base + 2), tab_b, tab_b]
                 + [_layer_spec(t.shape[1:], l) for t in tables],
        out_specs=[fwd(0), bwd(0)],
        out_shape=[out, out],
        scratch_shapes=[pltpu.VMEM((2, RET_HEADS // 2, LANES, LANES), F32)],
        compiler_params=_cparams(("arbitrary", "arbitrary")),
        name="retention_scan",
    )(p, p, p, cos_tab, sin_tab, p, p, p, cos_tab, sin_tab, *tables)


def _rope_tables(t_lat, t_ctx):
    m = RET_DK // 4
    inv = ROPE_BASE ** (-jnp.arange(m, dtype=F32) / m)
    rows = jnp.repeat(jnp.arange(t_lat // GRID_W, dtype=jnp.int32), GRID_W).astype(F32)
    cols = jnp.tile(jnp.arange(GRID_W, dtype=jnp.int32), t_lat // GRID_W).astype(F32)

    def half_tables(pos):
        ang = pos[:, None] * inv
        c, s = jnp.cos(ang), jnp.sin(ang)
        return jnp.concatenate([c, c], -1), jnp.concatenate([-s, s], -1)

    cr, sr = half_tables(rows)
    cc, sc = half_tables(cols)
    cos_h = jnp.concatenate([cr, cc], -1)
    sin_h = jnp.concatenate([sr, sc], -1)
    cos = jnp.tile(cos_h, (1, RET_HEADS))
    sin = jnp.tile(sin_h, (1, RET_HEADS))
    cos = jnp.concatenate([jnp.ones((t_ctx, RET_WIDTH), F32), cos], 0)
    sin = jnp.concatenate([jnp.zeros((t_ctx, RET_WIDTH), F32), sin], 0)
    return cos, sin


def _retention_decay_tables(log_gamma):
    c = RET_CHUNK
    i = jnp.arange(c, dtype=F32)
    diff = i[:, None] - i[None, :]
    lg = log_gamma[:, :, :, None, None]
    d_f = jnp.where(diff >= 0, jnp.exp(lg[:, 0] * diff), 0.0)
    d_b = jnp.where(diff <= 0, jnp.exp(lg[:, 1] * (-diff)), 0.0)
    dmat = jnp.stack([d_f, d_b], 1)
    dmat = jnp.concatenate([dmat[:, :, 0::2], dmat[:, :, 1::2]], axis=-1)
    lane_lg = jnp.repeat(log_gamma, RET_DK, axis=2)[:, :, None, :]
    col = i[None, :, None]
    rq = jnp.stack([jnp.exp(lane_lg[:, 0] * (col + 1.0)), jnp.exp(lane_lg[:, 1] * (c - col))], 1)
    rk = jnp.stack([jnp.exp(lane_lg[:, 0] * (c - 1.0 - col)), jnp.exp(lane_lg[:, 1] * col)], 1)
    cdec = jnp.exp(lane_lg[:, :, 0, :] * c)
    return dmat, rq, rk, cdec


def _s5_kernel(u_ref, base_ref, inj_ref, w2_ref, ac_ref, y_ref, toep_scr, a_scr, i_scr, hf_scr, hb_scr,
               *, n_ctx, n_all, bsz):
    width = S5_FOLD
    st2 = 2 * S5_STATE
    ch = S5_GROUP_CH
    lane = lax.broadcasted_iota(jnp.int32, (ch, width), 1)
    base_f = base_ref[0, 0, 0]
    base_b = base_ref[0, 0, 1]
    for s in range(S5_CHUNK):
        sh_f = s * ch
        sh_b = (S5_CHUNK - 1 - s) * ch
        part_f = jnp.where(lane >= sh_f, pltpu.roll(base_f, sh_f, 1), 0.0) if sh_f else base_f
        part_b = jnp.where(lane < width - sh_b, pltpu.roll(base_b, width - sh_b, 1), 0.0) if sh_b else base_b
        toep_scr[s * ch : (s + 1) * ch, :] = (part_f + part_b).astype(BF16)
    for b in range(bsz):
        ub = u_ref[0, :, b * width : (b + 1) * width].astype(BF16)
        a_scr[b] = _dot(ub, toep_scr[...])
        i_scr[b] = _dot(ub, inj_ref[0, 0])
    ac = ac_ref[0, 0]

    zero = jnp.zeros((1, st2), F32)
    state = [(zero, zero)] * (2 * bsz)
    for n in range(n_all):
        nb = n_ctx - 1 - n if n < n_ctx else n_all + n_ctx - 1 - n
        for b in range(bsz):
            for d, (row, scr) in enumerate(((n, hf_scr), (nb, hb_scr))):
                h, hsw = state[2 * b + d]
                scr[b, row : row + 1, :] = h
                inj = i_scr[b, row : row + 1, d * st2 : (d + 1) * st2]
                inj_sw = i_scr[b, row : row + 1, (d + 2) * st2 : (d + 3) * st2]
                ar, ai = ac[2 * d : 2 * d + 1, :], ac[2 * d + 1 : 2 * d + 2, :]
                state[2 * b + d] = (h * ar + hsw * ai + inj, hsw * ar - h * ai + inj_sw)
    for b in range(bsz):
        y_ref[0, :, b * width : (b + 1) * width] = (
            a_scr[b]
            + _dot(hf_scr[b].astype(BF16), w2_ref[0, 0, 0])
            + _dot(hb_scr[b].astype(BF16), w2_ref[0, 0, 1]))


def _s5_conv(ug, base, inj, w2, ac, l, bsz, n_ctx):
    g, n_all, _ = ug.shape
    width = S5_FOLD
    st2 = 2 * S5_STATE
    return pl.pallas_call(
        functools.partial(_s5_kernel, n_ctx=n_ctx, n_all=n_all, bsz=bsz),
        grid=(g,),
        in_specs=[
            pl.BlockSpec((1, n_all, bsz * width), lambda i: (i, 0, 0)),
            pl.BlockSpec((1, 1, 2, S5_GROUP_CH, width), lambda i: (l, i, 0, 0, 0)),
            pl.BlockSpec((1, 1, width, 4 * st2), lambda i: (l, i, 0, 0)),
            pl.BlockSpec((1, 1, 2, st2, width), lambda i: (l, i, 0, 0, 0)),
            pl.BlockSpec((1, 1, SUBLANES, st2), lambda i: (l, i, 0, 0)),
        ],
        out_specs=pl.BlockSpec((1, n_all, bsz * width), lambda i: (i, 0, 0)),
        out_shape=jax.ShapeDtypeStruct((g, n_all, bsz * width), F32),
        scratch_shapes=[pltpu.VMEM((width, width), BF16),
                        pltpu.VMEM((bsz, n_all, width), F32), pltpu.VMEM((bsz, n_all, 4 * st2), F32),
                        pltpu.VMEM((bsz, n_all, st2), F32), pltpu.VMEM((bsz, n_all, st2), F32)],
        compiler_params=_cparams(("arbitrary",)),
        name="s5_conv",
    )(ug, base, inj, w2, ac)


def _s5_weights(lam_re, lam_im, log_dt, b_re, b_im, c_re, c_im):
    cs, ch, p, g = S5_CHUNK, S5_GROUP_CH, S5_STATE, S5_GROUPS
    nl = lam_re.shape[0]
    lam_re = jnp.minimum(lam_re.astype(F32), -1e-4)
    lam_im = lam_im.astype(F32)
    dt = jnp.exp(log_dt.astype(F32))[..., None]
    mag = jnp.exp(dt * lam_re)
    abar_re, abar_im = mag * jnp.cos(dt * lam_im), mag * jnp.sin(dt * lam_im)
    den = jnp.square(lam_re) + jnp.square(lam_im)
    nr, ni = abar_re - 1.0, abar_im
    coef_re = ((nr * lam_re + ni * lam_im) / den)[..., None]
    coef_im = ((ni * lam_re - nr * lam_im) / den)[..., None]
    b_re, b_im = b_re.astype(F32), b_im.astype(F32)
    bb_re = coef_re * b_re - coef_im * b_im
    bb_im = coef_re * b_im + coef_im * b_re
    ct_re = jnp.swapaxes(c_re.astype(F32), -1, -2)
    ct_im = jnp.swapaxes(c_im.astype(F32), -1, -2)

    def powers(expo):
        e = expo.astype(F32)[None, :, None, None, :]
        m = jnp.exp(e * (dt * lam_re)[..., None])
        return m * jnp.cos(e * (dt * lam_im)[..., None]), m * jnp.sin(e * (dt * lam_im)[..., None])

    slots = jnp.arange(cs + 1)
    pr, pi = powers(jnp.stack([slots, cs - slots]))
    r_re = (pr[..., None] * ct_re[..., None, :] - pi[..., None] * ct_im[..., None, :]).reshape(nl, 2, g, p, (cs + 1) * ch)
    r_im = (pr[..., None] * ct_im[..., None, :] + pi[..., None] * ct_re[..., None, :]).reshape(nl, 2, g, p, (cs + 1) * ch)
    hi = lax.Precision.HIGHEST
    kern = (jnp.einsum('ldgpe,ldgpn->ldgen', bb_re, r_re, precision=hi)
            - jnp.einsum('ldgpe,ldgpn->ldgen', bb_im, r_im, precision=hi))
    w = cs * ch
    base = jnp.stack([kern[:, 0, :, :, :w], kern[:, 1, :, :, ch:]], axis=2)
    w2 = jnp.stack([jnp.concatenate([r_re[:, 0, :, :, ch:], -r_im[:, 0, :, :, ch:]], axis=2),
                    jnp.concatenate([r_re[:, 1, :, :, :w], -r_im[:, 1, :, :, :w]], axis=2)], axis=2)
    steps = jnp.arange(cs)
    qr, qi = powers(jnp.stack([cs - 1 - steps, steps]))
    qr = jnp.swapaxes(qr, -1, -2)[..., :, None, :]
    qi = jnp.swapaxes(qi, -1, -2)[..., :, None, :]
    bt_re = jnp.swapaxes(bb_re, -1, -2)[:, :, :, None]
    bt_im = jnp.swapaxes(bb_im, -1, -2)[:, :, :, None]
    cat = lambda a, b: jnp.concatenate([a, b], axis=-1)
    qr2, qi2 = cat(qr, qr), cat(qi, qi)
    ab = (qr2 * cat(bt_re, bt_im) + qi2 * cat(-bt_im, bt_re)).reshape(nl, 2, g, w, 2 * p)
    ab_sw = (qr2 * cat(bt_im, bt_re) + qi2 * cat(bt_re, -bt_im)).reshape(nl, 2, g, w, 2 * p)
    inj = jnp.concatenate([ab[:, 0], ab[:, 1], ab_sw[:, 0], ab_sw[:, 1]], axis=-1)
    ac_rows = []
    for d, slot in enumerate((cs, 0)):
        ar, ai = pr[:, d, :, :, slot], pi[:, d, :, :, slot]
        ac_rows += [jnp.concatenate([ar, ar], -1), jnp.concatenate([-ai, ai], -1)]
    ac = jnp.stack(ac_rows + [jnp.zeros_like(ac_rows[0])] * (SUBLANES - 4), axis=2)
    return base, inj.astype(BF16), w2.astype(BF16), ac


def _layer_norm(x, g, b):
    mu = jnp.mean(x, -1, keepdims=True)
    xc = x - mu
    var = jnp.mean(xc * xc, -1, keepdims=True)
    return xc * lax.rsqrt(var + LN_EPS) * g + b


def _route(logits_t):
    col = lambda i: logits_t[i : i + 1, :]
    gl = [col(i) for i in range(N_GROUPS)]
    gmax = functools.reduce(jnp.maximum, gl)
    g_idx = jnp.full_like(gmax, N_GROUPS - 1).astype(jnp.int32)
    for i in reversed(range(N_GROUPS - 1)):
        g_idx = jnp.where(gl[i] == gmax, i, g_idx)
    g_p = 1.0 / functools.reduce(lambda a, b: a + b, [jnp.exp(x - gmax) for x in gl])
    el = []
    for e in range(EXPERTS_PER_GROUP):
        v = col(N_GROUPS + (N_GROUPS - 1) * EXPERTS_PER_GROUP + e)
        for g in reversed(range(N_GROUPS - 1)):
            v = jnp.where(g_idx == g, col(N_GROUPS + g * EXPERTS_PER_GROUP + e), v)
        el.append(v)
    m1 = functools.reduce(jnp.maximum, el)
    i1 = jnp.full_like(g_idx, EXPERTS_PER_GROUP - 1)
    for e in reversed(range(EXPERTS_PER_GROUP - 1)):
        i1 = jnp.where(el[e] == m1, e, i1)
    rest = [jnp.where(i1 == e, -jnp.inf, el[e]) for e in range(EXPERTS_PER_GROUP)]
    m2 = functools.reduce(jnp.maximum, rest)
    i2 = jnp.full_like(g_idx, EXPERTS_PER_GROUP - 1)
    for e in reversed(range(EXPERTS_PER_GROUP - 1)):
        i2 = jnp.where((rest[e] == m2) & (i1 != e), e, i2)
    t = jnp.exp(m2 - m1)
    w1 = g_p / (1.0 + t)
    w2 = g_p * t / (1.0 + t)
    lo = jnp.minimum(i1, i2)
    hi = jnp.maximum(i1, i2)
    pair = jnp.where(hi == 1, 0, jnp.where(hi == 2, jnp.where(lo == 1, 1, 2), 3 + lo))
    bucket = g_idx * N_PAIRS + pair
    w_lower = jnp.where(i1 < i2, w1, w2)
    w_higher = jnp.where(i1 < i2, w2, w1)
    w_a = jnp.where(pair == 0, w_lower, w_higher)
    w_b = jnp.where(pair == 0, w_higher, w_lower)
    return bucket, w_a, w_b


def _merge_kernel(gate_ref, rg_ref, u_ref, hf_ref, hb_ref, rf_ref, rb_ref, yg_ref, x_ref, m_ref,
                  hgn_ref, gng_ref, gnb_ref, d_ref, gw_ref, gb_ref, wo_ref, lng_ref, lnb_ref,
                  wrh_ref, wrl_ref, br_ref, a128_ref, a64_ref, tri_ref, x1_ref, h2_ref, route_ref, cnt_ref, wt_scr,
                  *, alpha):
    @pl.when((pl.program_id(0) == 0) & (pl.program_id(1) == 0))
    def _():
        cnt_ref[...] = jnp.zeros_like(cnt_ref)

    o_hg = hf_ref[0] + hb_ref[0]
    ms = _dot2(o_hg * o_hg, a128_ref[...])
    hg = o_hg * lax.rsqrt(ms + LN_EPS) * hgn_ref[0] * _silu(gate_ref[0])
    o_rt = rf_ref[0] + rb_ref[0]
    mu = _dot2(o_rt, a64_ref[...])
    xc = o_rt - mu
    var = _dot2(xc * xc, a64_ref[...])
    rt = (xc * lax.rsqrt(var + LN_EPS) * gng_ref[0] + gnb_ref[0]) * _silu(rg_ref[0])
    rows_per_chunk = S5_CHUNK // SUBLANES
    n_chunks = yg_ref.shape[1]
    y5 = []
    for half in range(S5_WIDTH // LANES):
        for g_lo in range(SUBLANES):
            for s_hi in range(rows_per_chunk):
                wt_scr[pl.ds(s_hi * SUBLANES + g_lo, n_chunks, stride=S5_CHUNK), :] = (
                    yg_ref[half * SUBLANES + g_lo, :, s_hi * LANES : (s_hi + 1) * LANES])
        y5.append(_swap_sublane_lanegroup(wt_scr[...]))
    y5 = jnp.concatenate(y5, axis=-1)
    s5 = jax.nn.gelu(y5 + d_ref[0] * u_ref[0])
    s5 = s5 * jax.nn.sigmoid(_dot(s5.astype(BF16), gw_ref[0]) + gb_ref[0])
    cat = jnp.concatenate([hg, rt, s5], axis=-1).astype(BF16)
    y = _dot(cat, wo_ref[0])
    m = m_ref[0, 0, 0]
    x1 = _layer_norm(alpha * x_ref[0] + m[2:3, :] * y, lng_ref[0], lnb_ref[0])
    x1_ref[0] = x1
    h2 = x1 * (1.0 + m[4:5, :]) + m[3:4, :]
    h_hi, h_lo = _split_bf16(h2)
    logits_t = (_dot_nt(wrh_ref[0], h_hi) + _dot_nt(wrh_ref[0], h_lo) + _dot_nt(wrl_ref[0], h_hi)) + br_ref[0]
    bucket, _, _ = _route(logits_t)
    sub = lax.broadcasted_iota(jnp.int32, logits_t.shape, 0)
    h2_ref[0] = h2.reshape(h2_ref.shape[1:])
    onehot = sub == bucket
    before = _dot(onehot.astype(BF16), tri_ref[...]) + cnt_ref[...]
    rank = jnp.sum(jnp.where(onehot, before, 0.0), axis=0, keepdims=True)
    rsub = lax.broadcasted_iota(jnp.int32, route_ref.shape[1:], 0)
    route_ref[0] = jnp.where(rsub == 0, bucket.astype(F32), jnp.where(rsub == 1, rank, 0.0))
    cnt_ref[...] += jnp.sum(onehot.astype(F32), axis=1, keepdims=True)


def _merge(p, o_hf, o_hb, o_rf, o_rb, yg, x, modtab, layer_prm, const_prm, l, tm, nct, alpha):
    b, s, d = x.shape
    cpt = tm // S5_CHUNK

    def tok(width, col):
        return pl.BlockSpec((1, tm, width), lambda i, j: (i, j, col))

    def whole(a):
        return pl.BlockSpec(a.shape, lambda i, j: (0,) * a.ndim)

    rbase = 5 * HG_WIDTH // RET_WIDTH
    in_specs = [tok(HG_WIDTH, 2), tok(RET_WIDTH, rbase + 3), tok(S5_WIDTH, rbase + 4),
                tok(HG_WIDTH, 0), tok(HG_WIDTH, 0), tok(RET_WIDTH, 0), tok(RET_WIDTH, 0),
                pl.BlockSpec((S5_GROUPS, cpt, S5_FOLD), lambda i, j: (0, j, i)),
                tok(d, 0), _mod_spec(l, nct)]
    in_specs += [_layer_spec(a.shape[1:], l) for a in layer_prm]
    in_specs += [whole(a) for a in const_prm]
    return pl.pallas_call(
        functools.partial(_merge_kernel, alpha=alpha),
        grid=(b, s // tm),
        in_specs=in_specs,
        out_specs=[tok(d, 0), pl.BlockSpec((1, tm, d // LANES, LANES), lambda i, j: (i, j, 0, 0)),
                   pl.BlockSpec((1, SUBLANES, tm), lambda i, j: (i, 0, j)),
                   pl.BlockSpec((LANES, 1), lambda i, j: (0, 0))],
        out_shape=[jax.ShapeDtypeStruct((b, s, d), F32), jax.ShapeDtypeStruct((b, s, d // LANES, LANES), F32),
                   jax.ShapeDtypeStruct((b, SUBLANES, s), F32), jax.ShapeDtypeStruct((LANES, 1), F32)],
        scratch_shapes=[pltpu.VMEM((tm, LANES), F32)],
        compiler_params=_cparams(("arbitrary", "arbitrary")),
        name="merge_ln1_router",
    )(p, p, p, o_hf, o_hb, o_rf, o_rb, yg, x, modtab, *layer_prm, *const_prm)


def _routing_tables(route, counts, te, n_tiles):
    bucket = route[:, 0, :].astype(jnp.int32).reshape(-1)
    rank = route[:, 1, :].astype(jnp.int32).reshape(-1)
    cnt = counts[:N_BUCKETS, 0].astype(jnp.int32)
    padded = (cnt + te - 1) // te * te
    ends = jnp.cumsum(padded)
    pos = (ends - padded)[bucket] + rank
    n_used = ends[-1] // te
    tile = jnp.arange(n_tiles, dtype=jnp.int32)
    tb = jnp.minimum(jnp.searchsorted(ends, tile * te, side="right"), N_BUCKETS - 1).astype(jnp.int32)
    tb = jnp.where(tile < n_used, tb, tb[jnp.maximum(n_used - 1, 0)])
    group, pair = tb // N_PAIRS, tb % N_PAIRS
    ea = group * EXPERTS_PER_GROUP + jnp.asarray(PAIR_SLOT_A, jnp.int32)[pair]
    eb = group * EXPERTS_PER_GROUP + jnp.asarray(PAIR_SLOT_B, jnp.int32)[pair]
    return pos, ea, eb, n_used.reshape(1).astype(jnp.int32)


def _dispatch_kernel(pos_ref, src_ref, init_ref, out_ref, sem, *, tm, tiles_per_row):
    del init_ref
    base = (pl.program_id(0) * tiles_per_row + pl.program_id(1)) * tm

    def row_copy(r):
        return pltpu.make_async_copy(src_ref.at[0, r], out_ref.at[pos_ref[base + r]], sem)

    def start(g, carry):
        first = pl.multiple_of(g * DMA_UNROLL, DMA_UNROLL)
        for k in range(DMA_UNROLL):
            row_copy(first + k).start(priority=k % 2)
        return carry

    def wait(r, carry):
        row_copy(r).wait()
        return carry

    lax.fori_loop(0, tm // DMA_UNROLL, start, 0)
    lax.fori_loop(0, tm, wait, 0, unroll=DMA_UNROLL)


def _dispatch(pos, rows, init, tm):
    b, s, rt, rl = rows.shape
    return pl.pallas_call(
        functools.partial(_dispatch_kernel, tm=tm, tiles_per_row=s // tm),
        grid_spec=pltpu.PrefetchScalarGridSpec(
            num_scalar_prefetch=1,
            grid=(b, s // tm),
            in_specs=[pl.BlockSpec((1, tm, rt, rl), lambda i, j, pos_ref: (i, j, 0, 0)),
                      pl.BlockSpec(memory_space=pl.ANY)],
            out_specs=pl.BlockSpec(memory_space=pl.ANY),
            scratch_shapes=[pltpu.SemaphoreType.DMA(())],
        ),
        out_shape=jax.ShapeDtypeStruct(init.shape, F32),
        input_output_aliases={2: 0},
        compiler_params=_cparams(("arbitrary", "arbitrary")),
        name="moe_dispatch",
    )(pos, rows, init)


def _expert_kernel(ea_ref, eb_ref, nused_ref, hs_ref, wga_ref, wua_ref, wda_ref, wgb_ref, wub_ref, wdb_ref,
                   wr_ref, br_ref, o_ref, wg_scr, wu_scr, wd_scr):
    t = pl.program_id(0)
    prev = jnp.maximum(t - 1, 0)
    @pl.when((t == 0) | (ea_ref[t] != ea_ref[prev]))
    def _():
        wg_scr[0] = wga_ref[0, 0].astype(BF16)
        wu_scr[0] = wua_ref[0, 0].astype(BF16)
        wd_scr[0] = wda_ref[0, 0].astype(BF16)

    @pl.when((t == 0) | (eb_ref[t] != eb_ref[prev]))
    def _():
        wg_scr[1] = wgb_ref[0, 0].astype(BF16)
        wu_scr[1] = wub_ref[0, 0].astype(BF16)
        wd_scr[1] = wdb_ref[0, 0].astype(BF16)

    @pl.when(t < nused_ref[0])
    def _():
        d = wg_scr.shape[1]
        h32 = hs_ref[...].reshape(hs_ref.shape[0], d)
        h = h32.astype(BF16)
        logits_t = _dot_nt(wr_ref[0], h) + br_ref[0]
        sub = lax.broadcasted_iota(jnp.int32, logits_t.shape, 0)
        pick = lambda r: jnp.sum(jnp.where(sub == r, logits_t, 0.0), axis=0, keepdims=True)
        ea, eb = ea_ref[t], eb_ref[t]
        l_a, l_b = pick(N_GROUPS + ea), pick(N_GROUPS + eb)
        l_g = pick(lax.shift_right_logical(ea, EXPERTS_PER_GROUP.bit_length() - 1))
        g_sum = functools.reduce(lambda x, z: x + z, [jnp.exp(logits_t[i : i + 1, :] - l_g) for i in range(N_GROUPS)])
        m = jnp.maximum(l_a, l_b)
        t_a, t_b = jnp.exp(l_a - m), jnp.exp(l_b - m)
        inv = 1.0 / (g_sum * (t_a + t_b))
        w_cols = jnp.where(sub == 0, t_a * inv, jnp.where(sub == 1, t_b * inv, 0.0)).T
        y = None
        for e in range(2):
            act = (_silu(_dot(h, wg_scr[e])) * _dot(h, wu_scr[e])).astype(BF16)
            ye = w_cols[:, e : e + 1] * _dot(act, wd_scr[e])
            y = ye if y is None else y + ye
        o_ref[...] = y.reshape(o_ref.shape)

    @pl.when(t >= nused_ref[0])
    def _():
        o_ref[...] = jnp.zeros_like(o_ref)


def _experts(ea, eb, n_used, hs, wg, wu, wd, router, l, te):
    n_sorted, rt, rl = hs.shape
    _, _, d, eh = wg.shape

    def wspec(shape, which):
        return pl.BlockSpec((1, 1) + shape, lambda t, ea_ref, eb_ref, n_ref: (l, (ea_ref, eb_ref)[which][t], 0, 0))

    return pl.pallas_call(
        _expert_kernel,
        grid_spec=pltpu.PrefetchScalarGridSpec(
            num_scalar_prefetch=3,
            grid=(n_sorted // te,),
            in_specs=[pl.BlockSpec((te, rt, rl), lambda t, *_: (t, 0, 0)),
                      wspec((d, eh), 0), wspec((d, eh), 0), wspec((eh, d), 0),
                      wspec((d, eh), 1), wspec((d, eh), 1), wspec((eh, d), 1)]
                     + [_layer_spec(a.shape[1:], l) for a in router],
            out_specs=pl.BlockSpec((te, d // LANES, LANES), lambda t, *_: (t, 0, 0)),
            scratch_shapes=[pltpu.VMEM((2, d, eh), BF16), pltpu.VMEM((2, d, eh), BF16), pltpu.VMEM((2, eh, d), BF16)],
        ),
        out_shape=jax.ShapeDtypeStruct((n_sorted, d // LANES, LANES), F32),
        compiler_params=_cparams(("arbitrary",)),
        name="moe_experts",
    )(ea, eb, n_used, hs, wg, wu, wd, wg, wu, wd, *router)


def _gather_expert_rows(pos_ref, ys_ref, buf, sem, *, tm, tiles_per_row, first_tile):
    cols = pl.num_programs(1)
    n_steps = pl.num_programs(0) * cols
    step = pl.program_id(0) * cols + pl.program_id(1)
    slot = step % 2

    def row_copy(step_, slot_, r):
        tile = (step_ // cols) * tiles_per_row + first_tile + step_ % cols
        return pltpu.make_async_copy(ys_ref.at[pos_ref[tile * tm + r]], buf.at[slot_, r], sem.at[slot_])

    def start_tile(step_, slot_):
        def body(g, carry):
            first = pl.multiple_of(g * DMA_UNROLL, DMA_UNROLL)
            for k in range(DMA_UNROLL):
                row_copy(step_, slot_, first + k).start(priority=k % 2)
            return carry

        lax.fori_loop(0, tm // DMA_UNROLL, body, 0)

    @pl.when(step == 0)
    def _():
        start_tile(0, 0)

    @pl.when(step + 1 < n_steps)
    def _():
        start_tile(step + 1, 1 - slot)

    def wait(r, carry):
        row_copy(step, slot, r).wait()
        return carry

    lax.fori_loop(0, tm, wait, 0, unroll=DMA_UNROLL)
    return buf[slot].reshape(tm, buf.shape[2] * buf.shape[3])


def _combine_kernel(pos_ref, ys_ref, x1_ref, m_ref, lng_ref, lnb_ref, o_ref, buf, sem,
                    *, tm, tiles_per_row, first_tile, alpha):
    y = _gather_expert_rows(pos_ref, ys_ref, buf, sem, tm=tm, tiles_per_row=tiles_per_row, first_tile=first_tile)
    m = m_ref[0, 0, 0]
    o_ref[0] = _layer_norm(alpha * x1_ref[0] + m[5:6, :] * y, lng_ref[0], lnb_ref[0])


def _combine(pos, ys, x1, modtab, ln_g, ln_b, l, tm, nct, alpha, first_tile):
    b, s, d = x1.shape
    cols = s // tm - first_tile
    mod_spec = pl.BlockSpec((1, 1, 1, 6, d),
                            lambda i, j, pos_ref: (l, i, jnp.minimum((j + first_tile) // nct, 1), 0, 0))
    return pl.pallas_call(
        functools.partial(_combine_kernel, tm=tm, tiles_per_row=s // tm, first_tile=first_tile, alpha=alpha),
        grid_spec=pltpu.PrefetchScalarGridSpec(
            num_scalar_prefetch=1,
            grid=(b, cols),
            in_specs=[pl.BlockSpec(memory_space=pl.ANY),
                      pl.BlockSpec((1, tm, d), lambda i, j, pos_ref: (i, j + first_tile, 0)),
                      mod_spec, _layer_spec((1, d), l), _layer_spec((1, d), l)],
            out_specs=pl.BlockSpec((1, tm, d), lambda i, j, pos_ref: (i, j, 0)),
            scratch_shapes=[pltpu.VMEM((2, tm, d // LANES, LANES), F32), pltpu.SemaphoreType.DMA((2,))],
        ),
        out_shape=jax.ShapeDtypeStruct((b, cols * tm, d), F32),
        compiler_params=_cparams(("arbitrary", "arbitrary")),
        name="moe_combine_ln2",
    )(pos, ys, x1, modtab, ln_g, ln_b)


def _block_avg(width, group):
    idx = np.arange(width) // group
    return jnp.asarray((idx[:, None] == idx[None, :]).astype(np.float32) / group, dtype=BF16)


def kernel(x, c, ctx, c_ctx, w_mod, b_mod, w_in, hg_lb_raw, hg_norm_g, ret_decay_raw, ret_gn_g, ret_gn_b, s5_lam_re, s5_lam_im, s5_log_dt, s5_b_re, s5_b_im, s5_c_re, s5_c_im, s5_d, s5_glu_w, s5_glu_b, w_out, ln1_g, ln1_b, ln2_g, ln2_b, rg_w, rg_b, re_w, re_b, exp_w_gate, exp_w_up, exp_w_down):
    bsz, t_lat, d = x.shape
    t_ctx = ctx.shape[1]
    depth = w_mod.shape[0]
    alpha = (2.0 * depth) ** 0.25
    tm = TOKEN_TILE
    assert d == D_MODEL and t_lat % GRID_W == 0 and bsz < SUBLANES
    assert t_ctx % tm == 0 and t_lat % tm == 0, "context and latent lengths must be multiples of the token tile"
    nct = t_ctx // tm
    s = t_ctx + t_lat

    cvec = jnp.concatenate([c, c_ctx[None, :], jnp.zeros((SUBLANES - bsz - 1, d), F32)], 0)
    mod_all = _modulation(cvec, w_mod, b_mod)
    lat = mod_all[:, :bsz].reshape(depth, bsz, 6, d)
    cm = jnp.broadcast_to(mod_all[:, bsz].reshape(depth, 1, 6, d), (depth, bsz, 6, d))
    modtab = jnp.stack([cm, lat], axis=2)

    hg_lb = jnp.cumsum(jax.nn.softmax(hg_lb_raw.astype(F32), axis=0), axis=0)
    hg_lb = hg_lb - hg_lb[:1]
    ret_tables = _retention_decay_tables(jax.nn.log_sigmoid(ret_decay_raw.astype(F32)))
    cos_tab, sin_tab = _rope_tables(t_lat, t_ctx)
    s5_tabs = _s5_weights(s5_lam_re, s5_lam_im, s5_log_dt, s5_b_re, s5_b_im, s5_c_re, s5_c_im)
    pad_r = LANES - N_GROUPS - N_EXPERTS
    wr = jnp.concatenate([rg_w, re_w.reshape(depth, d, N_EXPERTS), jnp.zeros((depth, d, pad_r), F32)], axis=2)
    wr = jnp.swapaxes(wr, 1, 2)
    wr_hi, wr_lo = _split_bf16(wr)
    br = jnp.concatenate([rg_b, re_b.reshape(depth, N_EXPERTS), jnp.zeros((depth, pad_r), F32)], axis=1)[:, :, None]
    row = lambda a: a[:, None, :]
    layer_prm = [row(jnp.tile(hg_norm_g, (1, HG_HEADS))), row(ret_gn_g), row(ret_gn_b), row(s5_d),
                 s5_glu_w.astype(BF16), row(s5_glu_b), w_out.astype(BF16), row(ln1_g), row(ln1_b),
                 wr_hi, wr_lo, br]
    tri = jnp.asarray(np.triu(np.ones((tm, tm), np.float32), 1), dtype=BF16)
    const_prm = [_block_avg(HG_WIDTH, HG_DK), _block_avg(RET_WIDTH, RET_DK), tri]
    w_in_bf16 = w_in.astype(BF16)
    ln2_g, ln2_b = row(ln2_g), row(ln2_b)
    te = EXPERT_TILE
    n_tiles = -(-(bsz * s + N_BUCKETS * (te - 1)) // te)

    hs = jnp.zeros((n_tiles * te, d // LANES, LANES), F32)
    xs = jnp.concatenate([ctx, x], axis=1)
    for l in range(depth):
        if l == 0:
            p, ug = _inproj(xs, modtab, w_in_bf16, l, tm, nct)
        else:
            xs, p, ug = _ln2_inproj(pos, ys, x1, modtab, ln2_g, ln2_b, w_in_bf16, l, tm, nct, alpha)
        o_hf, o_hb = _hgrn(p, hg_lb, l, t_ctx // CHUNK)
        o_rf, o_rb = _retention(p, cos_tab, sin_tab, ret_tables, l, t_ctx // RET_CHUNK)
        yg = _s5_conv(ug, *s5_tabs, l, bsz, t_ctx // S5_CHUNK)
        x1, rows, route, counts = _merge(p, o_hf, o_hb, o_rf, o_rb, yg, xs, modtab, layer_prm, const_prm,
                                         l, tm, nct, alpha)
        pos, ea, eb, n_used = _routing_tables(route, counts, te, n_tiles)
        hs = _dispatch(pos, rows, hs, tm)
        ys = _experts(ea, eb, n_used, hs, exp_w_gate, exp_w_up, exp_w_down, (wr_hi, br), l, te)
    return _combine(pos, ys, x1, modtab, ln2_g, ln2_b, depth - 1, tm, nct, alpha, nct)
```

```python
import functools

import numpy as np
import jax
import jax.numpy as jnp
from jax import lax
from jax.experimental import pallas as pl
from jax.experimental.pallas import tpu as pltpu

F32 = jnp.float32
BF16 = jnp.bfloat16

D_MODEL = 1024
HG_WIDTH = 512
HG_HEADS = 4
HG_DK = HG_WIDTH // HG_HEADS
RET_WIDTH = 256
RET_HEADS = 4
RET_DK = RET_WIDTH // RET_HEADS
S5_WIDTH = 256
S5_GROUP_CH = 16
S5_GROUPS = S5_WIDTH // S5_GROUP_CH
S5_STATE = 64
IN_COLS = 5 * HG_WIDTH + 4 * RET_WIDTH + S5_WIDTH
CHUNK = 64
HGRN_BATCH = 4
RET_CHUNK = 256
RET_BATCH = 4
N_GROUPS = 4
EXPERTS_PER_GROUP = 4
N_EXPERTS = N_GROUPS * EXPERTS_PER_GROUP
N_PAIRS = EXPERTS_PER_GROUP * (EXPERTS_PER_GROUP - 1) // 2
N_BUCKETS = N_GROUPS * N_PAIRS
PAIR_SLOT_A = (0, 2, 2, 3, 3, 3)
PAIR_SLOT_B = (1, 1, 0, 0, 1, 2)
EXPERT_HIDDEN = D_MODEL // 2
LN_EPS = 1e-5
ROPE_BASE = 10000.0
GRID_W = 64

LANES = 128
SUBLANES = 8
TOKEN_TILE = 256
S5_CHUNK = 32
S5_FOLD = S5_CHUNK * S5_GROUP_CH
EXPERT_TILE = 256
DMA_UNROLL = 8
VMEM_LIMIT = 56 * 1024 * 1024


def _cparams(sem):
    return pltpu.CompilerParams(dimension_semantics=sem, vmem_limit_bytes=VMEM_LIMIT)


def _split_bf16(x):
    hi = x.astype(BF16)
    lo = (x - hi.astype(F32)).astype(BF16)
    return hi, lo


def _dot(a, b):
    return jnp.dot(a, b, preferred_element_type=F32)


def _dot3(a, b):
    ah, al = _split_bf16(a)
    bh, bl = _split_bf16(b)
    return _dot(ah, bh) + _dot(ah, bl) + _dot(al, bh)


def _dot2(a, b_bf16):
    ah, al = _split_bf16(a)
    return _dot(ah, b_bf16) + _dot(al, b_bf16)


def _dot_nt(a, b):
    return lax.dot_general(a, b, (((1,), (1,)), ((), ())), preferred_element_type=F32)


def _dot_tn(a, b):
    return lax.dot_general(a, b, (((0,), (0,)), ((), ())), preferred_element_type=F32)


def _silu(x):
    return x * jax.nn.sigmoid(x)


def _layer_spec(shape, l):
    zeros = (0,) * len(shape)
    return pl.BlockSpec((1,) + tuple(shape), lambda *_: (l,) + zeros)


def _mod_spec(l, nct):
    return pl.BlockSpec((1, 1, 1, 6, D_MODEL), lambda i, j, *_: (l, i, jnp.minimum(j // nct, 1), 0, 0))


def _swap_sublane_lanegroup(v):
    n = v.shape[0]
    r = lax.broadcasted_iota(jnp.int32, v.shape, 0)
    l = lax.broadcasted_iota(jnp.int32, v.shape, 1)
    for k in range(3):
        rb = (r >> k) & 1
        gb = (l >> (4 + k)) & 1
        sh = S5_GROUP_CH << k
        st = 1 << k
        a = pltpu.roll(pltpu.roll(v, LANES - sh, 1), st, 0)
        b = pltpu.roll(pltpu.roll(v, sh, 1), n - st, 0)
        v = jnp.where(rb == gb, v, jnp.where(rb == 1, a, b))
    return v


def _mod_kernel(c_ref, w_ref, b_ref, o_ref):
    sc = _silu(c_ref[...])
    o_ref[0] = _dot3(sc, w_ref[0]) + b_ref[0]


def _modulation(cvec, w_mod, b_mod):
    depth, d, n = w_mod.shape
    rows = cvec.shape[0]
    tn = 1536
    return pl.pallas_call(
        _mod_kernel,
        grid=(depth, n // tn),
        in_specs=[
            pl.BlockSpec((rows, d), lambda l, j: (0, 0)),
            pl.BlockSpec((1, d, tn), lambda l, j: (l, 0, j)),
            pl.BlockSpec((1, 1, tn), lambda l, j: (l, 0, j)),
        ],
        out_specs=pl.BlockSpec((1, rows, tn), lambda l, j: (l, 0, j)),
        out_shape=jax.ShapeDtypeStruct((depth, rows, n), F32),
        compiler_params=_cparams(("arbitrary", "arbitrary")),
        name="modulation",
    )(cvec, w_mod, b_mod.reshape(depth, 1, n))


def _inproj_kernel(x_ref, m_ref, w_ref, o_ref, ug_ref, wt_scr):
    _project(x_ref[0], m_ref, w_ref, o_ref, ug_ref, wt_scr)


def _ln2_inproj_kernel(pos_ref, ys_ref, x1_ref, mprev_ref, lng_ref, lnb_ref, m_ref, w_ref,
                       x2_ref, o_ref, ug_ref, buf, sem, wt_scr, *, tm, tiles_per_row, alpha):
    y = _gather_expert_rows(pos_ref, ys_ref, buf, sem, tm=tm, tiles_per_row=tiles_per_row, first_tile=0)
    g2 = mprev_ref[0, 0, 0][5:6, :]
    x2 = _layer_norm(alpha * x1_ref[0] + g2 * y, lng_ref[0], lnb_ref[0])
    x2_ref[0] = x2
    _project(x2, m_ref, w_ref, o_ref, ug_ref, wt_scr)


def _project(x, m_ref, w_ref, o_ref, ug_ref, wt_scr):
    m = m_ref[0, 0, 0]
    h = x * (1.0 + m[1:2, :]) + m[0:1, :]
    p = _dot(h.astype(BF16), w_ref[0])
    o_ref[0] = p
    rows_per_chunk = S5_CHUNK // SUBLANES
    n_chunks = p.shape[0] // S5_CHUNK
    for half in range(S5_WIDTH // LANES):
        lo = IN_COLS - S5_WIDTH + half * LANES
        wt_scr[...] = _swap_sublane_lanegroup(p[:, lo : lo + LANES])
        for g_lo in range(SUBLANES):
            for s_hi in range(rows_per_chunk):
                piece = wt_scr[pl.ds(s_hi * SUBLANES + g_lo, n_chunks, stride=S5_CHUNK), :]
                ug_ref[half * SUBLANES + g_lo, :, s_hi * LANES : (s_hi + 1) * LANES] = piece


def _inproj(x, modtab, w_in_bf16, l, tm, nct):
    b, s, d = x.shape
    n = w_in_bf16.shape[-1]
    cpt = tm // S5_CHUNK
    return pl.pallas_call(
        _inproj_kernel,
        grid=(b, s // tm),
        in_specs=[
            pl.BlockSpec((1, tm, d), lambda i, j: (i, j, 0)),
            _mod_spec(l, nct),
            _layer_spec((d, n), l),
        ],
        out_specs=[pl.BlockSpec((1, tm, n), lambda i, j: (i, j, 0)),
                   pl.BlockSpec((S5_GROUPS, cpt, S5_FOLD), lambda i, j: (0, j, i))],
        out_shape=[jax.ShapeDtypeStruct((b, s, n), F32),
                   jax.ShapeDtypeStruct((S5_GROUPS, s // S5_CHUNK, b * S5_FOLD), F32)],
        scratch_shapes=[pltpu.VMEM((tm, LANES), F32)],
        compiler_params=_cparams(("arbitrary", "arbitrary")),
        name="inproj",
    )(x, modtab, w_in_bf16)


def _ln2_inproj(pos, ys, x1, modtab, ln_g, ln_b, w_in_bf16, l, tm, nct, alpha):
    b, s, d = x1.shape
    n = w_in_bf16.shape[-1]
    cpt = tm // S5_CHUNK
    tok = lambda width: pl.BlockSpec((1, tm, width), lambda i, j, pos_ref: (i, j, 0))
    return pl.pallas_call(
        functools.partial(_ln2_inproj_kernel, tm=tm, tiles_per_row=s // tm, alpha=alpha),
        grid_spec=pltpu.PrefetchScalarGridSpec(
            num_scalar_prefetch=1,
            grid=(b, s // tm),
            in_specs=[pl.BlockSpec(memory_space=pl.ANY), tok(d),
                      _mod_spec(l - 1, nct), _layer_spec((1, d), l - 1), _layer_spec((1, d), l - 1),
                      _mod_spec(l, nct), _layer_spec((d, n), l)],
            out_specs=[tok(d), tok(n),
                       pl.BlockSpec((S5_GROUPS, cpt, S5_FOLD), lambda i, j, pos_ref: (0, j, i))],
            scratch_shapes=[pltpu.VMEM((2, tm, d // LANES, LANES), F32), pltpu.SemaphoreType.DMA((2,)),
                            pltpu.VMEM((tm, LANES), F32)],
        ),
        out_shape=[jax.ShapeDtypeStruct((b, s, d), F32), jax.ShapeDtypeStruct((b, s, n), F32),
                   jax.ShapeDtypeStruct((S5_GROUPS, s // S5_CHUNK, b * S5_FOLD), F32)],
        compiler_params=_cparams(("arbitrary", "arbitrary")),
        name="ln2_inproj",
    )(pos, ys, x1, modtab, ln_g, ln_b, modtab, w_in_bf16)


def _block_gate_products(f, reverse):
    c = f.shape[0]
    row = lax.broadcasted_iota(jnp.int32, (c, 1), 0)
    a, z, b = f, f, jnp.ones_like(f)
    out = []
    s = 1
    while s < c:
        out.append((a, b))
        if s < SUBLANES:
            z3 = z.reshape(c // SUBLANES, SUBLANES, z.shape[1])
            up = pltpu.roll(z3, s, 1).reshape(z.shape)
            dn = pltpu.roll(z3, SUBLANES - s, 1).reshape(z.shape)
            odd = (row & s) != 0
            if reverse:
                a = a * jnp.where(odd, 1.0, dn)
                b = b * jnp.where(odd, up, 1.0)
            else:
                a = a * jnp.where(odd, up, 1.0)
                b = b * jnp.where(odd, 1.0, dn)
            z = z * jnp.where(odd, up, dn)
        else:
            na, nb, nz = [], [], []
            for lo in range(0, c, 2 * s):
                ev, od = slice(lo, lo + s), slice(lo + s, lo + 2 * s)
                zz = z[ev] * z[od]
                if reverse:
                    na += [a[ev] * z[od], a[od]]
                    nb += [b[ev], b[od] * z[ev]]
                else:
                    na += [a[ev], a[od] * z[ev]]
                    nb += [b[ev] * z[od], b[od]]
                nz += [zz, zz]
            a, b, z = (jnp.concatenate(t, axis=0) for t in (na, nb, nz))
        s *= 2
    out.append((a, b))
    return out, z


def _hgrn_kernel(qf_ref, vf_ref, zf_ref, qb_ref, vb_ref, zb_ref, lb_ref, of_ref, ob_ref, st_ref):
    c = CHUNK

    @pl.when(pl.program_id(1) == 0)
    def _():
        st_ref[...] = jnp.zeros_like(st_ref)

    ri = lax.broadcasted_iota(jnp.int32, (c, c), 0)
    ci = lax.broadcasted_iota(jnp.int32, (c, c), 1)
    for d, (q_ref, v_ref, z_ref, o_ref) in enumerate(
        ((qf_ref, vf_ref, zf_ref, of_ref), (qb_ref, vb_ref, zb_ref, ob_ref))
    ):
        reverse = d == 1
        causal = (ri < ci) if reverse else (ri > ci)
        n_levels = c.bit_length() - 1
        masks = [(((ri >> lvl) ^ (ci >> lvl)) == 1) & causal for lvl in range(n_levels)]
        for bi, h in [(bi, h) for bi in range(q_ref.shape[0]) for h in range(HG_HEADS)]:
            hs = slice(h * HG_DK, (h + 1) * HG_DK)
            q = _silu(q_ref[bi, :, hs])
            v = v_ref[bi, :, hs].astype(BF16)
            lb = lb_ref[0, d : d + 1, hs]
            f = lb + (1.0 - lb) * jax.nn.sigmoid(z_ref[bi, :, hs])
            k = 1.0 - f
            levels, tot = _block_gate_products(f, reverse)
            scores = jnp.where(ri == ci, _dot_nt(q.astype(BF16), k.astype(BF16)), 0.0)
            for mask, (a, bb) in zip(masks, levels):
                scores = jnp.where(mask, _dot_nt((q * a).astype(BF16), (k * bb).astype(BF16)), scores)
            a_full, b_full = levels[-1]
            st = st_ref[bi, d, h]
            o_ref[bi, :, hs] = (_dot(scores.astype(BF16), v)
                                + _dot_nt((q * a_full).astype(BF16), st.astype(BF16)))
            st_ref[bi, d, h] = st * tot[0:1, :] + _dot_tn(v, (k * b_full).astype(BF16))


def _bwd_chunk(n, nc_ctx, nc_all):
    return jnp.where(n < nc_ctx, nc_ctx - 1 - n, nc_all + nc_ctx - 1 - n)


def _hgrn(p, lb_all, l, nc_ctx):
    b, s, _ = p.shape
    nc = s // CHUNK
    w = HG_WIDTH

    bb = HGRN_BATCH if b % HGRN_BATCH == 0 else 1

    def fwd(col):
        return pl.BlockSpec((bb, CHUNK, w), lambda i, n: (i, n, col))

    def bwd(col):
        return pl.BlockSpec((bb, CHUNK, w), lambda i, n: (i, _bwd_chunk(n, nc_ctx, nc), col))

    out = jax.ShapeDtypeStruct((b, s, w), F32)
    return pl.pallas_call(
        _hgrn_kernel,
        grid=(b // bb, nc),
        in_specs=[fwd(0), fwd(1), fwd(3), bwd(0), bwd(1), bwd(4), _layer_spec((2, w), l)],
        out_specs=[fwd(0), bwd(0)],
        out_shape=[out, out],
        scratch_shapes=[pltpu.VMEM((bb, 2, HG_HEADS, HG_DK, HG_DK), F32)],
        compiler_params=_cparams(("arbitrary", "arbitrary")),
        name="hgrn2_scan",
    )(p, p, p, p, p, p, lb_all)


def _swap_halves(x, half):
    n = x.shape[-1]
    lane = lax.broadcasted_iota(jnp.int32, (1, n), 1)
    lower = (lane & half) == 0
    return jnp.where(lower, pltpu.roll(x, n - half, 1), pltpu.roll(x, half, 1))


def _ret_kernel(qf_ref, kf_ref, vf_ref, cf_ref, sf_ref, qb_ref, kb_ref, vb_ref, cb_ref, sb_ref,
                dmat_ref, rq_ref, rk_ref, cd_ref, of_ref, ob_ref, st_ref):
    @pl.when(pl.program_id(1) == 0)
    def _():
        st_ref[...] = jnp.zeros_like(st_ref)

    half = RET_DK // 4
    for d, (q_ref, k_ref, v_ref, c_ref, s_ref, o_ref) in enumerate(
        ((qf_ref, kf_ref, vf_ref, cf_ref, sf_ref, of_ref), (qb_ref, kb_ref, vb_ref, cb_ref, sb_ref, ob_ref))
    ):
        cos = c_ref[...]
        sin = s_ref[...]
        cd = cd_ref[0, d : d + 1, :]
        lane = lax.broadcasted_iota(jnp.int32, (1, LANES), 1)
        left = lane < RET_DK
        same_head = (lax.broadcasted_iota(jnp.int32, (LANES, 1), 0) < RET_DK) == left
        zero = jnp.zeros((), BF16)
        for bi in range(q_ref.shape[0]):
            q = q_ref[bi]
            k = k_ref[bi] * (RET_DK ** -0.5)
            q = q * cos + _swap_halves(q, half) * sin
            k = k * cos + _swap_halves(k, half) * sin
            v = v_ref[bi]
            vt = v.T.astype(BF16)
            v = v.astype(BF16)
            q0 = q.astype(BF16)
            k0 = k.astype(BF16)
            qd = (q * rq_ref[0, d]).astype(BF16)
            kd = (k * rk_ref[0, d]).astype(BF16)
            for p in range(RET_HEADS // 2):
                ps = slice(p * LANES, (p + 1) * LANES)
                kp, vp = k0[:, ps], v[:, ps]
                k_blk = jnp.concatenate([jnp.where(left, kp, zero), jnp.where(left, zero, kp)], axis=0)
                v_blk = jnp.concatenate([jnp.where(left, vp, zero), jnp.where(left, zero, vp)], axis=0)
                scores = _dot_nt(q0[:, ps], k_blk) * dmat_ref[0, d, p]
                st = st_ref[bi, d, p]
                o_ref[bi, :, ps] = _dot(scores.astype(BF16), v_blk) + _dot_nt(qd[:, ps], st.astype(BF16))
                st_ref[bi, d, p] = st * cd[:, ps] + jnp.where(same_head, _dot(vt[ps, :], kd[:, ps]), 0.0)


def _retention(p, cos_tab, sin_tab, tables, l, nc_ctx):
    b, s, _ = p.shape
    nc = s // RET_CHUNK
    w = RET_WIDTH
    base = 5 * HG_WIDTH // w

    bb = RET_BATCH if b % RET_BATCH == 0 else 1

    def fwd(col):
        return pl.BlockSpec((bb, RET_CHUNK, w), lambda i, n: (i, n, col))

    def bwd(col):
        return pl.BlockSpec((bb, RET_CHUNK, w), lambda i, n: (i, _bwd_chunk(n, nc_ctx, nc), col))

    tab_f = pl.BlockSpec((RET_CHUNK, w), lambda i, n: (n, 0))
    tab_b = pl.BlockSpec((RET_CHUNK, w), lambda i, n: (_bwd_chunk(n, nc_ctx, nc), 0))
    out = jax.ShapeDtypeStruct((b, s, w), F32)
    return pl.pallas_call(
        _ret_kernel,
        grid=(b // bb, nc),
        in_specs=[fwd(base), fwd(base + 1), fwd(base + 2), tab_f, tab_f,
                  bwd(base), bwd(base + 1), bwd(base + 2), tab_b, tab_b]
                 + [_layer_spec(t.shape[1:], l) for t in tables],
        out_specs=[fwd(0), bwd(0)],
        out_shape=[out, out],
        scratch_shapes=[pltpu.VMEM((bb, 2, RET_HEADS // 2, LANES, LANES), F32)],
        compiler_params=_cparams(("arbitrary", "arbitrary")),
        name="retention_scan",
    )(p, p, p, cos_tab, sin_tab, p, p, p, cos_tab, sin_tab, *tables)


def _rope_tables(t_lat, t_ctx):
    m = RET_DK // 4
    inv = ROPE_BASE ** (-jnp.arange(m, dtype=F32) / m)
    rows = jnp.repeat(jnp.arange(t_lat // GRID_W, dtype=jnp.int32), GRID_W).astype(F32)
    cols = jnp.tile(jnp.arange(GRID_W, dtype=jnp.int32), t_lat // GRID_W).astype(F32)

    def half_tables(pos):
        ang = pos[:, None] * inv
        c, s = jnp.cos(ang), jnp.sin(ang)
        return jnp.concatenate([c, c], -1), jnp.concatenate([-s, s], -1)

    cr, sr = half_tables(rows)
    cc, sc = half_tables(cols)
    cos_h = jnp.concatenate([cr, cc], -1)
    sin_h = jnp.concatenate([sr, sc], -1)
    cos = jnp.tile(cos_h, (1, RET_HEADS))
    sin = jnp.tile(sin_h, (1, RET_HEADS))
    cos = jnp.concatenate([jnp.ones((t_ctx, RET_WIDTH), F32), cos], 0)
    sin = jnp.concatenate([jnp.zeros((t_ctx, RET_WIDTH), F32), sin], 0)
    return cos, sin


def _retention_decay_tables(log_gamma):
    c = RET_CHUNK
    i = jnp.arange(c, dtype=F32)
    diff = i[:, None] - i[None, :]
    lg = log_gamma[:, :, :, None, None]
    d_f = jnp.where(diff >= 0, jnp.exp(lg[:, 0] * diff), 0.0)
    d_b = jnp.where(diff <= 0, jnp.exp(lg[:, 1] * (-diff)), 0.0)
    dmat = jnp.stack([d_f, d_b], 1)
    dmat = jnp.concatenate([dmat[:, :, 0::2], dmat[:, :, 1::2]], axis=-1)
    lane_lg = jnp.repeat(log_gamma, RET_DK, axis=2)[:, :, None, :]
    col = i[None, :, None]
    rq = jnp.stack([jnp.exp(lane_lg[:, 0] * (col + 1.0)), jnp.exp(lane_lg[:, 1] * (c - col))], 1)
    rk = jnp.stack([jnp.exp(lane_lg[:, 0] * (c - 1.0 - col)), jnp.exp(lane_lg[:, 1] * col)], 1)
    cdec = jnp.exp(lane_lg[:, :, 0, :] * c)
    return dmat, rq, rk, cdec


def _s5_kernel(u_ref, base_ref, inj_ref, w2_ref, ac_ref, y_ref, toep_scr, a_scr, i_scr, hf_scr, hb_scr,
               *, n_ctx, n_all, bsz):
    width = S5_FOLD
    st2 = 2 * S5_STATE
    ch = S5_GROUP_CH
    lane = lax.broadcasted_iota(jnp.int32, (ch, width), 1)
    base_f = base_ref[0, 0, 0]
    base_b = base_ref[0, 0, 1]
    for s in range(S5_CHUNK):
        sh_f = s * ch
        sh_b = (S5_CHUNK - 1 - s) * ch
        part_f = jnp.where(lane >= sh_f, pltpu.roll(base_f, sh_f, 1), 0.0) if sh_f else base_f
        part_b = jnp.where(lane < width - sh_b, pltpu.roll(base_b, width - sh_b, 1), 0.0) if sh_b else base_b
        toep_scr[s * ch : (s + 1) * ch, :] = (part_f + part_b).astype(BF16)
    for b in range(bsz):
        ub = u_ref[0, :, b * width : (b + 1) * width].astype(BF16)
        a_scr[b] = _dot(ub, toep_scr[...])
        i_scr[b] = _dot(ub, inj_ref[0, 0])
    ac = ac_ref[0, 0]

    zero = jnp.zeros((1, st2), F32)
    state = [(zero, zero)] * (2 * bsz)
    for n in range(n_all):
        nb = n_ctx - 1 - n if n < n_ctx else n_all + n_ctx - 1 - n
        for b in range(bsz):
            for d, (row, scr) in enumerate(((n, hf_scr), (nb, hb_scr))):
                h, hsw = state[2 * b + d]
                scr[b, row : row + 1, :] = h
                inj = i_scr[b, row : row + 1, d * st2 : (d + 1) * st2]
                inj_sw = i_scr[b, row : row + 1, (d + 2) * st2 : (d + 3) * st2]
                ar, ai = ac[2 * d : 2 * d + 1, :], ac[2 * d + 1 : 2 * d + 2, :]
                state[2 * b + d] = (h * ar + hsw * ai + inj, hsw * ar - h * ai + inj_sw)
    for b in range(bsz):
        y_ref[0, :, b * width : (b + 1) * width] = (
            a_scr[b]
            + _dot(hf_scr[b].astype(BF16), w2_ref[0, 0, 0])
            + _dot(hb_scr[b].astype(BF16), w2_ref[0, 0, 1]))


def _s5_conv(ug, base, inj, w2, ac, l, bsz, n_ctx):
    g, n_all, _ = ug.shape
    width = S5_FOLD
    st2 = 2 * S5_STATE
    return pl.pallas_call(
        functools.partial(_s5_kernel, n_ctx=n_ctx, n_all=n_all, bsz=bsz),
        grid=(g,),
        in_specs=[
            pl.BlockSpec((1, n_all, bsz * width), lambda i: (i, 0, 0)),
            pl.BlockSpec((1, 1, 2, S5_GROUP_CH, width), lambda i: (l, i, 0, 0, 0)),
            pl.BlockSpec((1, 1, width, 4 * st2), lambda i: (l, i, 0, 0)),
            pl.BlockSpec((1, 1, 2, st2, width), lambda i: (l, i, 0, 0, 0)),
            pl.BlockSpec((1, 1, SUBLANES, st2), lambda i: (l, i, 0, 0)),
        ],
        out_specs=pl.BlockSpec((1, n_all, bsz * width), lambda i: (i, 0, 0)),
        out_shape=jax.ShapeDtypeStruct((g, n_all, bsz * width), F32),
        scratch_shapes=[pltpu.VMEM((width, width), BF16),
                        pltpu.VMEM((bsz, n_all, width), F32), pltpu.VMEM((bsz, n_all, 4 * st2), F32),
                        pltpu.VMEM((bsz, n_all, st2), F32), pltpu.VMEM((bsz, n_all, st2), F32)],
        compiler_params=_cparams(("arbitrary",)),
        name="s5_conv",
    )(ug, base, inj, w2, ac)


def _s5_weights(lam_re, lam_im, log_dt, b_re, b_im, c_re, c_im):
    cs, ch, p, g = S5_CHUNK, S5_GROUP_CH, S5_STATE, S5_GROUPS
    nl = lam_re.shape[0]
    lam_re = jnp.minimum(lam_re.astype(F32), -1e-4)
    lam_im = lam_im.astype(F32)
    dt = jnp.exp(log_dt.astype(F32))[..., None]
    mag = jnp.exp(dt * lam_re)
    abar_re, abar_im = mag * jnp.cos(dt * lam_im), mag * jnp.sin(dt * lam_im)
    den = jnp.square(lam_re) + jnp.square(lam_im)
    nr, ni = abar_re - 1.0, abar_im
    coef_re = ((nr * lam_re + ni * lam_im) / den)[..., None]
    coef_im = ((ni * lam_re - nr * lam_im) / den)[..., None]
    b_re, b_im = b_re.astype(F32), b_im.astype(F32)
    bb_re = coef_re * b_re - coef_im * b_im
    bb_im = coef_re * b_im + coef_im * b_re
    ct_re = jnp.swapaxes(c_re.astype(F32), -1, -2)
    ct_im = jnp.swapaxes(c_im.astype(F32), -1, -2)

    def powers(expo):
        e = expo.astype(F32)[None, :, None, None, :]
        m = jnp.exp(e * (dt * lam_re)[..., None])
        return m * jnp.cos(e * (dt * lam_im)[..., None]), m * jnp.sin(e * (dt * lam_im)[..., None])

    slots = jnp.arange(cs + 1)
    pr, pi = powers(jnp.stack([slots, cs - slots]))
    r_re = (pr[..., None] * ct_re[..., None, :] - pi[..., None] * ct_im[..., None, :]).reshape(nl, 2, g, p, (cs + 1) * ch)
    r_im = (pr[..., None] * ct_im[..., None, :] + pi[..., None] * ct_re[..., None, :]).reshape(nl, 2, g, p, (cs + 1) * ch)
    hi = lax.Precision.HIGHEST
    kern = (jnp.einsum('ldgpe,ldgpn->ldgen', bb_re, r_re, precision=hi)
            - jnp.einsum('ldgpe,ldgpn->ldgen', bb_im, r_im, precision=hi))
    w = cs * ch
    base = jnp.stack([kern[:, 0, :, :, :w], kern[:, 1, :, :, ch:]], axis=2)
    w2 = jnp.stack([jnp.concatenate([r_re[:, 0, :, :, ch:], -r_im[:, 0, :, :, ch:]], axis=2),
                    jnp.concatenate([r_re[:, 1, :, :, :w], -r_im[:, 1, :, :, :w]], axis=2)], axis=2)
    steps = jnp.arange(cs)
    qr, qi = powers(jnp.stack([cs - 1 - steps, steps]))
    qr = jnp.swapaxes(qr, -1, -2)[..., :, None, :]
    qi = jnp.swapaxes(qi, -1, -2)[..., :, None, :]
    bt_re = jnp.swapaxes(bb_re, -1, -2)[:, :, :, None]
    bt_im = jnp.swapaxes(bb_im, -1, -2)[:, :, :, None]
    cat = lambda a, b: jnp.concatenate([a, b], axis=-1)
    qr2, qi2 = cat(qr, qr), cat(qi, qi)
    ab = (qr2 * cat(bt_re, bt_im) + qi2 * cat(-bt_im, bt_re)).reshape(nl, 2, g, w, 2 * p)
    ab_sw = (qr2 * cat(bt_im, bt_re) + qi2 * cat(bt_re, -bt_im)).reshape(nl, 2, g, w, 2 * p)
    inj = jnp.concatenate([ab[:, 0], ab[:, 1], ab_sw[:, 0], ab_sw[:, 1]], axis=-1)
    ac_rows = []
    for d, slot in enumerate((cs, 0)):
        ar, ai = pr[:, d, :, :, slot], pi[:, d, :, :, slot]
        ac_rows += [jnp.concatenate([ar, ar], -1), jnp.concatenate([-ai, ai], -1)]
    ac = jnp.stack(ac_rows + [jnp.zeros_like(ac_rows[0])] * (SUBLANES - 4), axis=2)
    return base, inj.astype(BF16), w2.astype(BF16), ac


def _layer_norm(x, g, b):
    mu = jnp.mean(x, -1, keepdims=True)
    xc = x - mu
    var = jnp.mean(xc * xc, -1, keepdims=True)
    return xc * lax.rsqrt(var + LN_EPS) * g + b


def _route(logits_t):
    col = lambda i: logits_t[i : i + 1, :]
    gl = [col(i) for i in range(N_GROUPS)]
    gmax = functools.reduce(jnp.maximum, gl)
    g_idx = jnp.full_like(gmax, N_GROUPS - 1).astype(jnp.int32)
    for i in reversed(range(N_GROUPS - 1)):
        g_idx = jnp.where(gl[i] == gmax, i, g_idx)
    g_p = 1.0 / functools.reduce(lambda a, b: a + b, [jnp.exp(x - gmax) for x in gl])
    el = []
    for e in range(EXPERTS_PER_GROUP):
        v = col(N_GROUPS + (N_GROUPS - 1) * EXPERTS_PER_GROUP + e)
        for g in reversed(range(N_GROUPS - 1)):
            v = jnp.where(g_idx == g, col(N_GROUPS + g * EXPERTS_PER_GROUP + e), v)
        el.append(v)
    m1 = functools.reduce(jnp.maximum, el)
    i1 = jnp.full_like(g_idx, EXPERTS_PER_GROUP - 1)
    for e in reversed(range(EXPERTS_PER_GROUP - 1)):
        i1 = jnp.where(el[e] == m1, e, i1)
    rest = [jnp.where(i1 == e, -jnp.inf, el[e]) for e in range(EXPERTS_PER_GROUP)]
    m2 = functools.reduce(jnp.maximum, rest)
    i2 = jnp.full_like(g_idx, EXPERTS_PER_GROUP - 1)
    for e in reversed(range(EXPERTS_PER_GROUP - 1)):
        i2 = jnp.where((rest[e] == m2) & (i1 != e), e, i2)
    t = jnp.exp(m2 - m1)
    w1 = g_p / (1.0 + t)
    w2 = g_p * t / (1.0 + t)
    lo = jnp.minimum(i1, i2)
    hi = jnp.maximum(i1, i2)
    pair = jnp.where(hi == 1, 0, jnp.where(hi == 2, jnp.where(lo == 1, 1, 2), 3 + lo))
    bucket = g_idx * N_PAIRS + pair
    w_lower = jnp.where(i1 < i2, w1, w2)
    w_higher = jnp.where(i1 < i2, w2, w1)
    w_a = jnp.where(pair == 0, w_lower, w_higher)
    w_b = jnp.where(pair == 0, w_higher, w_lower)
    return bucket, w_a, w_b


def _merge_kernel(gate_ref, rg_ref, u_ref, hf_ref, hb_ref, rf_ref, rb_ref, yg_ref, x_ref, m_ref,
                  hgn_ref, gng_ref, gnb_ref, d_ref, gw_ref, gb_ref, wo_ref, lng_ref, lnb_ref,
                  wrh_ref, wrl_ref, br_ref, a128_ref, a64_ref, tri_ref, x1_ref, h2_ref, route_ref, cnt_ref, wt_scr,
                  *, alpha):
    @pl.when((pl.program_id(0) == 0) & (pl.program_id(1) == 0))
    def _():
        cnt_ref[...] = jnp.zeros_like(cnt_ref)

    o_hg = hf_ref[0] + hb_ref[0]
    ms = _dot2(o_hg * o_hg, a128_ref[...])
    hg = o_hg * lax.rsqrt(ms + LN_EPS) * hgn_ref[0] * _silu(gate_ref[0])
    o_rt = rf_ref[0] + rb_ref[0]
    mu = _dot2(o_rt, a64_ref[...])
    xc = o_rt - mu
    var = _dot2(xc * xc, a64_ref[...])
    rt = (xc * lax.rsqrt(var + LN_EPS) * gng_ref[0] + gnb_ref[0]) * _silu(rg_ref[0])
    rows_per_chunk = S5_CHUNK // SUBLANES
    n_chunks = yg_ref.shape[1]
    y5 = []
    for half in range(S5_WIDTH // LANES):
        for g_lo in range(SUBLANES):
            for s_hi in range(rows_per_chunk):
                wt_scr[pl.ds(s_hi * SUBLANES + g_lo, n_chunks, stride=S5_CHUNK), :] = (
                    yg_ref[half * SUBLANES + g_lo, :, s_hi * LANES : (s_hi + 1) * LANES])
        y5.append(_swap_sublane_lanegroup(wt_scr[...]))
    y5 = jnp.concatenate(y5, axis=-1)
    s5 = jax.nn.gelu(y5 + d_ref[0] * u_ref[0])
    s5 = s5 * jax.nn.sigmoid(_dot(s5.astype(BF16), gw_ref[0]) + gb_ref[0])
    cat = jnp.concatenate([hg, rt, s5], axis=-1).astype(BF16)
    y = _dot(cat, wo_ref[0])
    m = m_ref[0, 0, 0]
    x1 = _layer_norm(alpha * x_ref[0] + m[2:3, :] * y, lng_ref[0], lnb_ref[0])
    x1_ref[0] = x1
    h2 = x1 * (1.0 + m[4:5, :]) + m[3:4, :]
    h_hi, h_lo = _split_bf16(h2)
    logits_t = (_dot_nt(wrh_ref[0], h_hi) + _dot_nt(wrh_ref[0], h_lo) + _dot_nt(wrl_ref[0], h_hi)) + br_ref[0]
    bucket, _, _ = _route(logits_t)
    sub = lax.broadcasted_iota(jnp.int32, logits_t.shape, 0)
    h2_ref[0] = h2.reshape(h2_ref.shape[1:])
    onehot = sub == bucket
    before = _dot(onehot.astype(BF16), tri_ref[...]) + cnt_ref[...]
    rank = jnp.sum(jnp.where(onehot, before, 0.0), axis=0, keepdims=True)
    rsub = lax.broadcasted_iota(jnp.int32, route_ref.shape[1:], 0)
    route_ref[0] = jnp.where(rsub == 0, bucket.astype(F32), jnp.where(rsub == 1, rank, 0.0))
    cnt_ref[...] += jnp.sum(onehot.astype(F32), axis=1, keepdims=True)


def _merge(p, o_hf, o_hb, o_rf, o_rb, yg, x, modtab, layer_prm, const_prm, l, tm, nct, alpha):
    b, s, d = x.shape
    cpt = tm // S5_CHUNK

    def tok(width, col):
        return pl.BlockSpec((1, tm, width), lambda i, j: (i, j, col))

    def whole(a):
        return pl.BlockSpec(a.shape, lambda i, j: (0,) * a.ndim)

    rbase = 5 * HG_WIDTH // RET_WIDTH
    in_specs = [tok(HG_WIDTH, 2), tok(RET_WIDTH, rbase + 3), tok(S5_WIDTH, rbase + 4),
                tok(HG_WIDTH, 0), tok(HG_WIDTH, 0), tok(RET_WIDTH, 0), tok(RET_WIDTH, 0),
                pl.BlockSpec((S5_GROUPS, cpt, S5_FOLD), lambda i, j: (0, j, i)),
                tok(d, 0), _mod_spec(l, nct)]
    in_specs += [_layer_spec(a.shape[1:], l) for a in layer_prm]
    in_specs += [whole(a) for a in const_prm]
    return pl.pallas_call(
        functools.partial(_merge_kernel, alpha=alpha),
        grid=(b, s // tm),
        in_specs=in_specs,
        out_specs=[tok(d, 0), pl.BlockSpec((1, tm, d // LANES, LANES), lambda i, j: (i, j, 0, 0)),
                   pl.BlockSpec((1, SUBLANES, tm), lambda i, j: (i, 0, j)),
                   pl.BlockSpec((LANES, 1), lambda i, j: (0, 0))],
        out_shape=[jax.ShapeDtypeStruct((b, s, d), F32), jax.ShapeDtypeStruct((b, s, d // LANES, LANES), F32),
                   jax.ShapeDtypeStruct((b, SUBLANES, s), F32), jax.ShapeDtypeStruct((LANES, 1), F32)],
        scratch_shapes=[pltpu.VMEM((tm, LANES), F32)],
        compiler_params=_cparams(("arbitrary", "arbitrary")),
        name="merge_ln1_router",
    )(p, p, p, o_hf, o_hb, o_rf, o_rb, yg, x, modtab, *layer_prm, *const_prm)


def _routing_tables(route, counts, te, n_tiles):
    bucket = route[:, 0, :].astype(jnp.int32).reshape(-1)
    rank = route[:, 1, :].astype(jnp.int32).reshape(-1)
    cnt = counts[:N_BUCKETS, 0].astype(jnp.int32)
    padded = (cnt + te - 1) // te * te
    ends = jnp.cumsum(padded)
    pos = (ends - padded)[bucket] + rank
    n_used = ends[-1] // te
    tile = jnp.arange(n_tiles, dtype=jnp.int32)
    tb = jnp.minimum(jnp.searchsorted(ends, tile * te, side="right"), N_BUCKETS - 1).astype(jnp.int32)
    tb = jnp.where(tile < n_used, tb, tb[jnp.maximum(n_used - 1, 0)])
    group, pair = tb // N_PAIRS, tb % N_PAIRS
    ea = group * EXPERTS_PER_GROUP + jnp.asarray(PAIR_SLOT_A, jnp.int32)[pair]
    eb = group * EXPERTS_PER_GROUP + jnp.asarray(PAIR_SLOT_B, jnp.int32)[pair]
    return pos, ea, eb, n_used.reshape(1).astype(jnp.int32)


def _dispatch_kernel(pos_ref, src_ref, init_ref, out_ref, sem, *, tm, tiles_per_row):
    del init_ref
    base = (pl.program_id(0) * tiles_per_row + pl.program_id(1)) * tm

    def row_copy(r):
        return pltpu.make_async_copy(src_ref.at[0, r], out_ref.at[pos_ref[base + r]], sem)

    def start(g, carry):
        first = pl.multiple_of(g * DMA_UNROLL, DMA_UNROLL)
        for k in range(DMA_UNROLL):
            row_copy(first + k).start(priority=k % 2)
        return carry

    def wait(r, carry):
        row_copy(r).wait()
        return carry

    lax.fori_loop(0, tm // DMA_UNROLL, start, 0)
    lax.fori_loop(0, tm, wait, 0, unroll=DMA_UNROLL)


def _dispatch(pos, rows, init, tm):
    b, s, rt, rl = rows.shape
    return pl.pallas_call(
        functools.partial(_dispatch_kernel, tm=tm, tiles_per_row=s // tm),
        grid_spec=pltpu.PrefetchScalarGridSpec(
            num_scalar_prefetch=1,
            grid=(b, s // tm),
            in_specs=[pl.BlockSpec((1, tm, rt, rl), lambda i, j, pos_ref: (i, j, 0, 0)),
                      pl.BlockSpec(memory_space=pl.ANY)],
            out_specs=pl.BlockSpec(memory_space=pl.ANY),
            scratch_shapes=[pltpu.SemaphoreType.DMA(())],
        ),
        out_shape=jax.ShapeDtypeStruct(init.shape, F32),
        input_output_aliases={2: 0},
        compiler_params=_cparams(("arbitrary", "arbitrary")),
        name="moe_dispatch",
    )(pos, rows, init)


def _expert_kernel(ea_ref, eb_ref, nused_ref, hs_ref, wga_ref, wua_ref, wda_ref, wgb_ref, wub_ref, wdb_ref,
                   wr_ref, br_ref, o_ref, wg_scr, wu_scr, wd_scr):
    t = pl.program_id(0)
    prev = jnp.maximum(t - 1, 0)
    @pl.when((t == 0) | (ea_ref[t] != ea_ref[prev]))
    def _():
        wg_scr[0] = wga_ref[0, 0].astype(BF16)
        wu_scr[0] = wua_ref[0, 0].astype(BF16)
        wd_scr[0] = wda_ref[0, 0].astype(BF16)

    @pl.when((t == 0) | (eb_ref[t] != eb_ref[prev]))
    def _():
        wg_scr[1] = wgb_ref[0, 0].astype(BF16)
        wu_scr[1] = wub_ref[0, 0].astype(BF16)
        wd_scr[1] = wdb_ref[0, 0].astype(BF16)

    @pl.when(t < nused_ref[0])
    def _():
        d = wg_scr.shape[1]
        h32 = hs_ref[...].reshape(hs_ref.shape[0], d)
        h = h32.astype(BF16)
        logits_t = _dot_nt(wr_ref[0], h) + br_ref[0]
        sub = lax.broadcasted_iota(jnp.int32, logits_t.shape, 0)
        pick = lambda r: jnp.sum(jnp.where(sub == r, logits_t, 0.0), axis=0, keepdims=True)
        ea, eb = ea_ref[t], eb_ref[t]
        l_a, l_b = pick(N_GROUPS + ea), pick(N_GROUPS + eb)
        l_g = pick(lax.shift_right_logical(ea, EXPERTS_PER_GROUP.bit_length() - 1))
        g_sum = functools.reduce(lambda x, z: x + z, [jnp.exp(logits_t[i : i + 1, :] - l_g) for i in range(N_GROUPS)])
        m = jnp.maximum(l_a, l_b)
        t_a, t_b = jnp.exp(l_a - m), jnp.exp(l_b - m)
        inv = 1.0 / (g_sum * (t_a + t_b))
        w_cols = jnp.where(sub == 0, t_a * inv, jnp.where(sub == 1, t_b * inv, 0.0)).T
        y = None
        for e in range(2):
            act = (_silu(_dot(h, wg_scr[e])) * _dot(h, wu_scr[e])).astype(BF16)
            ye = w_cols[:, e : e + 1] * _dot(act, wd_scr[e])
            y = ye if y is None else y + ye
        o_ref[...] = y.reshape(o_ref.shape)

    @pl.when(t >= nused_ref[0])
    def _():
        o_ref[...] = jnp.zeros_like(o_ref)


def _experts(ea, eb, n_used, hs, wg, wu, wd, router, l, te):
    n_sorted, rt, rl = hs.shape
    _, _, d, eh = wg.shape

    def wspec(shape, which):
        return pl.BlockSpec((1, 1) + shape, lambda t, ea_ref, eb_ref, n_ref: (l, (ea_ref, eb_ref)[which][t], 0, 0))

    return pl.pallas_call(
        _expert_kernel,
        grid_spec=pltpu.PrefetchScalarGridSpec(
            num_scalar_prefetch=3,
            grid=(n_sorted // te,),
            in_specs=[pl.BlockSpec((te, rt, rl), lambda t, *_: (t, 0, 0)),
                      wspec((d, eh), 0), wspec((d, eh), 0), wspec((eh, d), 0),
                      wspec((d, eh), 1), wspec((d, eh), 1), wspec((eh, d), 1)]
                     + [_layer_spec(a.shape[1:], l) for a in router],
            out_specs=pl.BlockSpec((te, d // LANES, LANES), lambda t, *_: (t, 0, 0)),
            scratch_shapes=[pltpu.VMEM((2, d, eh), BF16), pltpu.VMEM((2, d, eh), BF16), pltpu.VMEM((2, eh, d), BF16)],
        ),
        out_shape=jax.ShapeDtypeStruct((n_sorted, d // LANES, LANES), F32),
        compiler_params=_cparams(("arbitrary",)),
        name="moe_experts",
    )(ea, eb, n_used, hs, wg, wu, wd, wg, wu, wd, *router)


def _gather_expert_rows(pos_ref, ys_ref, buf, sem, *, tm, tiles_per_row, first_tile):
    cols = pl.num_programs(1)
    n_steps = pl.num_programs(0) * cols
    step = pl.program_id(0) * cols + pl.program_id(1)
    slot = step % 2

    def row_copy(step_, slot_, r):
        tile = (step_ // cols) * tiles_per_row + first_tile + step_ % cols
        return pltpu.make_async_copy(ys_ref.at[pos_ref[tile * tm + r]], buf.at[slot_, r], sem.at[slot_])

    def start_tile(step_, slot_):
        def body(g, carry):
            first = pl.multiple_of(g * DMA_UNROLL, DMA_UNROLL)
            for k in range(DMA_UNROLL):
                row_copy(step_, slot_, first + k).start(priority=k % 2)
            return carry

        lax.fori_loop(0, tm // DMA_UNROLL, body, 0)

    @pl.when(step == 0)
    def _():
        start_tile(0, 0)

    @pl.when(step + 1 < n_steps)
    def _():
        start_tile(step + 1, 1 - slot)

    def wait(r, carry):
        row_copy(step, slot, r).wait()
        return carry

    lax.fori_loop(0, tm, wait, 0, unroll=DMA_UNROLL)
    return buf[slot].reshape(tm, buf.shape[2] * buf.shape[3])


def _combine_kernel(pos_ref, ys_ref, x1_ref, m_ref, lng_ref, lnb_ref, o_ref, buf, sem,
                    *, tm, tiles_per_row, first_tile, alpha):
    y = _gather_expert_rows(pos_ref, ys_ref, buf, sem, tm=tm, tiles_per_row=tiles_per_row, first_tile=first_tile)
    m = m_ref[0, 0, 0]
    o_ref[0] = _layer_norm(alpha * x1_ref[0] + m[5:6, :] * y, lng_ref[0], lnb_ref[0])


def _combine(pos, ys, x1, modtab, ln_g, ln_b, l, tm, nct, alpha, first_tile):
    b, s, d = x1.shape
    cols = s // tm - first_tile
    mod_spec = pl.BlockSpec((1, 1, 1, 6, d),
                            lambda i, j, pos_ref: (l, i, jnp.minimum((j + first_tile) // nct, 1), 0, 0))
    return pl.pallas_call(
        functools.partial(_combine_kernel, tm=tm, tiles_per_row=s // tm, first_tile=first_tile, alpha=alpha),
        grid_spec=pltpu.PrefetchScalarGridSpec(
            num_scalar_prefetch=1,
            grid=(b, cols),
            in_specs=[pl.BlockSpec(memory_space=pl.ANY),
                      pl.BlockSpec((1, tm, d), lambda i, j, pos_ref: (i, j + first_tile, 0)),
                      mod_spec, _layer_spec((1, d), l), _layer_spec((1, d), l)],
            out_specs=pl.BlockSpec((1, tm, d), lambda i, j, pos_ref: (i, j, 0)),
            scratch_shapes=[pltpu.VMEM((2, tm, d // LANES, LANES), F32), pltpu.SemaphoreType.DMA((2,))],
        ),
        out_shape=jax.ShapeDtypeStruct((b, cols * tm, d), F32),
        compiler_params=_cparams(("arbitrary", "arbitrary")),
        name="moe_combine_ln2",
    )(pos, ys, x1, modtab, ln_g, ln_b)


def _block_avg(width, group):
    idx = np.arange(width) // group
    return jnp.asarray((idx[:, None] == idx[None, :]).astype(np.float32) / group, dtype=BF16)


def kernel(x, c, ctx, c_ctx, w_mod, b_mod, w_in, hg_lb_raw, hg_norm_g, ret_decay_raw, ret_gn_g, ret_gn_b, s5_lam_re, s5_lam_im, s5_log_dt, s5_b_re, s5_b_im, s5_c_re, s5_c_im, s5_d, s5_glu_w, s5_glu_b, w_out, ln1_g, ln1_b, ln2_g, ln2_b, rg_w, rg_b, re_w, re_b, exp_w_gate, exp_w_up, exp_w_down):
    bsz, t_lat, d = x.shape
    t_ctx = ctx.shape[1]
    depth = w_mod.shape[0]
    alpha = (2.0 * depth) ** 0.25
    tm = TOKEN_TILE
    assert d == D_MODEL and t_lat % GRID_W == 0 and bsz < SUBLANES
    assert t_ctx % tm == 0 and t_lat % tm == 0, "context and latent lengths must be multiples of the token tile"
    nct = t_ctx // tm
    s = t_ctx + t_lat

    cvec = jnp.concatenate([c, c_ctx[None, :], jnp.zeros((SUBLANES - bsz - 1, d), F32)], 0)
    mod_all = _modulation(cvec, w_mod, b_mod)
    lat = mod_all[:, :bsz].reshape(depth, bsz, 6, d)
    cm = jnp.broadcast_to(mod_all[:, bsz].reshape(depth, 1, 6, d), (depth, bsz, 6, d))
    modtab = jnp.stack([cm, lat], axis=2)

    hg_lb = jnp.cumsum(jax.nn.softmax(hg_lb_raw.astype(F32), axis=0), axis=0)
    hg_lb = hg_lb - hg_lb[:1]
    ret_tables = _retention_decay_tables(jax.nn.log_sigmoid(ret_decay_raw.astype(F32)))
    cos_tab, sin_tab = _rope_tables(t_lat, t_ctx)
    s5_tabs = _s5_weights(s5_lam_re, s5_lam_im, s5_log_dt, s5_b_re, s5_b_im, s5_c_re, s5_c_im)
    pad_r = LANES - N_GROUPS - N_EXPERTS
    wr = jnp.concatenate([rg_w, re_w.reshape(depth, d, N_EXPERTS), jnp.zeros((depth, d, pad_r), F32)], axis=2)
    wr = jnp.swapaxes(wr, 1, 2)
    wr_hi, wr_lo = _split_bf16(wr)
    br = jnp.concatenate([rg_b, re_b.reshape(depth, N_EXPERTS), jnp.zeros((depth, pad_r), F32)], axis=1)[:, :, None]
    row = lambda a: a[:, None, :]
    layer_prm = [row(jnp.tile(hg_norm_g, (1, HG_HEADS))), row(ret_gn_g), row(ret_gn_b), row(s5_d),
                 s5_glu_w.astype(BF16), row(s5_glu_b), w_out.astype(BF16), row(ln1_g), row(ln1_b),
                 wr_hi, wr_lo, br]
    tri = jnp.asarray(np.triu(np.ones((tm, tm), np.float32), 1), dtype=BF16)
    const_prm = [_block_avg(HG_WIDTH, HG_DK), _block_avg(RET_WIDTH, RET_DK), tri]
    w_in_bf16 = w_in.astype(BF16)
    ln2_g, ln2_b = row(ln2_g), row(ln2_b)
    te = EXPERT_TILE
    n_tiles = -(-(bsz * s + N_BUCKETS * (te - 1)) // te)

    hs = jnp.zeros((n_tiles * te, d // LANES, LANES), F32)
    xs = jnp.concatenate([ctx, x], axis=1)
    for l in range(depth):
        if l == 0:
            p, ug = _inproj(xs, modtab, w_in_bf16, l, tm, nct)
        else:
            xs, p, ug = _ln2_inproj(pos, ys, x1, modtab, ln2_g, ln2_b, w_in_bf16, l, tm, nct, alpha)
        o_hf, o_hb = _hgrn(p, hg_lb, l, t_ctx // CHUNK)
        o_rf, o_rb = _retention(p, cos_tab, sin_tab, ret_tables, l, t_ctx // RET_CHUNK)
        yg = _s5_conv(ug, *s5_tabs, l, bsz, t_ctx // S5_CHUNK)
        x1, rows, route, counts = _merge(p, o_hf, o_hb, o_rf, o_rb, yg, xs, modtab, layer_prm, const_prm,
                                         l, tm, nct, alpha)
        pos, ea, eb, n_used = _routing_tables(route, counts, te, n_tiles)
        hs = _dispatch(pos, rows, hs, tm)
        ys = _experts(ea, eb, n_used, hs, exp_w_gate, exp_w_up, exp_w_down, (wr_hi, br), l, te)
    return _combine(pos, ys, x1, modtab, ln2_g, ln2_b, depth - 1, tm, nct, alpha, nct)
```

```python
import functools

import numpy as np
import jax
import jax.numpy as jnp
from jax import lax
from jax.experimental import pallas as pl
from jax.experimental.pallas import tpu as pltpu

F32 = jnp.float32
BF16 = jnp.bfloat16

D_MODEL = 1024
HG_WIDTH = 512
HG_HEADS = 4
HG_DK = HG_WIDTH // HG_HEADS
RET_WIDTH = 256
RET_HEADS = 4
RET_DK = RET_WIDTH // RET_HEADS
S5_WIDTH = 256
S5_GROUP_CH = 16
S5_GROUPS = S5_WIDTH // S5_GROUP_CH
S5_STATE = 64
IN_COLS = 5 * HG_WIDTH + 4 * RET_WIDTH + S5_WIDTH
CHUNK = 128
HGRN_BATCH = 4
RET_CHUNK = 256
RET_BATCH = 4
N_GROUPS = 4
EXPERTS_PER_GROUP = 4
N_EXPERTS = N_GROUPS * EXPERTS_PER_GROUP
N_PAIRS = EXPERTS_PER_GROUP * (EXPERTS_PER_GROUP - 1) // 2
N_BUCKETS = N_GROUPS * N_PAIRS
PAIR_SLOT_A = (0, 2, 2, 3, 3, 3)
PAIR_SLOT_B = (1, 1, 0, 0, 1, 2)
EXPERT_HIDDEN = D_MODEL // 2
LN_EPS = 1e-5
ROPE_BASE = 10000.0
GRID_W = 64

LANES = 128
SUBLANES = 8
TOKEN_TILE = 256
S5_CHUNK = 32
S5_FOLD = S5_CHUNK * S5_GROUP_CH
EXPERT_TILE = 256
DMA_UNROLL = 8
VMEM_LIMIT = 56 * 1024 * 1024


def _cparams(sem):
    return pltpu.CompilerParams(dimension_semantics=sem, vmem_limit_bytes=VMEM_LIMIT)


def _split_bf16(x):
    hi = x.astype(BF16)
    lo = (x - hi.astype(F32)).astype(BF16)
    return hi, lo


def _dot(a, b):
    return jnp.dot(a, b, preferred_element_type=F32)


def _dot3(a, b):
    ah, al = _split_bf16(a)
    bh, bl = _split_bf16(b)
    return _dot(ah, bh) + _dot(ah, bl) + _dot(al, bh)


def _dot2(a, b_bf16):
    ah, al = _split_bf16(a)
    return _dot(ah, b_bf16) + _dot(al, b_bf16)


def _dot_nt(a, b):
    return lax.dot_general(a, b, (((1,), (1,)), ((), ())), preferred_element_type=F32)


def _dot_tn(a, b):
    return lax.dot_general(a, b, (((0,), (0,)), ((), ())), preferred_element_type=F32)


def _silu(x):
    return x * jax.nn.sigmoid(x)


def _layer_spec(shape, l):
    zeros = (0,) * len(shape)
    return pl.BlockSpec((1,) + tuple(shape), lambda *_: (l,) + zeros)


def _mod_spec(l, nct):
    return pl.BlockSpec((1, 1, 1, 6, D_MODEL), lambda i, j, *_: (l, i, jnp.minimum(j // nct, 1), 0, 0))


def _swap_sublane_lanegroup(v):
    n = v.shape[0]
    r = lax.broadcasted_iota(jnp.int32, v.shape, 0)
    l = lax.broadcasted_iota(jnp.int32, v.shape, 1)
    for k in range(3):
        rb = (r >> k) & 1
        gb = (l >> (4 + k)) & 1
        sh = S5_GROUP_CH << k
        st = 1 << k
        a = pltpu.roll(pltpu.roll(v, LANES - sh, 1), st, 0)
        b = pltpu.roll(pltpu.roll(v, sh, 1), n - st, 0)
        v = jnp.where(rb == gb, v, jnp.where(rb == 1, a, b))
    return v


def _mod_kernel(c_ref, w_ref, b_ref, o_ref):
    sc = _silu(c_ref[...])
    o_ref[0] = _dot3(sc, w_ref[0]) + b_ref[0]


def _modulation(cvec, w_mod, b_mod):
    depth, d, n = w_mod.shape
    rows = cvec.shape[0]
    tn = 1536
    return pl.pallas_call(
        _mod_kernel,
        grid=(depth, n // tn),
        in_specs=[
            pl.BlockSpec((rows, d), lambda l, j: (0, 0)),
            pl.BlockSpec((1, d, tn), lambda l, j: (l, 0, j)),
            pl.BlockSpec((1, 1, tn), lambda l, j: (l, 0, j)),
        ],
        out_specs=pl.BlockSpec((1, rows, tn), lambda l, j: (l, 0, j)),
        out_shape=jax.ShapeDtypeStruct((depth, rows, n), F32),
        compiler_params=_cparams(("arbitrary", "arbitrary")),
        name="modulation",
    )(cvec, w_mod, b_mod.reshape(depth, 1, n))


def _inproj_kernel(x_ref, m_ref, w_ref, o_ref, ug_ref, wt_scr):
    _project(x_ref[0], m_ref, w_ref, o_ref, ug_ref, wt_scr)


def _ln2_inproj_kernel(pos_ref, ys_ref, x1_ref, mprev_ref, lng_ref, lnb_ref, m_ref, w_ref,
                       x2_ref, o_ref, ug_ref, buf, sem, wt_scr, *, tm, tiles_per_row, alpha):
    y = _gather_expert_rows(pos_ref, ys_ref, buf, sem, tm=tm, tiles_per_row=tiles_per_row, first_tile=0)
    g2 = mprev_ref[0, 0, 0][5:6, :]
    x2 = _layer_norm(alpha * x1_ref[0] + g2 * y, lng_ref[0], lnb_ref[0])
    x2_ref[0] = x2
    _project(x2, m_ref, w_ref, o_ref, ug_ref, wt_scr)


def _project(x, m_ref, w_ref, o_ref, ug_ref, wt_scr):
    m = m_ref[0, 0, 0]
    h = x * (1.0 + m[1:2, :]) + m[0:1, :]
    p = _dot(h.astype(BF16), w_ref[0])
    o_ref[0] = p
    rows_per_chunk = S5_CHUNK // SUBLANES
    n_chunks = p.shape[0] // S5_CHUNK
    for half in range(S5_WIDTH // LANES):
        lo = IN_COLS - S5_WIDTH + half * LANES
        wt_scr[...] = _swap_sublane_lanegroup(p[:, lo : lo + LANES])
        for g_lo in range(SUBLANES):
            for s_hi in range(rows_per_chunk):
                piece = wt_scr[pl.ds(s_hi * SUBLANES + g_lo, n_chunks, stride=S5_CHUNK), :]
                ug_ref[half * SUBLANES + g_lo, :, s_hi * LANES : (s_hi + 1) * LANES] = piece


def _inproj(x, modtab, w_in_bf16, l, tm, nct):
    b, s, d = x.shape
    n = w_in_bf16.shape[-1]
    cpt = tm // S5_CHUNK
    return pl.pallas_call(
        _inproj_kernel,
        grid=(b, s // tm),
        in_specs=[
            pl.BlockSpec((1, tm, d), lambda i, j: (i, j, 0)),
            _mod_spec(l, nct),
            _layer_spec((d, n), l),
        ],
        out_specs=[pl.BlockSpec((1, tm, n), lambda i, j: (i, j, 0)),
                   pl.BlockSpec((S5_GROUPS, cpt, S5_FOLD), lambda i, j: (0, j, i))],
        out_shape=[jax.ShapeDtypeStruct((b, s, n), F32),
                   jax.ShapeDtypeStruct((S5_GROUPS, s // S5_CHUNK, b * S5_FOLD), F32)],
        scratch_shapes=[pltpu.VMEM((tm, LANES), F32)],
        compiler_params=_cparams(("arbitrary", "arbitrary")),
        name="inproj",
    )(x, modtab, w_in_bf16)


def _ln2_inproj(pos, ys, x1, modtab, ln_g, ln_b, w_in_bf16, l, tm, nct, alpha):
    b, s, d = x1.shape
    n = w_in_bf16.shape[-1]
    cpt = tm // S5_CHUNK
    tok = lambda width: pl.BlockSpec((1, tm, width), lambda i, j, pos_ref: (i, j, 0))
    return pl.pallas_call(
        functools.partial(_ln2_inproj_kernel, tm=tm, tiles_per_row=s // tm, alpha=alpha),
        grid_spec=pltpu.PrefetchScalarGridSpec(
            num_scalar_prefetch=1,
            grid=(b, s // tm),
            in_specs=[pl.BlockSpec(memory_space=pl.ANY), tok(d),
                      _mod_spec(l - 1, nct), _layer_spec((1, d), l - 1), _layer_spec((1, d), l - 1),
                      _mod_spec(l, nct), _layer_spec((d, n), l)],
            out_specs=[tok(d), tok(n),
                       pl.BlockSpec((S5_GROUPS, cpt, S5_FOLD), lambda i, j, pos_ref: (0, j, i))],
            scratch_shapes=[pltpu.VMEM((2, tm, d // LANES, LANES), F32), pltpu.SemaphoreType.DMA((2,)),
                            pltpu.VMEM((tm, LANES), F32)],
        ),
        out_shape=[jax.ShapeDtypeStruct((b, s, d), F32), jax.ShapeDtypeStruct((b, s, n), F32),
                   jax.ShapeDtypeStruct((S5_GROUPS, s // S5_CHUNK, b * S5_FOLD), F32)],
        compiler_params=_cparams(("arbitrary", "arbitrary")),
        name="ln2_inproj",
    )(pos, ys, x1, modtab, ln_g, ln_b, modtab, w_in_bf16)


def _block_gate_products(f, reverse):
    c = f.shape[0]
    row = lax.broadcasted_iota(jnp.int32, (c, 1), 0)
    a, z, b = f, f, jnp.ones_like(f)
    out = []
    s = 1
    while s < c:
        out.append((a, b))
        if s < SUBLANES:
            z3 = z.reshape(c // SUBLANES, SUBLANES, z.shape[1])
            up = pltpu.roll(z3, s, 1).reshape(z.shape)
            dn = pltpu.roll(z3, SUBLANES - s, 1).reshape(z.shape)
            odd = (row & s) != 0
            if reverse:
                a = a * jnp.where(odd, 1.0, dn)
                b = b * jnp.where(odd, up, 1.0)
            else:
                a = a * jnp.where(odd, up, 1.0)
                b = b * jnp.where(odd, 1.0, dn)
            z = z * jnp.where(odd, up, dn)
        else:
            na, nb, nz = [], [], []
            for lo in range(0, c, 2 * s):
                ev, od = slice(lo, lo + s), slice(lo + s, lo + 2 * s)
                zz = z[ev] * z[od]
                if reverse:
                    na += [a[ev] * z[od], a[od]]
                    nb += [b[ev], b[od] * z[ev]]
                else:
                    na += [a[ev], a[od] * z[ev]]
                    nb += [b[ev] * z[od], b[od]]
                nz += [zz, zz]
            a, b, z = (jnp.concatenate(t, axis=0) for t in (na, nb, nz))
        s *= 2
    out.append((a, b))
    return out, z


def _hgrn_kernel(qf_ref, vf_ref, zf_ref, qb_ref, vb_ref, zb_ref, lb_ref, of_ref, ob_ref, st_ref):
    c = CHUNK

    @pl.when(pl.program_id(1) == 0)
    def _():
        st_ref[...] = jnp.zeros_like(st_ref)

    ri = lax.broadcasted_iota(jnp.int32, (c, c), 0)
    ci = lax.broadcasted_iota(jnp.int32, (c, c), 1)
    for d, (q_ref, v_ref, z_ref, o_ref) in enumerate(
        ((qf_ref, vf_ref, zf_ref, of_ref), (qb_ref, vb_ref, zb_ref, ob_ref))
    ):
        reverse = d == 1
        causal = (ri < ci) if reverse else (ri > ci)
        n_levels = c.bit_length() - 1
        masks = [(((ri >> lvl) ^ (ci >> lvl)) == 1) & causal for lvl in range(n_levels)]
        for bi, h in [(bi, h) for bi in range(q_ref.shape[0]) for h in range(HG_HEADS)]:
            hs = slice(h * HG_DK, (h + 1) * HG_DK)
            q = _silu(q_ref[bi, :, hs])
            v = v_ref[bi, :, hs].astype(BF16)
            lb = lb_ref[0, d : d + 1, hs]
            f = lb + (1.0 - lb) * jax.nn.sigmoid(z_ref[bi, :, hs])
            k = 1.0 - f
            levels, tot = _block_gate_products(f, reverse)
            scores = jnp.where(ri == ci, _dot_nt(q.astype(BF16), k.astype(BF16)), 0.0)
            for mask, (a, bb) in zip(masks, levels):
                scores = jnp.where(mask, _dot_nt((q * a).astype(BF16), (k * bb).astype(BF16)), scores)
            a_full, b_full = levels[-1]
            st = st_ref[bi, d, h]
            o_ref[bi, :, hs] = (_dot(scores.astype(BF16), v)
                                + _dot_nt((q * a_full).astype(BF16), st.astype(BF16)))
            st_ref[bi, d, h] = st * tot[0:1, :] + _dot_tn(v, (k * b_full).astype(BF16))


def _bwd_chunk(n, nc_ctx, nc_all):
    return jnp.where(n < nc_ctx, nc_ctx - 1 - n, nc_all + nc_ctx - 1 - n)


def _hgrn(p, lb_all, l, nc_ctx):
    b, s, _ = p.shape
    nc = s // CHUNK
    w = HG_WIDTH

    bb = HGRN_BATCH if b % HGRN_BATCH == 0 else 1

    def fwd(col):
        return pl.BlockSpec((bb, CHUNK, w), lambda i, n: (i, n, col))

    def bwd(col):
        return pl.BlockSpec((bb, CHUNK, w), lambda i, n: (i, _bwd_chunk(n, nc_ctx, nc), col))

    out = jax.ShapeDtypeStruct((b, s, w), F32)
    return pl.pallas_call(
        _hgrn_kernel,
        grid=(b // bb, nc),
        in_specs=[fwd(0), fwd(1), fwd(3), bwd(0), bwd(1), bwd(4), _layer_spec((2, w), l)],
        out_specs=[fwd(0), bwd(0)],
        out_shape=[out, out],
        scratch_shapes=[pltpu.VMEM((bb, 2, HG_HEADS, HG_DK, HG_DK), F32)],
        compiler_params=_cparams(("arbitrary", "arbitrary")),
        name="hgrn2_scan",
    )(p, p, p, p, p, p, lb_all)


def _swap_halves(x, half):
    n = x.shape[-1]
    lane = lax.broadcasted_iota(jnp.int32, (1, n), 1)
    lower = (lane & half) == 0
    return jnp.where(lower, pltpu.roll(x, n - half, 1), pltpu.roll(x, half, 1))


def _ret_kernel(qf_ref, kf_ref, vf_ref, cf_ref, sf_ref, qb_ref, kb_ref, vb_ref, cb_ref, sb_ref,
                dmat_ref, rq_ref, rk_ref, cd_ref, of_ref, ob_ref, st_ref):
    @pl.when(pl.program_id(1) == 0)
    def _():
        st_ref[...] = jnp.zeros_like(st_ref)

    half = RET_DK // 4
    for d, (q_ref, k_ref, v_ref, c_ref, s_ref, o_ref) in enumerate(
        ((qf_ref, kf_ref, vf_ref, cf_ref, sf_ref, of_ref), (qb_ref, kb_ref, vb_ref, cb_ref, sb_ref, ob_ref))
    ):
        cos = c_ref[...]
        sin = s_ref[...]
        cd = cd_ref[0, d : d + 1, :]
        lane = lax.broadcasted_iota(jnp.int32, (1, LANES), 1)
        left = lane < RET_DK
        same_head = (lax.broadcasted_iota(jnp.int32, (LANES, 1), 0) < RET_DK) == left
        zero = jnp.zeros((), BF16)
        for bi in range(q_ref.shape[0]):
            q = q_ref[bi]
            k = k_ref[bi] * (RET_DK ** -0.5)
            q = q * cos + _swap_halves(q, half) * sin
            k = k * cos + _swap_halves(k, half) * sin
            v = v_ref[bi]
            vt = v.T.astype(BF16)
            v = v.astype(BF16)
            q0 = q.astype(BF16)
            k0 = k.astype(BF16)
            qd = (q * rq_ref[0, d]).astype(BF16)
            kd = (k * rk_ref[0, d]).astype(BF16)
            for p in range(RET_HEADS // 2):
                ps = slice(p * LANES, (p + 1) * LANES)
                kp, vp = k0[:, ps], v[:, ps]
                k_blk = jnp.concatenate([jnp.where(left, kp, zero), jnp.where(left, zero, kp)], axis=0)
                v_blk = jnp.concatenate([jnp.where(left, vp, zero), jnp.where(left, zero, vp)], axis=0)
                scores = _dot_nt(q0[:, ps], k_blk) * dmat_ref[0, d, p]
                st = st_ref[bi, d, p]
                o_ref[bi, :, ps] = _dot(scores.astype(BF16), v_blk) + _dot_nt(qd[:, ps], st.astype(BF16))
                st_ref[bi, d, p] = st * cd[:, ps] + jnp.where(same_head, _dot(vt[ps, :], kd[:, ps]), 0.0)


def _retention(p, cos_tab, sin_tab, tables, l, nc_ctx):
    b, s, _ = p.shape
    nc = s // RET_CHUNK
    w = RET_WIDTH
    base = 5 * HG_WIDTH // w

    bb = RET_BATCH if b % RET_BATCH == 0 else 1

    def fwd(col):
        return pl.BlockSpec((bb, RET_CHUNK, w), lambda i, n: (i, n, col))

    def bwd(col):
        return pl.BlockSpec((bb, RET_CHUNK, w), lambda i, n: (i, _bwd_chunk(n, nc_ctx, nc), col))

    tab_f = pl.BlockSpec((RET_CHUNK, w), lambda i, n: (n, 0))
    tab_b = pl.BlockSpec((RET_CHUNK, w), lambda i, n: (_bwd_chunk(n, nc_ctx, nc), 0))
    out = jax.ShapeDtypeStruct((b, s, w), F32)
    return pl.pallas_call(
        _ret_kernel,
        grid=(b // bb, nc),
        in_specs=[fwd(base), fwd(base + 1), fwd(base + 2), tab_f, tab_f,
                  bwd(base), bwd(base + 1), bwd(base + 2), tab_b, tab_b]
                 + [_layer_spec(t.shape[1:], l) for t in tables],
        out_specs=[fwd(0), bwd(0)],
        out_shape=[out, out],
        scratch_shapes=[pltpu.VMEM((bb, 2, RET_HEADS // 2, LANES, LANES), F32)],
        compiler_params=_cparams(("arbitrary", "arbitrary")),
        name="retention_scan",
    )(p, p, p, cos_tab, sin_tab, p, p, p, cos_tab, sin_tab, *tables)


def _rope_tables(t_lat, t_ctx):
    m = RET_DK // 4
    inv = ROPE_BASE ** (-jnp.arange(m, dtype=F32) / m)
    rows = jnp.repeat(jnp.arange(t_lat // GRID_W, dtype=jnp.int32), GRID_W).astype(F32)
    cols = jnp.tile(jnp.arange(GRID_W, dtype=jnp.int32), t_lat // GRID_W).astype(F32)

    def half_tables(pos):
        ang = pos[:, None] * inv
        c, s = jnp.cos(ang), jnp.sin(ang)
        return jnp.concatenate([c, c], -1), jnp.concatenate([-s, s], -1)

    cr, sr = half_tables(rows)
    cc, sc = half_tables(cols)
    cos_h = jnp.concatenate([cr, cc], -1)
    sin_h = jnp.concatenate([sr, sc], -1)
    cos = jnp.tile(cos_h, (1, RET_HEADS))
    sin = jnp.tile(sin_h, (1, RET_HEADS))
    cos = jnp.concatenate([jnp.ones((t_ctx, RET_WIDTH), F32), cos], 0)
    sin = jnp.concatenate([jnp.zeros((t_ctx, RET_WIDTH), F32), sin], 0)
    return cos, sin


def _retention_decay_tables(log_gamma):
    c = RET_CHUNK
    i = jnp.arange(c, dtype=F32)
    diff = i[:, None] - i[None, :]
    lg = log_gamma[:, :, :, None, None]
    d_f = jnp.where(diff >= 0, jnp.exp(lg[:, 0] * diff), 0.0)
    d_b = jnp.where(diff <= 0, jnp.exp(lg[:, 1] * (-diff)), 0.0)
    dmat = jnp.stack([d_f, d_b], 1)
    dmat = jnp.concatenate([dmat[:, :, 0::2], dmat[:, :, 1::2]], axis=-1)
    lane_lg = jnp.repeat(log_gamma, RET_DK, axis=2)[:, :, None, :]
    col = i[None, :, None]
    rq = jnp.stack([jnp.exp(lane_lg[:, 0] * (col + 1.0)), jnp.exp(lane_lg[:, 1] * (c - col))], 1)
    rk = jnp.stack([jnp.exp(lane_lg[:, 0] * (c - 1.0 - col)), jnp.exp(lane_lg[:, 1] * col)], 1)
    cdec = jnp.exp(lane_lg[:, :, 0, :] * c)
    return dmat, rq, rk, cdec


def _s5_kernel(u_ref, base_ref, inj_ref, w2_ref, ac_ref, y_ref, toep_scr, a_scr, i_scr, hf_scr, hb_scr,
               *, n_ctx, n_all, bsz):
    width = S5_FOLD
    st2 = 2 * S5_STATE
    ch = S5_GROUP_CH
    lane = lax.broadcasted_iota(jnp.int32, (ch, width), 1)
    base_f = base_ref[0, 0, 0]
    base_b = base_ref[0, 0, 1]
    for s in range(S5_CHUNK):
        sh_f = s * ch
        sh_b = (S5_CHUNK - 1 - s) * ch
        part_f = jnp.where(lane >= sh_f, pltpu.roll(base_f, sh_f, 1), 0.0) if sh_f else base_f
        part_b = jnp.where(lane < width - sh_b, pltpu.roll(base_b, width - sh_b, 1), 0.0) if sh_b else base_b
        toep_scr[s * ch : (s + 1) * ch, :] = (part_f + part_b).astype(BF16)
    for b in range(bsz):
        ub = u_ref[0, :, b * width : (b + 1) * width].astype(BF16)
        a_scr[b] = _dot(ub, toep_scr[...])
        i_scr[b] = _dot(ub, inj_ref[0, 0])
    ac = ac_ref[0, 0]

    zero = jnp.zeros((1, st2), F32)
    state = [(zero, zero)] * (2 * bsz)
    for n in range(n_all):
        nb = n_ctx - 1 - n if n < n_ctx else n_all + n_ctx - 1 - n
        for b in range(bsz):
            for d, (row, scr) in enumerate(((n, hf_scr), (nb, hb_scr))):
                h, hsw = state[2 * b + d]
                scr[b, row : row + 1, :] = h
                inj = i_scr[b, row : row + 1, d * st2 : (d + 1) * st2]
                inj_sw = i_scr[b, row : row + 1, (d + 2) * st2 : (d + 3) * st2]
                ar, ai = ac[2 * d : 2 * d + 1, :], ac[2 * d + 1 : 2 * d + 2, :]
                state[2 * b + d] = (h * ar + hsw * ai + inj, hsw * ar - h * ai + inj_sw)
    for b in range(bsz):
        y_ref[0, :, b * width : (b + 1) * width] = (
            a_scr[b]
            + _dot(hf_scr[b].astype(BF16), w2_ref[0, 0, 0])
            + _dot(hb_scr[b].astype(BF16), w2_ref[0, 0, 1]))


def _s5_conv(ug, base, inj, w2, ac, l, bsz, n_ctx):
    g, n_all, _ = ug.shape
    width = S5_FOLD
    st2 = 2 * S5_STATE
    return pl.pallas_call(
        functools.partial(_s5_kernel, n_ctx=n_ctx, n_all=n_all, bsz=bsz),
        grid=(g,),
        in_specs=[
            pl.BlockSpec((1, n_all, bsz * width), lambda i: (i, 0, 0)),
            pl.BlockSpec((1, 1, 2, S5_GROUP_CH, width), lambda i: (l, i, 0, 0, 0)),
            pl.BlockSpec((1, 1, width, 4 * st2), lambda i: (l, i, 0, 0)),
            pl.BlockSpec((1, 1, 2, st2, width), lambda i: (l, i, 0, 0, 0)),
            pl.BlockSpec((1, 1, SUBLANES, st2), lambda i: (l, i, 0, 0)),
        ],
        out_specs=pl.BlockSpec((1, n_all, bsz * width), lambda i: (i, 0, 0)),
        out_shape=jax.ShapeDtypeStruct((g, n_all, bsz * width), F32),
        scratch_shapes=[pltpu.VMEM((width, width), BF16),
                        pltpu.VMEM((bsz, n_all, width), F32), pltpu.VMEM((bsz, n_all, 4 * st2), F32),
                        pltpu.VMEM((bsz, n_all, st2), F32), pltpu.VMEM((bsz, n_all, st2), F32)],
        compiler_params=_cparams(("arbitrary",)),
        name="s5_conv",
    )(ug, base, inj, w2, ac)


def _s5_weights(lam_re, lam_im, log_dt, b_re, b_im, c_re, c_im):
    cs, ch, p, g = S5_CHUNK, S5_GROUP_CH, S5_STATE, S5_GROUPS
    nl = lam_re.shape[0]
    lam_re = jnp.minimum(lam_re.astype(F32), -1e-4)
    lam_im = lam_im.astype(F32)
    dt = jnp.exp(log_dt.astype(F32))[..., None]
    mag = jnp.exp(dt * lam_re)
    abar_re, abar_im = mag * jnp.cos(dt * lam_im), mag * jnp.sin(dt * lam_im)
    den = jnp.square(lam_re) + jnp.square(lam_im)
    nr, ni = abar_re - 1.0, abar_im
    coef_re = ((nr * lam_re + ni * lam_im) / den)[..., None]
    coef_im = ((ni * lam_re - nr * lam_im) / den)[..., None]
    b_re, b_im = b_re.astype(F32), b_im.astype(F32)
    bb_re = coef_re * b_re - coef_im * b_im
    bb_im = coef_re * b_im + coef_im * b_re
    ct_re = jnp.swapaxes(c_re.astype(F32), -1, -2)
    ct_im = jnp.swapaxes(c_im.astype(F32), -1, -2)

    def powers(expo):
        e = expo.astype(F32)[None, :, None, None, :]
        m = jnp.exp(e * (dt * lam_re)[..., None])
        return m * jnp.cos(e * (dt * lam_im)[..., None]), m * jnp.sin(e * (dt * lam_im)[..., None])

    slots = jnp.arange(cs + 1)
    pr, pi = powers(jnp.stack([slots, cs - slots]))
    r_re = (pr[..., None] * ct_re[..., None, :] - pi[..., None] * ct_im[..., None, :]).reshape(nl, 2, g, p, (cs + 1) * ch)
    r_im = (pr[..., None] * ct_im[..., None, :] + pi[..., None] * ct_re[..., None, :]).reshape(nl, 2, g, p, (cs + 1) * ch)
    hi = lax.Precision.HIGHEST
    kern = (jnp.einsum('ldgpe,ldgpn->ldgen', bb_re, r_re, precision=hi)
            - jnp.einsum('ldgpe,ldgpn->ldgen', bb_im, r_im, precision=hi))
    w = cs * ch
    base = jnp.stack([kern[:, 0, :, :, :w], kern[:, 1, :, :, ch:]], axis=2)
    w2 = jnp.stack([jnp.concatenate([r_re[:, 0, :, :, ch:], -r_im[:, 0, :, :, ch:]], axis=2),
                    jnp.concatenate([r_re[:, 1, :, :, :w], -r_im[:, 1, :, :, :w]], axis=2)], axis=2)
    steps = jnp.arange(cs)
    qr, qi = powers(jnp.stack([cs - 1 - steps, steps]))
    qr = jnp.swapaxes(qr, -1, -2)[..., :, None, :]
    qi = jnp.swapaxes(qi, -1, -2)[..., :, None, :]
    bt_re = jnp.swapaxes(bb_re, -1, -2)[:, :, :, None]
    bt_im = jnp.swapaxes(bb_im, -1, -2)[:, :, :, None]
    cat = lambda a, b: jnp.concatenate([a, b], axis=-1)
    qr2, qi2 = cat(qr, qr), cat(qi, qi)
    ab = (qr2 * cat(bt_re, bt_im) + qi2 * cat(-bt_im, bt_re)).reshape(nl, 2, g, w, 2 * p)
    ab_sw = (qr2 * cat(bt_im, bt_re) + qi2 * cat(bt_re, -bt_im)).reshape(nl, 2, g, w, 2 * p)
    inj = jnp.concatenate([ab[:, 0], ab[:, 1], ab_sw[:, 0], ab_sw[:, 1]], axis=-1)
    ac_rows = []
    for d, slot in enumerate((cs, 0)):
        ar, ai = pr[:, d, :, :, slot], pi[:, d, :, :, slot]
        ac_rows += [jnp.concatenate([ar, ar], -1), jnp.concatenate([-ai, ai], -1)]
    ac = jnp.stack(ac_rows + [jnp.zeros_like(ac_rows[0])] * (SUBLANES - 4), axis=2)
    return base, inj.astype(BF16), w2.astype(BF16), ac


def _layer_norm(x, g, b):
    mu = jnp.mean(x, -1, keepdims=True)
    xc = x - mu
    var = jnp.mean(xc * xc, -1, keepdims=True)
    return xc * lax.rsqrt(var + LN_EPS) * g + b


def _route(logits_t):
    col = lambda i: logits_t[i : i + 1, :]
    gl = [col(i) for i in range(N_GROUPS)]
    gmax = functools.reduce(jnp.maximum, gl)
    g_idx = jnp.full_like(gmax, N_GROUPS - 1).astype(jnp.int32)
    for i in reversed(range(N_GROUPS - 1)):
        g_idx = jnp.where(gl[i] == gmax, i, g_idx)
    g_p = 1.0 / functools.reduce(lambda a, b: a + b, [jnp.exp(x - gmax) for x in gl])
    el = []
    for e in range(EXPERTS_PER_GROUP):
        v = col(N_GROUPS + (N_GROUPS - 1) * EXPERTS_PER_GROUP + e)
        for g in reversed(range(N_GROUPS - 1)):
            v = jnp.where(g_idx == g, col(N_GROUPS + g * EXPERTS_PER_GROUP + e), v)
        el.append(v)
    m1 = functools.reduce(jnp.maximum, el)
    i1 = jnp.full_like(g_idx, EXPERTS_PER_GROUP - 1)
    for e in reversed(range(EXPERTS_PER_GROUP - 1)):
        i1 = jnp.where(el[e] == m1, e, i1)
    rest = [jnp.where(i1 == e, -jnp.inf, el[e]) for e in range(EXPERTS_PER_GROUP)]
    m2 = functools.reduce(jnp.maximum, rest)
    i2 = jnp.full_like(g_idx, EXPERTS_PER_GROUP - 1)
    for e in reversed(range(EXPERTS_PER_GROUP - 1)):
        i2 = jnp.where((rest[e] == m2) & (i1 != e), e, i2)
    t = jnp.exp(m2 - m1)
    w1 = g_p / (1.0 + t)
    w2 = g_p * t / (1.0 + t)
    lo = jnp.minimum(i1, i2)
    hi = jnp.maximum(i1, i2)
    pair = jnp.where(hi == 1, 0, jnp.where(hi == 2, jnp.where(lo == 1, 1, 2), 3 + lo))
    bucket = g_idx * N_PAIRS + pair
    w_lower = jnp.where(i1 < i2, w1, w2)
    w_higher = jnp.where(i1 < i2, w2, w1)
    w_a = jnp.where(pair == 0, w_lower, w_higher)
    w_b = jnp.where(pair == 0, w_higher, w_lower)
    return bucket, w_a, w_b


def _merge_kernel(gate_ref, rg_ref, u_ref, hf_ref, hb_ref, rf_ref, rb_ref, yg_ref, x_ref, m_ref,
                  hgn_ref, gng_ref, gnb_ref, d_ref, gw_ref, gb_ref, wo_ref, lng_ref, lnb_ref,
                  wrh_ref, wrl_ref, br_ref, a128_ref, a64_ref, tri_ref, x1_ref, h2_ref, route_ref, cnt_ref, wt_scr,
                  *, alpha):
    @pl.when((pl.program_id(0) == 0) & (pl.program_id(1) == 0))
    def _():
        cnt_ref[...] = jnp.zeros_like(cnt_ref)

    o_hg = hf_ref[0] + hb_ref[0]
    ms = _dot2(o_hg * o_hg, a128_ref[...])
    hg = o_hg * lax.rsqrt(ms + LN_EPS) * hgn_ref[0] * _silu(gate_ref[0])
    o_rt = rf_ref[0] + rb_ref[0]
    mu = _dot2(o_rt, a64_ref[...])
    xc = o_rt - mu
    var = _dot2(xc * xc, a64_ref[...])
    rt = (xc * lax.rsqrt(var + LN_EPS) * gng_ref[0] + gnb_ref[0]) * _silu(rg_ref[0])
    rows_per_chunk = S5_CHUNK // SUBLANES
    n_chunks = yg_ref.shape[1]
    y5 = []
    for half in range(S5_WIDTH // LANES):
        for g_lo in range(SUBLANES):
            for s_hi in range(rows_per_chunk):
                wt_scr[pl.ds(s_hi * SUBLANES + g_lo, n_chunks, stride=S5_CHUNK), :] = (
                    yg_ref[half * SUBLANES + g_lo, :, s_hi * LANES : (s_hi + 1) * LANES])
        y5.append(_swap_sublane_lanegroup(wt_scr[...]))
    y5 = jnp.concatenate(y5, axis=-1)
    s5 = jax.nn.gelu(y5 + d_ref[0] * u_ref[0])
    s5 = s5 * jax.nn.sigmoid(_dot(s5.astype(BF16), gw_ref[0]) + gb_ref[0])
    cat = jnp.concatenate([hg, rt, s5], axis=-1).astype(BF16)
    y = _dot(cat, wo_ref[0])
    m = m_ref[0, 0, 0]
    x1 = _layer_norm(alpha * x_ref[0] + m[2:3, :] * y, lng_ref[0], lnb_ref[0])
    x1_ref[0] = x1
    h2 = x1 * (1.0 + m[4:5, :]) + m[3:4, :]
    h_hi, h_lo = _split_bf16(h2)
    logits_t = (_dot_nt(wrh_ref[0], h_hi) + _dot_nt(wrh_ref[0], h_lo) + _dot_nt(wrl_ref[0], h_hi)) + br_ref[0]
    bucket, _, _ = _route(logits_t)
    sub = lax.broadcasted_iota(jnp.int32, logits_t.shape, 0)
    h2_ref[0] = h2.reshape(h2_ref.shape[1:])
    onehot = sub == bucket
    before = _dot(onehot.astype(BF16), tri_ref[...]) + cnt_ref[...]
    rank = jnp.sum(jnp.where(onehot, before, 0.0), axis=0, keepdims=True)
    rsub = lax.broadcasted_iota(jnp.int32, route_ref.shape[1:], 0)
    route_ref[0] = jnp.where(rsub == 0, bucket.astype(F32), jnp.where(rsub == 1, rank, 0.0))
    cnt_ref[...] += jnp.sum(onehot.astype(F32), axis=1, keepdims=True)


def _merge(p, o_hf, o_hb, o_rf, o_rb, yg, x, modtab, layer_prm, const_prm, l, tm, nct, alpha):
    b, s, d = x.shape
    cpt = tm // S5_CHUNK

    def tok(width, col):
        return pl.BlockSpec((1, tm, width), lambda i, j: (i, j, col))

    def whole(a):
        return pl.BlockSpec(a.shape, lambda i, j: (0,) * a.ndim)

    rbase = 5 * HG_WIDTH // RET_WIDTH
    in_specs = [tok(HG_WIDTH, 2), tok(RET_WIDTH, rbase + 3), tok(S5_WIDTH, rbase + 4),
                tok(HG_WIDTH, 0), tok(HG_WIDTH, 0), tok(RET_WIDTH, 0), tok(RET_WIDTH, 0),
                pl.BlockSpec((S5_GROUPS, cpt, S5_FOLD), lambda i, j: (0, j, i)),
                tok(d, 0), _mod_spec(l, nct)]
    in_specs += [_layer_spec(a.shape[1:], l) for a in layer_prm]
    in_specs += [whole(a) for a in const_prm]
    return pl.pallas_call(
        functools.partial(_merge_kernel, alpha=alpha),
        grid=(b, s // tm),
        in_specs=in_specs,
        out_specs=[tok(d, 0), pl.BlockSpec((1, tm, d // LANES, LANES), lambda i, j: (i, j, 0, 0)),
                   pl.BlockSpec((1, SUBLANES, tm), lambda i, j: (i, 0, j)),
                   pl.BlockSpec((LANES, 1), lambda i, j: (0, 0))],
        out_shape=[jax.ShapeDtypeStruct((b, s, d), F32), jax.ShapeDtypeStruct((b, s, d // LANES, LANES), F32),
                   jax.ShapeDtypeStruct((b, SUBLANES, s), F32), jax.ShapeDtypeStruct((LANES, 1), F32)],
        scratch_shapes=[pltpu.VMEM((tm, LANES), F32)],
        compiler_params=_cparams(("arbitrary", "arbitrary")),
        name="merge_ln1_router",
    )(p, p, p, o_hf, o_hb, o_rf, o_rb, yg, x, modtab, *layer_prm, *const_prm)


def _routing_tables(route, counts, te, n_tiles):
    bucket = route[:, 0, :].astype(jnp.int32).reshape(-1)
    rank = route[:, 1, :].astype(jnp.int32).reshape(-1)
    cnt = counts[:N_BUCKETS, 0].astype(jnp.int32)
    padded = (cnt + te - 1) // te * te
    ends = jnp.cumsum(padded)
    pos = (ends - padded)[bucket] + rank
    n_used = ends[-1] // te
    tile = jnp.arange(n_tiles, dtype=jnp.int32)
    tb = jnp.minimum(jnp.searchsorted(ends, tile * te, side="right"), N_BUCKETS - 1).astype(jnp.int32)
    tb = jnp.where(tile < n_used, tb, tb[jnp.maximum(n_used - 1, 0)])
    group, pair = tb // N_PAIRS, tb % N_PAIRS
    ea = group * EXPERTS_PER_GROUP + jnp.asarray(PAIR_SLOT_A, jnp.int32)[pair]
    eb = group * EXPERTS_PER_GROUP + jnp.asarray(PAIR_SLOT_B, jnp.int32)[pair]
    return pos, ea, eb, n_used.reshape(1).astype(jnp.int32)


def _dispatch_kernel(pos_ref, src_ref, init_ref, out_ref, sem, *, tm, tiles_per_row):
    del init_ref
    base = (pl.program_id(0) * tiles_per_row + pl.program_id(1)) * tm

    def row_copy(r):
        return pltpu.make_async_copy(src_ref.at[0, r], out_ref.at[pos_ref[base + r]], sem)

    def start(g, carry):
        first = pl.multiple_of(g * DMA_UNROLL, DMA_UNROLL)
        for k in range(DMA_UNROLL):
            row_copy(first + k).start(priority=k % 2)
        return carry

    def wait(r, carry):
        row_copy(r).wait()
        return carry

    lax.fori_loop(0, tm // DMA_UNROLL, start, 0)
    lax.fori_loop(0, tm, wait, 0, unroll=DMA_UNROLL)


def _dispatch(pos, rows, init, tm):
    b, s, rt, rl = rows.shape
    return pl.pallas_call(
        functools.partial(_dispatch_kernel, tm=tm, tiles_per_row=s // tm),
        grid_spec=pltpu.PrefetchScalarGridSpec(
            num_scalar_prefetch=1,
            grid=(b, s // tm),
            in_specs=[pl.BlockSpec((1, tm, rt, rl), lambda i, j, pos_ref: (i, j, 0, 0)),
                      pl.BlockSpec(memory_space=pl.ANY)],
            out_specs=pl.BlockSpec(memory_space=pl.ANY),
            scratch_shapes=[pltpu.SemaphoreType.DMA(())],
        ),
        out_shape=jax.ShapeDtypeStruct(init.shape, F32),
        input_output_aliases={2: 0},
        compiler_params=_cparams(("arbitrary", "arbitrary")),
        name="moe_dispatch",
    )(pos, rows, init)


def _expert_kernel(ea_ref, eb_ref, nused_ref, hs_ref, wga_ref, wua_ref, wda_ref, wgb_ref, wub_ref, wdb_ref,
                   wr_ref, br_ref, o_ref, wg_scr, wu_scr, wd_scr):
    t = pl.program_id(0)
    prev = jnp.maximum(t - 1, 0)
    @pl.when((t == 0) | (ea_ref[t] != ea_ref[prev]))
    def _():
        wg_scr[0] = wga_ref[0, 0].astype(BF16)
        wu_scr[0] = wua_ref[0, 0].astype(BF16)
        wd_scr[0] = wda_ref[0, 0].astype(BF16)

    @pl.when((t == 0) | (eb_ref[t] != eb_ref[prev]))
    def _():
        wg_scr[1] = wgb_ref[0, 0].astype(BF16)
        wu_scr[1] = wub_ref[0, 0].astype(BF16)
        wd_scr[1] = wdb_ref[0, 0].astype(BF16)

    @pl.when(t < nused_ref[0])
    def _():
        d = wg_scr.shape[1]
        h32 = hs_ref[...].reshape(hs_ref.shape[0], d)
        h = h32.astype(BF16)
        logits_t = _dot_nt(wr_ref[0], h) + br_ref[0]
        sub = lax.broadcasted_iota(jnp.int32, logits_t.shape, 0)
        pick = lambda r: jnp.sum(jnp.where(sub == r, logits_t, 0.0), axis=0, keepdims=True)
        ea, eb = ea_ref[t], eb_ref[t]
        l_a, l_b = pick(N_GROUPS + ea), pick(N_GROUPS + eb)
        l_g = pick(lax.shift_right_logical(ea, EXPERTS_PER_GROUP.bit_length() - 1))
        g_sum = functools.reduce(lambda x, z: x + z, [jnp.exp(logits_t[i : i + 1, :] - l_g) for i in range(N_GROUPS)])
        m = jnp.maximum(l_a, l_b)
        t_a, t_b = jnp.exp(l_a - m), jnp.exp(l_b - m)
        inv = 1.0 / (g_sum * (t_a + t_b))
        w_cols = jnp.where(sub == 0, t_a * inv, jnp.where(sub == 1, t_b * inv, 0.0)).T
        y = None
        for e in range(2):
            act = (_silu(_dot(h, wg_scr[e])) * _dot(h, wu_scr[e])).astype(BF16)
            ye = w_cols[:, e : e + 1] * _dot(act, wd_scr[e])
            y = ye if y is None else y + ye
        o_ref[...] = y.reshape(o_ref.shape)

    @pl.when(t >= nused_ref[0])
    def _():
        o_ref[...] = jnp.zeros_like(o_ref)


def _experts(ea, eb, n_used, hs, wg, wu, wd, router, l, te):
    n_sorted, rt, rl = hs.shape
    _, _, d, eh = wg.shape

    def wspec(shape, which):
        return pl.BlockSpec((1, 1) + shape, lambda t, ea_ref, eb_ref, n_ref: (l, (ea_ref, eb_ref)[which][t], 0, 0))

    return pl.pallas_call(
        _expert_kernel,
        grid_spec=pltpu.PrefetchScalarGridSpec(
            num_scalar_prefetch=3,
            grid=(n_sorted // te,),
            in_specs=[pl.BlockSpec((te, rt, rl), lambda t, *_: (t, 0, 0)),
                      wspec((d, eh), 0), wspec((d, eh), 0), wspec((eh, d), 0),
                      wspec((d, eh), 1), wspec((d, eh), 1), wspec((eh, d), 1)]
                     + [_layer_spec(a.shape[1:], l) for a in router],
            out_specs=pl.BlockSpec((te, d // LANES, LANES), lambda t, *_: (t, 0, 0)),
            scratch_shapes=[pltpu.VMEM((2, d, eh), BF16), pltpu.VMEM((2, d, eh), BF16), pltpu.VMEM((2, eh, d), BF16)],
        ),
        out_shape=jax.ShapeDtypeStruct((n_sorted, d // LANES, LANES), F32),
        compiler_params=_cparams(("arbitrary",)),
        name="moe_experts",
    )(ea, eb, n_used, hs, wg, wu, wd, wg, wu, wd, *router)


def _gather_expert_rows(pos_ref, ys_ref, buf, sem, *, tm, tiles_per_row, first_tile):
    cols = pl.num_programs(1)
    n_steps = pl.num_programs(0) * cols
    step = pl.program_id(0) * cols + pl.program_id(1)
    slot = step % 2

    def row_copy(step_, slot_, r):
        tile = (step_ // cols) * tiles_per_row + first_tile + step_ % cols
        return pltpu.make_async_copy(ys_ref.at[pos_ref[tile * tm + r]], buf.at[slot_, r], sem.at[slot_])

    def start_tile(step_, slot_):
        def body(g, carry):
            first = pl.multiple_of(g * DMA_UNROLL, DMA_UNROLL)
            for k in range(DMA_UNROLL):
                row_copy(step_, slot_, first + k).start(priority=k % 2)
            return carry

        lax.fori_loop(0, tm // DMA_UNROLL, body, 0)

    @pl.when(step == 0)
    def _():
        start_tile(0, 0)

    @pl.when(step + 1 < n_steps)
    def _():
        start_tile(step + 1, 1 - slot)

    def wait(r, carry):
        row_copy(step, slot, r).wait()
        return carry

    lax.fori_loop(0, tm, wait, 0, unroll=DMA_UNROLL)
    return buf[slot].reshape(tm, buf.shape[2] * buf.shape[3])


def _combine_kernel(pos_ref, ys_ref, x1_ref, m_ref, lng_ref, lnb_ref, o_ref, buf, sem,
                    *, tm, tiles_per_row, first_tile, alpha):
    y = _gather_expert_rows(pos_ref, ys_ref, buf, sem, tm=tm, tiles_per_row=tiles_per_row, first_tile=first_tile)
    m = m_ref[0, 0, 0]
    o_ref[0] = _layer_norm(alpha * x1_ref[0] + m[5:6, :] * y, lng_ref[0], lnb_ref[0])


def _combine(pos, ys, x1, modtab, ln_g, ln_b, l, tm, nct, alpha, first_tile):
    b, s, d = x1.shape
    cols = s // tm - first_tile
    mod_spec = pl.BlockSpec((1, 1, 1, 6, d),
                            lambda i, j, pos_ref: (l, i, jnp.minimum((j + first_tile) // nct, 1), 0, 0))
    return pl.pallas_call(
        functools.partial(_combine_kernel, tm=tm, tiles_per_row=s // tm, first_tile=first_tile, alpha=alpha),
        grid_spec=pltpu.PrefetchScalarGridSpec(
            num_scalar_prefetch=1,
            grid=(b, cols),
            in_specs=[pl.BlockSpec(memory_space=pl.ANY),
                      pl.BlockSpec((1, tm, d), lambda i, j, pos_ref: (i, j + first_tile, 0)),
                      mod_spec, _layer_spec((1, d), l), _layer_spec((1, d), l)],
            out_specs=pl.BlockSpec((1, tm, d), lambda i, j, pos_ref: (i, j, 0)),
            scratch_shapes=[pltpu.VMEM((2, tm, d // LANES, LANES), F32), pltpu.SemaphoreType.DMA((2,))],
        ),
        out_shape=jax.ShapeDtypeStruct((b, cols * tm, d), F32),
        compiler_params=_cparams(("arbitrary", "arbitrary")),
        name="moe_combine_ln2",
    )(pos, ys, x1, modtab, ln_g, ln_b)


def _block_avg(width, group):
    idx = np.arange(width) // group
    return jnp.asarray((idx[:, None] == idx[None, :]).astype(np.float32) / group, dtype=BF16)


def kernel(x, c, ctx, c_ctx, w_mod, b_mod, w_in, hg_lb_raw, hg_norm_g, ret_decay_raw, ret_gn_g, ret_gn_b, s5_lam_re, s5_lam_im, s5_log_dt, s5_b_re, s5_b_im, s5_c_re, s5_c_im, s5_d, s5_glu_w, s5_glu_b, w_out, ln1_g, ln1_b, ln2_g, ln2_b, rg_w, rg_b, re_w, re_b, exp_w_gate, exp_w_up, exp_w_down):
    bsz, t_lat, d = x.shape
    t_ctx = ctx.shape[1]
    depth = w_mod.shape[0]
    alpha = (2.0 * depth) ** 0.25
    tm = TOKEN_TILE
    assert d == D_MODEL and t_lat % GRID_W == 0 and bsz < SUBLANES
    assert t_ctx % tm == 0 and t_lat % tm == 0, "context and latent lengths must be multiples of the token tile"
    nct = t_ctx // tm
    s = t_ctx + t_lat

    cvec = jnp.concatenate([c, c_ctx[None, :], jnp.zeros((SUBLANES - bsz - 1, d), F32)], 0)
    mod_all = _modulation(cvec, w_mod, b_mod)
    lat = mod_all[:, :bsz].reshape(depth, bsz, 6, d)
    cm = jnp.broadcast_to(mod_all[:, bsz].reshape(depth, 1, 6, d), (depth, bsz, 6, d))
    modtab = jnp.stack([cm, lat], axis=2)

    hg_lb = jnp.cumsum(jax.nn.softmax(hg_lb_raw.astype(F32), axis=0), axis=0)
    hg_lb = hg_lb - hg_lb[:1]
    ret_tables = _retention_decay_tables(jax.nn.log_sigmoid(ret_decay_raw.astype(F32)))
    cos_tab, sin_tab = _rope_tables(t_lat, t_ctx)
    s5_tabs = _s5_weights(s5_lam_re, s5_lam_im, s5_log_dt, s5_b_re, s5_b_im, s5_c_re, s5_c_im)
    pad_r = LANES - N_GROUPS - N_EXPERTS
    wr = jnp.concatenate([rg_w, re_w.reshape(depth, d, N_EXPERTS), jnp.zeros((depth, d, pad_r), F32)], axis=2)
    wr = jnp.swapaxes(wr, 1, 2)
    wr_hi, wr_lo = _split_bf16(wr)
    br = jnp.concatenate([rg_b, re_b.reshape(depth, N_EXPERTS), jnp.zeros((depth, pad_r), F32)], axis=1)[:, :, None]
    row = lambda a: a[:, None, :]
    layer_prm = [row(jnp.tile(hg_norm_g, (1, HG_HEADS))), row(ret_gn_g), row(ret_gn_b), row(s5_d),
                 s5_glu_w.astype(BF16), row(s5_glu_b), w_out.astype(BF16), row(ln1_g), row(ln1_b),
                 wr_hi, wr_lo, br]
    tri = jnp.asarray(np.triu(np.ones((tm, tm), np.float32), 1), dtype=BF16)
    const_prm = [_block_avg(HG_WIDTH, HG_DK), _block_avg(RET_WIDTH, RET_DK), tri]
    w_in_bf16 = w_in.astype(BF16)
    ln2_g, ln2_b = row(ln2_g), row(ln2_b)
    te = EXPERT_TILE
    n_tiles = -(-(bsz * s + N_BUCKETS * (te - 1)) // te)

    hs = jnp.zeros((n_tiles * te, d // LANES, LANES), F32)
    xs = jnp.concatenate([ctx, x], axis=1)
    for l in range(depth):
        if l == 0:
            p, ug = _inproj(xs, modtab, w_in_bf16, l, tm, nct)
        else:
            xs, p, ug = _ln2_inproj(pos, ys, x1, modtab, ln2_g, ln2_b, w_in_bf16, l, tm, nct, alpha)
        o_hf, o_hb = _hgrn(p, hg_lb, l, t_ctx // CHUNK)
        o_rf, o_rb = _retention(p, cos_tab, sin_tab, ret_tables, l, t_ctx // RET_CHUNK)
        yg = _s5_conv(ug, *s5_tabs, l, bsz, t_ctx // S5_CHUNK)
        x1, rows, route, counts = _merge(p, o_hf, o_hb, o_rf, o_rb, yg, xs, modtab, layer_prm, const_prm,
                                         l, tm, nct, alpha)
        pos, ea, eb, n_used = _routing_tables(route, counts, te, n_tiles)
        hs = _dispatch(pos, rows, hs, tm)
        ys = _experts(ea, eb, n_used, hs, exp_w_gate, exp_w_up, exp_w_down, (wr_hi, br), l, te)
    return _combine(pos, ys, x1, modtab, ln2_g, ln2_b, depth - 1, tm, nct, alpha, nct)
```

```python
import functools

import numpy as np
import jax
import jax.numpy as jnp
from jax import lax
from jax.experimental import pallas as pl
from jax.experimental.pallas import tpu as pltpu

F32 = jnp.float32
BF16 = jnp.bfloat16

D_MODEL = 1024
HG_WIDTH = 512
HG_HEADS = 4
HG_DK = HG_WIDTH // HG_HEADS
RET_WIDTH = 256
RET_HEADS = 4
RET_DK = RET_WIDTH // RET_HEADS
S5_WIDTH = 256
S5_GROUP_CH = 16
S5_GROUPS = S5_WIDTH // S5_GROUP_CH
S5_STATE = 64
IN_COLS = 5 * HG_WIDTH + 4 * RET_WIDTH + S5_WIDTH
CHUNK = 256
HGRN_BATCH = 4
RET_CHUNK = 256
RET_BATCH = 4
N_GROUPS = 4
EXPERTS_PER_GROUP = 4
N_EXPERTS = N_GROUPS * EXPERTS_PER_GROUP
N_PAIRS = EXPERTS_PER_GROUP * (EXPERTS_PER_GROUP - 1) // 2
N_BUCKETS = N_GROUPS * N_PAIRS
PAIR_SLOT_A = (0, 2, 2, 3, 3, 3)
PAIR_SLOT_B = (1, 1, 0, 0, 1, 2)
EXPERT_HIDDEN = D_MODEL // 2
LN_EPS = 1e-5
ROPE_BASE = 10000.0
GRID_W = 64

LANES = 128
SUBLANES = 8
TOKEN_TILE = 256
S5_CHUNK = 32
S5_FOLD = S5_CHUNK * S5_GROUP_CH
EXPERT_TILE = 256
DMA_UNROLL = 8
VMEM_LIMIT = 56 * 1024 * 1024


def _cparams(sem):
    return pltpu.CompilerParams(dimension_semantics=sem, vmem_limit_bytes=VMEM_LIMIT)


def _split_bf16(x):
    hi = x.astype(BF16)
    lo = (x - hi.astype(F32)).astype(BF16)
    return hi, lo


def _dot(a, b):
    return jnp.dot(a, b, preferred_element_type=F32)


def _dot3(a, b):
    ah, al = _split_bf16(a)
    bh, bl = _split_bf16(b)
    return _dot(ah, bh) + _dot(ah, bl) + _dot(al, bh)


def _dot2(a, b_bf16):
    ah, al = _split_bf16(a)
    return _dot(ah, b_bf16) + _dot(al, b_bf16)


def _dot_nt(a, b):
    return lax.dot_general(a, b, (((1,), (1,)), ((), ())), preferred_element_type=F32)


def _dot_tn(a, b):
    return lax.dot_general(a, b, (((0,), (0,)), ((), ())), preferred_element_type=F32)


def _silu(x):
    return x * jax.nn.sigmoid(x)


def _layer_spec(shape, l):
    zeros = (0,) * len(shape)
    return pl.BlockSpec((1,) + tuple(shape), lambda *_: (l,) + zeros)


def _mod_spec(l, nct):
    return pl.BlockSpec((1, 1, 1, 6, D_MODEL), lambda i, j, *_: (l, i, jnp.minimum(j // nct, 1), 0, 0))


def _swap_sublane_lanegroup(v):
    n = v.shape[0]
    r = lax.broadcasted_iota(jnp.int32, v.shape, 0)
    l = lax.broadcasted_iota(jnp.int32, v.shape, 1)
    for k in range(3):
        rb = (r >> k) & 1
        gb = (l >> (4 + k)) & 1
        sh = S5_GROUP_CH << k
        st = 1 << k
        a = pltpu.roll(pltpu.roll(v, LANES - sh, 1), st, 0)
        b = pltpu.roll(pltpu.roll(v, sh, 1), n - st, 0)
        v = jnp.where(rb == gb, v, jnp.where(rb == 1, a, b))
    return v


def _mod_kernel(c_ref, w_ref, b_ref, o_ref):
    sc = _silu(c_ref[...])
    o_ref[0] = _dot3(sc, w_ref[0]) + b_ref[0]


def _modulation(cvec, w_mod, b_mod):
    depth, d, n = w_mod.shape
    rows = cvec.shape[0]
    tn = 1536
    return pl.pallas_call(
        _mod_kernel,
        grid=(depth, n // tn),
        in_specs=[
            pl.BlockSpec((rows, d), lambda l, j: (0, 0)),
            pl.BlockSpec((1, d, tn), lambda l, j: (l, 0, j)),
            pl.BlockSpec((1, 1, tn), lambda l, j: (l, 0, j)),
        ],
        out_specs=pl.BlockSpec((1, rows, tn), lambda l, j: (l, 0, j)),
        out_shape=jax.ShapeDtypeStruct((depth, rows, n), F32),
        compiler_params=_cparams(("arbitrary", "arbitrary")),
        name="modulation",
    )(cvec, w_mod, b_mod.reshape(depth, 1, n))


def _inproj_kernel(x_ref, m_ref, w_ref, o_ref, ug_ref, wt_scr):
    _project(x_ref[0], m_ref, w_ref, o_ref, ug_ref, wt_scr)


def _ln2_inproj_kernel(pos_ref, ys_ref, x1_ref, mprev_ref, lng_ref, lnb_ref, m_ref, w_ref,
                       x2_ref, o_ref, ug_ref, buf, sem, wt_scr, *, tm, tiles_per_row, alpha):
    y = _gather_expert_rows(pos_ref, ys_ref, buf, sem, tm=tm, tiles_per_row=tiles_per_row, first_tile=0)
    g2 = mprev_ref[0, 0, 0][5:6, :]
    x2 = _layer_norm(alpha * x1_ref[0] + g2 * y, lng_ref[0], lnb_ref[0])
    x2_ref[0] = x2
    _project(x2, m_ref, w_ref, o_ref, ug_ref, wt_scr)


def _project(x, m_ref, w_ref, o_ref, ug_ref, wt_scr):
    m = m_ref[0, 0, 0]
    h = x * (1.0 + m[1:2, :]) + m[0:1, :]
    p = _dot(h.astype(BF16), w_ref[0])
    o_ref[0] = p
    rows_per_chunk = S5_CHUNK // SUBLANES
    n_chunks = p.shape[0] // S5_CHUNK
    for half in range(S5_WIDTH // LANES):
        lo = IN_COLS - S5_WIDTH + half * LANES
        wt_scr[...] = _swap_sublane_lanegroup(p[:, lo : lo + LANES])
        for g_lo in range(SUBLANES):
            for s_hi in range(rows_per_chunk):
                piece = wt_scr[pl.ds(s_hi * SUBLANES + g_lo, n_chunks, stride=S5_CHUNK), :]
                ug_ref[half * SUBLANES + g_lo, :, s_hi * LANES : (s_hi + 1) * LANES] = piece


def _inproj(x, modtab, w_in_bf16, l, tm, nct):
    b, s, d = x.shape
    n = w_in_bf16.shape[-1]
    cpt = tm // S5_CHUNK
    return pl.pallas_call(
        _inproj_kernel,
        grid=(b, s // tm),
        in_specs=[
            pl.BlockSpec((1, tm, d), lambda i, j: (i, j, 0)),
            _mod_spec(l, nct),
            _layer_spec((d, n), l),
        ],
        out_specs=[pl.BlockSpec((1, tm, n), lambda i, j: (i, j, 0)),
                   pl.BlockSpec((S5_GROUPS, cpt, S5_FOLD), lambda i, j: (0, j, i))],
        out_shape=[jax.ShapeDtypeStruct((b, s, n), F32),
                   jax.ShapeDtypeStruct((S5_GROUPS, s // S5_CHUNK, b * S5_FOLD), F32)],
        scratch_shapes=[pltpu.VMEM((tm, LANES), F32)],
        compiler_params=_cparams(("arbitrary", "arbitrary")),
        name="inproj",
    )(x, modtab, w_in_bf16)


def _ln2_inproj(pos, ys, x1, modtab, ln_g, ln_b, w_in_bf16, l, tm, nct, alpha):
    b, s, d = x1.shape
    n = w_in_bf16.shape[-1]
    cpt = tm // S5_CHUNK
    tok = lambda width: pl.BlockSpec((1, tm, width), lambda i, j, pos_ref: (i, j, 0))
    return pl.pallas_call(
        functools.partial(_ln2_inproj_kernel, tm=tm, tiles_per_row=s // tm, alpha=alpha),
        grid_spec=pltpu.PrefetchScalarGridSpec(
            num_scalar_prefetch=1,
            grid=(b, s // tm),
            in_specs=[pl.BlockSpec(memory_space=pl.ANY), tok(d),
                      _mod_spec(l - 1, nct), _layer_spec((1, d), l - 1), _layer_spec((1, d), l - 1),
                      _mod_spec(l, nct), _layer_spec((d, n), l)],
            out_specs=[tok(d), tok(n),
                       pl.BlockSpec((S5_GROUPS, cpt, S5_FOLD), lambda i, j, pos_ref: (0, j, i))],
            scratch_shapes=[pltpu.VMEM((2, tm, d // LANES, LANES), F32), pltpu.SemaphoreType.DMA((2,)),
                            pltpu.VMEM((tm, LANES), F32)],
        ),
        out_shape=[jax.ShapeDtypeStruct((b, s, d), F32), jax.ShapeDtypeStruct((b, s, n), F32),
                   jax.ShapeDtypeStruct((S5_GROUPS, s // S5_CHUNK, b * S5_FOLD), F32)],
        compiler_params=_cparams(("arbitrary", "arbitrary")),
        name="ln2_inproj",
    )(pos, ys, x1, modtab, ln_g, ln_b, modtab, w_in_bf16)


def _block_gate_products(f, reverse):
    c = f.shape[0]
    row = lax.broadcasted_iota(jnp.int32, (c, 1), 0)
    a, z, b = f, f, jnp.ones_like(f)
    out = []
    s = 1
    while s < c:
        out.append((a, b))
        if s < SUBLANES:
            z3 = z.reshape(c // SUBLANES, SUBLANES, z.shape[1])
            up = pltpu.roll(z3, s, 1).reshape(z.shape)
            dn = pltpu.roll(z3, SUBLANES - s, 1).reshape(z.shape)
            odd = (row & s) != 0
            if reverse:
                a = a * jnp.where(odd, 1.0, dn)
                b = b * jnp.where(odd, up, 1.0)
            else:
                a = a * jnp.where(odd, up, 1.0)
                b = b * jnp.where(odd, 1.0, dn)
            z = z * jnp.where(odd, up, dn)
        else:
            na, nb, nz = [], [], []
            for lo in range(0, c, 2 * s):
                ev, od = slice(lo, lo + s), slice(lo + s, lo + 2 * s)
                zz = z[ev] * z[od]
                if reverse:
                    na += [a[ev] * z[od], a[od]]
                    nb += [b[ev], b[od] * z[ev]]
                else:
                    na += [a[ev], a[od] * z[ev]]
                    nb += [b[ev] * z[od], b[od]]
                nz += [zz, zz]
            a, b, z = (jnp.concatenate(t, axis=0) for t in (na, nb, nz))
        s *= 2
    out.append((a, b))
    return out, z


def _hgrn_kernel(qf_ref, vf_ref, zf_ref, qb_ref, vb_ref, zb_ref, lb_ref, of_ref, ob_ref, st_ref):
    c = CHUNK

    @pl.when(pl.program_id(1) == 0)
    def _():
        st_ref[...] = jnp.zeros_like(st_ref)

    ri = lax.broadcasted_iota(jnp.int32, (c, c), 0)
    ci = lax.broadcasted_iota(jnp.int32, (c, c), 1)
    for d, (q_ref, v_ref, z_ref, o_ref) in enumerate(
        ((qf_ref, vf_ref, zf_ref, of_ref), (qb_ref, vb_ref, zb_ref, ob_ref))
    ):
        reverse = d == 1
        causal = (ri < ci) if reverse else (ri > ci)
        n_levels = c.bit_length() - 1
        masks = [(((ri >> lvl) ^ (ci >> lvl)) == 1) & causal for lvl in range(n_levels)]
        for bi, h in [(bi, h) for bi in range(q_ref.shape[0]) for h in range(HG_HEADS)]:
            hs = slice(h * HG_DK, (h + 1) * HG_DK)
            q = _silu(q_ref[bi, :, hs])
            v = v_ref[bi, :, hs].astype(BF16)
            lb = lb_ref[0, d : d + 1, hs]
            f = lb + (1.0 - lb) * jax.nn.sigmoid(z_ref[bi, :, hs])
            k = 1.0 - f
            levels, tot = _block_gate_products(f, reverse)
            scores = jnp.where(ri == ci, _dot_nt(q.astype(BF16), k.astype(BF16)), 0.0)
            for mask, (a, bb) in zip(masks, levels):
                scores = jnp.where(mask, _dot_nt((q * a).astype(BF16), (k * bb).astype(BF16)), scores)
            a_full, b_full = levels[-1]
            st = st_ref[bi, d, h]
            o_ref[bi, :, hs] = (_dot(scores.astype(BF16), v)
                                + _dot_nt((q * a_full).astype(BF16), st.astype(BF16)))
            st_ref[bi, d, h] = st * tot[0:1, :] + _dot_tn(v, (k * b_full).astype(BF16))


def _bwd_chunk(n, nc_ctx, nc_all):
    return jnp.where(n < nc_ctx, nc_ctx - 1 - n, nc_all + nc_ctx - 1 - n)


def _hgrn(p, lb_all, l, nc_ctx):
    b, s, _ = p.shape
    nc = s // CHUNK
    w = HG_WIDTH

    bb = HGRN_BATCH if b % HGRN_BATCH == 0 else 1

    def fwd(col):
        return pl.BlockSpec((bb, CHUNK, w), lambda i, n: (i, n, col))

    def bwd(col):
        return pl.BlockSpec((bb, CHUNK, w), lambda i, n: (i, _bwd_chunk(n, nc_ctx, nc), col))

    out = jax.ShapeDtypeStruct((b, s, w), F32)
    return pl.pallas_call(
        _hgrn_kernel,
        grid=(b // bb, nc),
        in_specs=[fwd(0), fwd(1), fwd(3), bwd(0), bwd(1), bwd(4), _layer_spec((2, w), l)],
        out_specs=[fwd(0), bwd(0)],
        out_shape=[out, out],
        scratch_shapes=[pltpu.VMEM((bb, 2, HG_HEADS, HG_DK, HG_DK), F32)],
        compiler_params=_cparams(("arbitrary", "arbitrary")),
        name="hgrn2_scan",
    )(p, p, p, p, p, p, lb_all)


def _swap_halves(x, half):
    n = x.shape[-1]
    lane = lax.broadcasted_iota(jnp.int32, (1, n), 1)
    lower = (lane & half) == 0
    return jnp.where(lower, pltpu.roll(x, n - half, 1), pltpu.roll(x, half, 1))


def _ret_kernel(qf_ref, kf_ref, vf_ref, cf_ref, sf_ref, qb_ref, kb_ref, vb_ref, cb_ref, sb_ref,
                dmat_ref, rq_ref, rk_ref, cd_ref, of_ref, ob_ref, st_ref):
    @pl.when(pl.program_id(1) == 0)
    def _():
        st_ref[...] = jnp.zeros_like(st_ref)

    half = RET_DK // 4
    for d, (q_ref, k_ref, v_ref, c_ref, s_ref, o_ref) in enumerate(
        ((qf_ref, kf_ref, vf_ref, cf_ref, sf_ref, of_ref), (qb_ref, kb_ref, vb_ref, cb_ref, sb_ref, ob_ref))
    ):
        cos = c_ref[...]
        sin = s_ref[...]
        cd = cd_ref[0, d : d + 1, :]
        lane = lax.broadcasted_iota(jnp.int32, (1, LANES), 1)
        left = lane < RET_DK
        same_head = (lax.broadcasted_iota(jnp.int32, (LANES, 1), 0) < RET_DK) == left
        zero = jnp.zeros((), BF16)
        for bi in range(q_ref.shape[0]):
            q = q_ref[bi]
            k = k_ref[bi] * (RET_DK ** -0.5)
            q = q * cos + _swap_halves(q, half) * sin
            k = k * cos + _swap_halves(k, half) * sin
            v = v_ref[bi]
            vt = v.T.astype(BF16)
            v = v.astype(BF16)
            q0 = q.astype(BF16)
            k0 = k.astype(BF16)
            qd = (q * rq_ref[0, d]).astype(BF16)
            kd = (k * rk_ref[0, d]).astype(BF16)
            for p in range(RET_HEADS // 2):
                ps = slice(p * LANES, (p + 1) * LANES)
                kp, vp = k0[:, ps], v[:, ps]
                k_blk = jnp.concatenate([jnp.where(left, kp, zero), jnp.where(left, zero, kp)], axis=0)
                v_blk = jnp.concatenate([jnp.where(left, vp, zero), jnp.where(left, zero, vp)], axis=0)
                scores = _dot_nt(q0[:, ps], k_blk) * dmat_ref[0, d, p]
                st = st_ref[bi, d, p]
                o_ref[bi, :, ps] = _dot(scores.astype(BF16), v_blk) + _dot_nt(qd[:, ps], st.astype(BF16))
                st_ref[bi, d, p] = st * cd[:, ps] + jnp.where(same_head, _dot(vt[ps, :], kd[:, ps]), 0.0)


def _retention(p, cos_tab, sin_tab, tables, l, nc_ctx):
    b, s, _ = p.shape
    nc = s // RET_CHUNK
    w = RET_WIDTH
    base = 5 * HG_WIDTH // w

    bb = RET_BATCH if b % RET_BATCH == 0 else 1

    def fwd(col):
        return pl.BlockSpec((bb, RET_CHUNK, w), lambda i, n: (i, n, col))

    def bwd(col):
        return pl.BlockSpec((bb, RET_CHUNK, w), lambda i, n: (i, _bwd_chunk(n, nc_ctx, nc), col))

    tab_f = pl.BlockSpec((RET_CHUNK, w), lambda i, n: (n, 0))
    tab_b = pl.BlockSpec((RET_CHUNK, w), lambda i, n: (_bwd_chunk(n, nc_ctx, nc), 0))
    out = jax.ShapeDtypeStruct((b, s, w), F32)
    return pl.pallas_call(
        _ret_kernel,
        grid=(b // bb, nc),
        in_specs=[fwd(base), fwd(base + 1), fwd(base + 2), tab_f, tab_f,
                  bwd(base), bwd(base + 1), bwd(base + 2), tab_b, tab_b]
                 + [_layer_spec(t.shape[1:], l) for t in tables],
        out_specs=[fwd(0), bwd(0)],
        out_shape=[out, out],
        scratch_shapes=[pltpu.VMEM((bb, 2, RET_HEADS // 2, LANES, LANES), F32)],
        compiler_params=_cparams(("arbitrary", "arbitrary")),
        name="retention_scan",
    )(p, p, p, cos_tab, sin_tab, p, p, p, cos_tab, sin_tab, *tables)


def _rope_tables(t_lat, t_ctx):
    m = RET_DK // 4
    inv = ROPE_BASE ** (-jnp.arange(m, dtype=F32) / m)
    rows = jnp.repeat(jnp.arange(t_lat // GRID_W, dtype=jnp.int32), GRID_W).astype(F32)
    cols = jnp.tile(jnp.arange(GRID_W, dtype=jnp.int32), t_lat // GRID_W).astype(F32)

    def half_tables(pos):
        ang = pos[:, None] * inv
        c, s = jnp.cos(ang), jnp.sin(ang)
        return jnp.concatenate([c, c], -1), jnp.concatenate([-s, s], -1)

    cr, sr = half_tables(rows)
    cc, sc = half_tables(cols)
    cos_h = jnp.concatenate([cr, cc], -1)
    sin_h = jnp.concatenate([sr, sc], -1)
    cos = jnp.tile(cos_h, (1, RET_HEADS))
    sin = jnp.tile(sin_h, (1, RET_HEADS))
    cos = jnp.concatenate([jnp.ones((t_ctx, RET_WIDTH), F32), cos], 0)
    sin = jnp.concatenate([jnp.zeros((t_ctx, RET_WIDTH), F32), sin], 0)
    return cos, sin


def _retention_decay_tables(log_gamma):
    c = RET_CHUNK
    i = jnp.arange(c, dtype=F32)
    diff = i[:, None] - i[None, :]
    lg = log_gamma[:, :, :, None, None]
    d_f = jnp.where(diff >= 0, jnp.exp(lg[:, 0] * diff), 0.0)
    d_b = jnp.where(diff <= 0, jnp.exp(lg[:, 1] * (-diff)), 0.0)
    dmat = jnp.stack([d_f, d_b], 1)
    dmat = jnp.concatenate([dmat[:, :, 0::2], dmat[:, :, 1::2]], axis=-1)
    lane_lg = jnp.repeat(log_gamma, RET_DK, axis=2)[:, :, None, :]
    col = i[None, :, None]
    rq = jnp.stack([jnp.exp(lane_lg[:, 0] * (col + 1.0)), jnp.exp(lane_lg[:, 1] * (c - col))], 1)
    rk = jnp.stack([jnp.exp(lane_lg[:, 0] * (c - 1.0 - col)), jnp.exp(lane_lg[:, 1] * col)], 1)
    cdec = jnp.exp(lane_lg[:, :, 0, :] * c)
    return dmat, rq, rk, cdec


def _s5_kernel(u_ref, base_ref, inj_ref, w2_ref, ac_ref, y_ref, toep_scr, a_scr, i_scr, hf_scr, hb_scr,
               *, n_ctx, n_all, bsz):
    width = S5_FOLD
    st2 = 2 * S5_STATE
    ch = S5_GROUP_CH
    lane = lax.broadcasted_iota(jnp.int32, (ch, width), 1)
    base_f = base_ref[0, 0, 0]
    base_b = base_ref[0, 0, 1]
    for s in range(S5_CHUNK):
        sh_f = s * ch
        sh_b = (S5_CHUNK - 1 - s) * ch
        part_f = jnp.where(lane >= sh_f, pltpu.roll(base_f, sh_f, 1), 0.0) if sh_f else base_f
        part_b = jnp.where(lane < width - sh_b, pltpu.roll(base_b, width - sh_b, 1), 0.0) if sh_b else base_b
        toep_scr[s * ch : (s + 1) * ch, :] = (part_f + part_b).astype(BF16)
    for b in range(bsz):
        ub = u_ref[0, :, b * width : (b + 1) * width].astype(BF16)
        a_scr[b] = _dot(ub, toep_scr[...])
        i_scr[b] = _dot(ub, inj_ref[0, 0])
    ac = ac_ref[0, 0]

    zero = jnp.zeros((1, st2), F32)
    state = [(zero, zero)] * (2 * bsz)
    for n in range(n_all):
        nb = n_ctx - 1 - n if n < n_ctx else n_all + n_ctx - 1 - n
        for b in range(bsz):
            for d, (row, scr) in enumerate(((n, hf_scr), (nb, hb_scr))):
                h, hsw = state[2 * b + d]
                scr[b, row : row + 1, :] = h
                inj = i_scr[b, row : row + 1, d * st2 : (d + 1) * st2]
                inj_sw = i_scr[b, row : row + 1, (d + 2) * st2 : (d + 3) * st2]
                ar, ai = ac[2 * d : 2 * d + 1, :], ac[2 * d + 1 : 2 * d + 2, :]
                state[2 * b + d] = (h * ar + hsw * ai + inj, hsw * ar - h * ai + inj_sw)
    for b in range(bsz):
        y_ref[0, :, b * width : (b + 1) * width] = (
            a_scr[b]
            + _dot(hf_scr[b].astype(BF16), w2_ref[0, 0, 0])
            + _dot(hb_scr[b].astype(BF16), w2_ref[0, 0, 1]))


def _s5_conv(ug, base, inj, w2, ac, l, bsz, n_ctx):
    g, n_all, _ = ug.shape
    width = S5_FOLD
    st2 = 2 * S5_STATE
    return pl.pallas_call(
        functools.partial(_s5_kernel, n_ctx=n_ctx, n_all=n_all, bsz=bsz),
        grid=(g,),
        in_specs=[
            pl.BlockSpec((1, n_all, bsz * width), lambda i: (i, 0, 0)),
            pl.BlockSpec((1, 1, 2, S5_GROUP_CH, width), lambda i: (l, i, 0, 0, 0)),
            pl.BlockSpec((1, 1, width, 4 * st2), lambda i: (l, i, 0, 0)),
            pl.BlockSpec((1, 1, 2, st2, width), lambda i: (l, i, 0, 0, 0)),
            pl.BlockSpec((1, 1, SUBLANES, st2), lambda i: (l, i, 0, 0)),
        ],
        out_specs=pl.BlockSpec((1, n_all, bsz * width), lambda i: (i, 0, 0)),
        out_shape=jax.ShapeDtypeStruct((g, n_all, bsz * width), F32),
        scratch_shapes=[pltpu.VMEM((width, width), BF16),
                        pltpu.VMEM((bsz, n_all, width), F32), pltpu.VMEM((bsz, n_all, 4 * st2), F32),
                        pltpu.VMEM((bsz, n_all, st2), F32), pltpu.VMEM((bsz, n_all, st2), F32)],
        compiler_params=_cparams(("arbitrary",)),
        name="s5_conv",
    )(ug, base, inj, w2, ac)


def _s5_weights(lam_re, lam_im, log_dt, b_re, b_im, c_re, c_im):
    cs, ch, p, g = S5_CHUNK, S5_GROUP_CH, S5_STATE, S5_GROUPS
    nl = lam_re.shape[0]
    lam_re = jnp.minimum(lam_re.astype(F32), -1e-4)
    lam_im = lam_im.astype(F32)
    dt = jnp.exp(log_dt.astype(F32))[..., None]
    mag = jnp.exp(dt * lam_re)
    abar_re, abar_im = mag * jnp.cos(dt * lam_im), mag * jnp.sin(dt * lam_im)
    den = jnp.square(lam_re) + jnp.square(lam_im)
    nr, ni = abar_re - 1.0, abar_im
    coef_re = ((nr * lam_re + ni * lam_im) / den)[..., None]
    coef_im = ((ni * lam_re - nr * lam_im) / den)[..., None]
    b_re, b_im = b_re.astype(F32), b_im.astype(F32)
    bb_re = coef_re * b_re - coef_im * b_im
    bb_im = coef_re * b_im + coef_im * b_re
    ct_re = jnp.swapaxes(c_re.astype(F32), -1, -2)
    ct_im = jnp.swapaxes(c_im.astype(F32), -1, -2)

    def powers(expo):
        e = expo.astype(F32)[None, :, None, None, :]
        m = jnp.exp(e * (dt * lam_re)[..., None])
        return m * jnp.cos(e * (dt * lam_im)[..., None]), m * jnp.sin(e * (dt * lam_im)[..., None])

    slots = jnp.arange(cs + 1)
    pr, pi = powers(jnp.stack([slots, cs - slots]))
    r_re = (pr[..., None] * ct_re[..., None, :] - pi[..., None] * ct_im[..., None, :]).reshape(nl, 2, g, p, (cs + 1) * ch)
    r_im = (pr[..., None] * ct_im[..., None, :] + pi[..., None] * ct_re[..., None, :]).reshape(nl, 2, g, p, (cs + 1) * ch)
    hi = lax.Precision.HIGHEST
    kern = (jnp.einsum('ldgpe,ldgpn->ldgen', bb_re, r_re, precision=hi)
            - jnp.einsum('ldgpe,ldgpn->ldgen', bb_im, r_im, precision=hi))
    w = cs * ch
    base = jnp.stack([kern[:, 0, :, :, :w], kern[:, 1, :, :, ch:]], axis=2)
    w2 = jnp.stack([jnp.concatenate([r_re[:, 0, :, :, ch:], -r_im[:, 0, :, :, ch:]], axis=2),
                    jnp.concatenate([r_re[:, 1, :, :, :w], -r_im[:, 1, :, :, :w]], axis=2)], axis=2)
    steps = jnp.arange(cs)
    qr, qi = powers(jnp.stack([cs - 1 - steps, steps]))
    qr = jnp.swapaxes(qr, -1, -2)[..., :, None, :]
    qi = jnp.swapaxes(qi, -1, -2)[..., :, None, :]
    bt_re = jnp.swapaxes(bb_re, -1, -2)[:, :, :, None]
    bt_im = jnp.swapaxes(bb_im, -1, -2)[:, :, :, None]
    cat = lambda a, b: jnp.concatenate([a, b], axis=-1)
    qr2, qi2 = cat(qr, qr), cat(qi, qi)
    ab = (qr2 * cat(bt_re, bt_im) + qi2 * cat(-bt_im, bt_re)).reshape(nl, 2, g, w, 2 * p)
    ab_sw = (qr2 * cat(bt_im, bt_re) + qi2 * cat(bt_re, -bt_im)).reshape(nl, 2, g, w, 2 * p)
    inj = jnp.concatenate([ab[:, 0], ab[:, 1], ab_sw[:, 0], ab_sw[:, 1]], axis=-1)
    ac_rows = []
    for d, slot in enumerate((cs, 0)):
        ar, ai = pr[:, d, :, :, slot], pi[:, d, :, :, slot]
        ac_rows += [jnp.concatenate([ar, ar], -1), jnp.concatenate([-ai, ai], -1)]
    ac = jnp.stack(ac_rows + [jnp.zeros_like(ac_rows[0])] * (SUBLANES - 4), axis=2)
    return base, inj.astype(BF16), w2.astype(BF16), ac


def _layer_norm(x, g, b):
    mu = jnp.mean(x, -1, keepdims=True)
    xc = x - mu
    var = jnp.mean(xc * xc, -1, keepdims=True)
    return xc * lax.rsqrt(var + LN_EPS) * g + b


def _route(logits_t):
    col = lambda i: logits_t[i : i + 1, :]
    gl = [col(i) for i in range(N_GROUPS)]
    gmax = functools.reduce(jnp.maximum, gl)
    g_idx = jnp.full_like(gmax, N_GROUPS - 1).astype(jnp.int32)
    for i in reversed(range(N_GROUPS - 1)):
        g_idx = jnp.where(gl[i] == gmax, i, g_idx)
    g_p = 1.0 / functools.reduce(lambda a, b: a + b, [jnp.exp(x - gmax) for x in gl])
    el = []
    for e in range(EXPERTS_PER_GROUP):
        v = col(N_GROUPS + (N_GROUPS - 1) * EXPERTS_PER_GROUP + e)
        for g in reversed(range(N_GROUPS - 1)):
            v = jnp.where(g_idx == g, col(N_GROUPS + g * EXPERTS_PER_GROUP + e), v)
        el.append(v)
    m1 = functools.reduce(jnp.maximum, el)
    i1 = jnp.full_like(g_idx, EXPERTS_PER_GROUP - 1)
    for e in reversed(range(EXPERTS_PER_GROUP - 1)):
        i1 = jnp.where(el[e] == m1, e, i1)
    rest = [jnp.where(i1 == e, -jnp.inf, el[e]) for e in range(EXPERTS_PER_GROUP)]
    m2 = functools.reduce(jnp.maximum, rest)
    i2 = jnp.full_like(g_idx, EXPERTS_PER_GROUP - 1)
    for e in reversed(range(EXPERTS_PER_GROUP - 1)):
        i2 = jnp.where((rest[e] == m2) & (i1 != e), e, i2)
    t = jnp.exp(m2 - m1)
    w1 = g_p / (1.0 + t)
    w2 = g_p * t / (1.0 + t)
    lo = jnp.minimum(i1, i2)
    hi = jnp.maximum(i1, i2)
    pair = jnp.where(hi == 1, 0, jnp.where(hi == 2, jnp.where(lo == 1, 1, 2), 3 + lo))
    bucket = g_idx * N_PAIRS + pair
    w_lower = jnp.where(i1 < i2, w1, w2)
    w_higher = jnp.where(i1 < i2, w2, w1)
    w_a = jnp.where(pair == 0, w_lower, w_higher)
    w_b = jnp.where(pair == 0, w_higher, w_lower)
    return bucket, w_a, w_b


def _merge_kernel(gate_ref, rg_ref, u_ref, hf_ref, hb_ref, rf_ref, rb_ref, yg_ref, x_ref, m_ref,
                  hgn_ref, gng_ref, gnb_ref, d_ref, gw_ref, gb_ref, wo_ref, lng_ref, lnb_ref,
                  wrh_ref, wrl_ref, br_ref, a128_ref, a64_ref, tri_ref, x1_ref, h2_ref, route_ref, cnt_ref, wt_scr,
                  *, alpha):
    @pl.when((pl.program_id(0) == 0) & (pl.program_id(1) == 0))
    def _():
        cnt_ref[...] = jnp.zeros_like(cnt_ref)

    o_hg = hf_ref[0] + hb_ref[0]
    ms = _dot2(o_hg * o_hg, a128_ref[...])
    hg = o_hg * lax.rsqrt(ms + LN_EPS) * hgn_ref[0] * _silu(gate_ref[0])
    o_rt = rf_ref[0] + rb_ref[0]
    mu = _dot2(o_rt, a64_ref[...])
    xc = o_rt - mu
    var = _dot2(xc * xc, a64_ref[...])
    rt = (xc * lax.rsqrt(var + LN_EPS) * gng_ref[0] + gnb_ref[0]) * _silu(rg_ref[0])
    rows_per_chunk = S5_CHUNK // SUBLANES
    n_chunks = yg_ref.shape[1]
    y5 = []
    for half in range(S5_WIDTH // LANES):
        for g_lo in range(SUBLANES):
            for s_hi in range(rows_per_chunk):
                wt_scr[pl.ds(s_hi * SUBLANES + g_lo, n_chunks, stride=S5_CHUNK), :] = (
                    yg_ref[half * SUBLANES + g_lo, :, s_hi * LANES : (s_hi + 1) * LANES])
        y5.append(_swap_sublane_lanegroup(wt_scr[...]))
    y5 = jnp.concatenate(y5, axis=-1)
    s5 = jax.nn.gelu(y5 + d_ref[0] * u_ref[0])
    s5 = s5 * jax.nn.sigmoid(_dot(s5.astype(BF16), gw_ref[0]) + gb_ref[0])
    cat = jnp.concatenate([hg, rt, s5], axis=-1).astype(BF16)
    y = _dot(cat, wo_ref[0])
    m = m_ref[0, 0, 0]
    x1 = _layer_norm(alpha * x_ref[0] + m[2:3, :] * y, lng_ref[0], lnb_ref[0])
    x1_ref[0] = x1
    h2 = x1 * (1.0 + m[4:5, :]) + m[3:4, :]
    h_hi, h_lo = _split_bf16(h2)
    logits_t = (_dot_nt(wrh_ref[0], h_hi) + _dot_nt(wrh_ref[0], h_lo) + _dot_nt(wrl_ref[0], h_hi)) + br_ref[0]
    bucket, _, _ = _route(logits_t)
    sub = lax.broadcasted_iota(jnp.int32, logits_t.shape, 0)
    h2_ref[0] = h2.reshape(h2_ref.shape[1:])
    onehot = sub == bucket
    before = _dot(onehot.astype(BF16), tri_ref[...]) + cnt_ref[...]
    rank = jnp.sum(jnp.where(onehot, before, 0.0), axis=0, keepdims=True)
    rsub = lax.broadcasted_iota(jnp.int32, route_ref.shape[1:], 0)
    route_ref[0] = jnp.where(rsub == 0, bucket.astype(F32), jnp.where(rsub == 1, rank, 0.0))
    cnt_ref[...] += jnp.sum(onehot.astype(F32), axis=1, keepdims=True)


def _merge(p, o_hf, o_hb, o_rf, o_rb, yg, x, modtab, layer_prm, const_prm, l, tm, nct, alpha):
    b, s, d = x.shape
    cpt = tm // S5_CHUNK

    def tok(width, col):
        return pl.BlockSpec((1, tm, width), lambda i, j: (i, j, col))

    def whole(a):
        return pl.BlockSpec(a.shape, lambda i, j: (0,) * a.ndim)

    rbase = 5 * HG_WIDTH // RET_WIDTH
    in_specs = [tok(HG_WIDTH, 2), tok(RET_WIDTH, rbase + 3), tok(S5_WIDTH, rbase + 4),
                tok(HG_WIDTH, 0), tok(HG_WIDTH, 0), tok(RET_WIDTH, 0), tok(RET_WIDTH, 0),
                pl.BlockSpec((S5_GROUPS, cpt, S5_FOLD), lambda i, j: (0, j, i)),
                tok(d, 0), _mod_spec(l, nct)]
    in_specs += [_layer_spec(a.shape[1:], l) for a in layer_prm]
    in_specs += [whole(a) for a in const_prm]
    return pl.pallas_call(
        functools.partial(_merge_kernel, alpha=alpha),
        grid=(b, s // tm),
        in_specs=in_specs,
        out_specs=[tok(d, 0), pl.BlockSpec((1, tm, d // LANES, LANES), lambda i, j: (i, j, 0, 0)),
                   pl.BlockSpec((1, SUBLANES, tm), lambda i, j: (i, 0, j)),
                   pl.BlockSpec((LANES, 1), lambda i, j: (0, 0))],
        out_shape=[jax.ShapeDtypeStruct((b, s, d), F32), jax.ShapeDtypeStruct((b, s, d // LANES, LANES), F32),
                   jax.ShapeDtypeStruct((b, SUBLANES, s), F32), jax.ShapeDtypeStruct((LANES, 1), F32)],
        scratch_shapes=[pltpu.VMEM((tm, LANES), F32)],
        compiler_params=_cparams(("arbitrary", "arbitrary")),
        name="merge_ln1_router",
    )(p, p, p, o_hf, o_hb, o_rf, o_rb, yg, x, modtab, *layer_prm, *const_prm)


def _routing_tables(route, counts, te, n_tiles):
    bucket = route[:, 0, :].astype(jnp.int32).reshape(-1)
    rank = route[:, 1, :].astype(jnp.int32).reshape(-1)
    cnt = counts[:N_BUCKETS, 0].astype(jnp.int32)
    padded = (cnt + te - 1) // te * te
    ends = jnp.cumsum(padded)
    pos = (ends - padded)[bucket] + rank
    n_used = ends[-1] // te
    tile = jnp.arange(n_tiles, dtype=jnp.int32)
    tb = jnp.minimum(jnp.searchsorted(ends, tile * te, side="right"), N_BUCKETS - 1).astype(jnp.int32)
    tb = jnp.where(tile < n_used, tb, tb[jnp.maximum(n_used - 1, 0)])
    group, pair = tb // N_PAIRS, tb % N_PAIRS
    ea = group * EXPERTS_PER_GROUP + jnp.asarray(PAIR_SLOT_A, jnp.int32)[pair]
    eb = group * EXPERTS_PER_GROUP + jnp.asarray(PAIR_SLOT_B, jnp.int32)[pair]
    return pos, ea, eb, n_used.reshape(1).astype(jnp.int32)


def _dispatch_kernel(pos_ref, src_ref, init_ref, out_ref, sem, *, tm, tiles_per_row):
    del init_ref
    base = (pl.program_id(0) * tiles_per_row + pl.program_id(1)) * tm

    def row_copy(r):
        return pltpu.make_async_copy(src_ref.at[0, r], out_ref.at[pos_ref[base + r]], sem)

    def start(g, carry):
        first = pl.multiple_of(g * DMA_UNROLL, DMA_UNROLL)
        for k in range(DMA_UNROLL):
            row_copy(first + k).start(priority=k % 2)
        return carry

    def wait(r, carry):
        row_copy(r).wait()
        return carry

    lax.fori_loop(0, tm // DMA_UNROLL, start, 0)
    lax.fori_loop(0, tm, wait, 0, unroll=DMA_UNROLL)


def _dispatch(pos, rows, init, tm):
    b, s, rt, rl = rows.shape
    return pl.pallas_call(
        functools.partial(_dispatch_kernel, tm=tm, tiles_per_row=s // tm),
        grid_spec=pltpu.PrefetchScalarGridSpec(
            num_scalar_prefetch=1,
            grid=(b, s // tm),
            in_specs=[pl.BlockSpec((1, tm, rt, rl), lambda i, j, pos_ref: (i, j, 0, 0)),
                      pl.BlockSpec(memory_space=pl.ANY)],
            out_specs=pl.BlockSpec(memory_space=pl.ANY),
            scratch_shapes=[pltpu.SemaphoreType.DMA(())],
        ),
        out_shape=jax.ShapeDtypeStruct(init.shape, F32),
        input_output_aliases={2: 0},
        compiler_params=_cparams(("arbitrary", "arbitrary")),
        name="moe_dispatch",
    )(pos, rows, init)


def _expert_kernel(ea_ref, eb_ref, nused_ref, hs_ref, wga_ref, wua_ref, wda_ref, wgb_ref, wub_ref, wdb_ref,
                   wr_ref, br_ref, o_ref, wg_scr, wu_scr, wd_scr):
    t = pl.program_id(0)
    prev = jnp.maximum(t - 1, 0)
    @pl.when((t == 0) | (ea_ref[t] != ea_ref[prev]))
    def _():
        wg_scr[0] = wga_ref[0, 0].astype(BF16)
        wu_scr[0] = wua_ref[0, 0].astype(BF16)
        wd_scr[0] = wda_ref[0, 0].astype(BF16)

    @pl.when((t == 0) | (eb_ref[t] != eb_ref[prev]))
    def _():
        wg_scr[1] = wgb_ref[0, 0].astype(BF16)
        wu_scr[1] = wub_ref[0, 0].astype(BF16)
        wd_scr[1] = wdb_ref[0, 0].astype(BF16)

    @pl.when(t < nused_ref[0])
    def _():
        d = wg_scr.shape[1]
        h32 = hs_ref[...].reshape(hs_ref.shape[0], d)
        h = h32.astype(BF16)
        logits_t = _dot_nt(wr_ref[0], h) + br_ref[0]
        sub = lax.broadcasted_iota(jnp.int32, logits_t.shape, 0)
        pick = lambda r: jnp.sum(jnp.where(sub == r, logits_t, 0.0), axis=0, keepdims=True)
        ea, eb = ea_ref[t], eb_ref[t]
        l_a, l_b = pick(N_GROUPS + ea), pick(N_GROUPS + eb)
        l_g = pick(lax.shift_right_logical(ea, EXPERTS_PER_GROUP.bit_length() - 1))
        g_sum = functools.reduce(lambda x, z: x + z, [jnp.exp(logits_t[i : i + 1, :] - l_g) for i in range(N_GROUPS)])
        m = jnp.maximum(l_a, l_b)
        t_a, t_b = jnp.exp(l_a - m), jnp.exp(l_b - m)
        inv = 1.0 / (g_sum * (t_a + t_b))
        w_cols = jnp.where(sub == 0, t_a * inv, jnp.where(sub == 1, t_b * inv, 0.0)).T
        y = None
        for e in range(2):
            act = (_silu(_dot(h, wg_scr[e])) * _dot(h, wu_scr[e])).astype(BF16)
            ye = w_cols[:, e : e + 1] * _dot(act, wd_scr[e])
            y = ye if y is None else y + ye
        o_ref[...] = y.reshape(o_ref.shape)

    @pl.when(t >= nused_ref[0])
    def _():
        o_ref[...] = jnp.zeros_like(o_ref)


def _experts(ea, eb, n_used, hs, wg, wu, wd, router, l, te):
    n_sorted, rt, rl = hs.shape
    _, _, d, eh = wg.shape

    def wspec(shape, which):
        return pl.BlockSpec((1, 1) + shape, lambda t, ea_ref, eb_ref, n_ref: (l, (ea_ref, eb_ref)[which][t], 0, 0))

    return pl.pallas_call(
        _expert_kernel,
        grid_spec=pltpu.PrefetchScalarGridSpec(
            num_scalar_prefetch=3,
            grid=(n_sorted // te,),
            in_specs=[pl.BlockSpec((te, rt, rl), lambda t, *_: (t, 0, 0)),
                      wspec((d, eh), 0), wspec((d, eh), 0), wspec((eh, d), 0),
                      wspec((d, eh), 1), wspec((d, eh), 1), wspec((eh, d), 1)]
                     + [_layer_spec(a.shape[1:], l) for a in router],
            out_specs=pl.BlockSpec((te, d // LANES, LANES), lambda t, *_: (t, 0, 0)),
            scratch_shapes=[pltpu.VMEM((2, d, eh), BF16), pltpu.VMEM((2, d, eh), BF16), pltpu.VMEM((2, eh, d), BF16)],
        ),
        out_shape=jax.ShapeDtypeStruct((n_sorted, d // LANES, LANES), F32),
        compiler_params=_cparams(("arbitrary",)),
        name="moe_experts",
    )(ea, eb, n_used, hs, wg, wu, wd, wg, wu, wd, *router)


def _gather_expert_rows(pos_ref, ys_ref, buf, sem, *, tm, tiles_per_row, first_tile):
    cols = pl.num_programs(1)
    n_steps = pl.num_programs(0) * cols
    step = pl.program_id(0) * cols + pl.program_id(1)
    slot = step % 2

    def row_copy(step_, slot_, r):
        tile = (step_ // cols) * tiles_per_row + first_tile + step_ % cols
        return pltpu.make_async_copy(ys_ref.at[pos_ref[tile * tm + r]], buf.at[slot_, r], sem.at[slot_])

    def start_tile(step_, slot_):
        def body(g, carry):
            first = pl.multiple_of(g * DMA_UNROLL, DMA_UNROLL)
            for k in range(DMA_UNROLL):
                row_copy(step_, slot_, first + k).start(priority=k % 2)
            return carry

        lax.fori_loop(0, tm // DMA_UNROLL, body, 0)

    @pl.when(step == 0)
    def _():
        start_tile(0, 0)

    @pl.when(step + 1 < n_steps)
    def _():
        start_tile(step + 1, 1 - slot)

    def wait(r, carry):
        row_copy(step, slot, r).wait()
        return carry

    lax.fori_loop(0, tm, wait, 0, unroll=DMA_UNROLL)
    return buf[slot].reshape(tm, buf.shape[2] * buf.shape[3])


def _combine_kernel(pos_ref, ys_ref, x1_ref, m_ref, lng_ref, lnb_ref, o_ref, buf, sem,
                    *, tm, tiles_per_row, first_tile, alpha):
    y = _gather_expert_rows(pos_ref, ys_ref, buf, sem, tm=tm, tiles_per_row=tiles_per_row, first_tile=first_tile)
    m = m_ref[0, 0, 0]
    o_ref[0] = _layer_norm(alpha * x1_ref[0] + m[5:6, :] * y, lng_ref[0], lnb_ref[0])


def _combine(pos, ys, x1, modtab, ln_g, ln_b, l, tm, nct, alpha, first_tile):
    b, s, d = x1.shape
    cols = s // tm - first_tile
    mod_spec = pl.BlockSpec((1, 1, 1, 6, d),
                            lambda i, j, pos_ref: (l, i, jnp.minimum((j + first_tile) // nct, 1), 0, 0))
    return pl.pallas_call(
        functools.partial(_combine_kernel, tm=tm, tiles_per_row=s // tm, first_tile=first_tile, alpha=alpha),
        grid_spec=pltpu.PrefetchScalarGridSpec(
            num_scalar_prefetch=1,
            grid=(b, cols),
            in_specs=[pl.BlockSpec(memory_space=pl.ANY),
                      pl.BlockSpec((1, tm, d), lambda i, j, pos_ref: (i, j + first_tile, 0)),
                      mod_spec, _layer_spec((1, d), l), _layer_spec((1, d), l)],
            out_specs=pl.BlockSpec((1, tm, d), lambda i, j, pos_ref: (i, j, 0)),
            scratch_shapes=[pltpu.VMEM((2, tm, d // LANES, LANES), F32), pltpu.SemaphoreType.DMA((2,))],
        ),
        out_shape=jax.ShapeDtypeStruct((b, cols * tm, d), F32),
        compiler_params=_cparams(("arbitrary", "arbitrary")),
        name="moe_combine_ln2",
    )(pos, ys, x1, modtab, ln_g, ln_b)


def _block_avg(width, group):
    idx = np.arange(width) // group
    return jnp.asarray((idx[:, None] == idx[None, :]).astype(np.float32) / group, dtype=BF16)


def kernel(x, c, ctx, c_ctx, w_mod, b_mod, w_in, hg_lb_raw, hg_norm_g, ret_decay_raw, ret_gn_g, ret_gn_b, s5_lam_re, s5_lam_im, s5_log_dt, s5_b_re, s5_b_im, s5_c_re, s5_c_im, s5_d, s5_glu_w, s5_glu_b, w_out, ln1_g, ln1_b, ln2_g, ln2_b, rg_w, rg_b, re_w, re_b, exp_w_gate, exp_w_up, exp_w_down):
    bsz, t_lat, d = x.shape
    t_ctx = ctx.shape[1]
    depth = w_mod.shape[0]
    alpha = (2.0 * depth) ** 0.25
    tm = TOKEN_TILE
    assert d == D_MODEL and t_lat % GRID_W == 0 and bsz < SUBLANES
    assert t_ctx % tm == 0 and t_lat % tm == 0, "context and latent lengths must be multiples of the token tile"
    nct = t_ctx // tm
    s = t_ctx + t_lat

    cvec = jnp.concatenate([c, c_ctx[None, :], jnp.zeros((SUBLANES - bsz - 1, d), F32)], 0)
    mod_all = _modulation(cvec, w_mod, b_mod)
    lat = mod_all[:, :bsz].reshape(depth, bsz, 6, d)
    cm = jnp.broadcast_to(mod_all[:, bsz].reshape(depth, 1, 6, d), (depth, bsz, 6, d))
    modtab = jnp.stack([cm, lat], axis=2)

    hg_lb = jnp.cumsum(jax.nn.softmax(hg_lb_raw.astype(F32), axis=0), axis=0)
    hg_lb = hg_lb - hg_lb[:1]
    ret_tables = _retention_decay_tables(jax.nn.log_sigmoid(ret_decay_raw.astype(F32)))
    cos_tab, sin_tab = _rope_tables(t_lat, t_ctx)
    s5_tabs = _s5_weights(s5_lam_re, s5_lam_im, s5_log_dt, s5_b_re, s5_b_im, s5_c_re, s5_c_im)
    pad_r = LANES - N_GROUPS - N_EXPERTS
    wr = jnp.concatenate([rg_w, re_w.reshape(depth, d, N_EXPERTS), jnp.zeros((depth, d, pad_r), F32)], axis=2)
    wr = jnp.swapaxes(wr, 1, 2)
    wr_hi, wr_lo = _split_bf16(wr)
    br = jnp.concatenate([rg_b, re_b.reshape(depth, N_EXPERTS), jnp.zeros((depth, pad_r), F32)], axis=1)[:, :, None]
    row = lambda a: a[:, None, :]
    layer_prm = [row(jnp.tile(hg_norm_g, (1, HG_HEADS))), row(ret_gn_g), row(ret_gn_b), row(s5_d),
                 s5_glu_w.astype(BF16), row(s5_glu_b), w_out.astype(BF16), row(ln1_g), row(ln1_b),
                 wr_hi, wr_lo, br]
    tri = jnp.asarray(np.triu(np.ones((tm, tm), np.float32), 1), dtype=BF16)
    const_prm = [_block_avg(HG_WIDTH, HG_DK), _block_avg(RET_WIDTH, RET_DK), tri]
    w_in_bf16 = w_in.astype(BF16)
    ln2_g, ln2_b = row(ln2_g), row(ln2_b)
    te = EXPERT_TILE
    n_tiles = -(-(bsz * s + N_BUCKETS * (te - 1)) // te)

    hs = jnp.zeros((n_tiles * te, d // LANES, LANES), F32)
    xs = jnp.concatenate([ctx, x], axis=1)
    for l in range(depth):
        if l == 0:
            p, ug = _inproj(xs, modtab, w_in_bf16, l, tm, nct)
        else:
            xs, p, ug = _ln2_inproj(pos, ys, x1, modtab, ln2_g, ln2_b, w_in_bf16, l, tm, nct, alpha)
        o_hf, o_hb = _hgrn(p, hg_lb, l, t_ctx // CHUNK)
        o_rf, o_rb = _retention(p, cos_tab, sin_tab, ret_tables, l, t_ctx // RET_CHUNK)
        yg = _s5_conv(ug, *s5_tabs, l, bsz, t_ctx // S5_CHUNK)
        x1, rows, route, counts = _merge(p, o_hf, o_hb, o_rf, o_rb, yg, xs, modtab, layer_prm, const_prm,
                                         l, tm, nct, alpha)
        pos, ea, eb, n_used = _routing_tables(route, counts, te, n_tiles)
        hs = _dispatch(pos, rows, hs, tm)
        ys = _experts(ea, eb, n_used, hs, exp_w_gate, exp_w_up, exp_w_down, (wr_hi, br), l, te)
    return _combine(pos, ys, x1, modtab, ln2_g, ln2_b, depth - 1, tm, nct, alpha, nct)
```
